```python
import jax
import jax.numpy as jnp
from jax import lax
import numpy as np


D_MODEL = 1024
BATCH = 32
SEQ = 256
DEPTH = 4
DEC_BATCH = 8
DEC_SEQ = 1024
PAST_LEN = 256

GRID_W = 64
N_MIXERS = 3
N_A = (DEPTH + 2) // 3
N_B = (DEPTH + 1) // 3
N_C = DEPTH // 3
NORM_EPS = 1e-6
D_FF = 256 * ((8 * D_MODEL // 3 + 255) // 256)
FFN_CONV = 3
ML_D_IN = 2 * D_MODEL
ML_HEADS = 4
ML_DK = ML_D_IN // (2 * ML_HEADS)
ML_DV = ML_D_IN // ML_HEADS
ML_CONV = 3
ML_CHUNK = 128
HEAD_DIM = 64
SWA_HEADS = D_MODEL // HEAD_DIM
SWA_KV = SWA_HEADS // 4
SWA_WINDOW = 128
Q_BLOCK = 128
ROPE_BASE = 10000.0
NA_HEADS = D_MODEL // HEAD_DIM
NA_KH = 8
NA_KW = 16
NEG_INF = -1e30

kernel_name = 'hybrid_mlstm_swa_natten_diffusion_step'


def _rmsnorm(x, w):
    xf = x.astype(jnp.float32)
    y = xf * lax.rsqrt(jnp.mean(xf * xf, axis=-1, keepdims=True) + NORM_EPS)
    return (y * w.astype(jnp.float32)).astype(x.dtype)


def _dwconv(x, w, b):
    K, C = w.shape
    y = lax.conv_general_dilated(x, w[:, None, :].astype(x.dtype), window_strides=(1,),
                                 padding=[(K // 2, K // 2)],
                                 dimension_numbers=('NWC', 'WIO', 'NWC'),
                                 feature_group_count=C)
    return y + b.astype(x.dtype)


def _heads(x, n):
    B, T, _ = x.shape
    return x.reshape(B, T, n, -1).transpose(0, 2, 1, 3)


def _unheads(x):
    B, n, T, hd = x.shape
    return x.transpose(0, 2, 1, 3).reshape(B, T, n * hd)


def _split_qkv(h, w, n_q, n_kv):
    qkv = h @ w
    a, b = n_q * HEAD_DIM, (n_q + n_kv) * HEAD_DIM
    return _heads(qkv[..., :a], n_q), _heads(qkv[..., a:b], n_kv), _heads(qkv[..., b:], n_kv)


def _axial_rope(x):
    T, hd = x.shape[2], x.shape[3]
    half = hd // 2
    quarter = half // 2
    pos = jnp.arange(T)
    row = (pos // GRID_W).astype(jnp.float32)
    col = (pos % GRID_W).astype(jnp.float32)
    inv = jnp.power(ROPE_BASE, -jnp.arange(quarter, dtype=jnp.float32) / quarter)

    def rot(xs, p):
        ang = p[:, None] * inv[None, :]
        cos, sin = jnp.cos(ang), jnp.sin(ang)
        x1 = xs[..., :quarter].astype(jnp.float32)
        x2 = xs[..., quarter:].astype(jnp.float32)
        return jnp.concatenate([x1 * cos - x2 * sin, x2 * cos + x1 * sin], axis=-1)

    return jnp.concatenate([rot(x[..., :half], row), rot(x[..., half:], col)], axis=-1).astype(x.dtype)


def _q_blocks(q, n_kv, bs):
    B, H, T, hd = q.shape
    return jnp.moveaxis(q.reshape(B, n_kv, H // n_kv, T // bs, bs, hd), 3, 0)


def _merge_blocks(o):
    nb, B, KV, G, bs, hd = o.shape
    return jnp.moveaxis(o, 0, 3).reshape(B, KV * G, nb * bs, hd)


def _attend(q, parts, sink):
    scale = q.shape[-1] ** -0.5
    logits = []
    for k_, v_, bias in parts:
        s = jnp.einsum('bkgqd,bksd->bkgqs', q, k_).astype(jnp.float32) * scale
        if bias is not None:
            s = s + bias
        logits.append(s)
    if sink is not None:
        logits.append(jnp.broadcast_to(sink.astype(jnp.float32)[None, :, :, None, None],
                                       logits[0].shape[:-1] + (1,)))
    p = jax.nn.softmax(jnp.concatenate(logits, axis=-1), axis=-1)
    out, off = None, 0
    for k_, v_, _ in parts:
        n = k_.shape[2]
        o = jnp.einsum('bkgqs,bksd->bkgqd', p[..., off:off + n].astype(v_.dtype), v_)
        out = o if out is None else out + o
        off += n
    return out


def _ctx_attention(q, k, v, sink):
    qb = _q_blocks(q, k.shape[1], Q_BLOCK)
    o = lax.map(lambda qq: _attend(qq, ((k, v, None),), sink), qb)
    return _merge_blocks(o)


def _swa_latent(q, k, v, k_ctx, v_ctx, sink):
    T = q.shape[2]
    span = Q_BLOCK + 2 * SWA_WINDOW
    pad = ((0, 0), (0, 0), (SWA_WINDOW, SWA_WINDOW), (0, 0))
    kp, vp = jnp.pad(k, pad), jnp.pad(v, pad)
    qb = _q_blocks(q, k.shape[1], Q_BLOCK)

    def block(args):
        qq, j = args
        s = j * Q_BLOCK
        k_loc = lax.dynamic_slice_in_dim(kp, s, span, axis=2)
        v_loc = lax.dynamic_slice_in_dim(vp, s, span, axis=2)
        qpos = s + jnp.arange(Q_BLOCK)
        kpos = s - SWA_WINDOW + jnp.arange(span)
        valid = (jnp.abs(qpos[:, None] - kpos[None, :]) <= SWA_WINDOW) & (kpos[None, :] >= 0) & (kpos[None, :] < T)
        bias = jnp.where(valid, 0.0, NEG_INF).astype(jnp.float32)
        return _attend(qq, ((k_loc, v_loc, bias), (k_ctx, v_ctx, None)), sink)

    o = lax.map(block, (qb, jnp.arange(T // Q_BLOCK)))
    return _merge_blocks(o)


def _na_latent(q, k, v, k_ctx, v_ctx, rpb):
    H, T = q.shape[1], q.shape[2]
    rows = T // GRID_W
    kh = min(NA_KH, rows)
    cq = jnp.arange(GRID_W)
    cs = jnp.clip(cq - NA_KW // 2, 0, GRID_W - NA_KW)
    col_valid = (cq[None, :] >= cs[:, None]) & (cq[None, :] < cs[:, None] + NA_KW)
    dc = jnp.clip(cq[None, :] - cq[:, None], -(NA_KW - 1), NA_KW - 1) + (NA_KW - 1)
    mask = jnp.broadcast_to(col_valid[:, None, :], (GRID_W, kh, GRID_W)).reshape(GRID_W, kh * GRID_W)
    qb = _q_blocks(q, H, GRID_W)

    def block(args):
        qq, r = args
        rs = jnp.clip(r - kh // 2, 0, rows - kh)
        k_loc = lax.dynamic_slice_in_dim(k, rs * GRID_W, kh * GRID_W, axis=2)
        v_loc = lax.dynamic_slice_in_dim(v, rs * GRID_W, kh * GRID_W, axis=2)
        dr = rs + jnp.arange(kh) - r + (NA_KH - 1)
        tab = jnp.take(rpb, dr, axis=1)
        bias = tab[:, :, dc]
        bias = bias.transpose(0, 2, 1, 3).reshape(H, GRID_W, kh * GRID_W).astype(jnp.float32)
        bias = jnp.where(mask[None], bias, NEG_INF)[None, :, None]
        return _attend(qq, ((k_loc, v_loc, bias), (k_ctx, v_ctx, None)), None)

    o = lax.map(block, (qb, jnp.arange(rows)))
    return _merge_blocks(o)


def _mlstm_scan(q, k, v, ig, fg, C0, n0, m0):
    B, NH, T, DK = q.shape
    L = ML_CHUNK
    nc = T // L
    f32 = jnp.float32

    def chunks(a):
        return jnp.moveaxis(a.reshape(a.shape[:2] + (nc, L) + a.shape[3:]), 2, 0)

    qs = chunks(q.astype(f32) * DK ** -0.5)
    ks = chunks(k.astype(f32))
    vs = chunks(v.astype(f32))
    iss = chunks(ig.astype(f32))
    lfs = chunks(jax.nn.log_sigmoid(fg.astype(f32)))
    causal = jnp.tril(jnp.ones((L, L), dtype=bool))

    def step(carry, xs):
        C, n, m = carry
        qc, kc, vc, ic, lfc = xs
        b = jnp.cumsum(lfc, axis=-1)
        a = b + m[..., None]
        D = jnp.where(causal, b[..., :, None] - b[..., None, :] + ic[..., None, :], NEG_INF)
        m_t = jnp.maximum(a, jnp.max(D, axis=-1))
        w_inter = jnp.exp(a - m_t)
        S = jnp.einsum('bhtk,bhsk->bhts', qc, kc) * jnp.exp(D - m_t[..., None])
        num = w_inter[..., None] * jnp.einsum('bhtk,bhkv->bhtv', qc, C) + jnp.einsum('bhts,bhsv->bhtv', S, vc)
        den = w_inter * jnp.einsum('bhtk,bhk->bht', qc, n) + jnp.sum(S, axis=-1)
        h = num / jnp.maximum(jnp.abs(den), jnp.exp(-m_t))[..., None]
        bL = b[..., -1]
        m_new = m_t[..., -1]
        w_C = jnp.exp(bL + m - m_new)
        w_s = jnp.exp(bL[..., None] - b + ic - m_new[..., None])
        C_new = w_C[..., None, None] * C + jnp.einsum('bhs,bhsk,bhsv->bhkv', w_s, kc, vc)
        n_new = w_C[..., None] * n + jnp.einsum('bhs,bhsk->bhk', w_s, kc)
        return (C_new, n_new, m_new), h

    (C, n, m), h = lax.scan(step, (C0.astype(f32), n0.astype(f32), m0.astype(f32)),
                            (qs, ks, vs, iss, lfs))
    h = jnp.moveaxis(h, 0, 2).reshape(B, NH, T, -1)
    return h, C, n, m


def _mlstm_mixer(h, C0, n0, m0, w_up, conv_w, conv_b, w_qk, w_v, w_gate, b_gate, norm_w, skip, w_down):
    B, T, _ = h.shape
    xz = h @ w_up
    x_m, z = xz[..., :ML_D_IN], xz[..., ML_D_IN:]
    x_c = jax.nn.silu(_dwconv(x_m, conv_w, conv_b))
    qk = x_c @ w_qk
    q = _heads(qk[..., :ML_HEADS * ML_DK], ML_HEADS)
    k = _heads(qk[..., ML_HEADS * ML_DK:], ML_HEADS)
    v = _heads(x_m @ w_v, ML_HEADS)
    g = (x_m @ w_gate + b_gate).astype(jnp.float32).reshape(B, T, 2, 2, ML_HEADS).transpose(2, 3, 0, 4, 1)
    h_f, C_f, n_f, m_f = _mlstm_scan(q, k, v, g[0, 0], g[0, 1], C0[:, 0], n0[:, 0], m0[:, 0])
    h_b, C_b, n_b, m_b = _mlstm_scan(jnp.flip(q, 2), jnp.flip(k, 2), jnp.flip(v, 2),
                                     jnp.flip(g[1, 0], 2), jnp.flip(g[1, 1], 2),
                                     C0[:, 1], n0[:, 1], m0[:, 1])
    hs = h_f + jnp.flip(h_b, 2)
    mu = jnp.mean(hs, axis=-1, keepdims=True)
    var = jnp.mean(jnp.square(hs - mu), axis=-1, keepdims=True)
    hn = _unheads((hs - mu) * lax.rsqrt(var + NORM_EPS)).astype(h.dtype) * norm_w
    out = (hn + skip * x_c) * jax.nn.silu(z)
    return (out @ w_down, jnp.stack([C_f, C_b], axis=1), jnp.stack([n_f, n_b], axis=1),
            jnp.stack([m_f, m_b], axis=1))


def _conv_ffn(h, w_up, conv_w, conv_b, w_down):
    u = _dwconv(h @ w_up, conv_w, conv_b)
    g, val = u[..., :D_FF], u[..., D_FF:]
    return (jax.nn.silu(g) * val) @ w_down


def setup_inputs(seed: int = 0) -> dict:
    key = jax.random.key(seed)
    ks = iter(jax.random.split(key, 48))

    def nrm(shape, scale=1.0):
        return jax.random.normal(next(ks), shape, jnp.float32) * scale

    b_i = nrm((N_A, 2, 1, ML_HEADS), 0.1)
    b_f = 3.0 + nrm((N_A, 2, 1, ML_HEADS), 0.5)
    ml_b_gate = jnp.concatenate([b_i, b_f], axis=2).reshape(N_A, 4 * ML_HEADS)
    return {
        'x_prompt': nrm((BATCH, SEQ, D_MODEL)),
        'x_sample': nrm((DEC_BATCH, DEC_SEQ, D_MODEL)),
        'state_mlstm_C': nrm((DEC_BATCH, N_A, 2, ML_HEADS, ML_DK, ML_DV), 0.1),
        'state_mlstm_n': nrm((DEC_BATCH, N_A, 2, ML_HEADS, ML_DK), 0.5),
        'state_mlstm_m': nrm((DEC_BATCH, N_A, 2, ML_HEADS), 1.0),
        'cache_swa_k': nrm((DEC_BATCH, N_B, SWA_KV, PAST_LEN, HEAD_DIM)),
        'cache_swa_v': nrm((DEC_BATCH, N_B, SWA_KV, PAST_LEN, HEAD_DIM)),
        'cache_na_k': nrm((DEC_BATCH, N_C, NA_HEADS, PAST_LEN, HEAD_DIM)),
        'cache_na_v': nrm((DEC_BATCH, N_C, NA_HEADS, PAST_LEN, HEAD_DIM)),
        'c': nrm((DEC_BATCH, D_MODEL)),
        'c_ctx': nrm((D_MODEL,)),
        'ada_w': nrm((DEPTH, D_MODEL, 6 * D_MODEL), 0.5 * D_MODEL ** -0.5),
        'ada_b': nrm((DEPTH, 6 * D_MODEL), 0.02),
        'norm_w': 1.0 + nrm((DEPTH, 2, D_MODEL), 0.02),
        'final_norm_w': 1.0 + nrm((D_MODEL,), 0.02),
        'ffn_w_up': nrm((DEPTH, D_MODEL, 2 * D_FF), D_MODEL ** -0.5),
        'ffn_conv_w': nrm((DEPTH, FFN_CONV, 2 * D_FF), FFN_CONV ** -0.5),
        'ffn_conv_b': nrm((DEPTH, 2 * D_FF), 0.02),
        'ffn_w_down': nrm((DEPTH, D_FF, D_MODEL), D_FF ** -0.5),
        'ml_w_up': nrm((N_A, D_MODEL, 2 * ML_D_IN), D_MODEL ** -0.5),
        'ml_conv_w': nrm((N_A, ML_CONV, ML_D_IN), ML_CONV ** -0.5),
        'ml_conv_b': nrm((N_A, ML_D_IN), 0.02),
        'ml_w_qk': nrm((N_A, ML_D_IN, 2 * ML_HEADS * ML_DK), ML_D_IN ** -0.5),
        'ml_w_v': nrm((N_A, ML_D_IN, ML_HEADS * ML_DV), ML_D_IN ** -0.5),
        'ml_w_gate': nrm((N_A, ML_D_IN, 4 * ML_HEADS), 0.1 * ML_D_IN ** -0.5),
        'ml_b_gate': ml_b_gate,
        'ml_norm_w': 1.0 + nrm((N_A, ML_D_IN), 0.02),
        'ml_skip': 1.0 + nrm((N_A, ML_D_IN), 0.02),
        'ml_w_down': nrm((N_A, ML_D_IN, D_MODEL), ML_D_IN ** -0.5),
        'swa_w_qkv': nrm((N_B, D_MODEL, (SWA_HEADS + 2 * SWA_KV) * HEAD_DIM), D_MODEL ** -0.5),
        'swa_sink': nrm((N_B, SWA_HEADS), 0.5),
        'swa_w_o': nrm((N_B, SWA_HEADS * HEAD_DIM, D_MODEL), (SWA_HEADS * HEAD_DIM) ** -0.5),
        'na_w_qkv': nrm((N_C, D_MODEL, 3 * NA_HEADS * HEAD_DIM), D_MODEL ** -0.5),
        'na_rpb': nrm((N_C, NA_HEADS, 2 * NA_KH - 1, 2 * NA_KW - 1), 0.1),
        'na_w_o': nrm((N_C, NA_HEADS * HEAD_DIM, D_MODEL), (NA_HEADS * HEAD_DIM) ** -0.5),
    }


def reference(x_prompt, x_sample, state_mlstm_C, state_mlstm_n, state_mlstm_m,
              cache_swa_k, cache_swa_v, cache_na_k, cache_na_v, c, c_ctx,
              ada_w, ada_b, norm_w, final_norm_w,
              ffn_w_up, ffn_conv_w, ffn_conv_b, ffn_w_down,
              ml_w_up, ml_conv_w, ml_conv_b, ml_w_qk, ml_w_v, ml_w_gate, ml_b_gate,
              ml_norm_w, ml_skip, ml_w_down,
              swa_w_qkv, swa_sink, swa_w_o,
              na_w_qkv, na_rpb, na_w_o):
    xc, xl = x_prompt, x_sample
    Bp, Bd = xc.shape[0], xl.shape[0]
    sc = jax.nn.silu(c_ctx)
    sl = jax.nn.silu(c)
    C0 = jnp.zeros((Bp, 2, ML_HEADS, ML_DK, ML_DV), jnp.float32)
    n0 = jnp.zeros((Bp, 2, ML_HEADS, ML_DK), jnp.float32)
    m0 = jnp.zeros((Bp, 2, ML_HEADS), jnp.float32)
    new_C, new_n, new_m = [], [], []
    new_sk, new_sv, new_nk, new_nv = [], [], [], []

    for i in range(DEPTH):
        kind, j = i % N_MIXERS, i // N_MIXERS
        mod_c = (sc @ ada_w[i] + ada_b[i]).reshape(6, D_MODEL)
        mod_l = (sl @ ada_w[i] + ada_b[i]).reshape(Bd, 6, 1, D_MODEL)
        hc = _rmsnorm(xc, norm_w[i, 0]) * (1.0 + mod_c[1]) + mod_c[0]
        hl = _rmsnorm(xl, norm_w[i, 0]) * (1.0 + mod_l[:, 1]) + mod_l[:, 0]
        if kind == 0:
            oc, Cf, nf, mf = _mlstm_mixer(hc, C0, n0, m0, ml_w_up[j], ml_conv_w[j], ml_conv_b[j],
                                          ml_w_qk[j], ml_w_v[j], ml_w_gate[j], ml_b_gate[j],
                                          ml_norm_w[j], ml_skip[j], ml_w_down[j])
            ol, _, _, _ = _mlstm_mixer(hl, state_mlstm_C[:, j], state_mlstm_n[:, j], state_mlstm_m[:, j],
                                       ml_w_up[j], ml_conv_w[j], ml_conv_b[j],
                                       ml_w_qk[j], ml_w_v[j], ml_w_gate[j], ml_b_gate[j],
                                       ml_norm_w[j], ml_skip[j], ml_w_down[j])
            new_C.append(Cf)
            new_n.append(nf)
            new_m.append(mf)
        elif kind == 1:
            sink = swa_sink[j].reshape(SWA_KV, SWA_HEADS // SWA_KV)
            q_c, k_c, v_c = _split_qkv(hc, swa_w_qkv[j], SWA_HEADS, SWA_KV)
            oc = _unheads(_ctx_attention(q_c, k_c, v_c, sink)) @ swa_w_o[j]
            q_l, k_l, v_l = _split_qkv(hl, swa_w_qkv[j], SWA_HEADS, SWA_KV)
            ol = _unheads(_swa_latent(_axial_rope(q_l), _axial_rope(k_l), v_l,
                                      cache_swa_k[:, j], cache_swa_v[:, j], sink)) @ swa_w_o[j]
            new_sk.append(k_c)
            new_sv.append(v_c)
        else:
            q_c, k_c, v_c = _split_qkv(hc, na_w_qkv[j], NA_HEADS, NA_HEADS)
            oc = _unheads(_ctx_attention(q_c, k_c, v_c, None)) @ na_w_o[j]
            q_l, k_l, v_l = _split_qkv(hl, na_w_qkv[j], NA_HEADS, NA_HEADS)
            ol = _unheads(_na_latent(q_l, k_l, v_l, cache_na_k[:, j], cache_na_v[:, j], na_rpb[j])) @ na_w_o[j]
            new_nk.append(k_c)
            new_nv.append(v_c)
        xc = xc + mod_c[2] * oc
        xl = xl + mod_l[:, 2] * ol
        hc = _rmsnorm(xc, norm_w[i, 1]) * (1.0 + mod_c[4]) + mod_c[3]
        hl = _rmsnorm(xl, norm_w[i, 1]) * (1.0 + mod_l[:, 4]) + mod_l[:, 3]
        xc = xc + mod_c[5] * _conv_ffn(hc, ffn_w_up[i], ffn_conv_w[i], ffn_conv_b[i], ffn_w_down[i])
        xl = xl + mod_l[:, 5] * _conv_ffn(hl, ffn_w_up[i], ffn_conv_w[i], ffn_conv_b[i], ffn_w_down[i])

    y_prompt = _rmsnorm(xc, final_norm_w)
    y_sample = _rmsnorm(xl, final_norm_w)
    return (y_prompt, y_sample,
            jnp.stack(new_C, axis=1), jnp.stack(new_n, axis=1), jnp.stack(new_m, axis=1),
            jnp.stack(new_sk, axis=1), jnp.stack(new_sv, axis=1),
            jnp.stack(new_nk, axis=1), jnp.stack(new_nv, axis=1))
```

```python
import functools

import jax
import jax.numpy as jnp
import numpy as np
from jax import lax
from jax.experimental import pallas as pl
from jax.experimental.pallas import tpu as pltpu

F32 = jnp.float32
BF16 = jnp.bfloat16

D_MODEL = 1024
BATCH = 32
SEQ = 256
DEPTH = 4
DEC_BATCH = 8
DEC_SEQ = 1024
GRID_W = 64
N_MIXERS = 3
NORM_EPS = 1e-6
D_FF = 2816
ML_D_IN = 2 * D_MODEL
ML_HEADS = 4
ML_DK = ML_D_IN // (2 * ML_HEADS)
ML_DV = ML_D_IN // ML_HEADS
ML_CHUNK = 128
HEAD_DIM = 64
SWA_HEADS = D_MODEL // HEAD_DIM
SWA_KV = SWA_HEADS // 4
SWA_WINDOW = 128
Q_BLOCK = 128
ROPE_BASE = 10000.0
NA_HEADS = D_MODEL // HEAD_DIM
NA_KH = 8
NA_KW = 16
NEG_INF = -1e30

N_CTX = BATCH * SEQ
N_LAT = DEC_BATCH * DEC_SEQ
N_TOK = N_CTX + N_LAT
TM = 1024
N_CTX_TILES = N_CTX // TM
MOD_ROWS = 16
VMEM_LIMIT_BYTES = 56 * 1024 * 1024


def _params(*sem):
    return pltpu.CompilerParams(dimension_semantics=sem, vmem_limit_bytes=VMEM_LIMIT_BYTES)


def _mod_row(i):
    return jnp.where(i < N_CTX_TILES, 0, i - (N_CTX_TILES - 1))


def _silu(x):
    return x / (1.0 + jnp.exp(-x))


def _norm_mod(x, nw, shift, scale):
    y = x * lax.rsqrt(jnp.mean(x * x, axis=-1, keepdims=True) + NORM_EPS) * nw
    return y * (1.0 + scale) + shift


def _seq_masks(tile, shape):
    rows = lax.broadcasted_iota(jnp.int32, shape, 0)
    seq_m1 = jnp.where(tile < N_CTX_TILES, SEQ - 1, DEC_SEQ - 1)
    pos = rows & seq_m1
    return pos != 0, pos != seq_m1


def _dwconv_rows(u, cw, cb, not_first, not_last):
    n = u.shape[0]
    prev = jnp.where(not_first, pltpu.roll(u, 1, 0), 0.0)
    nxt = jnp.where(not_last, pltpu.roll(u, n - 1, 0), 0.0)
    return cw[0:1, :] * prev + cw[1:2, :] * u + cw[2:3, :] * nxt + cb


def _ada_kernel(c_ref, w_ref, b_ref, o_ref):
    s = _silu(c_ref[...]).astype(BF16)
    o_ref[...] = jnp.dot(s, w_ref[...].astype(BF16), preferred_element_type=F32) + b_ref[...]


def _ada_mod(cond, ada_w, ada_b):
    tn = 1536
    n = 6 * D_MODEL
    out = pl.pallas_call(
        _ada_kernel,
        grid=(DEPTH, n // tn),
        in_specs=[
            pl.BlockSpec((MOD_ROWS, D_MODEL), lambda l, j: (0, 0)),
            pl.BlockSpec((None, D_MODEL, tn), lambda l, j: (l, 0, j)),
            pl.BlockSpec((None, 1, tn), lambda l, j: (l, 0, j)),
        ],
        out_specs=pl.BlockSpec((None, MOD_ROWS, tn), lambda l, j: (l, 0, j)),
        out_shape=jax.ShapeDtypeStruct((DEPTH, MOD_ROWS, n), F32),
        compiler_params=_params("arbitrary", "arbitrary"),
        name="ada_mod",
    )(cond, ada_w, ada_b.reshape(DEPTH, 1, n))
    return out.reshape(DEPTH, MOD_ROWS, 6, D_MODEL)


def _norm_mm_kernel(x_ref, mod_ref, nw_ref, w_ref, o_ref, h_scr):
    @pl.when(pl.program_id(1) == 0)
    def _():
        h = _norm_mod(x_ref[...], nw_ref[...], mod_ref[0:1, :], mod_ref[1:2, :])
        h_scr[...] = h.astype(BF16)

    o_ref[...] = jnp.dot(h_scr[...], w_ref[...], preferred_element_type=F32).astype(o_ref.dtype)


def _norm_mm(x, mod, nw, w, tn, out_dtype):
    n = w.shape[1]
    return pl.pallas_call(
        _norm_mm_kernel,
        grid=(N_TOK // TM, n // tn),
        in_specs=[
            pl.BlockSpec((TM, D_MODEL), lambda i, j: (i, 0)),
            pl.BlockSpec((None, 6, D_MODEL), lambda i, j: (_mod_row(i), 0, 0)),
            pl.BlockSpec((1, D_MODEL), lambda i, j: (0, 0)),
            pl.BlockSpec((D_MODEL, tn), lambda i, j: (0, j)),
        ],
        out_specs=pl.BlockSpec((TM, tn), lambda i, j: (i, j)),
        out_shape=jax.ShapeDtypeStruct((N_TOK, n), out_dtype),
        scratch_shapes=[pltpu.VMEM((TM, D_MODEL), BF16)],
        compiler_params=_params("arbitrary", "arbitrary"),
        name="norm_mm",
    )(x, mod, nw.reshape(1, D_MODEL), w)


def _mm_kernel(a_ref, w_ref, b_ref, o_ref):
    acc = jnp.dot(a_ref[...], w_ref[...], preferred_element_type=F32)
    o_ref[...] = (acc + b_ref[...]).astype(o_ref.dtype)


def _mm(a, w, bias, tn, out_dtype):
    m, k = a.shape
    n = w.shape[1]
    return pl.pallas_call(
        _mm_kernel,
        grid=(m // TM, n // tn),
        in_specs=[
            pl.BlockSpec((TM, k), lambda i, j: (i, 0)),
            pl.BlockSpec((k, tn), lambda i, j: (0, j)),
            pl.BlockSpec((1, tn), lambda i, j: (0, j)),
        ],
        out_specs=pl.BlockSpec((TM, tn), lambda i, j: (i, j)),
        out_shape=jax.ShapeDtypeStruct((m, n), out_dtype),
        compiler_params=_params("arbitrary", "arbitrary"),
        name="mm",
    )(a, w, bias)


def _mm_res_kernel(a_ref, w_ref, x_ref, mod_ref, o_ref):
    acc = jnp.dot(a_ref[...], w_ref[...], preferred_element_type=F32)
    o_ref[...] = x_ref[...] + mod_ref[2:3, :] * acc


def _mm_res(a, w, x, mod):
    k = a.shape[1]
    tn = 512
    return pl.pallas_call(
        _mm_res_kernel,
        grid=(N_TOK // TM, D_MODEL // tn),
        in_specs=[
            pl.BlockSpec((TM, k), lambda i, j: (i, 0)),
            pl.BlockSpec((k, tn), lambda i, j: (0, j)),
            pl.BlockSpec((TM, tn), lambda i, j: (i, j)),
            pl.BlockSpec((None, 6, tn), lambda i, j: (_mod_row(i), 0, j)),
        ],
        out_specs=pl.BlockSpec((TM, tn), lambda i, j: (i, j)),
        out_shape=jax.ShapeDtypeStruct((N_TOK, D_MODEL), F32),
        compiler_params=_params("arbitrary", "arbitrary"),
        name="mm_res",
    )(a, w, x, mod)


FFN_TF = 256


def _ffn_kernel(x_ref, mod_ref, nw_ref, wg_ref, wv_ref, cwg_ref, cwv_ref, cbg_ref, cbv_ref, wd_ref,
                o_ref, h_scr, acc_scr):
    i = pl.program_id(0)
    j = pl.program_id(1)

    @pl.when(j == 0)
    def _():
        h = _norm_mod(x_ref[...], nw_ref[...], mod_ref[3:4, :], mod_ref[4:5, :])
        h_scr[...] = h.astype(BF16)
        acc_scr[...] = jnp.zeros_like(acc_scr)

    h = h_scr[...]
    not_first, not_last = _seq_masks(i, (TM, FFN_TF))
    ug = jnp.dot(h, wg_ref[...], preferred_element_type=F32)
    g = _dwconv_rows(ug, cwg_ref[...], cbg_ref[...], not_first, not_last)
    uv = jnp.dot(h, wv_ref[...], preferred_element_type=F32)
    v = _dwconv_rows(uv, cwv_ref[...], cbv_ref[...], not_first, not_last)
    a = (_silu(g) * v).astype(BF16)
    acc_scr[...] += jnp.dot(a, wd_ref[...], preferred_element_type=F32)

    @pl.when(j == pl.num_programs(1) - 1)
    def _():
        o_ref[...] = x_ref[...] + mod_ref[5:6, :] * acc_scr[...]


def _conv_ffn(x, mod, nw, w_up, conv_w, conv_b, w_down):
    nf = D_FF // FFN_TF
    conv_b = conv_b.reshape(1, 2 * D_FF)
    return pl.pallas_call(
        _ffn_kernel,
        grid=(N_TOK // TM, nf),
        in_specs=[
            pl.BlockSpec((TM, D_MODEL), lambda i, j: (i, 0)),
            pl.BlockSpec((None, 6, D_MODEL), lambda i, j: (_mod_row(i), 0, 0)),
            pl.BlockSpec((1, D_MODEL), lambda i, j: (0, 0)),
            pl.BlockSpec((D_MODEL, FFN_TF), lambda i, j: (0, j)),
            pl.BlockSpec((D_MODEL, FFN_TF), lambda i, j: (0, j + nf)),
            pl.BlockSpec((3, FFN_TF), lambda i, j: (0, j)),
            pl.BlockSpec((3, FFN_TF), lambda i, j: (0, j + nf)),
            pl.BlockSpec((1, FFN_TF), lambda i, j: (0, j)),
            pl.BlockSpec((1, FFN_TF), lambda i, j: (0, j + nf)),
            pl.BlockSpec((FFN_TF, D_MODEL), lambda i, j: (j, 0)),
        ],
        out_specs=pl.BlockSpec((TM, D_MODEL), lambda i, j: (i, 0)),
        out_shape=jax.ShapeDtypeStruct((N_TOK, D_MODEL), F32),
        scratch_shapes=[pltpu.VMEM((TM, D_MODEL), BF16), pltpu.VMEM((TM, D_MODEL), F32)],
        compiler_params=_params("arbitrary", "arbitrary"),
        name="conv_ffn",
    )(x, mod, nw.reshape(1, D_MODEL), w_up, w_up, conv_w, conv_w, conv_b, conv_b, w_down)


ML_TN = 512


def _ml_up_kernel(x_ref, mod_ref, nw_ref, wm_ref, wz_ref, cw_ref, cb_ref, xm_ref, xc_ref, z_ref, h_scr):
    i = pl.program_id(0)

    @pl.when(pl.program_id(1) == 0)
    def _():
        h = _norm_mod(x_ref[...], nw_ref[...], mod_ref[0:1, :], mod_ref[1:2, :])
        h_scr[...] = h.astype(BF16)

    h = h_scr[...]
    not_first, not_last = _seq_masks(i, (TM, ML_TN))
    xm = jnp.dot(h, wm_ref[...], preferred_element_type=F32)
    xm_ref[...] = xm.astype(BF16)
    xc_ref[...] = _silu(_dwconv_rows(xm, cw_ref[...], cb_ref[...], not_first, not_last)).astype(BF16)
    z_ref[...] = jnp.dot(h, wz_ref[...], preferred_element_type=F32).astype(BF16)


def _ml_up(x, mod, nw, w_up, conv_w, conv_b):
    nj = ML_D_IN // ML_TN
    col = pl.BlockSpec((TM, ML_TN), lambda i, j: (i, j))
    shp = jax.ShapeDtypeStruct((N_TOK, ML_D_IN), BF16)
    return pl.pallas_call(
        _ml_up_kernel,
        grid=(N_TOK // TM, nj),
        in_specs=[
            pl.BlockSpec((TM, D_MODEL), lambda i, j: (i, 0)),
            pl.BlockSpec((None, 6, D_MODEL), lambda i, j: (_mod_row(i), 0, 0)),
            pl.BlockSpec((1, D_MODEL), lambda i, j: (0, 0)),
            pl.BlockSpec((D_MODEL, ML_TN), lambda i, j: (0, j)),
            pl.BlockSpec((D_MODEL, ML_TN), lambda i, j: (0, j + nj)),
            pl.BlockSpec((3, ML_TN), lambda i, j: (0, j)),
            pl.BlockSpec((1, ML_TN), lambda i, j: (0, j)),
        ],
        out_specs=[col, col, col],
        out_shape=[shp, shp, shp],
        scratch_shapes=[pltpu.VMEM((TM, D_MODEL), BF16)],
        compiler_params=_params("arbitrary", "arbitrary"),
        name="ml_up",
    )(x, mod, nw.reshape(1, D_MODEL), w_up, w_up, conv_w, conv_b.reshape(1, ML_D_IN))


def _log_sigmoid(x):
    return jnp.minimum(x, 0.0) - jnp.log(1.0 + jnp.exp(-jnp.abs(x)))


def _ml_scan_kernel(*refs, seq, has_init, out_state):
    q_ref, k_ref, v_ref, g_ref, gt_ref, xc_ref, z_ref, nw_ref, skip_ref = refs[:9]
    pos = 9
    if has_init:
        c0_ref, n0_ref, m0_ref = refs[pos:pos + 3]
        pos += 3
    o_ref = refs[pos]
    pos += 1
    if out_state:
        cout_ref, nout_ref, mout_ref = refs[pos:pos + 3]
        pos += 3
    hs_scr, c_scr = refs[pos:pos + 2]

    L = ML_CHUNK
    nc = seq // L
    scale = ML_DK ** -0.5
    row_i = lax.broadcasted_iota(jnp.int32, (L, L), 0)
    col_i = lax.broadcasted_iota(jnp.int32, (L, L), 1)

    for d in (0, 1):
        keep = (col_i <= row_i) if d == 0 else (col_i >= row_i)
        keep_f = keep.astype(F32)
        keep_t_f = ((row_i <= col_i) if d == 0 else (row_i >= col_i)).astype(F32)
        end = L - 1 if d == 0 else 0

        if has_init:
            c_scr[...] = c0_ref[d]
            n_init = n0_ref[d:d + 1, :]
            m_init = m0_ref[d:d + 1, 0:1]
        else:
            c_scr[...] = jnp.zeros_like(c_scr)
            n_init = jnp.zeros((1, ML_DK), F32)
            m_init = jnp.zeros((1, 1), F32)

        def step(c, carry, d=d, keep=keep, keep_f=keep_f, keep_t_f=keep_t_f, end=end):
            n_row, m = carry
            cc = c if d == 0 else nc - 1 - c
            r0 = pl.multiple_of(cc * L, L)
            qc = q_ref[pl.ds(r0, L), :]
            kc = k_ref[pl.ds(r0, L), :]
            vc = v_ref[pl.ds(r0, L), :]
            gcol = g_ref[cc]
            grow = gt_ref[cc]
            ig_col = gcol[:, 2 * d:2 * d + 1]
            ig_row = grow[2 * d:2 * d + 1, :]
            lf_col = _log_sigmoid(gcol[:, 2 * d + 1:2 * d + 2])
            lf_row = _log_sigmoid(grow[2 * d + 1:2 * d + 2, :])
            b_col = jnp.sum(keep_f * lf_row, axis=1, keepdims=True)
            b_row = jnp.sum(keep_t_f * lf_col, axis=0, keepdims=True)
            a_col = b_col + m
            dmat = jnp.where(keep, b_col - b_row + ig_row, NEG_INF)
            m_t = jnp.maximum(a_col, jnp.max(dmat, axis=1, keepdims=True))
            w_inter = jnp.exp(a_col - m_t)
            qk = lax.dot_general(qc, kc, (((1,), (1,)), ((), ())), preferred_element_type=F32)
            s = qk * scale * jnp.exp(dmat - m_t)
            inter = jnp.dot(qc, c_scr[...].astype(BF16), preferred_element_type=F32) * scale
            num = w_inter * inter + jnp.dot(s.astype(BF16), vc, preferred_element_type=F32)
            qn = jnp.sum(qc.astype(F32) * n_row, axis=1, keepdims=True) * scale
            den = w_inter * qn + jnp.sum(s, axis=1, keepdims=True)
            h = num / jnp.maximum(jnp.abs(den), jnp.exp(-m_t))

            b_end = b_col[end:end + 1, :]
            m_new = m_t[end:end + 1, :]
            w_c = jnp.exp(b_end + m - m_new)
            w_s = jnp.exp(b_end - b_col + ig_col - m_new)
            wk = w_s * kc.astype(F32)
            upd = lax.dot_general(wk.astype(BF16), vc, (((0,), (0,)), ((), ())), preferred_element_type=F32)
            c_scr[...] = w_c * c_scr[...] + upd
            n_new = w_c * n_row + jnp.sum(wk, axis=0, keepdims=True)

            if d == 0:
                hs_scr[pl.ds(r0, L), :] = h
            else:
                hs = hs_scr[pl.ds(r0, L), :] + h
                mu = jnp.mean(hs, axis=1, keepdims=True)
                cen = hs - mu
                var = jnp.mean(cen * cen, axis=1, keepdims=True)
                hn = cen * lax.rsqrt(var + NORM_EPS) * nw_ref[...]
                xc = xc_ref[pl.ds(r0, L), :].astype(F32)
                z = z_ref[pl.ds(r0, L), :].astype(F32)
                o_ref[pl.ds(r0, L), :] = ((hn + skip_ref[...] * xc) * _silu(z)).astype(o_ref.dtype)
            return n_new, m_new

        n_fin, m_fin = lax.fori_loop(0, nc, step, (n_init, m_init))
        if out_state:
            cout_ref[d] = c_scr[...]
            nout_ref[d:d + 1, :] = n_fin
            mout_ref[d:d + 1, :] = jnp.broadcast_to(m_fin, (1, 128))


def _ml_scan(qk, v, g4, g4t, xc, z, norm_w, skip, *, seq, nb, row0, init=None):
    L = ML_CHUNK
    nc = seq // L
    rb = row0 // seq
    has_init = init is not None
    in_specs = [
        pl.BlockSpec((seq, ML_DK), lambda b, h: (rb + b, h)),
        pl.BlockSpec((seq, ML_DK), lambda b, h: (rb + b, ML_HEADS + h)),
        pl.BlockSpec((seq, ML_DV), lambda b, h: (rb + b, h)),
        pl.BlockSpec((None, nc, L, 4), lambda b, h: (h, rb + b, 0, 0)),
        pl.BlockSpec((None, nc, 4, L), lambda b, h: (h, rb + b, 0, 0)),
        pl.BlockSpec((seq, ML_DV), lambda b, h: (rb + b, h)),
        pl.BlockSpec((seq, ML_DV), lambda b, h: (rb + b, h)),
        pl.BlockSpec((1, ML_DV), lambda b, h: (0, h)),
        pl.BlockSpec((1, ML_DV), lambda b, h: (0, h)),
    ]
    args = [qk, qk, v, g4, g4t, xc, z, norm_w.reshape(1, ML_D_IN), skip.reshape(1, ML_D_IN)]
    if has_init:
        c0, layer, n0, m0 = init
        in_specs += [
            pl.BlockSpec((None, None, 2, None, ML_DK, ML_DV), lambda b, h: (b, layer, 0, h, 0, 0)),
            pl.BlockSpec((None, None, 2, ML_DK), lambda b, h: (b, h, 0, 0)),
            pl.BlockSpec((None, None, 2, 128), lambda b, h: (b, h, 0, 0)),
        ]
        args += [c0, n0, m0]
    out_specs = [pl.BlockSpec((seq, ML_DV), lambda b, h: (b, h))]
    out_shape = [jax.ShapeDtypeStruct((nb * seq, ML_D_IN), BF16)]
    if not has_init:
        out_specs += [
            pl.BlockSpec((None, 2, None, ML_DK, ML_DV), lambda b, h: (b, 0, h, 0, 0)),
            pl.BlockSpec((None, None, 2, ML_DK), lambda b, h: (b, h, 0, 0)),
            pl.BlockSpec((None, None, 2, 128), lambda b, h: (b, h, 0, 0)),
        ]
        out_shape += [
            jax.ShapeDtypeStruct((nb, 2, ML_HEADS, ML_DK, ML_DV), F32),
            jax.ShapeDtypeStruct((nb, ML_HEADS, 2, ML_DK), F32),
            jax.ShapeDtypeStruct((nb, ML_HEADS, 2, 128), F32),
        ]
    return pl.pallas_call(
        functools.partial(_ml_scan_kernel, seq=seq, has_init=has_init, out_state=not has_init),
        grid=(nb, ML_HEADS),
        in_specs=in_specs,
        out_specs=out_specs,
        out_shape=out_shape,
        scratch_shapes=[pltpu.VMEM((seq, ML_DV), F32), pltpu.VMEM((ML_DK, ML_DV), F32)],
        compiler_params=_params("arbitrary", "arbitrary"),
        name="ml_scan_init" if has_init else "ml_scan_zero",
    )(*args)


def _mlstm_layer(x, mod, nw, p, state):
    xm, xc, z = _ml_up(x, mod, nw, p["w_up"], p["conv_w"], p["conv_b"])
    zero_b = jnp.zeros((1, ML_D_IN), F32)
    qk = _mm(xc, p["w_qk"], zero_b, 1024, BF16)
    v = _mm(xm, p["w_v"], zero_b, 1024, BF16)
    g = _mm(xm, p["w_gate"], p["b_gate"], 128, F32)[:, :4 * ML_HEADS]
    L = ML_CHUNK
    g4 = g.reshape(N_TOK // L, L, 4, ML_HEADS).transpose(3, 0, 1, 2)
    g4t = g4.transpose(0, 1, 3, 2)
    oc, c_fin, n_fin, m_fin = _ml_scan(qk, v, g4, g4t, xc, z, p["norm_w"], p["skip"],
                                       seq=SEQ, nb=BATCH, row0=0)
    (ol,) = _ml_scan(qk, v, g4, g4t, xc, z, p["norm_w"], p["skip"],
                     seq=DEC_SEQ, nb=DEC_BATCH, row0=N_CTX, init=state)
    o = jnp.concatenate([oc, ol], axis=0)
    x = _mm_res(o, p["w_down"], x, mod)
    return x, c_fin, n_fin.transpose(0, 2, 1, 3), m_fin[..., 0].transpose(0, 2, 1)


def _softmax_parts(scores, sink_col):
    m = functools.reduce(jnp.maximum, [jnp.max(s, axis=1, keepdims=True) for s in scores])
    if sink_col is not None:
        m = jnp.maximum(m, sink_col)
    ps = [jnp.exp(s - m) for s in scores]
    den = functools.reduce(jnp.add, [jnp.sum(p, axis=1, keepdims=True) for p in ps])
    if sink_col is not None:
        den = den + jnp.exp(sink_col - m)
    return ps, den


def _sink_col(sink_ref, kv, groups, rows_per_group):
    rows = lax.broadcasted_iota(jnp.int32, (groups * rows_per_group, 1), 0)
    col = jnp.full((groups * rows_per_group, 1), sink_ref[kv * groups], F32)
    for gi in range(1, groups):
        col = jnp.where(rows >= gi * rows_per_group, sink_ref[kv * groups + gi], col)
    return col


def _qk(q, k):
    return lax.dot_general(q, k, (((1,), (1,)), ((), ())), preferred_element_type=F32) * (HEAD_DIM ** -0.5)


def _ctx_attn_kernel(*refs, groups, has_sink):
    if has_sink:
        sink_ref, q_ref, k_ref, v_ref, o_ref = refs
    else:
        q_ref, k_ref, v_ref, o_ref = refs
    q = q_ref[...].reshape(groups * SEQ, HEAD_DIM)
    sink = _sink_col(sink_ref, pl.program_id(1), groups, SEQ) if has_sink else None
    (p,), den = _softmax_parts([_qk(q, k_ref[...])], sink)
    o = jnp.dot(p.astype(BF16), v_ref[...], preferred_element_type=F32) / den
    o_ref[...] = o.reshape(groups, SEQ, HEAD_DIM).astype(o_ref.dtype)


def _ctx_attn(q, k, v, sink):
    b, kvh, groups = q.shape[:3]
    has_sink = sink is not None
    qspec = pl.BlockSpec((None, None, groups, SEQ, HEAD_DIM), lambda i, j: (i, j, 0, 0, 0))
    kspec = pl.BlockSpec((None, None, SEQ, HEAD_DIM), lambda i, j: (i, j, 0, 0))
    in_specs = [qspec, kspec, kspec]
    args = [q, k, v]
    if has_sink:
        in_specs = [pl.BlockSpec(memory_space=pltpu.SMEM)] + in_specs
        args = [sink.reshape(-1)] + args
    return pl.pallas_call(
        functools.partial(_ctx_attn_kernel, groups=groups, has_sink=has_sink),
        grid=(b, kvh),
        in_specs=in_specs,
        out_specs=qspec,
        out_shape=jax.ShapeDtypeStruct(q.shape, BF16),
        compiler_params=_params("arbitrary", "arbitrary"),
        name="ctx_attn",
    )(*args)


SWA_SPAN = Q_BLOCK + 2 * SWA_WINDOW


def _swa_lat_kernel(sink_ref, q_ref, k_ref, v_ref, kc_ref, vc_ref, o_ref, *, groups):
    kv = pl.program_id(1)
    j = pl.program_id(2)
    rows = groups * Q_BLOCK
    q = q_ref[...].reshape(rows, HEAD_DIM)
    start = pl.multiple_of(jnp.clip((j - 1) * Q_BLOCK, 0, DEC_SEQ - SWA_SPAN), Q_BLOCK)
    k_loc = k_ref[pl.ds(start, SWA_SPAN), :]
    v_loc = v_ref[pl.ds(start, SWA_SPAN), :]
    qpos = j * Q_BLOCK + (lax.broadcasted_iota(jnp.int32, (rows, SWA_SPAN), 0) & (Q_BLOCK - 1))
    kpos = start + lax.broadcasted_iota(jnp.int32, (rows, SWA_SPAN), 1)
    s_loc = jnp.where(jnp.abs(qpos - kpos) <= SWA_WINDOW, _qk(q, k_loc), NEG_INF)
    s_ctx = _qk(q, kc_ref[...])
    (p_loc, p_ctx), den = _softmax_parts([s_loc, s_ctx], _sink_col(sink_ref, kv, groups, Q_BLOCK))
    o = (jnp.dot(p_loc.astype(BF16), v_loc, preferred_element_type=F32)
         + jnp.dot(p_ctx.astype(BF16), vc_ref[...], preferred_element_type=F32)) / den
    o_ref[...] = o.reshape(groups, Q_BLOCK, HEAD_DIM).astype(o_ref.dtype)


def _swa_latent(q, k, v, kc, vc, sink):
    b, kvh, groups = q.shape[:3]
    kspec = pl.BlockSpec((None, None, DEC_SEQ, HEAD_DIM), lambda i, h, j: (i, h, 0, 0))
    cspec = pl.BlockSpec((None, None, SEQ, HEAD_DIM), lambda i, h, j: (i, h, 0, 0))
    qspec = pl.BlockSpec((None, None, groups, Q_BLOCK, HEAD_DIM), lambda i, h, j: (i, h, 0, j, 0))
    return pl.pallas_call(
        functools.partial(_swa_lat_kernel, groups=groups),
        grid=(b, kvh, DEC_SEQ // Q_BLOCK),
        in_specs=[pl.BlockSpec(memory_space=pltpu.SMEM), qspec, kspec, kspec, cspec, cspec],
        out_specs=qspec,
        out_shape=jax.ShapeDtypeStruct(q.shape, BF16),
        compiler_params=_params("arbitrary", "arbitrary", "arbitrary"),
        name="swa_latent",
    )(sink.reshape(-1), q, k, v, kc, vc)


NA_QT = 256
NA_SPAN = 768


def _na_start(j):
    return (j // 2) * (DEC_SEQ - NA_SPAN)


def _na_lat_kernel(q_ref, k_ref, v_ref, kc_ref, vc_ref, bias_ref, o_ref):
    j = pl.program_id(1)
    q = q_ref[...]
    start = pl.multiple_of(_na_start(j), 256)
    k_loc = k_ref[pl.ds(start, NA_SPAN), :]
    v_loc = v_ref[pl.ds(start, NA_SPAN), :]
    s_loc = _qk(q, k_loc) + bias_ref[...]
    s_ctx = _qk(q, kc_ref[...])
    (p_loc, p_ctx), den = _softmax_parts([s_loc, s_ctx], None)
    o = (jnp.dot(p_loc.astype(BF16), v_loc, preferred_element_type=F32)
         + jnp.dot(p_ctx.astype(BF16), vc_ref[...], preferred_element_type=F32)) / den
    o_ref[...] = o.astype(o_ref.dtype)


def _na_bias(rpb):
    rows = DEC_SEQ // GRID_W
    pos = np.arange(DEC_SEQ)
    r, c = pos // GRID_W, pos % GRID_W
    rs = np.clip(r - NA_KH // 2, 0, rows - NA_KH)
    cs = np.clip(c - NA_KW // 2, 0, GRID_W - NA_KW)
    valid = ((r[None, :] >= rs[:, None]) & (r[None, :] < rs[:, None] + NA_KH)
             & (c[None, :] >= cs[:, None]) & (c[None, :] < cs[:, None] + NA_KW))
    dr = np.clip(r[None, :] - r[:, None] + NA_KH - 1, 0, 2 * NA_KH - 2)
    dc = np.clip(c[None, :] - c[:, None], -(NA_KW - 1), NA_KW - 1) + NA_KW - 1
    tiles = []
    for t in range(DEC_SEQ // NA_QT):
        qs = slice(t * NA_QT, (t + 1) * NA_QT)
        ks = slice((t // 2) * (DEC_SEQ - NA_SPAN), (t // 2) * (DEC_SEQ - NA_SPAN) + NA_SPAN)
        assert not valid[qs, :ks.start].any() and not valid[qs, ks.stop:].any()
        tiles.append(jnp.where(valid[qs, ks][None], rpb[:, dr[qs, ks], dc[qs, ks]], NEG_INF))
    return jnp.stack(tiles, axis=1).astype(F32)


def _na_latent(q, k, v, kc, vc, bias):
    b, h = q.shape[:2]
    kspec = pl.BlockSpec((None, None, DEC_SEQ, HEAD_DIM), lambda hh, j, i: (i, hh, 0, 0))
    cspec = pl.BlockSpec((None, None, SEQ, HEAD_DIM), lambda hh, j, i: (i, hh, 0, 0))
    qspec = pl.BlockSpec((None, None, NA_QT, HEAD_DIM), lambda hh, j, i: (i, hh, j, 0))
    return pl.pallas_call(
        _na_lat_kernel,
        grid=(h, DEC_SEQ // NA_QT, b),
        in_specs=[qspec, kspec, kspec, cspec, cspec,
                  pl.BlockSpec((None, None, NA_QT, NA_SPAN), lambda hh, j, i: (hh, j, 0, 0))],
        out_specs=qspec,
        out_shape=jax.ShapeDtypeStruct(q.shape, BF16),
        compiler_params=_params("arbitrary", "arbitrary", "arbitrary"),
        name="na_latent",
    )(q, k, v, kc, vc, bias)


ROPE_COLS = (SWA_HEADS + SWA_KV) * HEAD_DIM


def _rope_kernel(x_ref, cos_ref, sin_ref, o_ref):
    x = x_ref[...]
    lane = lax.broadcasted_iota(jnp.int32, x.shape, 1)
    first = (lane & (HEAD_DIM // 4)) == 0
    partner = jnp.where(first, pltpu.roll(x, 128 - HEAD_DIM // 4, 1), pltpu.roll(x, HEAD_DIM // 4, 1))
    o_ref[...] = x * cos_ref[...] + partner * sin_ref[...]


def _rope_tables():
    quarter = HEAD_DIM // 4
    pos = np.arange(DEC_SEQ)
    inv = np.power(ROPE_BASE, -np.arange(quarter, dtype=np.float32) / quarter).astype(np.float32)
    d = np.arange(128) % HEAD_DIM
    p = np.where((d < HEAD_DIM // 2)[None, :], (pos // GRID_W)[:, None], (pos % GRID_W)[:, None]).astype(np.float32)
    ang = p * inv[d % quarter][None, :]
    sign = np.where((d // quarter) % 2 == 0, -1.0, 1.0)[None, :]
    return jnp.asarray(np.cos(ang), F32), jnp.asarray(np.sin(ang) * sign, F32)


def _rope_latent(qkv):
    cos, sin = _rope_tables()
    rb = N_CTX // DEC_SEQ
    tab = pl.BlockSpec((DEC_SEQ, 128), lambda i, j: (0, 0))
    return pl.pallas_call(
        _rope_kernel,
        grid=(DEC_BATCH, ROPE_COLS // 128),
        in_specs=[pl.BlockSpec((DEC_SEQ, 128), lambda i, j: (rb + i, j)), tab, tab],
        out_specs=pl.BlockSpec((DEC_SEQ, 128), lambda i, j: (i, j)),
        out_shape=jax.ShapeDtypeStruct((N_LAT, ROPE_COLS), F32),
        compiler_params=_params("arbitrary", "arbitrary"),
        name="rope",
    )(qkv, cos, sin)


def _split_heads(a, batch, seq, kvh, groups):
    a = a.reshape(batch, seq, kvh, groups, HEAD_DIM).transpose(0, 2, 3, 1, 4)
    return a.astype(BF16)


def _merge_heads(o):
    b, kvh, groups, seq, hd = o.shape
    return o.transpose(0, 3, 1, 2, 4).reshape(b * seq, kvh * groups * hd)


def _swa_layer(x, mod, nw, w_qkv, sink, w_o, cache_k, cache_v):
    qkv = _norm_mm(x, mod, nw, w_qkv, 512, F32)
    nq, nk = SWA_HEADS * HEAD_DIM, SWA_KV * HEAD_DIM
    g = SWA_HEADS // SWA_KV
    qc, kc, vc = qkv[:N_CTX, :nq], qkv[:N_CTX, nq:nq + nk], qkv[:N_CTX, nq + nk:]
    k_new = kc.reshape(BATCH, SEQ, SWA_KV, HEAD_DIM).transpose(0, 2, 1, 3)
    v_new = vc.reshape(BATCH, SEQ, SWA_KV, HEAD_DIM).transpose(0, 2, 1, 3)
    oc = _ctx_attn(_split_heads(qc, BATCH, SEQ, SWA_KV, g), k_new.astype(BF16), v_new.astype(BF16), sink)
    roped = _rope_latent(qkv)
    ql = _split_heads(roped[:, :nq], DEC_BATCH, DEC_SEQ, SWA_KV, g)
    kl = _split_heads(roped[:, nq:], DEC_BATCH, DEC_SEQ, SWA_KV, 1)[:, :, 0]
    vl = _split_heads(qkv[N_CTX:, nq + nk:], DEC_BATCH, DEC_SEQ, SWA_KV, 1)[:, :, 0]
    ol = _swa_latent(ql, kl, vl, cache_k.astype(BF16), cache_v.astype(BF16), sink)
    o = jnp.concatenate([_merge_heads(oc), _merge_heads(ol)], axis=0)
    return _mm_res(o, w_o, x, mod), k_new, v_new


def _na_layer(x, mod, nw, w_qkv, rpb, w_o, cache_k, cache_v):
    qkv = _norm_mm(x, mod, nw, w_qkv, 1024, F32)
    n = NA_HEADS * HEAD_DIM
    qc, kc, vc = qkv[:N_CTX, :n], qkv[:N_CTX, n:2 * n], qkv[:N_CTX, 2 * n:]
    k_new = kc.reshape(BATCH, SEQ, NA_HEADS, HEAD_DIM).transpose(0, 2, 1, 3)
    v_new = vc.reshape(BATCH, SEQ, NA_HEADS, HEAD_DIM).transpose(0, 2, 1, 3)
    oc = _ctx_attn(_split_heads(qc, BATCH, SEQ, NA_HEADS, 1), k_new.astype(BF16), v_new.astype(BF16), None)
    ql, kl, vl = (_split_heads(qkv[N_CTX:, t * n:(t + 1) * n], DEC_BATCH, DEC_SEQ, NA_HEADS, 1)[:, :, 0]
                  for t in range(3))
    ol = _na_latent(ql, kl, vl, cache_k.astype(BF16), cache_v.astype(BF16), _na_bias(rpb))
    o = jnp.concatenate([_merge_heads(oc), _merge_heads(ol[:, :, None])], axis=0)
    return _mm_res(o, w_o, x, mod), k_new, v_new


def _final_norm_kernel(x_ref, w_ref, o_ref):
    x = x_ref[...]
    o_ref[...] = x * lax.rsqrt(jnp.mean(x * x, axis=-1, keepdims=True) + NORM_EPS) * w_ref[...]


def _final_norm(x, w):
    return pl.pallas_call(
        _final_norm_kernel,
        grid=(N_TOK // TM,),
        in_specs=[pl.BlockSpec((TM, D_MODEL), lambda i: (i, 0)), pl.BlockSpec((1, D_MODEL), lambda i: (0, 0))],
        out_specs=pl.BlockSpec((TM, D_MODEL), lambda i: (i, 0)),
        out_shape=jax.ShapeDtypeStruct((N_TOK, D_MODEL), F32),
        compiler_params=_params("arbitrary"),
        name="final_norm",
    )(x, w.reshape(1, D_MODEL))


def kernel(x_prompt, x_sample, state_mlstm_C, state_mlstm_n, state_mlstm_m, cache_swa_k, cache_swa_v, cache_na_k, cache_na_v, c, c_ctx, ada_w, ada_b, norm_w, final_norm_w, ffn_w_up, ffn_conv_w, ffn_conv_b, ffn_w_down, ml_w_up, ml_conv_w, ml_conv_b, ml_w_qk, ml_w_v, ml_w_gate, ml_b_gate, ml_norm_w, ml_skip, ml_w_down, swa_w_qkv, swa_sink, swa_w_o, na_w_qkv, na_rpb, na_w_o):
    x = jnp.concatenate([x_prompt.reshape(N_CTX, D_MODEL), x_sample.reshape(N_LAT, D_MODEL)], axis=0)
    cond = jnp.concatenate([c_ctx[None], c, jnp.zeros((MOD_ROWS - 1 - DEC_BATCH, D_MODEL), F32)], axis=0)
    mods = _ada_mod(cond, ada_w, ada_b)

    new_c, new_n, new_m = [], [], []
    new_sk = new_sv = new_nk = new_nv = None
    for i in range(DEPTH):
        kind, j = i % N_MIXERS, i // N_MIXERS
        mod = mods[i]
        if kind == 0:
            gate_w = jnp.pad(ml_w_gate[j], ((0, 0), (0, 128 - 4 * ML_HEADS))).astype(BF16)
            gate_b = jnp.pad(ml_b_gate[j], (0, 128 - 4 * ML_HEADS)).reshape(1, 128)
            p = dict(w_up=ml_w_up[j].astype(BF16), conv_w=ml_conv_w[j], conv_b=ml_conv_b[j],
                     w_qk=ml_w_qk[j].astype(BF16), w_v=ml_w_v[j].astype(BF16), w_gate=gate_w, b_gate=gate_b,
                     norm_w=ml_norm_w[j], skip=ml_skip[j], w_down=ml_w_down[j].astype(BF16))
            n0 = state_mlstm_n[:, j].transpose(0, 2, 1, 3)
            m0 = jnp.broadcast_to(state_mlstm_m[:, j].transpose(0, 2, 1)[..., None], (DEC_BATCH, ML_HEADS, 2, 128))
            x, cf, nf, mf = _mlstm_layer(x, mod, norm_w[i, 0], p, (state_mlstm_C, j, n0, m0))
            new_c.append(cf)
            new_n.append(nf)
            new_m.append(mf)
        elif kind == 1:
            x, k_new, v_new = _swa_layer(x, mod, norm_w[i, 0], swa_w_qkv[j].astype(BF16), swa_sink[j],
                                         swa_w_o[j].astype(BF16), cache_swa_k[:, j], cache_swa_v[:, j])
            new_sk, new_sv = k_new[:, None], v_new[:, None]
        else:
            x, k_new, v_new = _na_layer(x, mod, norm_w[i, 0], na_w_qkv[j].astype(BF16), na_rpb[j],
                                        na_w_o[j].astype(BF16), cache_na_k[:, j], cache_na_v[:, j])
            new_nk, new_nv = k_new[:, None], v_new[:, None]
        x = _conv_ffn(x, mod, norm_w[i, 1], ffn_w_up[i].astype(BF16), ffn_conv_w[i], ffn_conv_b[i],
                      ffn_w_down[i].astype(BF16))

    y = _final_norm(x, final_norm_w)
    return (y[:N_CTX].reshape(BATCH, SEQ, D_MODEL), y[N_CTX:].reshape(DEC_BATCH, DEC_SEQ, D_MODEL),
            jnp.stack(new_c, axis=1), jnp.stack(new_n, axis=1), jnp.stack(new_m, axis=1),
            new_sk, new_sv, new_nk, new_nv)
```

```python
import functools

import jax
import jax.numpy as jnp
import numpy as np
from jax import lax
from jax.experimental import pallas as pl
from jax.experimental.pallas import tpu as pltpu

F32 = jnp.float32
BF16 = jnp.bfloat16

D_MODEL = 1024
BATCH = 32
SEQ = 256
DEPTH = 4
DEC_BATCH = 8
DEC_SEQ = 1024
GRID_W = 64
N_MIXERS = 3
NORM_EPS = 1e-6
D_FF = 2816
ML_D_IN = 2 * D_MODEL
ML_HEADS = 4
ML_DK = ML_D_IN // (2 * ML_HEADS)
ML_DV = ML_D_IN // ML_HEADS
ML_CHUNK = 128
HEAD_DIM = 64
SWA_HEADS = D_MODEL // HEAD_DIM
SWA_KV = SWA_HEADS // 4
SWA_WINDOW = 128
Q_BLOCK = 128
ROPE_BASE = 10000.0
NA_HEADS = D_MODEL // HEAD_DIM
NA_KH = 8
NA_KW = 16
NEG_INF = -1e30

N_CTX = BATCH * SEQ
N_LAT = DEC_BATCH * DEC_SEQ
N_TOK = N_CTX + N_LAT
TM = 1024
N_CTX_TILES = N_CTX // TM
MOD_ROWS = 16
VMEM_LIMIT_BYTES = 56 * 1024 * 1024


def _params(*sem):
    return pltpu.CompilerParams(dimension_semantics=sem, vmem_limit_bytes=VMEM_LIMIT_BYTES)


def _mod_row(i):
    return jnp.where(i < N_CTX_TILES, 0, i - (N_CTX_TILES - 1))


def _silu(x):
    return x / (1.0 + jnp.exp(-x))


def _norm_mod(x, nw, shift, scale):
    y = x * lax.rsqrt(jnp.mean(x * x, axis=-1, keepdims=True) + NORM_EPS) * nw
    return y * (1.0 + scale) + shift


def _seq_masks(tile, shape):
    rows = lax.broadcasted_iota(jnp.int32, shape, 0)
    seq_m1 = jnp.where(tile < N_CTX_TILES, SEQ - 1, DEC_SEQ - 1)
    pos = rows & seq_m1
    return pos != 0, pos != seq_m1


def _dwconv_rows(u, cw, cb, not_first, not_last):
    n = u.shape[0]
    prev = jnp.where(not_first, pltpu.roll(u, 1, 0), 0.0)
    nxt = jnp.where(not_last, pltpu.roll(u, n - 1, 0), 0.0)
    return cw[0:1, :] * prev + cw[1:2, :] * u + cw[2:3, :] * nxt + cb


def _ada_kernel(c_ref, w_ref, b_ref, o_ref):
    s = _silu(c_ref[...]).astype(BF16)
    o_ref[...] = jnp.dot(s, w_ref[...].astype(BF16), preferred_element_type=F32) + b_ref[...]


def _ada_mod(cond, ada_w, ada_b):
    tn = 1536
    n = 6 * D_MODEL
    out = pl.pallas_call(
        _ada_kernel,
        grid=(DEPTH, n // tn),
        in_specs=[
            pl.BlockSpec((MOD_ROWS, D_MODEL), lambda l, j: (0, 0)),
            pl.BlockSpec((None, D_MODEL, tn), lambda l, j: (l, 0, j)),
            pl.BlockSpec((None, 1, tn), lambda l, j: (l, 0, j)),
        ],
        out_specs=pl.BlockSpec((None, MOD_ROWS, tn), lambda l, j: (l, 0, j)),
        out_shape=jax.ShapeDtypeStruct((DEPTH, MOD_ROWS, n), F32),
        compiler_params=_params("arbitrary", "arbitrary"),
        name="ada_mod",
    )(cond, ada_w, ada_b.reshape(DEPTH, 1, n))
    return out.reshape(DEPTH, MOD_ROWS, 6, D_MODEL)


QKV_TN = 512


def _rotate_pairs(a, cos, sin):
    lane = lax.broadcasted_iota(jnp.int32, a.shape, 1)
    first = (lane & (HEAD_DIM // 4)) == 0
    n = a.shape[1]
    partner = jnp.where(first, pltpu.roll(a, n - HEAD_DIM // 4, 1), pltpu.roll(a, HEAD_DIM // 4, 1))
    return a * cos + partner * sin


def _qkv_kernel(*refs, q_blocks, rope_blocks):
    if rope_blocks:
        x_ref, mod_ref, nw_ref, w_ref, cos_ref, sin_ref, q_ref, kv_ref, h_scr = refs
    else:
        x_ref, mod_ref, nw_ref, w_ref, q_ref, kv_ref, h_scr = refs
    i = pl.program_id(0)
    j = pl.program_id(1)

    @pl.when(j == 0)
    def _():
        h = _norm_mod(x_ref[...], nw_ref[...], mod_ref[0:1, :], mod_ref[1:2, :])
        h_scr[...] = h.astype(BF16)

    acc = jnp.dot(h_scr[...], w_ref[...], preferred_element_type=F32)

    def emit(val):
        @pl.when(j < q_blocks)
        def _():
            q_ref[...] = val.astype(BF16)

        @pl.when(j >= q_blocks)
        def _():
            kv_ref[...] = val

    if rope_blocks:
        rotate = jnp.logical_and(i >= N_CTX_TILES, j < rope_blocks)

        @pl.when(rotate)
        def _():
            emit(_rotate_pairs(acc, cos_ref[...], sin_ref[...]))

        @pl.when(jnp.logical_not(rotate))
        def _():
            emit(acc)
    else:
        emit(acc)


def _qkv_proj(x, mod, nw, w, n_q, rope_cols=0):
    n = w.shape[1]
    q_blocks = n_q // QKV_TN
    rope_blocks = rope_cols // QKV_TN
    in_specs = [
        pl.BlockSpec((TM, D_MODEL), lambda i, j: (i, 0)),
        pl.BlockSpec((None, 6, D_MODEL), lambda i, j: (_mod_row(i), 0, 0)),
        pl.BlockSpec((1, D_MODEL), lambda i, j: (0, 0)),
        pl.BlockSpec((D_MODEL, QKV_TN), lambda i, j: (0, j)),
    ]
    args = [x, mod, nw.reshape(1, D_MODEL), w]
    if rope_blocks:
        tab = pl.BlockSpec((DEC_SEQ, QKV_TN), lambda i, j: (0, 0))
        in_specs += [tab, tab]
        args += list(_rope_tables(QKV_TN))
    return pl.pallas_call(
        functools.partial(_qkv_kernel, q_blocks=q_blocks, rope_blocks=rope_blocks),
        grid=(N_TOK // TM, n // QKV_TN),
        in_specs=in_specs,
        out_specs=[pl.BlockSpec((TM, QKV_TN), lambda i, j: (i, jnp.minimum(j, q_blocks - 1))),
                   pl.BlockSpec((TM, QKV_TN), lambda i, j: (i, jnp.maximum(j - q_blocks, 0)))],
        out_shape=[jax.ShapeDtypeStruct((N_TOK, n_q), BF16), jax.ShapeDtypeStruct((N_TOK, n - n_q), F32)],
        scratch_shapes=[pltpu.VMEM((TM, D_MODEL), BF16)],
        compiler_params=_params("arbitrary", "arbitrary"),
        name="qkv_proj",
    )(*args)


def _mm_kernel(a_ref, w_ref, b_ref, o_ref):
    acc = jnp.dot(a_ref[...], w_ref[...], preferred_element_type=F32)
    o_ref[...] = (acc + b_ref[...]).astype(o_ref.dtype)


def _mm(a, w, bias, tn, out_dtype):
    m, k = a.shape
    n = w.shape[1]
    return pl.pallas_call(
        _mm_kernel,
        grid=(m // TM, n // tn),
        in_specs=[
            pl.BlockSpec((TM, k), lambda i, j: (i, 0)),
            pl.BlockSpec((k, tn), lambda i, j: (0, j)),
            pl.BlockSpec((1, tn), lambda i, j: (0, j)),
        ],
        out_specs=pl.BlockSpec((TM, tn), lambda i, j: (i, j)),
        out_shape=jax.ShapeDtypeStruct((m, n), out_dtype),
        compiler_params=_params("arbitrary", "arbitrary"),
        name="mm",
    )(a, w, bias)


def _mm_res_kernel(ac_ref, al_ref, w_ref, x_ref, mod_ref, o_ref):
    def emit(a_ref):
        acc = jnp.dot(a_ref[...], w_ref[...], preferred_element_type=F32)
        o_ref[...] = x_ref[...] + mod_ref[2:3, :] * acc

    is_ctx = pl.program_id(0) < N_CTX_TILES
    pl.when(is_ctx)(functools.partial(emit, ac_ref))
    pl.when(jnp.logical_not(is_ctx))(functools.partial(emit, al_ref))


def _mm_res(a_ctx, a_lat, w, x, mod):
    k = a_ctx.shape[1]
    return pl.pallas_call(
        _mm_res_kernel,
        grid=(N_TOK // TM,),
        in_specs=[
            pl.BlockSpec((TM, k), lambda i: (jnp.minimum(i, N_CTX_TILES - 1), 0)),
            pl.BlockSpec((TM, k), lambda i: (jnp.maximum(i - N_CTX_TILES, 0), 0)),
            pl.BlockSpec((k, D_MODEL), lambda i: (0, 0)),
            pl.BlockSpec((TM, D_MODEL), lambda i: (i, 0)),
            pl.BlockSpec((None, 6, D_MODEL), lambda i: (_mod_row(i), 0, 0)),
        ],
        out_specs=pl.BlockSpec((TM, D_MODEL), lambda i: (i, 0)),
        out_shape=jax.ShapeDtypeStruct((N_TOK, D_MODEL), F32),
        compiler_params=_params("arbitrary"),
        name="mm_res",
    )(a_ctx, a_lat, w, x, mod)


FFN_TF = 256


def _ffn_kernel(x_ref, mod_ref, nw_ref, wg_ref, wv_ref, cwg_ref, cwv_ref, cbg_ref, cbv_ref, wd_ref,
                o_ref, h_scr, acc_scr):
    i = pl.program_id(0)
    j = pl.program_id(1)

    @pl.when(j == 0)
    def _():
        h = _norm_mod(x_ref[...], nw_ref[...], mod_ref[3:4, :], mod_ref[4:5, :])
        h_scr[...] = h.astype(BF16)
        acc_scr[...] = jnp.zeros_like(acc_scr)

    h = h_scr[...]
    not_first, not_last = _seq_masks(i, (TM, FFN_TF))
    ug = jnp.dot(h, wg_ref[...], preferred_element_type=F32)
    g = _dwconv_rows(ug, cwg_ref[...], cbg_ref[...], not_first, not_last)
    uv = jnp.dot(h, wv_ref[...], preferred_element_type=F32)
    v = _dwconv_rows(uv, cwv_ref[...], cbv_ref[...], not_first, not_last)
    a = (_silu(g) * v).astype(BF16)
    acc_scr[...] += jnp.dot(a, wd_ref[...], preferred_element_type=F32)

    @pl.when(j == pl.num_programs(1) - 1)
    def _():
        o_ref[...] = x_ref[...] + mod_ref[5:6, :] * acc_scr[...]


def _conv_ffn(x, mod, nw, w_up, conv_w, conv_b, w_down):
    nf = D_FF // FFN_TF
    conv_b = conv_b.reshape(1, 2 * D_FF)
    return pl.pallas_call(
        _ffn_kernel,
        grid=(N_TOK // TM, nf),
        in_specs=[
            pl.BlockSpec((TM, D_MODEL), lambda i, j: (i, 0)),
            pl.BlockSpec((None, 6, D_MODEL), lambda i, j: (_mod_row(i), 0, 0)),
            pl.BlockSpec((1, D_MODEL), lambda i, j: (0, 0)),
            pl.BlockSpec((D_MODEL, FFN_TF), lambda i, j: (0, j)),
            pl.BlockSpec((D_MODEL, FFN_TF), lambda i, j: (0, j + nf)),
            pl.BlockSpec((3, FFN_TF), lambda i, j: (0, j)),
            pl.BlockSpec((3, FFN_TF), lambda i, j: (0, j + nf)),
            pl.BlockSpec((1, FFN_TF), lambda i, j: (0, j)),
            pl.BlockSpec((1, FFN_TF), lambda i, j: (0, j + nf)),
            pl.BlockSpec((FFN_TF, D_MODEL), lambda i, j: (j, 0)),
        ],
        out_specs=pl.BlockSpec((TM, D_MODEL), lambda i, j: (i, 0)),
        out_shape=jax.ShapeDtypeStruct((N_TOK, D_MODEL), F32),
        scratch_shapes=[pltpu.VMEM((TM, D_MODEL), BF16), pltpu.VMEM((TM, D_MODEL), F32)],
        compiler_params=_params("arbitrary", "arbitrary"),
        name="conv_ffn",
    )(x, mod, nw.reshape(1, D_MODEL), w_up, w_up, conv_w, conv_w, conv_b, conv_b, w_down)


ML_TN = 512


def _ml_up_kernel(x_ref, mod_ref, nw_ref, wm_ref, wz_ref, cw_ref, cb_ref, xm_ref, xc_ref, z_ref, h_scr):
    i = pl.program_id(0)

    @pl.when(pl.program_id(1) == 0)
    def _():
        h = _norm_mod(x_ref[...], nw_ref[...], mod_ref[0:1, :], mod_ref[1:2, :])
        h_scr[...] = h.astype(BF16)

    h = h_scr[...]
    not_first, not_last = _seq_masks(i, (TM, ML_TN))
    xm = jnp.dot(h, wm_ref[...], preferred_element_type=F32)
    xm_ref[...] = xm.astype(BF16)
    xc_ref[...] = _silu(_dwconv_rows(xm, cw_ref[...], cb_ref[...], not_first, not_last)).astype(BF16)
    z_ref[...] = jnp.dot(h, wz_ref[...], preferred_element_type=F32).astype(BF16)


def _ml_up(x, mod, nw, w_up, conv_w, conv_b):
    nj = ML_D_IN // ML_TN
    col = pl.BlockSpec((TM, ML_TN), lambda i, j: (i, j))
    shp = jax.ShapeDtypeStruct((N_TOK, ML_D_IN), BF16)
    return pl.pallas_call(
        _ml_up_kernel,
        grid=(N_TOK // TM, nj),
        in_specs=[
            pl.BlockSpec((TM, D_MODEL), lambda i, j: (i, 0)),
            pl.BlockSpec((None, 6, D_MODEL), lambda i, j: (_mod_row(i), 0, 0)),
            pl.BlockSpec((1, D_MODEL), lambda i, j: (0, 0)),
            pl.BlockSpec((D_MODEL, ML_TN), lambda i, j: (0, j)),
            pl.BlockSpec((D_MODEL, ML_TN), lambda i, j: (0, j + nj)),
            pl.BlockSpec((3, ML_TN), lambda i, j: (0, j)),
            pl.BlockSpec((1, ML_TN), lambda i, j: (0, j)),
        ],
        out_specs=[col, col, col],
        out_shape=[shp, shp, shp],
        scratch_shapes=[pltpu.VMEM((TM, D_MODEL), BF16)],
        compiler_params=_params("arbitrary", "arbitrary"),
        name="ml_up",
    )(x, mod, nw.reshape(1, D_MODEL), w_up, w_up, conv_w, conv_b.reshape(1, ML_D_IN))


def _log_sigmoid(x):
    return jnp.minimum(x, 0.0) - jnp.log(1.0 + jnp.exp(-jnp.abs(x)))


def _ml_scan_kernel(*refs, seq, has_init, out_state):
    q_ref, k_ref, v_ref, g_ref, gt_ref, xc_ref, z_ref, nw_ref, skip_ref = refs[:9]
    pos = 9
    if has_init:
        c0_ref, n0_ref, m0_ref = refs[pos:pos + 3]
        pos += 3
    o_ref = refs[pos]
    pos += 1
    if out_state:
        cout_ref, nout_ref, mout_ref = refs[pos:pos + 3]
        pos += 3
    hs_scr, c_scr = refs[pos:pos + 2]

    L = ML_CHUNK
    nc = seq // L
    scale = ML_DK ** -0.5
    row_i = lax.broadcasted_iota(jnp.int32, (L, L), 0)
    col_i = lax.broadcasted_iota(jnp.int32, (L, L), 1)

    for d in (0, 1):
        keep = (col_i <= row_i) if d == 0 else (col_i >= row_i)
        keep_f = keep.astype(F32)
        keep_t_f = ((row_i <= col_i) if d == 0 else (row_i >= col_i)).astype(F32)
        end = L - 1 if d == 0 else 0

        if has_init:
            c_scr[...] = c0_ref[d]
            n_init = n0_ref[d:d + 1, :]
            m_init = m0_ref[d:d + 1, 0:1]
        else:
            c_scr[...] = jnp.zeros_like(c_scr)
            n_init = jnp.zeros((1, ML_DK), F32)
            m_init = jnp.zeros((1, 1), F32)

        def step(c, carry, d=d, keep=keep, keep_f=keep_f, keep_t_f=keep_t_f, end=end):
            n_row, m = carry
            cc = c if d == 0 else nc - 1 - c
            r0 = pl.multiple_of(cc * L, L)
            qc = q_ref[pl.ds(r0, L), :]
            kc = k_ref[pl.ds(r0, L), :]
            vc = v_ref[pl.ds(r0, L), :]
            gcol = g_ref[cc]
            grow = gt_ref[cc]
            ig_col = gcol[:, 2 * d:2 * d + 1]
            ig_row = grow[2 * d:2 * d + 1, :]
            lf_col = _log_sigmoid(gcol[:, 2 * d + 1:2 * d + 2])
            lf_row = _log_sigmoid(grow[2 * d + 1:2 * d + 2, :])
            b_col = jnp.sum(keep_f * lf_row, axis=1, keepdims=True)
            b_row = jnp.sum(keep_t_f * lf_col, axis=0, keepdims=True)
            a_col = b_col + m
            dmat = jnp.where(keep, b_col - b_row + ig_row, NEG_INF)
            m_t = jnp.maximum(a_col, jnp.max(dmat, axis=1, keepdims=True))
            w_inter = jnp.exp(a_col - m_t)
            qk = lax.dot_general(qc, kc, (((1,), (1,)), ((), ())), preferred_element_type=F32)
            s = qk * scale * jnp.exp(dmat - m_t)
            inter = jnp.dot(qc, c_scr[...].astype(BF16), preferred_element_type=F32) * scale
            num = w_inter * inter + jnp.dot(s.astype(BF16), vc, preferred_element_type=F32)
            qn = jnp.sum(qc.astype(F32) * n_row, axis=1, keepdims=True) * scale
            den = w_inter * qn + jnp.sum(s, axis=1, keepdims=True)
            h = num / jnp.maximum(jnp.abs(den), jnp.exp(-m_t))

            b_end = b_col[end:end + 1, :]
            m_new = m_t[end:end + 1, :]
            w_c = jnp.exp(b_end + m - m_new)
            w_s = jnp.exp(b_end - b_col + ig_col - m_new)
            wk = w_s * kc.astype(F32)
            upd = lax.dot_general(wk.astype(BF16), vc, (((0,), (0,)), ((), ())), preferred_element_type=F32)
            c_scr[...] = w_c * c_scr[...] + upd
            n_new = w_c * n_row + jnp.sum(wk, axis=0, keepdims=True)

            if d == 0:
                hs_scr[pl.ds(r0, L), :] = h
            else:
                hs = hs_scr[pl.ds(r0, L), :] + h
                mu = jnp.mean(hs, axis=1, keepdims=True)
                cen = hs - mu
                var = jnp.mean(cen * cen, axis=1, keepdims=True)
                hn = cen * lax.rsqrt(var + NORM_EPS) * nw_ref[...]
                xc = xc_ref[pl.ds(r0, L), :].astype(F32)
                z = z_ref[pl.ds(r0, L), :].astype(F32)
                o_ref[pl.ds(r0, L), :] = ((hn + skip_ref[...] * xc) * _silu(z)).astype(o_ref.dtype)
            return n_new, m_new

        n_fin, m_fin = lax.fori_loop(0, nc, step, (n_init, m_init))
        if out_state:
            cout_ref[d] = c_scr[...]
            nout_ref[d:d + 1, :] = n_fin
            mout_ref[d:d + 1, :] = jnp.broadcast_to(m_fin, (1, 128))


def _ml_scan(qk, v, g4, g4t, xc, z, norm_w, skip, *, seq, nb, row0, init=None):
    L = ML_CHUNK
    nc = seq // L
    rb = row0 // seq
    has_init = init is not None
    in_specs = [
        pl.BlockSpec((seq, ML_DK), lambda b, h: (rb + b, h)),
        pl.BlockSpec((seq, ML_DK), lambda b, h: (rb + b, ML_HEADS + h)),
        pl.BlockSpec((seq, ML_DV), lambda b, h: (rb + b, h)),
        pl.BlockSpec((None, nc, L, 4), lambda b, h: (h, rb + b, 0, 0)),
        pl.BlockSpec((None, nc, 4, L), lambda b, h: (h, rb + b, 0, 0)),
        pl.BlockSpec((seq, ML_DV), lambda b, h: (rb + b, h)),
        pl.BlockSpec((seq, ML_DV), lambda b, h: (rb + b, h)),
        pl.BlockSpec((1, ML_DV), lambda b, h: (0, h)),
        pl.BlockSpec((1, ML_DV), lambda b, h: (0, h)),
    ]
    args = [qk, qk, v, g4, g4t, xc, z, norm_w.reshape(1, ML_D_IN), skip.reshape(1, ML_D_IN)]
    if has_init:
        c0, layer, n0, m0 = init
        in_specs += [
            pl.BlockSpec((None, None, 2, None, ML_DK, ML_DV), lambda b, h: (b, layer, 0, h, 0, 0)),
            pl.BlockSpec((None, None, 2, ML_DK), lambda b, h: (b, h, 0, 0)),
            pl.BlockSpec((None, None, 2, 128), lambda b, h: (b, h, 0, 0)),
        ]
        args += [c0, n0, m0]
    out_specs = [pl.BlockSpec((seq, ML_DV), lambda b, h: (b, h))]
    out_shape = [jax.ShapeDtypeStruct((nb * seq, ML_D_IN), BF16)]
    if not has_init:
        out_specs += [
            pl.BlockSpec((None, 2, None, ML_DK, ML_DV), lambda b, h: (b, 0, h, 0, 0)),
            pl.BlockSpec((None, None, 2, ML_DK), lambda b, h: (b, h, 0, 0)),
            pl.BlockSpec((None, None, 2, 128), lambda b, h: (b, h, 0, 0)),
        ]
        out_shape += [
            jax.ShapeDtypeStruct((nb, 2, ML_HEADS, ML_DK, ML_DV), F32),
            jax.ShapeDtypeStruct((nb, ML_HEADS, 2, ML_DK), F32),
            jax.ShapeDtypeStruct((nb, ML_HEADS, 2, 128), F32),
        ]
    return pl.pallas_call(
        functools.partial(_ml_scan_kernel, seq=seq, has_init=has_init, out_state=not has_init),
        grid=(nb, ML_HEADS),
        in_specs=in_specs,
        out_specs=out_specs,
        out_shape=out_shape,
        scratch_shapes=[pltpu.VMEM((seq, ML_DV), F32), pltpu.VMEM((ML_DK, ML_DV), F32)],
        compiler_params=_params("arbitrary", "arbitrary"),
        name="ml_scan_init" if has_init else "ml_scan_zero",
    )(*args)


def _mlstm_layer(x, mod, nw, p, state):
    xm, xc, z = _ml_up(x, mod, nw, p["w_up"], p["conv_w"], p["conv_b"])
    zero_b = jnp.zeros((1, ML_D_IN), F32)
    qk = _mm(xc, p["w_qk"], zero_b, 1024, BF16)
    v = _mm(xm, p["w_v"], zero_b, 1024, BF16)
    g = _mm(xm, p["w_gate"], p["b_gate"], 128, F32)[:, :4 * ML_HEADS]
    L = ML_CHUNK
    g4 = g.reshape(N_TOK // L, L, 4, ML_HEADS).transpose(3, 0, 1, 2)
    g4t = g4.transpose(0, 1, 3, 2)
    oc, c_fin, n_fin, m_fin = _ml_scan(qk, v, g4, g4t, xc, z, p["norm_w"], p["skip"],
                                       seq=SEQ, nb=BATCH, row0=0)
    (ol,) = _ml_scan(qk, v, g4, g4t, xc, z, p["norm_w"], p["skip"],
                     seq=DEC_SEQ, nb=DEC_BATCH, row0=N_CTX, init=state)
    x = _mm_res(oc, ol, p["w_down"], x, mod)
    return x, c_fin, n_fin.transpose(0, 2, 1, 3), m_fin[..., 0].transpose(0, 2, 1)


def _softmax_parts(scores, sink_col):
    m = functools.reduce(jnp.maximum, [jnp.max(s, axis=1, keepdims=True) for s in scores])
    if sink_col is not None:
        m = jnp.maximum(m, sink_col)
    ps = [jnp.exp(s - m) for s in scores]
    den = functools.reduce(jnp.add, [jnp.sum(p, axis=1, keepdims=True) for p in ps])
    if sink_col is not None:
        den = den + jnp.exp(sink_col - m)
    return ps, den


LANES = 128


def _qk(q, k):
    return lax.dot_general(q, k, (((1,), (1,)), ((), ())), preferred_element_type=F32) * (HEAD_DIM ** -0.5)


def _attend_tiles(q_tiles, parts, sinks):
    r = q_tiles[0].shape[0]
    lo = lax.broadcasted_iota(jnp.int32, (r, LANES), 1) < HEAD_DIM
    zero = jnp.zeros((r, LANES), BF16)
    qs = jnp.concatenate([jnp.where(sel, t, zero) for t in q_tiles for sel in (lo, jnp.logical_not(lo))], axis=0)
    scores = [post(_qk(qs, k2)) for k2, _, post in parts]
    sink_col = None
    if sinks is not None:
        rows = lax.broadcasted_iota(jnp.int32, (qs.shape[0], 1), 0)
        sink_col = jnp.full((qs.shape[0], 1), sinks[0], F32)
        for hi in range(1, len(sinks)):
            sink_col = jnp.where(rows >= hi * r, sinks[hi], sink_col)
    ps, den = _softmax_parts(scores, sink_col)
    o = functools.reduce(jnp.add, [jnp.dot(p.astype(BF16), v2, preferred_element_type=F32)
                                   for p, (_, v2, _) in zip(ps, parts)]) / den
    return [jnp.where(lo, o[2 * a * r:(2 * a + 1) * r], o[(2 * a + 1) * r:(2 * a + 2) * r])
            for a in range(len(q_tiles))]


def _identity(s):
    return s


def _ctx_attn_kernel(*refs, tiles_per_kv, has_sink):
    if has_sink:
        sink_ref, q_ref, k_ref, v_ref, o_ref = refs
    else:
        q_ref, k_ref, v_ref, o_ref = refs
    for t in range(k_ref.shape[1] // LANES):
        k2 = k_ref[:, t * LANES:(t + 1) * LANES].astype(BF16)
        v2 = v_ref[:, t * LANES:(t + 1) * LANES].astype(BF16)
        first = t * tiles_per_kv
        q_tiles = [q_ref[:, (first + a) * LANES:(first + a + 1) * LANES] for a in range(tiles_per_kv)]
        sinks = [sink_ref[2 * first + hi] for hi in range(2 * tiles_per_kv)] if has_sink else None
        outs = _attend_tiles(q_tiles, [(k2, v2, _identity)], sinks)
        for a, o in enumerate(outs):
            o_ref[:, (first + a) * LANES:(first + a + 1) * LANES] = o.astype(o_ref.dtype)


def _ctx_attn(q, kv, n_kv_cols, sink):
    nq = q.shape[1]
    has_sink = sink is not None
    in_specs = [pl.BlockSpec((SEQ, nq), lambda b: (b, 0)),
                pl.BlockSpec((SEQ, n_kv_cols), lambda b: (b, 0)),
                pl.BlockSpec((SEQ, n_kv_cols), lambda b: (b, 1))]
    args = [q, kv, kv]
    if has_sink:
        in_specs = [pl.BlockSpec(memory_space=pltpu.SMEM)] + in_specs
        args = [sink] + args
    return pl.pallas_call(
        functools.partial(_ctx_attn_kernel, tiles_per_kv=nq // n_kv_cols, has_sink=has_sink),
        grid=(BATCH,),
        in_specs=in_specs,
        out_specs=pl.BlockSpec((SEQ, nq), lambda b: (b, 0)),
        out_shape=jax.ShapeDtypeStruct((N_CTX, nq), BF16),
        compiler_params=_params("arbitrary"),
        name="ctx_attn",
    )(*args)


SWA_SPAN = Q_BLOCK + 2 * SWA_WINDOW
SWA_KV_COLS = SWA_KV * LANES


def _swa_lat_kernel(sink_ref, q_ref, k_ref, v_ref, kc_ref, vc_ref, o_ref):
    j = pl.program_id(1)
    start = pl.multiple_of(jnp.clip((j - 1) * Q_BLOCK, 0, DEC_SEQ - SWA_SPAN), Q_BLOCK)
    rows = 4 * Q_BLOCK
    qpos = j * Q_BLOCK + (lax.broadcasted_iota(jnp.int32, (rows, SWA_SPAN), 0) & (Q_BLOCK - 1))
    kpos = start + lax.broadcasted_iota(jnp.int32, (rows, SWA_SPAN), 1)
    in_window = jnp.abs(qpos - kpos) <= SWA_WINDOW

    def window(s):
        return jnp.where(in_window, s, NEG_INF)

    for t in range(SWA_KV):
        cols = slice(t * LANES, (t + 1) * LANES)
        k_loc = k_ref[pl.ds(start, SWA_SPAN), cols].astype(BF16)
        v_loc = v_ref[pl.ds(start, SWA_SPAN), cols].astype(BF16)
        q_tiles = [q_ref[:, (2 * t + a) * LANES:(2 * t + a + 1) * LANES] for a in range(2)]
        sinks = [sink_ref[4 * t + hi] for hi in range(4)]
        outs = _attend_tiles(q_tiles, [(k_loc, v_loc, window), (kc_ref[:, cols], vc_ref[:, cols], _identity)], sinks)
        for a, o in enumerate(outs):
            o_ref[:, (2 * t + a) * LANES:(2 * t + a + 1) * LANES] = o.astype(o_ref.dtype)


def _swa_latent(q, kv, kc, vc, sink):
    nq = q.shape[1]
    rb = N_CTX // DEC_SEQ
    qb = N_CTX // Q_BLOCK
    nj = DEC_SEQ // Q_BLOCK
    cspec = pl.BlockSpec((None, SEQ, SWA_KV_COLS), lambda b, j: (b, 0, 0))
    return pl.pallas_call(
        _swa_lat_kernel,
        grid=(DEC_BATCH, nj),
        in_specs=[pl.BlockSpec(memory_space=pltpu.SMEM),
                  pl.BlockSpec((Q_BLOCK, nq), lambda b, j: (qb + b * nj + j, 0)),
                  pl.BlockSpec((DEC_SEQ, SWA_KV_COLS), lambda b, j: (rb + b, 0)),
                  pl.BlockSpec((DEC_SEQ, SWA_KV_COLS), lambda b, j: (rb + b, 1)),
                  cspec, cspec],
        out_specs=pl.BlockSpec((Q_BLOCK, nq), lambda b, j: (b * nj + j, 0)),
        out_shape=jax.ShapeDtypeStruct((N_LAT, nq), BF16),
        compiler_params=_params("arbitrary", "arbitrary"),
        name="swa_latent",
    )(sink, q, kv, kv, kc, vc)


NA_QT = 256
NA_SPAN = 768


def _na_start(j):
    return (j // 2) * (DEC_SEQ - NA_SPAN)


NA_ROWS = DEC_SEQ // GRID_W
NA_DR = 2 * NA_KH - 1
NA_DC = 2 * NA_KW - 1


def _na_blocks_kernel(rpb_ref, onehot_ref, valid_ref, o_ref):
    t = jnp.dot(rpb_ref[...], onehot_ref[...], preferred_element_type=F32, precision=lax.Precision.HIGHEST)
    o_ref[...] = jnp.where(valid_ref[...] > 0.5, t, NEG_INF)


def _na_bias_blocks(rpb):
    h = rpb.shape[0]
    kpad = 32
    cq, ck = np.meshgrid(np.arange(GRID_W), np.arange(GRID_W), indexing="ij")
    dc = (np.clip(ck - cq, -(NA_KW - 1), NA_KW - 1) + NA_KW - 1).reshape(-1)
    cs = np.clip(cq - NA_KW // 2, 0, GRID_W - NA_KW)
    valid = ((ck >= cs) & (ck < cs + NA_KW)).reshape(1, -1).astype(np.float32)
    onehot = (np.arange(kpad)[:, None] == dc[None, :]).astype(np.float32)
    rpb2 = jnp.pad(rpb.reshape(h * NA_DR, NA_DC), ((0, 0), (0, kpad - NA_DC)))
    n = GRID_W * GRID_W
    blocks = pl.pallas_call(
        _na_blocks_kernel,
        grid=(1,),
        in_specs=[pl.BlockSpec((h * NA_DR, kpad), lambda i: (0, 0)),
                  pl.BlockSpec((kpad, n), lambda i: (0, 0)),
                  pl.BlockSpec((1, n), lambda i: (0, 0))],
        out_specs=pl.BlockSpec((h * NA_DR, n), lambda i: (0, 0)),
        out_shape=jax.ShapeDtypeStruct((h * NA_DR, n), F32),
        compiler_params=_params("arbitrary"),
        name="na_bias_blocks",
    )(rpb2, jnp.asarray(onehot), jnp.asarray(valid))
    blocks = blocks.reshape(h, NA_DR, GRID_W, GRID_W)
    padded = jnp.pad(blocks, ((0, 0), (1, 1), (0, 0), (0, 0)), constant_values=NEG_INF)
    return jnp.concatenate([padded[:, :-1], padded[:, 1:]], axis=-1)


def _na_lat_kernel(q_ref, k_ref, v_ref, kc_ref, vc_ref, blk_ref, o_ref, bias_scr):
    j = pl.program_id(0)
    start = pl.multiple_of(_na_start(j), 256)

    @pl.when(pl.program_id(2) == 0)
    def _():
        lane_lo = lax.broadcasted_iota(jnp.int32, (GRID_W, LANES), 1) < GRID_W
        for rq_l in range(NA_QT // GRID_W):
            rq = j * (NA_QT // GRID_W) + rq_l
            rs = jnp.clip(rq - NA_KH // 2, 0, NA_ROWS - NA_KH)
            for kp in range(NA_SPAN // LANES):
                rk = start // GRID_W + 2 * kp
                idx = jnp.clip(rk - rq + NA_KH, 0, NA_DR)
                in_band = [jnp.logical_and(r >= rs, r < rs + NA_KH).astype(jnp.int32) for r in (rk, rk + 1)]
                ok = jnp.where(lane_lo, in_band[0], in_band[1]) > 0
                for hh in range(2):
                    bias_scr[hh * NA_QT + rq_l * GRID_W:hh * NA_QT + (rq_l + 1) * GRID_W,
                             kp * LANES:(kp + 1) * LANES] = jnp.where(ok, blk_ref[hh, idx], NEG_INF)

    def add_bias(s):
        return s + bias_scr[...]

    k_loc = k_ref[pl.ds(start, NA_SPAN), :].astype(BF16)
    v_loc = v_ref[pl.ds(start, NA_SPAN), :].astype(BF16)
    (o,) = _attend_tiles([q_ref[...]], [(k_loc, v_loc, add_bias), (kc_ref[...], vc_ref[...], _identity)], None)
    o_ref[...] = o.astype(o_ref.dtype)


def _na_latent(q, kv, kc, vc, blocks):
    nq = q.shape[1]
    n_tiles = nq // LANES
    nj = DEC_SEQ // NA_QT
    rb = N_CTX // DEC_SEQ
    qb = N_CTX // NA_QT
    cspec = pl.BlockSpec((None, SEQ, LANES), lambda j, p, b: (b, 0, p))
    return pl.pallas_call(
        _na_lat_kernel,
        grid=(nj, n_tiles, DEC_BATCH),
        in_specs=[pl.BlockSpec((NA_QT, LANES), lambda j, p, b: (qb + b * nj + j, p)),
                  pl.BlockSpec((DEC_SEQ, LANES), lambda j, p, b: (rb + b, p)),
                  pl.BlockSpec((DEC_SEQ, LANES), lambda j, p, b: (rb + b, n_tiles + p)),
                  cspec, cspec,
                  pl.BlockSpec((2, NA_DR + 1, GRID_W, LANES), lambda j, p, b: (p, 0, 0, 0))],
        out_specs=pl.BlockSpec((NA_QT, LANES), lambda j, p, b: (b * nj + j, p)),
        out_shape=jax.ShapeDtypeStruct((N_LAT, nq), BF16),
        scratch_shapes=[pltpu.VMEM((2 * NA_QT, NA_SPAN), F32)],
        compiler_params=_params("arbitrary", "arbitrary", "arbitrary"),
        name="na_latent",
    )(q, kv, kv, kc, vc, blocks)


def _rope_tables(width):
    quarter = HEAD_DIM // 4
    pos = np.arange(DEC_SEQ)
    inv = np.power(ROPE_BASE, -np.arange(quarter, dtype=np.float32) / quarter).astype(np.float32)
    d = np.arange(width) % HEAD_DIM
    p = np.where((d < HEAD_DIM // 2)[None, :], (pos // GRID_W)[:, None], (pos % GRID_W)[:, None]).astype(np.float32)
    ang = p * inv[d % quarter][None, :]
    sign = np.where((d // quarter) % 2 == 0, -1.0, 1.0)[None, :]
    return jnp.asarray(np.cos(ang), F32), jnp.asarray(np.sin(ang) * sign, F32)


def _cache_rows(cache, dup):
    b, h, s, hd = cache.shape
    rows = jnp.broadcast_to(cache.transpose(0, 2, 1, 3)[:, :, :, None, :], (b, s, h, dup, hd))
    return rows.reshape(b, s, h * dup * hd).astype(BF16)


def _swa_layer(x, mod, nw, w_qkv, sink, w_o, cache_k, cache_v):
    nq, nk = SWA_HEADS * HEAD_DIM, SWA_KV * HEAD_DIM

    def dup_heads(w):
        return jnp.broadcast_to(w.reshape(D_MODEL, SWA_KV, 1, HEAD_DIM),
                                (D_MODEL, SWA_KV, 2, HEAD_DIM)).reshape(D_MODEL, SWA_KV_COLS)

    w = jnp.concatenate([w_qkv[:, :nq], dup_heads(w_qkv[:, nq:nq + nk]), dup_heads(w_qkv[:, nq + nk:])], axis=1)
    q, kv = _qkv_proj(x, mod, nw, w.astype(BF16), nq, rope_cols=nq + SWA_KV_COLS)
    kv_ctx = kv[:N_CTX].reshape(BATCH, SEQ, 2, SWA_KV, 2, HEAD_DIM)[:, :, :, :, 0]
    k_new = kv_ctx[:, :, 0].transpose(0, 2, 1, 3)
    v_new = kv_ctx[:, :, 1].transpose(0, 2, 1, 3)
    oc = _ctx_attn(q, kv, SWA_KV_COLS, sink)
    ol = _swa_latent(q, kv, _cache_rows(cache_k, 2), _cache_rows(cache_v, 2), sink)
    return _mm_res(oc, ol, w_o, x, mod), k_new, v_new


def _na_layer(x, mod, nw, w_qkv, rpb, w_o, cache_k, cache_v):
    n = NA_HEADS * HEAD_DIM
    q, kv = _qkv_proj(x, mod, nw, w_qkv, n)
    kv_ctx = kv[:N_CTX].reshape(BATCH, SEQ, 2, NA_HEADS, HEAD_DIM)
    k_new = kv_ctx[:, :, 0].transpose(0, 2, 1, 3)
    v_new = kv_ctx[:, :, 1].transpose(0, 2, 1, 3)
    oc = _ctx_attn(q, kv, n, None)
    ol = _na_latent(q, kv, _cache_rows(cache_k, 1), _cache_rows(cache_v, 1), _na_bias_blocks(rpb))
    return _mm_res(oc, ol, w_o, x, mod), k_new, v_new


def _final_norm_kernel(x_ref, w_ref, o_ref):
    x = x_ref[...]
    o_ref[...] = x * lax.rsqrt(jnp.mean(x * x, axis=-1, keepdims=True) + NORM_EPS) * w_ref[...]


def _final_norm(x, w):
    return pl.pallas_call(
        _final_norm_kernel,
        grid=(N_TOK // TM,),
        in_specs=[pl.BlockSpec((TM, D_MODEL), lambda i: (i, 0)), pl.BlockSpec((1, D_MODEL), lambda i: (0, 0))],
        out_specs=pl.BlockSpec((TM, D_MODEL), lambda i: (i, 0)),
        out_shape=jax.ShapeDtypeStruct((N_TOK, D_MODEL), F32),
        compiler_params=_params("arbitrary"),
        name="final_norm",
    )(x, w.reshape(1, D_MODEL))


def kernel(x_prompt, x_sample, state_mlstm_C, state_mlstm_n, state_mlstm_m, cache_swa_k, cache_swa_v, cache_na_k, cache_na_v, c, c_ctx, ada_w, ada_b, norm_w, final_norm_w, ffn_w_up, ffn_conv_w, ffn_conv_b, ffn_w_down, ml_w_up, ml_conv_w, ml_conv_b, ml_w_qk, ml_w_v, ml_w_gate, ml_b_gate, ml_norm_w, ml_skip, ml_w_down, swa_w_qkv, swa_sink, swa_w_o, na_w_qkv, na_rpb, na_w_o):
    x = jnp.concatenate([x_prompt.reshape(N_CTX, D_MODEL), x_sample.reshape(N_LAT, D_MODEL)], axis=0)
    cond = jnp.concatenate([c_ctx[None], c, jnp.zeros((MOD_ROWS - 1 - DEC_BATCH, D_MODEL), F32)], axis=0)
    mods = _ada_mod(cond, ada_w, ada_b)

    new_c, new_n, new_m = [], [], []
    new_sk = new_sv = new_nk = new_nv = None
    for i in range(DEPTH):
        kind, j = i % N_MIXERS, i // N_MIXERS
        mod = mods[i]
        if kind == 0:
            gate_w = jnp.pad(ml_w_gate[j], ((0, 0), (0, 128 - 4 * ML_HEADS))).astype(BF16)
            gate_b = jnp.pad(ml_b_gate[j], (0, 128 - 4 * ML_HEADS)).reshape(1, 128)
            p = dict(w_up=ml_w_up[j].astype(BF16), conv_w=ml_conv_w[j], conv_b=ml_conv_b[j],
                     w_qk=ml_w_qk[j].astype(BF16), w_v=ml_w_v[j].astype(BF16), w_gate=gate_w, b_gate=gate_b,
                     norm_w=ml_norm_w[j], skip=ml_skip[j], w_down=ml_w_down[j].astype(BF16))
            n0 = state_mlstm_n[:, j].transpose(0, 2, 1, 3)
            m0 = jnp.broadcast_to(state_mlstm_m[:, j].transpose(0, 2, 1)[..., None], (DEC_BATCH, ML_HEADS, 2, 128))
            x, cf, nf, mf = _mlstm_layer(x, mod, norm_w[i, 0], p, (state_mlstm_C, j, n0, m0))
            new_c.append(cf)
            new_n.append(nf)
            new_m.append(mf)
        elif kind == 1:
            x, k_new, v_new = _swa_layer(x, mod, norm_w[i, 0], swa_w_qkv[j].astype(BF16), swa_sink[j],
                                         swa_w_o[j].astype(BF16), cache_swa_k[:, j], cache_swa_v[:, j])
            new_sk, new_sv = k_new[:, None], v_new[:, None]
        else:
            x, k_new, v_new = _na_layer(x, mod, norm_w[i, 0], na_w_qkv[j].astype(BF16), na_rpb[j],
                                        na_w_o[j].astype(BF16), cache_na_k[:, j], cache_na_v[:, j])
            new_nk, new_nv = k_new[:, None], v_new[:, None]
        x = _conv_ffn(x, mod, norm_w[i, 1], ffn_w_up[i].astype(BF16), ffn_conv_w[i], ffn_conv_b[i],
                      ffn_w_down[i].astype(BF16))

    y = _final_norm(x, final_norm_w)
    return (y[:N_CTX].reshape(BATCH, SEQ, D_MODEL), y[N_CTX:].reshape(DEC_BATCH, DEC_SEQ, D_MODEL),
            jnp.stack(new_c, axis=1), jnp.stack(new_n, axis=1), jnp.stack(new_m, axis=1),
            new_sk, new_sv, new_nk, new_nv)
```

```python
import functools

import jax
import jax.numpy as jnp
import numpy as np
from jax import lax
from jax.experimental import pallas as pl
from jax.experimental.pallas import tpu as pltpu

F32 = jnp.float32
BF16 = jnp.bfloat16

D_MODEL = 1024
BATCH = 32
SEQ = 256
DEPTH = 4
DEC_BATCH = 8
DEC_SEQ = 1024
GRID_W = 64
N_MIXERS = 3
N_ML_LAYERS = (DEPTH + 2) // 3
NORM_EPS = 1e-6
D_FF = 2816
ML_D_IN = 2 * D_MODEL
ML_HEADS = 4
ML_DK = ML_D_IN // (2 * ML_HEADS)
ML_DV = ML_D_IN // ML_HEADS
ML_CHUNK = 128
HEAD_DIM = 64
SWA_HEADS = D_MODEL // HEAD_DIM
SWA_KV = SWA_HEADS // 4
SWA_WINDOW = 128
Q_BLOCK = 128
ROPE_BASE = 10000.0
NA_HEADS = D_MODEL // HEAD_DIM
NA_KH = 8
NA_KW = 16
NEG_INF = -1e30

N_CTX = BATCH * SEQ
N_LAT = DEC_BATCH * DEC_SEQ
N_TOK = N_CTX + N_LAT
TM = 1024
N_CTX_TILES = N_CTX // TM
MOD_ROWS = 16
VMEM_LIMIT_BYTES = 56 * 1024 * 1024


def _params(*sem):
    return pltpu.CompilerParams(dimension_semantics=sem, vmem_limit_bytes=VMEM_LIMIT_BYTES)


def _mod_row(i):
    return jnp.where(i < N_CTX_TILES, 0, i - (N_CTX_TILES - 1))


def _silu(x):
    return x / (1.0 + jnp.exp(-x))


def _norm_mod(x, nw, shift, scale):
    y = x * lax.rsqrt(jnp.mean(x * x, axis=-1, keepdims=True) + NORM_EPS) * nw
    return y * (1.0 + scale) + shift


SUBLANES = 8
LANES = 128


def _dwconv_rows(u, cw, cb, seq):
    r, c = u.shape
    n_groups, per_seq = r // SUBLANES, seq // SUBLANES
    g = u.reshape(n_groups, SUBLANES, c)
    sub = lax.broadcasted_iota(jnp.int32, g.shape, 1)
    down = pltpu.roll(g, 1, 1)
    up = pltpu.roll(g, SUBLANES - 1, 1)
    zero = jnp.zeros((1, SUBLANES, c), F32)
    from_prev, from_next = [], []
    for s in range(0, n_groups, per_seq):
        from_prev += [zero, down[s:s + per_seq - 1]]
        from_next += [up[s + 1:s + per_seq], zero]
    prev = jnp.where(sub == 0, jnp.concatenate(from_prev, axis=0), down)
    nxt = jnp.where(sub == SUBLANES - 1, jnp.concatenate(from_next, axis=0), up)
    out = cw[0:1, :] * prev + cw[1:2, :] * g + cw[2:3, :] * nxt + cb
    return out.reshape(r, c)


def _by_tile_kind(tile, body):
    pl.when(tile < N_CTX_TILES)(functools.partial(body, SEQ))
    pl.when(tile >= N_CTX_TILES)(functools.partial(body, DEC_SEQ))


def _ada_kernel(c_ref, w_ref, b_ref, o_ref):
    s = _silu(c_ref[...]).astype(BF16)
    o_ref[...] = jnp.dot(s, w_ref[...].astype(BF16), preferred_element_type=F32) + b_ref[...]


def _ada_mod(cond, ada_w, ada_b):
    tn = 1536
    n = 6 * D_MODEL
    out = pl.pallas_call(
        _ada_kernel,
        grid=(DEPTH, n // tn),
        in_specs=[
            pl.BlockSpec((MOD_ROWS, D_MODEL), lambda l, j: (0, 0)),
            pl.BlockSpec((None, D_MODEL, tn), lambda l, j: (l, 0, j)),
            pl.BlockSpec((None, 1, tn), lambda l, j: (l, 0, j)),
        ],
        out_specs=pl.BlockSpec((None, MOD_ROWS, tn), lambda l, j: (l, 0, j)),
        out_shape=jax.ShapeDtypeStruct((DEPTH, MOD_ROWS, n), F32),
        compiler_params=_params("arbitrary", "arbitrary"),
        name="ada_mod",
    )(cond, ada_w, ada_b.reshape(DEPTH, 1, n))
    return out.reshape(DEPTH, MOD_ROWS, 6, D_MODEL)


QKV_TN = 512


def _rotate_pairs(a, cos, sin):
    lane = lax.broadcasted_iota(jnp.int32, a.shape, 1)
    first = (lane & (HEAD_DIM // 4)) == 0
    n = a.shape[1]
    partner = jnp.where(first, pltpu.roll(a, n - HEAD_DIM // 4, 1), pltpu.roll(a, HEAD_DIM // 4, 1))
    return a * cos + partner * sin


def _qkv_kernel(*refs, q_blocks, rope_blocks):
    if rope_blocks:
        x_ref, mod_ref, nw_ref, w_ref, cos_ref, sin_ref, q_ref, kv_ref, h_scr = refs
    else:
        x_ref, mod_ref, nw_ref, w_ref, q_ref, kv_ref, h_scr = refs
    i = pl.program_id(0)
    j = pl.program_id(1)

    @pl.when(j == 0)
    def _():
        h = _norm_mod(x_ref[...], nw_ref[...], mod_ref[0:1, :], mod_ref[1:2, :])
        h_scr[...] = h.astype(BF16)

    acc = jnp.dot(h_scr[...], w_ref[...], preferred_element_type=F32)

    def emit(val):
        @pl.when(j < q_blocks)
        def _():
            q_ref[...] = val.astype(BF16)

        @pl.when(j >= q_blocks)
        def _():
            kv_ref[...] = val

    if rope_blocks:
        rotate = jnp.logical_and(i >= N_CTX_TILES, j < rope_blocks)

        @pl.when(rotate)
        def _():
            emit(_rotate_pairs(acc, cos_ref[...], sin_ref[...]))

        @pl.when(jnp.logical_not(rotate))
        def _():
            emit(acc)
    else:
        emit(acc)


def _qkv_proj(x, mod, nw, w, n_q, rope_cols=0):
    n = w.shape[1]
    q_blocks = n_q // QKV_TN
    rope_blocks = rope_cols // QKV_TN
    in_specs = [
        pl.BlockSpec((TM, D_MODEL), lambda i, j: (i, 0)),
        pl.BlockSpec((None, 6, D_MODEL), lambda i, j: (_mod_row(i), 0, 0)),
        pl.BlockSpec((1, D_MODEL), lambda i, j: (0, 0)),
        pl.BlockSpec((D_MODEL, QKV_TN), lambda i, j: (0, j)),
    ]
    args = [x, mod, nw.reshape(1, D_MODEL), w]
    if rope_blocks:
        tab = pl.BlockSpec((DEC_SEQ, QKV_TN), lambda i, j: (0, 0))
        in_specs += [tab, tab]
        args += list(_rope_tables(QKV_TN))
    return pl.pallas_call(
        functools.partial(_qkv_kernel, q_blocks=q_blocks, rope_blocks=rope_blocks),
        grid=(N_TOK // TM, n // QKV_TN),
        in_specs=in_specs,
        out_specs=[pl.BlockSpec((TM, QKV_TN), lambda i, j: (i, jnp.minimum(j, q_blocks - 1))),
                   pl.BlockSpec((TM, QKV_TN), lambda i, j: (i, jnp.maximum(j - q_blocks, 0)))],
        out_shape=[jax.ShapeDtypeStruct((N_TOK, n_q), BF16), jax.ShapeDtypeStruct((N_TOK, n - n_q), F32)],
        scratch_shapes=[pltpu.VMEM((TM, D_MODEL), BF16)],
        compiler_params=_params("arbitrary", "arbitrary"),
        name="qkv_proj",
    )(*args)


def _mm_kernel(a_ref, w_ref, b_ref, o_ref):
    acc = jnp.dot(a_ref[...], w_ref[...], preferred_element_type=F32)
    o_ref[...] = (acc + b_ref[...]).astype(o_ref.dtype)


def _mm(a, w, bias, tn, out_dtype):
    m, k = a.shape
    n = w.shape[1]
    return pl.pallas_call(
        _mm_kernel,
        grid=(m // TM, n // tn),
        in_specs=[
            pl.BlockSpec((TM, k), lambda i, j: (i, 0)),
            pl.BlockSpec((k, tn), lambda i, j: (0, j)),
            pl.BlockSpec((1, tn), lambda i, j: (0, j)),
        ],
        out_specs=pl.BlockSpec((TM, tn), lambda i, j: (i, j)),
        out_shape=jax.ShapeDtypeStruct((m, n), out_dtype),
        compiler_params=_params("arbitrary", "arbitrary"),
        name="mm",
    )(a, w, bias)


def _mm_res_kernel(ac_ref, al_ref, w_ref, x_ref, mod_ref, o_ref):
    def emit(a_ref):
        acc = jnp.dot(a_ref[...], w_ref[...], preferred_element_type=F32)
        o_ref[...] = x_ref[...] + mod_ref[2:3, :] * acc

    is_ctx = pl.program_id(0) < N_CTX_TILES
    pl.when(is_ctx)(functools.partial(emit, ac_ref))
    pl.when(jnp.logical_not(is_ctx))(functools.partial(emit, al_ref))


def _mm_res(a_ctx, a_lat, w, x, mod):
    k = a_ctx.shape[1]
    return pl.pallas_call(
        _mm_res_kernel,
        grid=(N_TOK // TM,),
        in_specs=[
            pl.BlockSpec((TM, k), lambda i: (jnp.minimum(i, N_CTX_TILES - 1), 0)),
            pl.BlockSpec((TM, k), lambda i: (jnp.maximum(i - N_CTX_TILES, 0), 0)),
            pl.BlockSpec((k, D_MODEL), lambda i: (0, 0)),
            pl.BlockSpec((TM, D_MODEL), lambda i: (i, 0)),
            pl.BlockSpec((None, 6, D_MODEL), lambda i: (_mod_row(i), 0, 0)),
        ],
        out_specs=pl.BlockSpec((TM, D_MODEL), lambda i: (i, 0)),
        out_shape=jax.ShapeDtypeStruct((N_TOK, D_MODEL), F32),
        compiler_params=_params("arbitrary"),
        name="mm_res",
    )(a_ctx, a_lat, w, x, mod)


FFN_TF = 256


def _ffn_kernel(x_ref, mod_ref, nw_ref, wg_ref, wv_ref, cwg_ref, cwv_ref, cbg_ref, cbv_ref, wd_ref,
                o_ref, h_scr, acc_scr):
    i = pl.program_id(0)
    j = pl.program_id(1)

    @pl.when(j == 0)
    def _():
        h = _norm_mod(x_ref[...], nw_ref[...], mod_ref[3:4, :], mod_ref[4:5, :])
        h_scr[...] = h.astype(BF16)
        acc_scr[...] = jnp.zeros_like(acc_scr)

    def hidden_block(seq):
        h = h_scr[...]
        g = _dwconv_rows(jnp.dot(h, wg_ref[...], preferred_element_type=F32), cwg_ref[...], cbg_ref[...], seq)
        v = _dwconv_rows(jnp.dot(h, wv_ref[...], preferred_element_type=F32), cwv_ref[...], cbv_ref[...], seq)
        a = (_silu(g) * v).astype(BF16)
        acc_scr[...] += jnp.dot(a, wd_ref[...], preferred_element_type=F32)

    _by_tile_kind(i, hidden_block)

    @pl.when(j == pl.num_programs(1) - 1)
    def _():
        o_ref[...] = x_ref[...] + mod_ref[5:6, :] * acc_scr[...]


def _conv_ffn(x, mod, nw, w_up, conv_w, conv_b, w_down):
    nf = D_FF // FFN_TF
    conv_b = conv_b.reshape(1, 2 * D_FF)
    return pl.pallas_call(
        _ffn_kernel,
        grid=(N_TOK // TM, nf),
        in_specs=[
            pl.BlockSpec((TM, D_MODEL), lambda i, j: (i, 0)),
            pl.BlockSpec((None, 6, D_MODEL), lambda i, j: (_mod_row(i), 0, 0)),
            pl.BlockSpec((1, D_MODEL), lambda i, j: (0, 0)),
            pl.BlockSpec((D_MODEL, FFN_TF), lambda i, j: (0, j)),
            pl.BlockSpec((D_MODEL, FFN_TF), lambda i, j: (0, j + nf)),
            pl.BlockSpec((3, FFN_TF), lambda i, j: (0, j)),
            pl.BlockSpec((3, FFN_TF), lambda i, j: (0, j + nf)),
            pl.BlockSpec((1, FFN_TF), lambda i, j: (0, j)),
            pl.BlockSpec((1, FFN_TF), lambda i, j: (0, j + nf)),
            pl.BlockSpec((FFN_TF, D_MODEL), lambda i, j: (j, 0)),
        ],
        out_specs=pl.BlockSpec((TM, D_MODEL), lambda i, j: (i, 0)),
        out_shape=jax.ShapeDtypeStruct((N_TOK, D_MODEL), F32),
        scratch_shapes=[pltpu.VMEM((TM, D_MODEL), BF16), pltpu.VMEM((TM, D_MODEL), F32)],
        compiler_params=_params("arbitrary", "arbitrary"),
        name="conv_ffn",
    )(x, mod, nw.reshape(1, D_MODEL), w_up, w_up, conv_w, conv_w, conv_b, conv_b, w_down)


ML_TN = 512
ML_EXT = ML_DV + LANES


def _ml_up_kernel(x_ref, mod_ref, nw_ref, wm_ref, wz_ref, cw_ref, cb_ref, xm_ref, xc_ref, z_ref, h_scr):
    i = pl.program_id(0)

    @pl.when(pl.program_id(1) == 0)
    def _():
        h = _norm_mod(x_ref[...], nw_ref[...], mod_ref[0:1, :], mod_ref[1:2, :])
        h_scr[...] = h.astype(BF16)

    def column_block(seq):
        h = h_scr[...]
        xm = jnp.dot(h, wm_ref[...], preferred_element_type=F32)
        xm_ref[...] = xm.astype(BF16)
        xc_ref[...] = _silu(_dwconv_rows(xm, cw_ref[...], cb_ref[...], seq)).astype(BF16)
        z_ref[...] = jnp.dot(h, wz_ref[...], preferred_element_type=F32).astype(BF16)

    _by_tile_kind(i, column_block)


def _ml_up(x, mod, nw, w_up, conv_w, conv_b):
    nj = ML_D_IN // ML_TN
    col = pl.BlockSpec((TM, ML_TN), lambda i, j: (i, j))
    shp = jax.ShapeDtypeStruct((N_TOK, ML_D_IN), BF16)
    return pl.pallas_call(
        _ml_up_kernel,
        grid=(N_TOK // TM, nj),
        in_specs=[
            pl.BlockSpec((TM, D_MODEL), lambda i, j: (i, 0)),
            pl.BlockSpec((None, 6, D_MODEL), lambda i, j: (_mod_row(i), 0, 0)),
            pl.BlockSpec((1, D_MODEL), lambda i, j: (0, 0)),
            pl.BlockSpec((D_MODEL, ML_TN), lambda i, j: (0, j)),
            pl.BlockSpec((D_MODEL, ML_TN), lambda i, j: (0, j + nj)),
            pl.BlockSpec((3, ML_TN), lambda i, j: (0, j)),
            pl.BlockSpec((1, ML_TN), lambda i, j: (0, j)),
        ],
        out_specs=[col, col, col],
        out_shape=[shp, shp, shp],
        scratch_shapes=[pltpu.VMEM((TM, D_MODEL), BF16)],
        compiler_params=_params("arbitrary", "arbitrary"),
        name="ml_up",
    )(x, mod, nw.reshape(1, D_MODEL), w_up, w_up, conv_w, conv_b.reshape(1, ML_D_IN))


def _log_sigmoid(x):
    return jnp.minimum(x, 0.0) - jnp.log(1.0 + jnp.exp(-jnp.abs(x)))


def _ml_scan_kernel(*refs, seq, has_init, out_state, n_unused_inputs):
    q_ref, kt_ref, v_ref, g_ref, gt_ref, xc_ref, z_ref, nw_ref, skip_ref = refs[:9]
    pos = 9
    if has_init:
        c0_ref, n0_ref, m0_ref = refs[pos:pos + 3]
        pos += 3
    pos += n_unused_inputs
    o_ref = refs[pos]
    pos += 1
    if out_state:
        cout_ref, nout_ref, mout_ref = refs[pos:pos + 3]
        pos += 3
    hs_scr, c_scr = refs[pos:pos + 2]

    L = ML_CHUNK
    nc = seq // L
    ext_tiles = ML_EXT // LANES
    row_i = lax.broadcasted_iota(jnp.int32, (L, L), 0)
    col_i = lax.broadcasted_iota(jnp.int32, (L, L), 1)
    ones_tile = jnp.ones((L, LANES), BF16)

    def lanes(a, n):
        return jnp.concatenate([a] * n, axis=1)

    for d in (0, 1):
        keep = (col_i <= row_i) if d == 0 else (col_i >= row_i)
        keep_f = keep.astype(F32)
        keep_t_f = ((row_i <= col_i) if d == 0 else (row_i >= col_i)).astype(F32)
        end = L - 1 if d == 0 else 0

        if has_init:
            c_scr[:, :ML_DV] = c0_ref[d]
            c_scr[:, ML_DV:] = n0_ref[d]
            m = m0_ref[d:d + 1, :]
        else:
            c_scr[...] = jnp.zeros_like(c_scr)
            m = jnp.zeros((1, LANES), F32)

        for c in range(nc):
            cc = c if d == 0 else nc - 1 - c
            rows = slice(cc * L, (cc + 1) * L)
            qc = q_ref[rows, :]
            ktc = kt_ref[cc]
            v_ext = jnp.concatenate([v_ref[rows, :], ones_tile], axis=1)
            gcol = g_ref[cc]
            grow = gt_ref[cc]
            ig_row = grow[2 * d:2 * d + 1, :]
            lf_col = _log_sigmoid(gcol[:, 2 * d + 1:2 * d + 2])
            lf_row = _log_sigmoid(grow[2 * d + 1:2 * d + 2, :])
            b_col = jnp.broadcast_to(jnp.sum(keep_f * lf_row, axis=1, keepdims=True), (L, LANES))
            b_row = jnp.sum(keep_t_f * lf_col, axis=0, keepdims=True)
            dmat = jnp.where(keep, b_col - b_row + ig_row, NEG_INF)
            m_loc = jnp.broadcast_to(jnp.max(dmat, axis=1, keepdims=True), (L, LANES))
            p_loc = jnp.exp(dmat - m_loc)
            s_loc = jnp.dot(qc, ktc, preferred_element_type=F32) * p_loc
            intra = jnp.dot(s_loc.astype(BF16), v_ext, preferred_element_type=F32)

            m_t = jnp.maximum(b_col + m, m_loc)
            w_inter = jnp.exp(b_col + m - m_t)
            w_intra = jnp.exp(m_loc - m_t)
            inter = jnp.dot(qc, c_scr[...].astype(BF16), preferred_element_type=F32)
            hx = lanes(w_inter, ext_tiles) * inter + lanes(w_intra, ext_tiles) * intra
            inv = 1.0 / jnp.maximum(jnp.abs(hx[:, ML_DV:]), jnp.exp(-m_t))
            h = hx[:, :ML_DV] * lanes(inv, ML_DV // LANES)

            b_end = b_col[end:end + 1, :]
            m_loc_end = m_loc[end:end + 1, :]
            m_new = jnp.maximum(b_end + m, m_loc_end)
            w_c = jnp.exp(b_end + m - m_new)
            w_s = p_loc[end:end + 1, :] * jnp.exp(m_loc_end - m_new)
            upd = jnp.dot((ktc.astype(F32) * w_s).astype(BF16), v_ext, preferred_element_type=F32)
            c_scr[...] = lanes(w_c, ext_tiles) * c_scr[...] + upd
            m = m_new

            if d == 0:
                hs_scr[rows, :] = h
            else:
                hs = hs_scr[rows, :] + h
                mu = jnp.mean(hs, axis=1, keepdims=True)
                cen = hs - mu
                var = jnp.mean(cen * cen, axis=1, keepdims=True)
                hn = cen * lax.rsqrt(var + NORM_EPS) * nw_ref[...]
                xc = xc_ref[rows, :].astype(F32)
                z = z_ref[rows, :].astype(F32)
                o_ref[rows, :] = ((hn + skip_ref[...] * xc) * _silu(z)).astype(o_ref.dtype)

        if out_state:
            cout_ref[d] = c_scr[:, :ML_DV]
            nout_ref[d] = c_scr[:, ML_DV:]
            mout_ref[d:d + 1, :] = m


def _ml_scan(q, kt, v, g4, g4t, xc, z, norm_w, skip, *, seq, nb, row0, init=None, final_c=None):
    L = ML_CHUNK
    nc = seq // L
    rb = row0 // seq
    has_init = init is not None
    in_specs = [
        pl.BlockSpec((seq, ML_DK), lambda b, h: (rb + b, h)),
        pl.BlockSpec((None, nc, ML_DK, L), lambda b, h: (h, rb + b, 0, 0)),
        pl.BlockSpec((seq, ML_DV), lambda b, h: (rb + b, h)),
        pl.BlockSpec((None, nc, L, 4), lambda b, h: (h, rb + b, 0, 0)),
        pl.BlockSpec((None, nc, 4, L), lambda b, h: (h, rb + b, 0, 0)),
        pl.BlockSpec((seq, ML_DV), lambda b, h: (rb + b, h)),
        pl.BlockSpec((seq, ML_DV), lambda b, h: (rb + b, h)),
        pl.BlockSpec((1, ML_DV), lambda b, h: (0, h)),
        pl.BlockSpec((1, ML_DV), lambda b, h: (0, h)),
    ]
    args = [q, kt, v, g4, g4t, xc, z, norm_w.reshape(1, ML_D_IN), skip.reshape(1, ML_D_IN)]
    state_n = pl.BlockSpec((None, None, 2, ML_DK, LANES), lambda b, h: (b, h, 0, 0, 0))
    state_m = pl.BlockSpec((None, None, 2, LANES), lambda b, h: (b, h, 0, 0))
    if has_init:
        c0, layer, n0, m0 = init
        in_specs += [pl.BlockSpec((None, None, 2, None, ML_DK, ML_DV), lambda b, h: (b, layer, 0, h, 0, 0)),
                     state_n, state_m]
        args += [c0, n0, m0]
    out_specs = [pl.BlockSpec((seq, ML_DV), lambda b, h: (b, h))]
    out_shape = [jax.ShapeDtypeStruct((nb * seq, ML_D_IN), BF16)]
    aliases = {}
    if not has_init:
        layer_out, c_all = final_c
        out_specs += [pl.BlockSpec((None, None, 2, None, ML_DK, ML_DV), lambda b, h: (b, layer_out, 0, h, 0, 0)),
                      state_n, state_m]
        out_shape += [
            jax.ShapeDtypeStruct((nb, N_ML_LAYERS, 2, ML_HEADS, ML_DK, ML_DV), F32),
            jax.ShapeDtypeStruct((nb, ML_HEADS, 2, ML_DK, LANES), F32),
            jax.ShapeDtypeStruct((nb, ML_HEADS, 2, LANES), F32),
        ]
        if c_all is not None:
            in_specs.append(pl.BlockSpec(memory_space=pl.ANY))
            args.append(c_all)
            aliases = {len(args) - 1: 1}
    return pl.pallas_call(
        functools.partial(_ml_scan_kernel, seq=seq, has_init=has_init, out_state=not has_init,
                          n_unused_inputs=len(aliases)),
        grid=(nb, ML_HEADS),
        in_specs=in_specs,
        out_specs=out_specs,
        out_shape=out_shape,
        scratch_shapes=[pltpu.VMEM((seq, ML_DV), F32), pltpu.VMEM((ML_DK, ML_EXT), F32)],
        input_output_aliases=aliases,
        compiler_params=_params("arbitrary", "arbitrary"),
        name="ml_scan_init" if has_init else "ml_scan_zero",
    )(*args)


def _ml_qk_kernel(xc_ref, w_ref, q_ref, kt_ref):
    acc = jnp.dot(xc_ref[...], w_ref[...], preferred_element_type=F32)
    j = pl.program_id(1)

    @pl.when(j == 0)
    def _():
        q_ref[...] = (acc * ML_DK ** -0.5).astype(BF16)

    @pl.when(j == 1)
    def _():
        for h in range(ML_HEADS):
            kt = acc[:, h * ML_DK:(h + 1) * ML_DK].T
            for c in range(TM // ML_CHUNK):
                kt_ref[h, c] = kt[:, c * ML_CHUNK:(c + 1) * ML_CHUNK].astype(BF16)


def _ml_qk(xc, w_qk):
    n = ML_HEADS * ML_DK
    cpt = TM // ML_CHUNK
    return pl.pallas_call(
        _ml_qk_kernel,
        grid=(N_TOK // TM, 2),
        in_specs=[pl.BlockSpec((TM, ML_D_IN), lambda i, j: (i, 0)),
                  pl.BlockSpec((ML_D_IN, n), lambda i, j: (0, j))],
        out_specs=[pl.BlockSpec((TM, n), lambda i, j: (i, 0)),
                   pl.BlockSpec((ML_HEADS, cpt, ML_DK, ML_CHUNK), lambda i, j: (0, i, 0, 0))],
        out_shape=[jax.ShapeDtypeStruct((N_TOK, n), BF16),
                   jax.ShapeDtypeStruct((ML_HEADS, N_TOK // ML_CHUNK, ML_DK, ML_CHUNK), BF16)],
        compiler_params=_params("arbitrary", "arbitrary"),
        name="ml_qk",
    )(xc, w_qk)


def _mlstm_layer(x, mod, nw, p, state, final_c):
    xm, xc, z = _ml_up(x, mod, nw, p["w_up"], p["conv_w"], p["conv_b"])
    q, kt = _ml_qk(xc, p["w_qk"])
    v = _mm(xm, p["w_v"], jnp.zeros((1, ML_D_IN), F32), 1024, BF16)
    g = _mm(xm, p["w_gate"], p["b_gate"], 128, F32)[:, :4 * ML_HEADS]
    L = ML_CHUNK
    g4 = g.reshape(N_TOK // L, L, 4, ML_HEADS).transpose(3, 0, 1, 2)
    g4t = g4.transpose(0, 1, 3, 2)
    oc, c_fin, n_fin, m_fin = _ml_scan(q, kt, v, g4, g4t, xc, z, p["norm_w"], p["skip"],
                                       seq=SEQ, nb=BATCH, row0=0, final_c=final_c)
    (ol,) = _ml_scan(q, kt, v, g4, g4t, xc, z, p["norm_w"], p["skip"],
                     seq=DEC_SEQ, nb=DEC_BATCH, row0=N_CTX, init=state)
    x = _mm_res(oc, ol, p["w_down"], x, mod)
    return x, c_fin, n_fin[..., 0].transpose(0, 2, 1, 3), m_fin[..., 0].transpose(0, 2, 1)


def _softmax_parts(scores, sink_col):
    m = functools.reduce(jnp.maximum, [jnp.max(s, axis=1, keepdims=True) for s in scores])
    if sink_col is not None:
        m = jnp.maximum(m, sink_col)
    ps = [jnp.exp(s - m) for s in scores]
    den = functools.reduce(jnp.add, [jnp.sum(p, axis=1, keepdims=True) for p in ps])
    if sink_col is not None:
        den = den + jnp.exp(sink_col - m)
    return ps, den


def _qk(q, k):
    return lax.dot_general(q, k, (((1,), (1,)), ((), ())), preferred_element_type=F32) * (HEAD_DIM ** -0.5)


def _attend_tiles(q_tiles, parts, sinks):
    r = q_tiles[0].shape[0]
    lo = lax.broadcasted_iota(jnp.int32, (r, LANES), 1) < HEAD_DIM
    zero = jnp.zeros((r, LANES), BF16)
    qs = jnp.concatenate([jnp.where(sel, t, zero) for t in q_tiles for sel in (lo, jnp.logical_not(lo))], axis=0)
    scores = [post(_qk(qs, k2)) for k2, _, post in parts]
    sink_col = None
    if sinks is not None:
        rows = lax.broadcasted_iota(jnp.int32, (qs.shape[0], 1), 0)
        sink_col = jnp.full((qs.shape[0], 1), sinks[0], F32)
        for hi in range(1, len(sinks)):
            sink_col = jnp.where(rows >= hi * r, sinks[hi], sink_col)
    ps, den = _softmax_parts(scores, sink_col)
    o = functools.reduce(jnp.add, [jnp.dot(p.astype(BF16), v2, preferred_element_type=F32)
                                   for p, (_, v2, _) in zip(ps, parts)]) / den
    return [jnp.where(lo, o[2 * a * r:(2 * a + 1) * r], o[(2 * a + 1) * r:(2 * a + 2) * r])
            for a in range(len(q_tiles))]


def _identity(s):
    return s


def _ctx_attn_kernel(*refs, tiles_per_kv, has_sink):
    if has_sink:
        sink_ref, q_ref, k_ref, v_ref, o_ref = refs
    else:
        q_ref, k_ref, v_ref, o_ref = refs
    for t in range(k_ref.shape[1] // LANES):
        k2 = k_ref[:, t * LANES:(t + 1) * LANES].astype(BF16)
        v2 = v_ref[:, t * LANES:(t + 1) * LANES].astype(BF16)
        first = t * tiles_per_kv
        q_tiles = [q_ref[:, (first + a) * LANES:(first + a + 1) * LANES] for a in range(tiles_per_kv)]
        sinks = [sink_ref[2 * first + hi] for hi in range(2 * tiles_per_kv)] if has_sink else None
        outs = _attend_tiles(q_tiles, [(k2, v2, _identity)], sinks)
        for a, o in enumerate(outs):
            o_ref[:, (first + a) * LANES:(first + a + 1) * LANES] = o.astype(o_ref.dtype)


def _ctx_attn(q, kv, n_kv_cols, sink):
    nq = q.shape[1]
    has_sink = sink is not None
    in_specs = [pl.BlockSpec((SEQ, nq), lambda b: (b, 0)),
                pl.BlockSpec((SEQ, n_kv_cols), lambda b: (b, 0)),
                pl.BlockSpec((SEQ, n_kv_cols), lambda b: (b, 1))]
    args = [q, kv, kv]
    if has_sink:
        in_specs = [pl.BlockSpec(memory_space=pltpu.SMEM)] + in_specs
        args = [sink] + args
    return pl.pallas_call(
        functools.partial(_ctx_attn_kernel, tiles_per_kv=nq // n_kv_cols, has_sink=has_sink),
        grid=(BATCH,),
        in_specs=in_specs,
        out_specs=pl.BlockSpec((SEQ, nq), lambda b: (b, 0)),
        out_shape=jax.ShapeDtypeStruct((N_CTX, nq), BF16),
        compiler_params=_params("arbitrary"),
        name="ctx_attn",
    )(*args)


SWA_SPAN = Q_BLOCK + 2 * SWA_WINDOW
SWA_KV_COLS = SWA_KV * LANES


def _swa_lat_kernel(sink_ref, q_ref, k_ref, v_ref, kc_ref, vc_ref, o_ref):
    j = pl.program_id(1)
    start = pl.multiple_of(jnp.clip((j - 1) * Q_BLOCK, 0, DEC_SEQ - SWA_SPAN), Q_BLOCK)
    rows = 4 * Q_BLOCK
    qpos = j * Q_BLOCK + (lax.broadcasted_iota(jnp.int32, (rows, SWA_SPAN), 0) & (Q_BLOCK - 1))
    kpos = start + lax.broadcasted_iota(jnp.int32, (rows, SWA_SPAN), 1)
    in_window = jnp.abs(qpos - kpos) <= SWA_WINDOW

    def window(s):
        return jnp.where(in_window, s, NEG_INF)

    for t in range(SWA_KV):
        cols = slice(t * LANES, (t + 1) * LANES)
        k_loc = k_ref[pl.ds(start, SWA_SPAN), cols].astype(BF16)
        v_loc = v_ref[pl.ds(start, SWA_SPAN), cols].astype(BF16)
        q_tiles = [q_ref[:, (2 * t + a) * LANES:(2 * t + a + 1) * LANES] for a in range(2)]
        sinks = [sink_ref[4 * t + hi] for hi in range(4)]
        outs = _attend_tiles(q_tiles, [(k_loc, v_loc, window), (kc_ref[:, cols], vc_ref[:, cols], _identity)], sinks)
        for a, o in enumerate(outs):
            o_ref[:, (2 * t + a) * LANES:(2 * t + a + 1) * LANES] = o.astype(o_ref.dtype)


def _swa_latent(q, kv, kc, vc, sink):
    nq = q.shape[1]
    rb = N_CTX // DEC_SEQ
    qb = N_CTX // Q_BLOCK
    nj = DEC_SEQ // Q_BLOCK
    cspec = pl.BlockSpec((None, SEQ, SWA_KV_COLS), lambda b, j: (b, 0, 0))
    return pl.pallas_call(
        _swa_lat_kernel,
        grid=(DEC_BATCH, nj),
        in_specs=[pl.BlockSpec(memory_space=pltpu.SMEM),
                  pl.BlockSpec((Q_BLOCK, nq), lambda b, j: (qb + b * nj + j, 0)),
                  pl.BlockSpec((DEC_SEQ, SWA_KV_COLS), lambda b, j: (rb + b, 0)),
                  pl.BlockSpec((DEC_SEQ, SWA_KV_COLS), lambda b, j: (rb + b, 1)),
                  cspec, cspec],
        out_specs=pl.BlockSpec((Q_BLOCK, nq), lambda b, j: (b * nj + j, 0)),
        out_shape=jax.ShapeDtypeStruct((N_LAT, nq), BF16),
        compiler_params=_params("arbitrary", "arbitrary"),
        name="swa_latent",
    )(sink, q, kv, kv, kc, vc)


NA_QT = 256
NA_SPAN = 768


def _na_start(j):
    return (j // 2) * (DEC_SEQ - NA_SPAN)


NA_ROWS = DEC_SEQ // GRID_W
NA_DR = 2 * NA_KH - 1
NA_DC = 2 * NA_KW - 1


def _na_blocks_kernel(rpb_ref, onehot_ref, valid_ref, o_ref):
    t = jnp.dot(rpb_ref[...], onehot_ref[...], preferred_element_type=F32, precision=lax.Precision.HIGHEST)
    o_ref[...] = jnp.where(valid_ref[...] > 0.5, t, NEG_INF)


def _na_bias_blocks(rpb):
    h = rpb.shape[0]
    kpad = 32
    cq, ck = np.meshgrid(np.arange(GRID_W), np.arange(GRID_W), indexing="ij")
    dc = (np.clip(ck - cq, -(NA_KW - 1), NA_KW - 1) + NA_KW - 1).reshape(-1)
    cs = np.clip(cq - NA_KW // 2, 0, GRID_W - NA_KW)
    valid = ((ck >= cs) & (ck < cs + NA_KW)).reshape(1, -1).astype(np.float32)
    onehot = (np.arange(kpad)[:, None] == dc[None, :]).astype(np.float32)
    rpb2 = jnp.pad(rpb.reshape(h * NA_DR, NA_DC), ((0, 0), (0, kpad - NA_DC)))
    n = GRID_W * GRID_W
    blocks = pl.pallas_call(
        _na_blocks_kernel,
        grid=(1,),
        in_specs=[pl.BlockSpec((h * NA_DR, kpad), lambda i: (0, 0)),
                  pl.BlockSpec((kpad, n), lambda i: (0, 0)),
                  pl.BlockSpec((1, n), lambda i: (0, 0))],
        out_specs=pl.BlockSpec((h * NA_DR, n), lambda i: (0, 0)),
        out_shape=jax.ShapeDtypeStruct((h * NA_DR, n), F32),
        compiler_params=_params("arbitrary"),
        name="na_bias_blocks",
    )(rpb2, jnp.asarray(onehot), jnp.asarray(valid))
    blocks = blocks.reshape(h, NA_DR, GRID_W, GRID_W)
    padded = jnp.pad(blocks, ((0, 0), (1, 1), (0, 0), (0, 0)), constant_values=NEG_INF)
    return jnp.concatenate([padded[:, :-1], padded[:, 1:]], axis=-1)


def _na_lat_kernel(q_ref, k_ref, v_ref, kc_ref, vc_ref, blk_ref, o_ref, bias_scr):
    j = pl.program_id(0)
    start = pl.multiple_of(_na_start(j), 256)

    @pl.when(pl.program_id(2) == 0)
    def _():
        lane_lo = lax.broadcasted_iota(jnp.int32, (GRID_W, LANES), 1) < GRID_W
        for rq_l in range(NA_QT // GRID_W):
            rq = j * (NA_QT // GRID_W) + rq_l
            rs = jnp.clip(rq - NA_KH // 2, 0, NA_ROWS - NA_KH)
            for kp in range(NA_SPAN // LANES):
                rk = start // GRID_W + 2 * kp
                idx = jnp.clip(rk - rq + NA_KH, 0, NA_DR)
                in_band = [jnp.logical_and(r >= rs, r < rs + NA_KH).astype(jnp.int32) for r in (rk, rk + 1)]
                ok = jnp.where(lane_lo, in_band[0], in_band[1]) > 0
                for hh in range(2):
                    bias_scr[hh * NA_QT + rq_l * GRID_W:hh * NA_QT + (rq_l + 1) * GRID_W,
                             kp * LANES:(kp + 1) * LANES] = jnp.where(ok, blk_ref[hh, idx], NEG_INF)

    def add_bias(s):
        return s + bias_scr[...]

    k_loc = k_ref[pl.ds(start, NA_SPAN), :].astype(BF16)
    v_loc = v_ref[pl.ds(start, NA_SPAN), :].astype(BF16)
    (o,) = _attend_tiles([q_ref[...]], [(k_loc, v_loc, add_bias), (kc_ref[...], vc_ref[...], _identity)], None)
    o_ref[...] = o.astype(o_ref.dtype)


def _na_latent(q, kv, kc, vc, blocks):
    nq = q.shape[1]
    n_tiles = nq // LANES
    nj = DEC_SEQ // NA_QT
    rb = N_CTX // DEC_SEQ
    qb = N_CTX // NA_QT
    cspec = pl.BlockSpec((None, SEQ, LANES), lambda j, p, b: (b, 0, p))
    return pl.pallas_call(
        _na_lat_kernel,
        grid=(nj, n_tiles, DEC_BATCH),
        in_specs=[pl.BlockSpec((NA_QT, LANES), lambda j, p, b: (qb + b * nj + j, p)),
                  pl.BlockSpec((DEC_SEQ, LANES), lambda j, p, b: (rb + b, p)),
                  pl.BlockSpec((DEC_SEQ, LANES), lambda j, p, b: (rb + b, n_tiles + p)),
                  cspec, cspec,
                  pl.BlockSpec((2, NA_DR + 1, GRID_W, LANES), lambda j, p, b: (p, 0, 0, 0))],
        out_specs=pl.BlockSpec((NA_QT, LANES), lambda j, p, b: (b * nj + j, p)),
        out_shape=jax.ShapeDtypeStruct((N_LAT, nq), BF16),
        scratch_shapes=[pltpu.VMEM((2 * NA_QT, NA_SPAN), F32)],
        compiler_params=_params("arbitrary", "arbitrary", "arbitrary"),
        name="na_latent",
    )(q, kv, kv, kc, vc, blocks)


def _rope_tables(width):
    quarter = HEAD_DIM // 4
    pos = np.arange(DEC_SEQ)
    inv = np.power(ROPE_BASE, -np.arange(quarter, dtype=np.float32) / quarter).astype(np.float32)
    d = np.arange(width) % HEAD_DIM
    p = np.where((d < HEAD_DIM // 2)[None, :], (pos // GRID_W)[:, None], (pos % GRID_W)[:, None]).astype(np.float32)
    ang = p * inv[d % quarter][None, :]
    sign = np.where((d // quarter) % 2 == 0, -1.0, 1.0)[None, :]
    return jnp.asarray(np.cos(ang), F32), jnp.asarray(np.sin(ang) * sign, F32)


def _cache_rows(cache, dup):
    b, h, s, hd = cache.shape
    rows = jnp.broadcast_to(cache.transpose(0, 2, 1, 3)[:, :, :, None, :], (b, s, h, dup, hd))
    return rows.reshape(b, s, h * dup * hd).astype(BF16)


def _swa_layer(x, mod, nw, w_qkv, sink, w_o, cache_k, cache_v):
    nq, nk = SWA_HEADS * HEAD_DIM, SWA_KV * HEAD_DIM

    def dup_heads(w):
        return jnp.broadcast_to(w.reshape(D_MODEL, SWA_KV, 1, HEAD_DIM),
                                (D_MODEL, SWA_KV, 2, HEAD_DIM)).reshape(D_MODEL, SWA_KV_COLS)

    w = jnp.concatenate([w_qkv[:, :nq], dup_heads(w_qkv[:, nq:nq + nk]), dup_heads(w_qkv[:, nq + nk:])], axis=1)
    q, kv = _qkv_proj(x, mod, nw, w.astype(BF16), nq, rope_cols=nq + SWA_KV_COLS)
    kv_ctx = kv[:N_CTX].reshape(BATCH, SEQ, 2, SWA_KV, 2, HEAD_DIM)[:, :, :, :, 0]
    k_new = kv_ctx[:, :, 0].transpose(0, 2, 1, 3)
    v_new = kv_ctx[:, :, 1].transpose(0, 2, 1, 3)
    oc = _ctx_attn(q, kv, SWA_KV_COLS, sink)
    ol = _swa_latent(q, kv, _cache_rows(cache_k, 2), _cache_rows(cache_v, 2), sink)
    return _mm_res(oc, ol, w_o, x, mod), k_new, v_new


def _na_layer(x, mod, nw, w_qkv, rpb, w_o, cache_k, cache_v):
    n = NA_HEADS * HEAD_DIM
    q, kv = _qkv_proj(x, mod, nw, w_qkv, n)
    kv_ctx = kv[:N_CTX].reshape(BATCH, SEQ, 2, NA_HEADS, HEAD_DIM)
    k_new = kv_ctx[:, :, 0].transpose(0, 2, 1, 3)
    v_new = kv_ctx[:, :, 1].transpose(0, 2, 1, 3)
    oc = _ctx_attn(q, kv, n, None)
    ol = _na_latent(q, kv, _cache_rows(cache_k, 1), _cache_rows(cache_v, 1), _na_bias_blocks(rpb))
    return _mm_res(oc, ol, w_o, x, mod), k_new, v_new


def _final_norm_kernel(x_ref, w_ref, o_ref):
    x = x_ref[...]
    o_ref[...] = x * lax.rsqrt(jnp.mean(x * x, axis=-1, keepdims=True) + NORM_EPS) * w_ref[...]


def _final_norm(x, w):
    return pl.pallas_call(
        _final_norm_kernel,
        grid=(N_TOK // TM,),
        in_specs=[pl.BlockSpec((TM, D_MODEL), lambda i: (i, 0)), pl.BlockSpec((1, D_MODEL), lambda i: (0, 0))],
        out_specs=pl.BlockSpec((TM, D_MODEL), lambda i: (i, 0)),
        out_shape=jax.ShapeDtypeStruct((N_TOK, D_MODEL), F32),
        compiler_params=_params("arbitrary"),
        name="final_norm",
    )(x, w.reshape(1, D_MODEL))


def kernel(x_prompt, x_sample, state_mlstm_C, state_mlstm_n, state_mlstm_m, cache_swa_k, cache_swa_v, cache_na_k, cache_na_v, c, c_ctx, ada_w, ada_b, norm_w, final_norm_w, ffn_w_up, ffn_conv_w, ffn_conv_b, ffn_w_down, ml_w_up, ml_conv_w, ml_conv_b, ml_w_qk, ml_w_v, ml_w_gate, ml_b_gate, ml_norm_w, ml_skip, ml_w_down, swa_w_qkv, swa_sink, swa_w_o, na_w_qkv, na_rpb, na_w_o):
    x = jnp.concatenate([x_prompt.reshape(N_CTX, D_MODEL), x_sample.reshape(N_LAT, D_MODEL)], axis=0)
    cond = jnp.concatenate([c_ctx[None], c, jnp.zeros((MOD_ROWS - 1 - DEC_BATCH, D_MODEL), F32)], axis=0)
    mods = _ada_mod(cond, ada_w, ada_b)

    new_c, new_n, new_m = None, [], []
    new_sk = new_sv = new_nk = new_nv = None
    for i in range(DEPTH):
        kind, j = i % N_MIXERS, i // N_MIXERS
        mod = mods[i]
        if kind == 0:
            gate_w = jnp.pad(ml_w_gate[j], ((0, 0), (0, 128 - 4 * ML_HEADS))).astype(BF16)
            gate_b = jnp.pad(ml_b_gate[j], (0, 128 - 4 * ML_HEADS)).reshape(1, 128)
            p = dict(w_up=ml_w_up[j].astype(BF16), conv_w=ml_conv_w[j], conv_b=ml_conv_b[j],
                     w_qk=ml_w_qk[j].astype(BF16), w_v=ml_w_v[j].astype(BF16), w_gate=gate_w, b_gate=gate_b,
                     norm_w=ml_norm_w[j], skip=ml_skip[j], w_down=ml_w_down[j].astype(BF16))
            n0 = jnp.broadcast_to(state_mlstm_n[:, j].transpose(0, 2, 1, 3)[..., None],
                                  (DEC_BATCH, ML_HEADS, 2, ML_DK, LANES))
            m0 = jnp.broadcast_to(state_mlstm_m[:, j].transpose(0, 2, 1)[..., None], (DEC_BATCH, ML_HEADS, 2, LANES))
            x, new_c, nf, mf = _mlstm_layer(x, mod, norm_w[i, 0], p, (state_mlstm_C, j, n0, m0), (j, new_c))
            new_n.append(nf)
            new_m.append(mf)
        elif kind == 1:
            x, k_new, v_new = _swa_layer(x, mod, norm_w[i, 0], swa_w_qkv[j].astype(BF16), swa_sink[j],
                                         swa_w_o[j].astype(BF16), cache_swa_k[:, j], cache_swa_v[:, j])
            new_sk, new_sv = k_new[:, None], v_new[:, None]
        else:
            x, k_new, v_new = _na_layer(x, mod, norm_w[i, 0], na_w_qkv[j].astype(BF16), na_rpb[j],
                                        na_w_o[j].astype(BF16), cache_na_k[:, j], cache_na_v[:, j])
            new_nk, new_nv = k_new[:, None], v_new[:, None]
        x = _conv_ffn(x, mod, norm_w[i, 1], ffn_w_up[i].astype(BF16), ffn_conv_w[i], ffn_conv_b[i],
                      ffn_w_down[i].astype(BF16))

    y = _final_norm(x, final_norm_w)
    return (y[:N_CTX].reshape(BATCH, SEQ, D_MODEL), y[N_CTX:].reshape(DEC_BATCH, DEC_SEQ, D_MODEL),
            new_c, jnp.stack(new_n, axis=1), jnp.stack(new_m, axis=1),
            new_sk, new_sv, new_nk, new_nv)
```

```python
import functools

import jax
import jax.numpy as jnp
import numpy as np
from jax import lax
from jax.experimental import pallas as pl
from jax.experimental.pallas import tpu as pltpu

F32 = jnp.float32
BF16 = jnp.bfloat16

D_MODEL = 1024
BATCH = 32
SEQ = 256
DEPTH = 4
DEC_BATCH = 8
DEC_SEQ = 1024
GRID_W = 64
N_MIXERS = 3
N_ML_LAYERS = (DEPTH + 2) // 3
NORM_EPS = 1e-6
D_FF = 2816
ML_D_IN = 2 * D_MODEL
ML_HEADS = 4
ML_DK = ML_D_IN // (2 * ML_HEADS)
ML_DV = ML_D_IN // ML_HEADS
ML_CHUNK = 128
HEAD_DIM = 64
SWA_HEADS = D_MODEL // HEAD_DIM
SWA_KV = SWA_HEADS // 4
SWA_WINDOW = 128
Q_BLOCK = 128
ROPE_BASE = 10000.0
NA_HEADS = D_MODEL // HEAD_DIM
NA_KH = 8
NA_KW = 16
NEG_INF = -1e30

N_CTX = BATCH * SEQ
N_LAT = DEC_BATCH * DEC_SEQ
N_TOK = N_CTX + N_LAT
TM = 1024
N_CTX_TILES = N_CTX // TM
MOD_ROWS = 16
VMEM_LIMIT_BYTES = 56 * 1024 * 1024


def _params(*sem):
    return pltpu.CompilerParams(dimension_semantics=sem, vmem_limit_bytes=VMEM_LIMIT_BYTES)


def _mod_row(i):
    return jnp.where(i < N_CTX_TILES, 0, i - (N_CTX_TILES - 1))


def _silu(x):
    return x / (1.0 + jnp.exp(-x))


def _norm_mod(x, nw, shift, scale):
    y = x * lax.rsqrt(jnp.mean(x * x, axis=-1, keepdims=True) + NORM_EPS) * nw
    return y * (1.0 + scale) + shift


SUBLANES = 8
LANES = 128


def _dwconv_rows(u, cw, cb, seq):
    r, c = u.shape
    n_groups, per_seq = r // SUBLANES, seq // SUBLANES
    g = u.reshape(n_groups, SUBLANES, c)
    sub = lax.broadcasted_iota(jnp.int32, g.shape, 1)
    down = pltpu.roll(g, 1, 1)
    up = pltpu.roll(g, SUBLANES - 1, 1)
    zero = jnp.zeros((1, SUBLANES, c), F32)
    from_prev, from_next = [], []
    for s in range(0, n_groups, per_seq):
        from_prev += [zero, down[s:s + per_seq - 1]]
        from_next += [up[s + 1:s + per_seq], zero]
    prev = jnp.where(sub == 0, jnp.concatenate(from_prev, axis=0), down)
    nxt = jnp.where(sub == SUBLANES - 1, jnp.concatenate(from_next, axis=0), up)
    out = cw[0:1, :] * prev + cw[1:2, :] * g + cw[2:3, :] * nxt + cb
    return out.reshape(r, c)


def _by_tile_kind(tile, body):
    pl.when(tile < N_CTX_TILES)(functools.partial(body, SEQ))
    pl.when(tile >= N_CTX_TILES)(functools.partial(body, DEC_SEQ))


def _ada_kernel(c_ref, w_ref, b_ref, o_ref):
    s = _silu(c_ref[...]).astype(BF16)
    o_ref[...] = jnp.dot(s, w_ref[...].astype(BF16), preferred_element_type=F32) + b_ref[...]


def _ada_mod(cond, ada_w, ada_b):
    tn = 1536
    n = 6 * D_MODEL
    out = pl.pallas_call(
        _ada_kernel,
        grid=(DEPTH, n // tn),
        in_specs=[
            pl.BlockSpec((MOD_ROWS, D_MODEL), lambda l, j: (0, 0)),
            pl.BlockSpec((None, D_MODEL, tn), lambda l, j: (l, 0, j)),
            pl.BlockSpec((None, 1, tn), lambda l, j: (l, 0, j)),
        ],
        out_specs=pl.BlockSpec((None, MOD_ROWS, tn), lambda l, j: (l, 0, j)),
        out_shape=jax.ShapeDtypeStruct((DEPTH, MOD_ROWS, n), F32),
        compiler_params=_params("arbitrary", "arbitrary"),
        name="ada_mod",
    )(cond, ada_w, ada_b.reshape(DEPTH, 1, n))
    return out.reshape(DEPTH, MOD_ROWS, 6, D_MODEL)


QKV_TN = 512


def _rotate_pairs(a, cos, sin):
    lane = lax.broadcasted_iota(jnp.int32, a.shape, 1)
    first = (lane & (HEAD_DIM // 4)) == 0
    n = a.shape[1]
    partner = jnp.where(first, pltpu.roll(a, n - HEAD_DIM // 4, 1), pltpu.roll(a, HEAD_DIM // 4, 1))
    return a * cos + partner * sin


def _qkv_kernel(*refs, q_blocks, rope_blocks):
    if rope_blocks:
        x_ref, mod_ref, nw_ref, w_ref, cos_ref, sin_ref, q_ref, kv_ref, h_scr = refs
    else:
        x_ref, mod_ref, nw_ref, w_ref, q_ref, kv_ref, h_scr = refs
    i = pl.program_id(0)
    j = pl.program_id(1)

    @pl.when(j == 0)
    def _():
        h = _norm_mod(x_ref[...], nw_ref[...], mod_ref[0:1, :], mod_ref[1:2, :])
        h_scr[...] = h.astype(BF16)

    acc = jnp.dot(h_scr[...], w_ref[...], preferred_element_type=F32)

    def emit(val):
        @pl.when(j < q_blocks)
        def _():
            q_ref[...] = val.astype(BF16)

        @pl.when(j >= q_blocks)
        def _():
            kv_ref[...] = val

    if rope_blocks:
        rotate = jnp.logical_and(i >= N_CTX_TILES, j < rope_blocks)

        @pl.when(rotate)
        def _():
            emit(_rotate_pairs(acc, cos_ref[...], sin_ref[...]))

        @pl.when(jnp.logical_not(rotate))
        def _():
            emit(acc)
    else:
        emit(acc)


def _qkv_proj(x, mod, nw, w, n_q, rope_cols=0):
    n = w.shape[1]
    q_blocks = n_q // QKV_TN
    rope_blocks = rope_cols // QKV_TN
    in_specs = [
        pl.BlockSpec((TM, D_MODEL), lambda i, j: (i, 0)),
        pl.BlockSpec((None, 6, D_MODEL), lambda i, j: (_mod_row(i), 0, 0)),
        pl.BlockSpec((1, D_MODEL), lambda i, j: (0, 0)),
        pl.BlockSpec((D_MODEL, QKV_TN), lambda i, j: (0, j)),
    ]
    args = [x, mod, nw.reshape(1, D_MODEL), w]
    if rope_blocks:
        tab = pl.BlockSpec((DEC_SEQ, QKV_TN), lambda i, j: (0, 0))
        in_specs += [tab, tab]
        args += list(_rope_tables(QKV_TN))
    return pl.pallas_call(
        functools.partial(_qkv_kernel, q_blocks=q_blocks, rope_blocks=rope_blocks),
        grid=(N_TOK // TM, n // QKV_TN),
        in_specs=in_specs,
        out_specs=[pl.BlockSpec((TM, QKV_TN), lambda i, j: (i, jnp.minimum(j, q_blocks - 1))),
                   pl.BlockSpec((TM, QKV_TN), lambda i, j: (i, jnp.maximum(j - q_blocks, 0)))],
        out_shape=[jax.ShapeDtypeStruct((N_TOK, n_q), BF16), jax.ShapeDtypeStruct((N_TOK, n - n_q), F32)],
        scratch_shapes=[pltpu.VMEM((TM, D_MODEL), BF16)],
        compiler_params=_params("arbitrary", "arbitrary"),
        name="qkv_proj",
    )(*args)


def _mm_kernel(a_ref, w_ref, b_ref, o_ref):
    acc = jnp.dot(a_ref[...], w_ref[...], preferred_element_type=F32)
    o_ref[...] = (acc + b_ref[...]).astype(o_ref.dtype)


def _mm(a, w, bias, tn, out_dtype):
    m, k = a.shape
    n = w.shape[1]
    return pl.pallas_call(
        _mm_kernel,
        grid=(m // TM, n // tn),
        in_specs=[
            pl.BlockSpec((TM, k), lambda i, j: (i, 0)),
            pl.BlockSpec((k, tn), lambda i, j: (0, j)),
            pl.BlockSpec((1, tn), lambda i, j: (0, j)),
        ],
        out_specs=pl.BlockSpec((TM, tn), lambda i, j: (i, j)),
        out_shape=jax.ShapeDtypeStruct((m, n), out_dtype),
        compiler_params=_params("arbitrary", "arbitrary"),
        name="mm",
    )(a, w, bias)


def _mm_res_kernel(ac_ref, al_ref, w_ref, x_ref, mod_ref, o_ref):
    def emit(a_ref):
        acc = jnp.dot(a_ref[...], w_ref[...], preferred_element_type=F32)
        o_ref[...] = x_ref[...] + mod_ref[2:3, :] * acc

    is_ctx = pl.program_id(0) < N_CTX_TILES
    pl.when(is_ctx)(functools.partial(emit, ac_ref))
    pl.when(jnp.logical_not(is_ctx))(functools.partial(emit, al_ref))


def _mm_res(a_ctx, a_lat, w, x, mod):
    k = a_ctx.shape[1]
    return pl.pallas_call(
        _mm_res_kernel,
        grid=(N_TOK // TM,),
        in_specs=[
            pl.BlockSpec((TM, k), lambda i: (jnp.minimum(i, N_CTX_TILES - 1), 0)),
            pl.BlockSpec((TM, k), lambda i: (jnp.maximum(i - N_CTX_TILES, 0), 0)),
            pl.BlockSpec((k, D_MODEL), lambda i: (0, 0)),
            pl.BlockSpec((TM, D_MODEL), lambda i: (i, 0)),
            pl.BlockSpec((None, 6, D_MODEL), lambda i: (_mod_row(i), 0, 0)),
        ],
        out_specs=pl.BlockSpec((TM, D_MODEL), lambda i: (i, 0)),
        out_shape=jax.ShapeDtypeStruct((N_TOK, D_MODEL), F32),
        compiler_params=_params("arbitrary"),
        name="mm_res",
    )(a_ctx, a_lat, w, x, mod)


FFN_TF = 256
FFN_BLOCKS = D_FF // FFN_TF


def _ffn_kernel(x_ref, mod_ref, nw_ref, wup_ref, cw_ref, cb_ref, wd_ref, o_ref, h_scr, acc_scr, raw_a, raw_b):
    nb = FFN_BLOCKS
    h_scr[...] = _norm_mod(x_ref[...], nw_ref[...], mod_ref[3:4, :], mod_ref[4:5, :]).astype(BF16)
    acc_scr[...] = jnp.zeros_like(acc_scr)

    def run(seq):
        def project(k, raw):
            h = h_scr[...]
            raw[0] = jnp.dot(h, wup_ref[k], preferred_element_type=F32)
            raw[1] = jnp.dot(h, wup_ref[nb + k], preferred_element_type=F32)

        def consume(k, raw):
            g = _dwconv_rows(raw[0], cw_ref[k], cb_ref[k], seq)
            v = _dwconv_rows(raw[1], cw_ref[nb + k], cb_ref[nb + k], seq)
            a = (_silu(g) * v).astype(BF16)
            acc_scr[...] += jnp.dot(a, wd_ref[k], preferred_element_type=F32)

        project(0, raw_a)

        def two_stages(t, carry):
            k = 2 * t
            project(k + 1, raw_b)
            consume(k, raw_a)
            project(k + 2, raw_a)
            consume(k + 1, raw_b)
            return carry

        lax.fori_loop(0, (nb - 1) // 2, two_stages, 0)
        consume(nb - 1, raw_a)

    _by_tile_kind(pl.program_id(0), run)
    o_ref[...] = x_ref[...] + mod_ref[5:6, :] * acc_scr[...]


def _conv_ffn(x, mod, nw, w_up, conv_w, conv_b, w_down):
    nb = FFN_BLOCKS
    w_up = w_up.reshape(D_MODEL, 2 * nb, FFN_TF).transpose(1, 0, 2)
    conv_w = conv_w.reshape(3, 2 * nb, FFN_TF).transpose(1, 0, 2)
    conv_b = conv_b.reshape(2 * nb, 1, FFN_TF)
    w_down = w_down.reshape(nb, FFN_TF, D_MODEL)

    def resident(shape):
        return pl.BlockSpec(shape, lambda i: (0,) * len(shape), pipeline_mode=pl.Buffered(1))

    return pl.pallas_call(
        _ffn_kernel,
        grid=(N_TOK // TM,),
        in_specs=[
            pl.BlockSpec((TM, D_MODEL), lambda i: (i, 0)),
            pl.BlockSpec((None, 6, D_MODEL), lambda i: (_mod_row(i), 0, 0)),
            pl.BlockSpec((1, D_MODEL), lambda i: (0, 0)),
            resident((2 * nb, D_MODEL, FFN_TF)),
            resident((2 * nb, 3, FFN_TF)),
            resident((2 * nb, 1, FFN_TF)),
            resident((nb, FFN_TF, D_MODEL)),
        ],
        out_specs=pl.BlockSpec((TM, D_MODEL), lambda i: (i, 0)),
        out_shape=jax.ShapeDtypeStruct((N_TOK, D_MODEL), F32),
        scratch_shapes=[pltpu.VMEM((TM, D_MODEL), BF16), pltpu.VMEM((TM, D_MODEL), F32),
                        pltpu.VMEM((2, TM, FFN_TF), F32), pltpu.VMEM((2, TM, FFN_TF), F32)],
        compiler_params=_params("arbitrary"),
        name="conv_ffn",
    )(x, mod, nw.reshape(1, D_MODEL), w_up, conv_w, conv_b, w_down)


ML_TN = 512
ML_EXT = ML_DV + LANES


def _ml_up_kernel(x_ref, mod_ref, nw_ref, wm_ref, wz_ref, cw_ref, cb_ref, xm_ref, xc_ref, z_ref, h_scr, raw_scr):
    i = pl.program_id(0)

    @pl.when(pl.program_id(1) == 0)
    def _():
        h = _norm_mod(x_ref[...], nw_ref[...], mod_ref[0:1, :], mod_ref[1:2, :])
        h_scr[...] = h.astype(BF16)

    def column_block(seq):
        h = h_scr[...]
        raw_scr[...] = jnp.dot(h, wm_ref[...], preferred_element_type=F32)
        z_ref[...] = jnp.dot(h, wz_ref[...], preferred_element_type=F32).astype(BF16)
        xm = raw_scr[...]
        xm_ref[...] = xm.astype(BF16)
        xc_ref[...] = _silu(_dwconv_rows(xm, cw_ref[...], cb_ref[...], seq)).astype(BF16)

    _by_tile_kind(i, column_block)


def _ml_up(x, mod, nw, w_up, conv_w, conv_b):
    nj = ML_D_IN // ML_TN
    col = pl.BlockSpec((TM, ML_TN), lambda i, j: (i, j))
    shp = jax.ShapeDtypeStruct((N_TOK, ML_D_IN), BF16)
    return pl.pallas_call(
        _ml_up_kernel,
        grid=(N_TOK // TM, nj),
        in_specs=[
            pl.BlockSpec((TM, D_MODEL), lambda i, j: (i, 0)),
            pl.BlockSpec((None, 6, D_MODEL), lambda i, j: (_mod_row(i), 0, 0)),
            pl.BlockSpec((1, D_MODEL), lambda i, j: (0, 0)),
            pl.BlockSpec((D_MODEL, ML_TN), lambda i, j: (0, j)),
            pl.BlockSpec((D_MODEL, ML_TN), lambda i, j: (0, j + nj)),
            pl.BlockSpec((3, ML_TN), lambda i, j: (0, j)),
            pl.BlockSpec((1, ML_TN), lambda i, j: (0, j)),
        ],
        out_specs=[col, col, col],
        out_shape=[shp, shp, shp],
        scratch_shapes=[pltpu.VMEM((TM, D_MODEL), BF16), pltpu.VMEM((TM, ML_TN), F32)],
        compiler_params=_params("arbitrary", "arbitrary"),
        name="ml_up",
    )(x, mod, nw.reshape(1, D_MODEL), w_up, w_up, conv_w, conv_b.reshape(1, ML_D_IN))


def _log_sigmoid(x):
    return jnp.minimum(x, 0.0) - jnp.log(1.0 + jnp.exp(-jnp.abs(x)))


def _ml_scan_kernel(*refs, seq, has_init, out_state, n_unused_inputs):
    q_ref, kt_ref, v_ref, g_ref, gt_ref, xc_ref, z_ref, nw_ref, skip_ref = refs[:9]
    pos = 9
    if has_init:
        c0_ref, n0_ref, m0_ref = refs[pos:pos + 3]
        pos += 3
    pos += n_unused_inputs
    o_ref = refs[pos]
    pos += 1
    if out_state:
        cout_ref, nout_ref, mout_ref = refs[pos:pos + 3]
        pos += 3
    hs_scr, c_scr = refs[pos:pos + 2]

    L = ML_CHUNK
    nc = seq // L
    ext_tiles = ML_EXT // LANES
    row_i = lax.broadcasted_iota(jnp.int32, (L, L), 0)
    col_i = lax.broadcasted_iota(jnp.int32, (L, L), 1)
    ones_tile = jnp.ones((L, LANES), BF16)

    def lanes(a, n):
        return jnp.concatenate([a] * n, axis=1)

    for d in (0, 1):
        keep = (col_i <= row_i) if d == 0 else (col_i >= row_i)
        keep_f = keep.astype(F32)
        keep_t_f = ((row_i <= col_i) if d == 0 else (row_i >= col_i)).astype(F32)
        end = L - 1 if d == 0 else 0

        if has_init:
            c_scr[:, :ML_DV] = c0_ref[d]
            c_scr[:, ML_DV:] = n0_ref[d]
            m = m0_ref[d:d + 1, :]
        else:
            c_scr[...] = jnp.zeros_like(c_scr)
            m = jnp.zeros((1, LANES), F32)

        for c in range(nc):
            cc = c if d == 0 else nc - 1 - c
            rows = slice(cc * L, (cc + 1) * L)
            qc = q_ref[rows, :]
            ktc = kt_ref[cc]
            v_ext = jnp.concatenate([v_ref[rows, :], ones_tile], axis=1)
            gcol = g_ref[cc]
            grow = gt_ref[cc]
            ig_row = grow[2 * d:2 * d + 1, :]
            lf_col = _log_sigmoid(gcol[:, 2 * d + 1:2 * d + 2])
            lf_row = _log_sigmoid(grow[2 * d + 1:2 * d + 2, :])
            b_col = jnp.broadcast_to(jnp.sum(keep_f * lf_row, axis=1, keepdims=True), (L, LANES))
            b_row = jnp.sum(keep_t_f * lf_col, axis=0, keepdims=True)
            dmat = jnp.where(keep, b_col - b_row + ig_row, NEG_INF)
            m_loc = jnp.broadcast_to(jnp.max(dmat, axis=1, keepdims=True), (L, LANES))
            p_loc = jnp.exp(dmat - m_loc)
            s_loc = jnp.dot(qc, ktc, preferred_element_type=F32) * p_loc
            intra = jnp.dot(s_loc.astype(BF16), v_ext, preferred_element_type=F32)

            m_t = jnp.maximum(b_col + m, m_loc)
            w_inter = jnp.exp(b_col + m - m_t)
            w_intra = jnp.exp(m_loc - m_t)
            inter = jnp.dot(qc, c_scr[...].astype(BF16), preferred_element_type=F32)
            hx = lanes(w_inter, ext_tiles) * inter + lanes(w_intra, ext_tiles) * intra
            inv = 1.0 / jnp.maximum(jnp.abs(hx[:, ML_DV:]), jnp.exp(-m_t))
            h = hx[:, :ML_DV] * lanes(inv, ML_DV // LANES)

            b_end = b_col[end:end + 1, :]
            m_loc_end = m_loc[end:end + 1, :]
            m_new = jnp.maximum(b_end + m, m_loc_end)
            w_c = jnp.exp(b_end + m - m_new)
            w_s = p_loc[end:end + 1, :] * jnp.exp(m_loc_end - m_new)
            upd = jnp.dot((ktc.astype(F32) * w_s).astype(BF16), v_ext, preferred_element_type=F32)
            c_scr[...] = lanes(w_c, ext_tiles) * c_scr[...] + upd
            m = m_new

            if d == 0:
                hs_scr[rows, :] = h
            else:
                hs = hs_scr[rows, :] + h
                mu = jnp.mean(hs, axis=1, keepdims=True)
                cen = hs - mu
                var = jnp.mean(cen * cen, axis=1, keepdims=True)
                hn = cen * lax.rsqrt(var + NORM_EPS) * nw_ref[...]
                xc = xc_ref[rows, :].astype(F32)
                z = z_ref[rows, :].astype(F32)
                o_ref[rows, :] = ((hn + skip_ref[...] * xc) * _silu(z)).astype(o_ref.dtype)

        if out_state:
            cout_ref[d] = c_scr[:, :ML_DV]
            nout_ref[d:d + 1, :] = c_scr[:, ML_DV:].T[0:1, :]
            mout_ref[d:d + 1, :] = m


def _ml_scan(q, kt, v, g4, g4t, xc, z, norm_w, skip, *, seq, nb, row0, init=None, final_c=None):
    L = ML_CHUNK
    nc = seq // L
    rb = row0 // seq
    has_init = init is not None
    in_specs = [
        pl.BlockSpec((seq, ML_DK), lambda b, h: (rb + b, h)),
        pl.BlockSpec((None, nc, ML_DK, L), lambda b, h: (h, rb + b, 0, 0)),
        pl.BlockSpec((seq, ML_DV), lambda b, h: (rb + b, h)),
        pl.BlockSpec((None, nc, L, 4), lambda b, h: (h, rb + b, 0, 0)),
        pl.BlockSpec((None, nc, 4, L), lambda b, h: (h, rb + b, 0, 0)),
        pl.BlockSpec((seq, ML_DV), lambda b, h: (rb + b, h)),
        pl.BlockSpec((seq, ML_DV), lambda b, h: (rb + b, h)),
        pl.BlockSpec((1, ML_DV), lambda b, h: (0, h)),
        pl.BlockSpec((1, ML_DV), lambda b, h: (0, h)),
    ]
    args = [q, kt, v, g4, g4t, xc, z, norm_w.reshape(1, ML_D_IN), skip.reshape(1, ML_D_IN)]
    state_n = pl.BlockSpec((None, None, 2, ML_DK, LANES), lambda b, h: (b, h, 0, 0, 0))
    state_m = pl.BlockSpec((None, None, 2, LANES), lambda b, h: (b, h, 0, 0))
    if has_init:
        c0, layer, n0, m0 = init
        in_specs += [pl.BlockSpec((None, None, 2, None, ML_DK, ML_DV), lambda b, h: (b, layer, 0, h, 0, 0)),
                     state_n, state_m]
        args += [c0, n0, m0]
    out_specs = [pl.BlockSpec((seq, ML_DV), lambda b, h: (b, h))]
    out_shape = [jax.ShapeDtypeStruct((nb * seq, ML_D_IN), BF16)]
    aliases = {}
    if not has_init:
        layer_out, c_all = final_c
        out_specs += [pl.BlockSpec((None, None, 2, None, ML_DK, ML_DV), lambda b, h: (b, layer_out, 0, h, 0, 0)),
                      pl.BlockSpec((None, None, 2, ML_DK), lambda b, h: (b, h, 0, 0)), state_m]
        out_shape += [
            jax.ShapeDtypeStruct((nb, N_ML_LAYERS, 2, ML_HEADS, ML_DK, ML_DV), F32),
            jax.ShapeDtypeStruct((nb, ML_HEADS, 2, ML_DK), F32),
            jax.ShapeDtypeStruct((nb, ML_HEADS, 2, LANES), F32),
        ]
        if c_all is not None:
            in_specs.append(pl.BlockSpec(memory_space=pl.ANY))
            args.append(c_all)
            aliases = {len(args) - 1: 1}
    return pl.pallas_call(
        functools.partial(_ml_scan_kernel, seq=seq, has_init=has_init, out_state=not has_init,
                          n_unused_inputs=len(aliases)),
        grid=(nb, ML_HEADS),
        in_specs=in_specs,
        out_specs=out_specs,
        out_shape=out_shape,
        scratch_shapes=[pltpu.VMEM((seq, ML_DV), F32), pltpu.VMEM((ML_DK, ML_EXT), F32)],
        input_output_aliases=aliases,
        compiler_params=_params("arbitrary", "arbitrary"),
        name="ml_scan_init" if has_init else "ml_scan_zero",
    )(*args)


def _ml_qk_kernel(xc_ref, w_ref, q_ref, kt_ref):
    acc = jnp.dot(xc_ref[...], w_ref[...], preferred_element_type=F32)
    j = pl.program_id(1)

    @pl.when(j == 0)
    def _():
        q_ref[...] = (acc * ML_DK ** -0.5).astype(BF16)

    @pl.when(j == 1)
    def _():
        for h in range(ML_HEADS):
            kt = acc[:, h * ML_DK:(h + 1) * ML_DK].T
            for c in range(TM // ML_CHUNK):
                kt_ref[h, c] = kt[:, c * ML_CHUNK:(c + 1) * ML_CHUNK].astype(BF16)


def _ml_qk(xc, w_qk):
    n = ML_HEADS * ML_DK
    cpt = TM // ML_CHUNK
    return pl.pallas_call(
        _ml_qk_kernel,
        grid=(N_TOK // TM, 2),
        in_specs=[pl.BlockSpec((TM, ML_D_IN), lambda i, j: (i, 0)),
                  pl.BlockSpec((ML_D_IN, n), lambda i, j: (0, j))],
        out_specs=[pl.BlockSpec((TM, n), lambda i, j: (i, 0)),
                   pl.BlockSpec((ML_HEADS, cpt, ML_DK, ML_CHUNK), lambda i, j: (0, i, 0, 0))],
        out_shape=[jax.ShapeDtypeStruct((N_TOK, n), BF16),
                   jax.ShapeDtypeStruct((ML_HEADS, N_TOK // ML_CHUNK, ML_DK, ML_CHUNK), BF16)],
        compiler_params=_params("arbitrary", "arbitrary"),
        name="ml_qk",
    )(xc, w_qk)


def _mlstm_layer(x, mod, nw, p, state, final_c):
    xm, xc, z = _ml_up(x, mod, nw, p["w_up"], p["conv_w"], p["conv_b"])
    q, kt = _ml_qk(xc, p["w_qk"])
    v = _mm(xm, p["w_v"], jnp.zeros((1, ML_D_IN), F32), 1024, BF16)
    g = _mm(xm, p["w_gate"], p["b_gate"], 128, F32)[:, :4 * ML_HEADS]
    L = ML_CHUNK
    g4 = g.reshape(N_TOK // L, L, 4, ML_HEADS).transpose(3, 0, 1, 2)
    g4t = g4.transpose(0, 1, 3, 2)
    oc, c_fin, n_fin, m_fin = _ml_scan(q, kt, v, g4, g4t, xc, z, p["norm_w"], p["skip"],
                                       seq=SEQ, nb=BATCH, row0=0, final_c=final_c)
    (ol,) = _ml_scan(q, kt, v, g4, g4t, xc, z, p["norm_w"], p["skip"],
                     seq=DEC_SEQ, nb=DEC_BATCH, row0=N_CTX, init=state)
    x = _mm_res(oc, ol, p["w_down"], x, mod)
    return x, c_fin, n_fin.transpose(0, 2, 1, 3), m_fin[..., 0].transpose(0, 2, 1)


def _softmax_parts(scores, sink_col):
    m = functools.reduce(jnp.maximum, [jnp.max(s, axis=1, keepdims=True) for s in scores])
    if sink_col is not None:
        m = jnp.maximum(m, sink_col)
    ps = [jnp.exp(s - m) for s in scores]
    den = functools.reduce(jnp.add, [jnp.sum(p, axis=1, keepdims=True) for p in ps])
    if sink_col is not None:
        den = den + jnp.exp(sink_col - m)
    return ps, den


def _qk(q, k):
    return lax.dot_general(q, k, (((1,), (1,)), ((), ())), preferred_element_type=F32) * (HEAD_DIM ** -0.5)


def _attend_tiles(q_tiles, parts, sinks):
    return _attend_groups([(q_tiles, parts, sinks)])[0]


def _attend_groups(groups):
    r = groups[0][0][0].shape[0]
    lo = lax.broadcasted_iota(jnp.int32, (r, LANES), 1) < HEAD_DIM
    zero = jnp.zeros((r, LANES), BF16)
    all_scores = []
    for q_tiles, parts, _ in groups:
        qs = jnp.concatenate([jnp.where(sel, t, zero) for t in q_tiles for sel in (lo, jnp.logical_not(lo))], axis=0)
        all_scores.append([post(_qk(qs, k2)) for k2, _, post in parts])
    outs = []
    for scores, (q_tiles, parts, sinks) in zip(all_scores, groups):
        n_rows = scores[0].shape[0]
        sink_col = None
        if sinks is not None:
            rows = lax.broadcasted_iota(jnp.int32, (n_rows, 1), 0)
            sink_col = jnp.full((n_rows, 1), sinks[0], F32)
            for hi in range(1, len(sinks)):
                sink_col = jnp.where(rows >= hi * r, sinks[hi], sink_col)
        ps, den = _softmax_parts(scores, sink_col)
        o = functools.reduce(jnp.add, [jnp.dot(p.astype(BF16), v2, preferred_element_type=F32)
                                       for p, (_, v2, _) in zip(ps, parts)]) / den
        outs.append([jnp.where(lo, o[2 * a * r:(2 * a + 1) * r], o[(2 * a + 1) * r:(2 * a + 2) * r])
                     for a in range(len(q_tiles))])
    return outs


def _identity(s):
    return s


def _ctx_attn_kernel(*refs, tiles_per_kv, has_sink):
    if has_sink:
        sink_ref, q_ref, k_ref, v_ref, o_ref = refs
    else:
        q_ref, k_ref, v_ref, o_ref = refs
    groups = []
    for t in range(k_ref.shape[1] // LANES):
        k2 = k_ref[:, t * LANES:(t + 1) * LANES].astype(BF16)
        v2 = v_ref[:, t * LANES:(t + 1) * LANES].astype(BF16)
        first = t * tiles_per_kv
        q_tiles = [q_ref[:, (first + a) * LANES:(first + a + 1) * LANES] for a in range(tiles_per_kv)]
        sinks = [sink_ref[2 * first + hi] for hi in range(2 * tiles_per_kv)] if has_sink else None
        groups.append((q_tiles, [(k2, v2, _identity)], sinks))
    for t, outs in enumerate(_attend_groups(groups)):
        for a, o in enumerate(outs):
            tile = t * tiles_per_kv + a
            o_ref[:, tile * LANES:(tile + 1) * LANES] = o.astype(o_ref.dtype)


def _ctx_attn(q, kv, n_kv_cols, sink):
    nq = q.shape[1]
    has_sink = sink is not None
    in_specs = [pl.BlockSpec((SEQ, nq), lambda b: (b, 0)),
                pl.BlockSpec((SEQ, n_kv_cols), lambda b: (b, 0)),
                pl.BlockSpec((SEQ, n_kv_cols), lambda b: (b, 1))]
    args = [q, kv, kv]
    if has_sink:
        in_specs = [pl.BlockSpec(memory_space=pltpu.SMEM)] + in_specs
        args = [sink] + args
    return pl.pallas_call(
        functools.partial(_ctx_attn_kernel, tiles_per_kv=nq // n_kv_cols, has_sink=has_sink),
        grid=(BATCH,),
        in_specs=in_specs,
        out_specs=pl.BlockSpec((SEQ, nq), lambda b: (b, 0)),
        out_shape=jax.ShapeDtypeStruct((N_CTX, nq), BF16),
        compiler_params=_params("arbitrary"),
        name="ctx_attn",
    )(*args)


SWA_SPAN = Q_BLOCK + 2 * SWA_WINDOW
SWA_KV_COLS = SWA_KV * LANES


def _swa_lat_kernel(sink_ref, q_ref, k_ref, v_ref, kc_ref, vc_ref, o_ref):
    j = pl.program_id(1)
    start = pl.multiple_of(jnp.clip((j - 1) * Q_BLOCK, 0, DEC_SEQ - SWA_SPAN), Q_BLOCK)
    rows = 4 * Q_BLOCK
    qpos = j * Q_BLOCK + (lax.broadcasted_iota(jnp.int32, (rows, SWA_SPAN), 0) & (Q_BLOCK - 1))
    kpos = start + lax.broadcasted_iota(jnp.int32, (rows, SWA_SPAN), 1)
    in_window = jnp.abs(qpos - kpos) <= SWA_WINDOW

    def window(s):
        return jnp.where(in_window, s, NEG_INF)

    groups = []
    for t in range(SWA_KV):
        cols = slice(t * LANES, (t + 1) * LANES)
        k_loc = k_ref[pl.ds(start, SWA_SPAN), cols].astype(BF16)
        v_loc = v_ref[pl.ds(start, SWA_SPAN), cols].astype(BF16)
        q_tiles = [q_ref[:, (2 * t + a) * LANES:(2 * t + a + 1) * LANES] for a in range(2)]
        sinks = [sink_ref[4 * t + hi] for hi in range(4)]
        groups.append((q_tiles, [(k_loc, v_loc, window), (kc_ref[:, cols], vc_ref[:, cols], _identity)], sinks))
    for t, outs in enumerate(_attend_groups(groups)):
        for a, o in enumerate(outs):
            o_ref[:, (2 * t + a) * LANES:(2 * t + a + 1) * LANES] = o.astype(o_ref.dtype)


def _swa_latent(q, kv, kc, vc, sink):
    nq = q.shape[1]
    rb = N_CTX // DEC_SEQ
    qb = N_CTX // Q_BLOCK
    nj = DEC_SEQ // Q_BLOCK
    cspec = pl.BlockSpec((None, SEQ, SWA_KV_COLS), lambda b, j: (b, 0, 0))
    return pl.pallas_call(
        _swa_lat_kernel,
        grid=(DEC_BATCH, nj),
        in_specs=[pl.BlockSpec(memory_space=pltpu.SMEM),
                  pl.BlockSpec((Q_BLOCK, nq), lambda b, j: (qb + b * nj + j, 0)),
                  pl.BlockSpec((DEC_SEQ, SWA_KV_COLS), lambda b, j: (rb + b, 0)),
                  pl.BlockSpec((DEC_SEQ, SWA_KV_COLS), lambda b, j: (rb + b, 1)),
                  cspec, cspec],
        out_specs=pl.BlockSpec((Q_BLOCK, nq), lambda b, j: (b * nj + j, 0)),
        out_shape=jax.ShapeDtypeStruct((N_LAT, nq), BF16),
        compiler_params=_params("arbitrary", "arbitrary"),
        name="swa_latent",
    )(sink, q, kv, kv, kc, vc)


NA_QT = 256
NA_SPAN = 768
NA_TILES = 4


def _na_start(j):
    return (j // 2) * (DEC_SEQ - NA_SPAN)


NA_ROWS = DEC_SEQ // GRID_W
NA_DR = 2 * NA_KH - 1
NA_DC = 2 * NA_KW - 1


def _na_blocks_kernel(rpb_ref, onehot_ref, valid_ref, o_ref):
    t = jnp.dot(rpb_ref[...], onehot_ref[...], preferred_element_type=F32, precision=lax.Precision.HIGHEST)
    o_ref[...] = jnp.where(valid_ref[...] > 0.5, t, NEG_INF)


def _na_bias_blocks(rpb):
    h = rpb.shape[0]
    kpad = 32
    cq, ck = np.meshgrid(np.arange(GRID_W), np.arange(GRID_W), indexing="ij")
    dc = (np.clip(ck - cq, -(NA_KW - 1), NA_KW - 1) + NA_KW - 1).reshape(-1)
    cs = np.clip(cq - NA_KW // 2, 0, GRID_W - NA_KW)
    valid = ((ck >= cs) & (ck < cs + NA_KW)).reshape(1, -1).astype(np.float32)
    onehot = (np.arange(kpad)[:, None] == dc[None, :]).astype(np.float32)
    rpb2 = jnp.pad(rpb.reshape(h * NA_DR, NA_DC), ((0, 0), (0, kpad - NA_DC)))
    n = GRID_W * GRID_W
    blocks = pl.pallas_call(
        _na_blocks_kernel,
        grid=(1,),
        in_specs=[pl.BlockSpec((h * NA_DR, kpad), lambda i: (0, 0)),
                  pl.BlockSpec((kpad, n), lambda i: (0, 0)),
                  pl.BlockSpec((1, n), lambda i: (0, 0))],
        out_specs=pl.BlockSpec((h * NA_DR, n), lambda i: (0, 0)),
        out_shape=jax.ShapeDtypeStruct((h * NA_DR, n), F32),
        compiler_params=_params("arbitrary"),
        name="na_bias_blocks",
    )(rpb2, jnp.asarray(onehot), jnp.asarray(valid))
    blocks = blocks.reshape(h, NA_DR, GRID_W, GRID_W)
    padded = jnp.pad(blocks, ((0, 0), (1, 1), (0, 0), (0, 0)), constant_values=NEG_INF)
    return jnp.concatenate([padded[:, :-1], padded[:, 1:]], axis=-1)


def _na_lat_kernel(q_ref, k_ref, v_ref, kc_ref, vc_ref, blk_ref, o_ref, bias_scr):
    j = pl.program_id(0)
    start = pl.multiple_of(_na_start(j), 256)

    @pl.when(pl.program_id(2) == 0)
    def _():
        lane_lo = lax.broadcasted_iota(jnp.int32, (GRID_W, LANES), 1) < GRID_W
        for rq_l in range(NA_QT // GRID_W):
            rq = j * (NA_QT // GRID_W) + rq_l
            rs = jnp.clip(rq - NA_KH // 2, 0, NA_ROWS - NA_KH)
            for kp in range(NA_SPAN // LANES):
                rk = start // GRID_W + 2 * kp
                idx = jnp.clip(rk - rq + NA_KH, 0, NA_DR)
                in_band = [jnp.logical_and(r >= rs, r < rs + NA_KH).astype(jnp.int32) for r in (rk, rk + 1)]
                ok = jnp.where(lane_lo, in_band[0], in_band[1]) > 0
                for hh in range(2 * NA_TILES):
                    bias_scr[hh * NA_QT + rq_l * GRID_W:hh * NA_QT + (rq_l + 1) * GRID_W,
                             kp * LANES:(kp + 1) * LANES] = jnp.where(ok, blk_ref[hh, idx], NEG_INF)

    groups = []
    for t in range(NA_TILES):
        cols = slice(t * LANES, (t + 1) * LANES)

        def add_bias(s, t=t):
            return s + bias_scr[2 * t * NA_QT:2 * (t + 1) * NA_QT, :]

        k_loc = k_ref[pl.ds(start, NA_SPAN), cols].astype(BF16)
        v_loc = v_ref[pl.ds(start, NA_SPAN), cols].astype(BF16)
        groups.append(([q_ref[:, cols]], [(k_loc, v_loc, add_bias), (kc_ref[:, cols], vc_ref[:, cols], _identity)],
                       None))
    for t, (o,) in enumerate(_attend_groups(groups)):
        o_ref[:, t * LANES:(t + 1) * LANES] = o.astype(o_ref.dtype)


def _na_latent(q, kv, kc, vc, blocks):
    nq = q.shape[1]
    width = NA_TILES * LANES
    n_steps = nq // width
    nj = DEC_SEQ // NA_QT
    rb = N_CTX // DEC_SEQ
    qb = N_CTX // NA_QT
    cspec = pl.BlockSpec((None, SEQ, width), lambda j, p, b: (b, 0, p))
    return pl.pallas_call(
        _na_lat_kernel,
        grid=(nj, n_steps, DEC_BATCH),
        in_specs=[pl.BlockSpec((NA_QT, width), lambda j, p, b: (qb + b * nj + j, p)),
                  pl.BlockSpec((DEC_SEQ, width), lambda j, p, b: (rb + b, p)),
                  pl.BlockSpec((DEC_SEQ, width), lambda j, p, b: (rb + b, n_steps + p)),
                  cspec, cspec,
                  pl.BlockSpec((2 * NA_TILES, NA_DR + 1, GRID_W, LANES), lambda j, p, b: (p, 0, 0, 0))],
        out_specs=pl.BlockSpec((NA_QT, width), lambda j, p, b: (b * nj + j, p)),
        out_shape=jax.ShapeDtypeStruct((N_LAT, nq), BF16),
        scratch_shapes=[pltpu.VMEM((2 * NA_TILES * NA_QT, NA_SPAN), F32)],
        compiler_params=_params("arbitrary", "arbitrary", "arbitrary"),
        name="na_latent",
    )(q, kv, kv, kc, vc, blocks)


def _rope_tables(width):
    quarter = HEAD_DIM // 4
    pos = np.arange(DEC_SEQ)
    inv = np.power(ROPE_BASE, -np.arange(quarter, dtype=np.float32) / quarter).astype(np.float32)
    d = np.arange(width) % HEAD_DIM
    p = np.where((d < HEAD_DIM // 2)[None, :], (pos // GRID_W)[:, None], (pos % GRID_W)[:, None]).astype(np.float32)
    ang = p * inv[d % quarter][None, :]
    sign = np.where((d // quarter) % 2 == 0, -1.0, 1.0)[None, :]
    return jnp.asarray(np.cos(ang), F32), jnp.asarray(np.sin(ang) * sign, F32)


def _cache_rows(cache, dup):
    b, h, s, hd = cache.shape
    rows = jnp.broadcast_to(cache.transpose(0, 2, 1, 3)[:, :, :, None, :], (b, s, h, dup, hd))
    return rows.reshape(b, s, h * dup * hd).astype(BF16)


def _swa_layer(x, mod, nw, w_qkv, sink, w_o, cache_k, cache_v):
    nq, nk = SWA_HEADS * HEAD_DIM, SWA_KV * HEAD_DIM

    def dup_heads(w):
        return jnp.broadcast_to(w.reshape(D_MODEL, SWA_KV, 1, HEAD_DIM),
                                (D_MODEL, SWA_KV, 2, HEAD_DIM)).reshape(D_MODEL, SWA_KV_COLS)

    w = jnp.concatenate([w_qkv[:, :nq], dup_heads(w_qkv[:, nq:nq + nk]), dup_heads(w_qkv[:, nq + nk:])], axis=1)
    q, kv = _qkv_proj(x, mod, nw, w.astype(BF16), nq, rope_cols=nq + SWA_KV_COLS)
    kv_ctx = kv[:N_CTX].reshape(BATCH, SEQ, 2, SWA_KV, 2, HEAD_DIM)[:, :, :, :, 0]
    k_new = kv_ctx[:, :, 0].transpose(0, 2, 1, 3)
    v_new = kv_ctx[:, :, 1].transpose(0, 2, 1, 3)
    oc = _ctx_attn(q, kv, SWA_KV_COLS, sink)
    ol = _swa_latent(q, kv, _cache_rows(cache_k, 2), _cache_rows(cache_v, 2), sink)
    return _mm_res(oc, ol, w_o, x, mod), k_new, v_new


def _na_layer(x, mod, nw, w_qkv, rpb, w_o, cache_k, cache_v):
    n = NA_HEADS * HEAD_DIM
    q, kv = _qkv_proj(x, mod, nw, w_qkv, n)
    kv_ctx = kv[:N_CTX].reshape(BATCH, SEQ, 2, NA_HEADS, HEAD_DIM)
    k_new = kv_ctx[:, :, 0].transpose(0, 2, 1, 3)
    v_new = kv_ctx[:, :, 1].transpose(0, 2, 1, 3)
    oc = _ctx_attn(q, kv, n, None)
    ol = _na_latent(q, kv, _cache_rows(cache_k, 1), _cache_rows(cache_v, 1), _na_bias_blocks(rpb))
    return _mm_res(oc, ol, w_o, x, mod), k_new, v_new


def _final_norm_kernel(x_ref, w_ref, oc_ref, ol_ref):
    x = x_ref[...]
    y = x * lax.rsqrt(jnp.mean(x * x, axis=-1, keepdims=True) + NORM_EPS) * w_ref[...]
    is_ctx = pl.program_id(0) < N_CTX_TILES

    @pl.when(is_ctx)
    def _():
        oc_ref[...] = y

    @pl.when(jnp.logical_not(is_ctx))
    def _():
        ol_ref[...] = y


def _final_norm(x, w):
    return pl.pallas_call(
        _final_norm_kernel,
        grid=(N_TOK // TM,),
        in_specs=[pl.BlockSpec((TM, D_MODEL), lambda i: (i, 0)), pl.BlockSpec((1, D_MODEL), lambda i: (0, 0))],
        out_specs=[pl.BlockSpec((TM, D_MODEL), lambda i: (jnp.minimum(i, N_CTX_TILES - 1), 0)),
                   pl.BlockSpec((TM, D_MODEL), lambda i: (jnp.maximum(i - N_CTX_TILES, 0), 0))],
        out_shape=[jax.ShapeDtypeStruct((N_CTX, D_MODEL), F32), jax.ShapeDtypeStruct((N_LAT, D_MODEL), F32)],
        compiler_params=_params("arbitrary"),
        name="final_norm",
    )(x, w.reshape(1, D_MODEL))


def kernel(x_prompt, x_sample, state_mlstm_C, state_mlstm_n, state_mlstm_m, cache_swa_k, cache_swa_v, cache_na_k, cache_na_v, c, c_ctx, ada_w, ada_b, norm_w, final_norm_w, ffn_w_up, ffn_conv_w, ffn_conv_b, ffn_w_down, ml_w_up, ml_conv_w, ml_conv_b, ml_w_qk, ml_w_v, ml_w_gate, ml_b_gate, ml_norm_w, ml_skip, ml_w_down, swa_w_qkv, swa_sink, swa_w_o, na_w_qkv, na_rpb, na_w_o):
    x = jnp.concatenate([x_prompt.reshape(N_CTX, D_MODEL), x_sample.reshape(N_LAT, D_MODEL)], axis=0)
    cond = jnp.concatenate([c_ctx[None], c, jnp.zeros((MOD_ROWS - 1 - DEC_BATCH, D_MODEL), F32)], axis=0)
    mods = _ada_mod(cond, ada_w, ada_b)

    new_c, new_n, new_m = None, [], []
    new_sk = new_sv = new_nk = new_nv = None
    for i in range(DEPTH):
        kind, j = i % N_MIXERS, i // N_MIXERS
        mod = mods[i]
        if kind == 0:
            gate_w = jnp.pad(ml_w_gate[j], ((0, 0), (0, 128 - 4 * ML_HEADS))).astype(BF16)
            gate_b = jnp.pad(ml_b_gate[j], (0, 128 - 4 * ML_HEADS)).reshape(1, 128)
            p = dict(w_up=ml_w_up[j].astype(BF16), conv_w=ml_conv_w[j], conv_b=ml_conv_b[j],
                     w_qk=ml_w_qk[j].astype(BF16), w_v=ml_w_v[j].astype(BF16), w_gate=gate_w, b_gate=gate_b,
                     norm_w=ml_norm_w[j], skip=ml_skip[j], w_down=ml_w_down[j].astype(BF16))
            n0 = jnp.broadcast_to(state_mlstm_n[:, j].transpose(0, 2, 1, 3)[..., None],
                                  (DEC_BATCH, ML_HEADS, 2, ML_DK, LANES))
            m0 = jnp.broadcast_to(state_mlstm_m[:, j].transpose(0, 2, 1)[..., None], (DEC_BATCH, ML_HEADS, 2, LANES))
            x, new_c, nf, mf = _mlstm_layer(x, mod, norm_w[i, 0], p, (state_mlstm_C, j, n0, m0), (j, new_c))
            new_n.append(nf)
            new_m.append(mf)
        elif kind == 1:
            x, k_new, v_new = _swa_layer(x, mod, norm_w[i, 0], swa_w_qkv[j].astype(BF16), swa_sink[j],
                                         swa_w_o[j].astype(BF16), cache_swa_k[:, j], cache_swa_v[:, j])
            new_sk, new_sv = k_new[:, None], v_new[:, None]
        else:
            x, k_new, v_new = _na_layer(x, mod, norm_w[i, 0], na_w_qkv[j].astype(BF16), na_rpb[j],
                                        na_w_o[j].astype(BF16), cache_na_k[:, j], cache_na_v[:, j])
            new_nk, new_nv = k_new[:, None], v_new[:, None]
        x = _conv_ffn(x, mod, norm_w[i, 1], ffn_w_up[i].astype(BF16), ffn_conv_w[i], ffn_conv_b[i],
                      ffn_w_down[i].astype(BF16))

    y_ctx, y_lat = _final_norm(x, final_norm_w)
    return (y_ctx.reshape(BATCH, SEQ, D_MODEL), y_lat.reshape(DEC_BATCH, DEC_SEQ, D_MODEL),
            new_c, jnp.stack(new_n, axis=1), jnp.stack(new_m, axis=1),
            new_sk, new_sv, new_nk, new_nv)
```

```python
import functools

import jax
import jax.numpy as jnp
import numpy as np
from jax import lax
from jax.experimental import pallas as pl
from jax.experimental.pallas import tpu as pltpu

F32 = jnp.float32
BF16 = jnp.bfloat16

D_MODEL = 1024
BATCH = 32
SEQ = 256
DEPTH = 4
DEC_BATCH = 8
DEC_SEQ = 1024
GRID_W = 64
N_MIXERS = 3
N_ML_LAYERS = (DEPTH + 2) // 3
NORM_EPS = 1e-6
D_FF = 2816
ML_D_IN = 2 * D_MODEL
ML_HEADS = 4
ML_DK = ML_D_IN // (2 * ML_HEADS)
ML_DV = ML_D_IN // ML_HEADS
ML_CHUNK = 128
HEAD_DIM = 64
SWA_HEADS = D_MODEL // HEAD_DIM
SWA_KV = SWA_HEADS // 4
SWA_WINDOW = 128
Q_BLOCK = 128
ROPE_BASE = 10000.0
NA_HEADS = D_MODEL // HEAD_DIM
NA_KH = 8
NA_KW = 16
NEG_INF = -1e30

N_CTX = BATCH * SEQ
N_LAT = DEC_BATCH * DEC_SEQ
N_TOK = N_CTX + N_LAT
TM = 1024
N_CTX_TILES = N_CTX // TM
MOD_ROWS = 16
VMEM_LIMIT_BYTES = 56 * 1024 * 1024


def _params(*sem):
    return pltpu.CompilerParams(dimension_semantics=sem, vmem_limit_bytes=VMEM_LIMIT_BYTES)


def _mod_row(i):
    return jnp.where(i < N_CTX_TILES, 0, i - (N_CTX_TILES - 1))


def _silu(x):
    return x / (1.0 + jnp.exp(-x))


def _norm_mod(x, nw, shift, scale):
    y = x * lax.rsqrt(jnp.mean(x * x, axis=-1, keepdims=True) + NORM_EPS) * nw
    return y * (1.0 + scale) + shift


SUBLANES = 8
LANES = 128


def _dwconv_rows(u, cw, cb, seq):
    r, c = u.shape
    n_groups, per_seq = r // SUBLANES, seq // SUBLANES
    g = u.reshape(n_groups, SUBLANES, c)
    sub = lax.broadcasted_iota(jnp.int32, g.shape, 1)
    down = pltpu.roll(g, 1, 1)
    up = pltpu.roll(g, SUBLANES - 1, 1)
    zero = jnp.zeros((1, SUBLANES, c), F32)
    from_prev, from_next = [], []
    for s in range(0, n_groups, per_seq):
        from_prev += [zero, down[s:s + per_seq - 1]]
        from_next += [up[s + 1:s + per_seq], zero]
    prev = jnp.where(sub == 0, jnp.concatenate(from_prev, axis=0), down)
    nxt = jnp.where(sub == SUBLANES - 1, jnp.concatenate(from_next, axis=0), up)
    out = cw[0:1, :] * prev + cw[1:2, :] * g + cw[2:3, :] * nxt + cb
    return out.reshape(r, c)


def _by_tile_kind(tile, body):
    pl.when(tile < N_CTX_TILES)(functools.partial(body, SEQ))
    pl.when(tile >= N_CTX_TILES)(functools.partial(body, DEC_SEQ))


def _ada_kernel(c_ref, w_ref, b_ref, o_ref):
    s = _silu(c_ref[...]).astype(BF16)
    o_ref[...] = jnp.dot(s, w_ref[...].astype(BF16), preferred_element_type=F32) + b_ref[...]


def _ada_mod(cond, ada_w, ada_b):
    tn = 1536
    n = 6 * D_MODEL
    out = pl.pallas_call(
        _ada_kernel,
        grid=(DEPTH, n // tn),
        in_specs=[
            pl.BlockSpec((MOD_ROWS, D_MODEL), lambda l, j: (0, 0)),
            pl.BlockSpec((None, D_MODEL, tn), lambda l, j: (l, 0, j)),
            pl.BlockSpec((None, 1, tn), lambda l, j: (l, 0, j)),
        ],
        out_specs=pl.BlockSpec((None, MOD_ROWS, tn), lambda l, j: (l, 0, j)),
        out_shape=jax.ShapeDtypeStruct((DEPTH, MOD_ROWS, n), F32),
        compiler_params=_params("arbitrary", "arbitrary"),
        name="ada_mod",
    )(cond, ada_w, ada_b.reshape(DEPTH, 1, n))
    return out.reshape(DEPTH, MOD_ROWS, 6, D_MODEL)


QKV_TN = 512


def _rotate_pairs(a, cos, sin):
    lane = lax.broadcasted_iota(jnp.int32, a.shape, 1)
    first = (lane & (HEAD_DIM // 4)) == 0
    n = a.shape[1]
    partner = jnp.where(first, pltpu.roll(a, n - HEAD_DIM // 4, 1), pltpu.roll(a, HEAD_DIM // 4, 1))
    return a * cos + partner * sin


def _qkv_kernel(*refs, q_blocks, rope_blocks):
    if rope_blocks:
        x_ref, mod_ref, nw_ref, w_ref, cos_ref, sin_ref, q_ref, kv_ref, h_scr = refs
    else:
        x_ref, mod_ref, nw_ref, w_ref, q_ref, kv_ref, h_scr = refs
    i = pl.program_id(0)
    j = pl.program_id(1)

    @pl.when(j == 0)
    def _():
        h = _norm_mod(x_ref[...], nw_ref[...], mod_ref[0:1, :], mod_ref[1:2, :])
        h_scr[...] = h.astype(BF16)

    acc = jnp.dot(h_scr[...], w_ref[...], preferred_element_type=F32)

    def emit(val):
        @pl.when(j < q_blocks)
        def _():
            q_ref[...] = val.astype(BF16)

        @pl.when(j >= q_blocks)
        def _():
            kv_ref[...] = val

    if rope_blocks:
        rotate = jnp.logical_and(i >= N_CTX_TILES, j < rope_blocks)

        @pl.when(rotate)
        def _():
            emit(_rotate_pairs(acc, cos_ref[...], sin_ref[...]))

        @pl.when(jnp.logical_not(rotate))
        def _():
            emit(acc)
    else:
        emit(acc)


def _qkv_proj(x, mod, nw, w, n_q, rope_cols=0):
    n = w.shape[1]
    q_blocks = n_q // QKV_TN
    rope_blocks = rope_cols // QKV_TN
    in_specs = [
        pl.BlockSpec((TM, D_MODEL), lambda i, j: (i, 0)),
        pl.BlockSpec((None, 6, D_MODEL), lambda i, j: (_mod_row(i), 0, 0)),
        pl.BlockSpec((1, D_MODEL), lambda i, j: (0, 0)),
        pl.BlockSpec((D_MODEL, QKV_TN), lambda i, j: (0, j)),
    ]
    args = [x, mod, nw.reshape(1, D_MODEL), w]
    if rope_blocks:
        tab = pl.BlockSpec((DEC_SEQ, QKV_TN), lambda i, j: (0, 0))
        in_specs += [tab, tab]
        args += list(_rope_tables(QKV_TN))
    return pl.pallas_call(
        functools.partial(_qkv_kernel, q_blocks=q_blocks, rope_blocks=rope_blocks),
        grid=(N_TOK // TM, n // QKV_TN),
        in_specs=in_specs,
        out_specs=[pl.BlockSpec((TM, QKV_TN), lambda i, j: (i, jnp.minimum(j, q_blocks - 1))),
                   pl.BlockSpec((TM, QKV_TN), lambda i, j: (i, jnp.maximum(j - q_blocks, 0)))],
        out_shape=[jax.ShapeDtypeStruct((N_TOK, n_q), BF16), jax.ShapeDtypeStruct((N_TOK, n - n_q), F32)],
        scratch_shapes=[pltpu.VMEM((TM, D_MODEL), BF16)],
        compiler_params=_params("arbitrary", "arbitrary"),
        name="qkv_proj",
    )(*args)


def _mm_kernel(a_ref, w_ref, b_ref, o_ref):
    acc = jnp.dot(a_ref[...], w_ref[...], preferred_element_type=F32)
    o_ref[...] = (acc + b_ref[...]).astype(o_ref.dtype)


def _mm(a, w, bias, tn, out_dtype):
    m, k = a.shape
    n = w.shape[1]
    return pl.pallas_call(
        _mm_kernel,
        grid=(m // TM, n // tn),
        in_specs=[
            pl.BlockSpec((TM, k), lambda i, j: (i, 0)),
            pl.BlockSpec((k, tn), lambda i, j: (0, j)),
            pl.BlockSpec((1, tn), lambda i, j: (0, j)),
        ],
        out_specs=pl.BlockSpec((TM, tn), lambda i, j: (i, j)),
        out_shape=jax.ShapeDtypeStruct((m, n), out_dtype),
        compiler_params=_params("arbitrary", "arbitrary"),
        name="mm",
    )(a, w, bias)


def _row_sources(x, width):
    (ctx, ctx_first), (lat, lat_first) = (((x[0], 0), (x[1], 0)) if isinstance(x, tuple)
                                          else ((x, 0), (x, N_CTX_TILES)))

    def spec(first, lo, hi):
        return pl.BlockSpec((TM, width), lambda i, *_: (jnp.clip(i, lo, hi) - lo + first, 0))

    return ([spec(ctx_first, 0, N_CTX_TILES - 1), spec(lat_first, N_CTX_TILES, N_TOK // TM - 1)], [ctx, lat])


def _mm_res_kernel(ac_ref, al_ref, w_ref, xc_ref, xl_ref, mod_ref, o_ref):
    def emit(a_ref, x_ref):
        acc = jnp.dot(a_ref[...], w_ref[...], preferred_element_type=F32)
        o_ref[...] = x_ref[...] + mod_ref[2:3, :] * acc

    is_ctx = pl.program_id(0) < N_CTX_TILES
    pl.when(is_ctx)(functools.partial(emit, ac_ref, xc_ref))
    pl.when(jnp.logical_not(is_ctx))(functools.partial(emit, al_ref, xl_ref))


def _mm_res(a_ctx, a_lat, w, x, mod):
    k = a_ctx.shape[1]
    a_specs, a_arrays = _row_sources((a_ctx, a_lat), k)
    x_specs, x_arrays = _row_sources(x, D_MODEL)
    return pl.pallas_call(
        _mm_res_kernel,
        grid=(N_TOK // TM,),
        in_specs=a_specs + [pl.BlockSpec((k, D_MODEL), lambda i: (0, 0))] + x_specs
        + [pl.BlockSpec((None, 6, D_MODEL), lambda i: (_mod_row(i), 0, 0))],
        out_specs=pl.BlockSpec((TM, D_MODEL), lambda i: (i, 0)),
        out_shape=jax.ShapeDtypeStruct((N_TOK, D_MODEL), F32),
        compiler_params=_params("arbitrary"),
        name="mm_res",
    )(*a_arrays, w, *x_arrays, mod)


FFN_TF = 256
FFN_BLOCKS = D_FF // FFN_TF


def _ffn_kernel(x_ref, mod_ref, nw_ref, wup_ref, cw_ref, cb_ref, wd_ref, o_ref, h_scr, acc_scr, raw_a, raw_b):
    nb = FFN_BLOCKS
    h_scr[...] = _norm_mod(x_ref[...], nw_ref[...], mod_ref[3:4, :], mod_ref[4:5, :]).astype(BF16)
    acc_scr[...] = jnp.zeros_like(acc_scr)

    def run(seq):
        def project(k, raw):
            raw[...] = jnp.dot(h_scr[...], wup_ref[k], preferred_element_type=F32)

        def consume(k, raw):
            u = _dwconv_rows(raw[...], cw_ref[k], cb_ref[k], seq)
            a = (_silu(u[:, :FFN_TF]) * u[:, FFN_TF:]).astype(BF16)
            acc_scr[...] += jnp.dot(a, wd_ref[k], preferred_element_type=F32)

        project(0, raw_a)

        def two_stages(t, carry):
            k = 2 * t
            project(k + 1, raw_b)
            consume(k, raw_a)
            project(k + 2, raw_a)
            consume(k + 1, raw_b)
            return carry

        lax.fori_loop(0, (nb - 1) // 2, two_stages, 0)
        consume(nb - 1, raw_a)

    _by_tile_kind(pl.program_id(0), run)
    o_ref[...] = x_ref[...] + mod_ref[5:6, :] * acc_scr[...]


def _conv_ffn(x, mod, nw, w_up, conv_w, conv_b, w_down):
    nb = FFN_BLOCKS
    def blocks(a):
        r = a.shape[0]
        return a.reshape(r, 2, nb, FFN_TF).transpose(2, 0, 1, 3).reshape(nb, r, 2 * FFN_TF)

    w_up = blocks(w_up)
    conv_w = blocks(conv_w)
    conv_b = blocks(conv_b.reshape(1, 2 * D_FF))
    w_down = w_down.reshape(nb, FFN_TF, D_MODEL)

    def resident(shape):
        return pl.BlockSpec(shape, lambda i: (0,) * len(shape), pipeline_mode=pl.Buffered(1))

    return pl.pallas_call(
        _ffn_kernel,
        grid=(N_TOK // TM,),
        in_specs=[
            pl.BlockSpec((TM, D_MODEL), lambda i: (i, 0)),
            pl.BlockSpec((None, 6, D_MODEL), lambda i: (_mod_row(i), 0, 0)),
            pl.BlockSpec((1, D_MODEL), lambda i: (0, 0)),
            resident((nb, D_MODEL, 2 * FFN_TF)),
            resident((nb, 3, 2 * FFN_TF)),
            resident((nb, 1, 2 * FFN_TF)),
            resident((nb, FFN_TF, D_MODEL)),
        ],
        out_specs=pl.BlockSpec((TM, D_MODEL), lambda i: (i, 0)),
        out_shape=jax.ShapeDtypeStruct((N_TOK, D_MODEL), F32),
        scratch_shapes=[pltpu.VMEM((TM, D_MODEL), BF16), pltpu.VMEM((TM, D_MODEL), F32),
                        pltpu.VMEM((TM, 2 * FFN_TF), F32), pltpu.VMEM((TM, 2 * FFN_TF), F32)],
        compiler_params=_params("arbitrary"),
        name="conv_ffn",
    )(x, mod, nw.reshape(1, D_MODEL), w_up, conv_w, conv_b, w_down)


ML_TN = 512
ML_EXT = ML_DV + LANES
ML_SEQ_PER_STEP = 2


def _ml_up_kernel(xc_src_ref, xl_src_ref, mod_ref, nw_ref, wm_ref, wz_ref, cw_ref, cb_ref, xm_ref, xc_ref, z_ref, h_scr, raw_scr):
    i = pl.program_id(0)

    def normalise(x_ref):
        h = _norm_mod(x_ref[...], nw_ref[...], mod_ref[0:1, :], mod_ref[1:2, :])
        h_scr[...] = h.astype(BF16)

    first_block = pl.program_id(1) == 0
    is_ctx = i < N_CTX_TILES
    pl.when(jnp.logical_and(first_block, is_ctx))(functools.partial(normalise, xc_src_ref))
    pl.when(jnp.logical_and(first_block, jnp.logical_not(is_ctx)))(functools.partial(normalise, xl_src_ref))

    def column_block(seq):
        h = h_scr[...]
        raw_scr[...] = jnp.dot(h, wm_ref[...], preferred_element_type=F32)
        z_ref[...] = jnp.dot(h, wz_ref[...], preferred_element_type=F32).astype(BF16)
        xm = raw_scr[...]
        xm_ref[...] = xm.astype(BF16)
        xc_ref[...] = _silu(_dwconv_rows(xm, cw_ref[...], cb_ref[...], seq)).astype(BF16)

    _by_tile_kind(i, column_block)


def _ml_up(x, mod, nw, w_up, conv_w, conv_b):
    nj = ML_D_IN // ML_TN
    x_specs, x_arrays = _row_sources(x, D_MODEL)
    col = pl.BlockSpec((TM, ML_TN), lambda i, j: (i, j))
    shp = jax.ShapeDtypeStruct((N_TOK, ML_D_IN), BF16)
    return pl.pallas_call(
        _ml_up_kernel,
        grid=(N_TOK // TM, nj),
        in_specs=x_specs + [
            pl.BlockSpec((None, 6, D_MODEL), lambda i, j: (_mod_row(i), 0, 0)),
            pl.BlockSpec((1, D_MODEL), lambda i, j: (0, 0)),
            pl.BlockSpec((D_MODEL, ML_TN), lambda i, j: (0, j)),
            pl.BlockSpec((D_MODEL, ML_TN), lambda i, j: (0, j + nj)),
            pl.BlockSpec((3, ML_TN), lambda i, j: (0, j)),
            pl.BlockSpec((1, ML_TN), lambda i, j: (0, j)),
        ],
        out_specs=[col, col, col],
        out_shape=[shp, shp, shp],
        scratch_shapes=[pltpu.VMEM((TM, D_MODEL), BF16), pltpu.VMEM((TM, ML_TN), F32)],
        compiler_params=_params("arbitrary", "arbitrary"),
        name="ml_up",
    )(*x_arrays, mod, nw.reshape(1, D_MODEL), w_up, w_up, conv_w, conv_b.reshape(1, ML_D_IN))


def _log_sigmoid(x):
    return jnp.minimum(x, 0.0) - jnp.log(1.0 + jnp.exp(-jnp.abs(x)))


def _ml_scan_kernel(*refs, seq, n_seq, has_init, out_state, n_unused_inputs):
    q_ref, kt_ref, v_ref, g_ref, gt_ref, xc_ref, z_ref, nw_ref, skip_ref = refs[:9]
    pos = 9
    if has_init:
        c0_ref, n0_ref, m0_ref = refs[pos:pos + 3]
        pos += 3
    pos += n_unused_inputs
    o_ref = refs[pos]
    pos += 1
    if out_state:
        cout_ref, nout_ref, mout_ref = refs[pos:pos + 3]
        pos += 3
    hs_scr, c_scr = refs[pos:pos + 2]

    L = ML_CHUNK
    nc = seq // L
    ext_tiles = ML_EXT // LANES
    row_i = lax.broadcasted_iota(jnp.int32, (L, L), 0)
    col_i = lax.broadcasted_iota(jnp.int32, (L, L), 1)
    ones_tile = jnp.ones((L, LANES), BF16)

    def lanes(a, n):
        return jnp.concatenate([a] * n, axis=1)

    for d in (0, 1):
        keep = (col_i <= row_i) if d == 0 else (col_i >= row_i)
        keep_f = keep.astype(F32)
        keep_t_f = ((row_i <= col_i) if d == 0 else (row_i >= col_i)).astype(F32)
        end = L - 1 if d == 0 else 0

        ms = []
        for s in range(n_seq):
            if has_init:
                c_scr[s, :, :ML_DV] = c0_ref[s, d]
                c_scr[s, :, ML_DV:] = n0_ref[s, d]
                ms.append(m0_ref[s, d:d + 1, :])
            else:
                c_scr[s] = jnp.zeros((ML_DK, ML_EXT), F32)
                ms.append(jnp.zeros((1, LANES), F32))

        for c, s in [(c, s) for c in range(nc) for s in range(n_seq)]:
            cc = c if d == 0 else nc - 1 - c
            chunk = s * nc + cc
            rows = slice(chunk * L, (chunk + 1) * L)
            m = ms[s]
            qc = q_ref[rows, :]
            ktc = kt_ref[chunk]
            v_ext = jnp.concatenate([v_ref[rows, :], ones_tile], axis=1)
            gcol = g_ref[chunk]
            grow = gt_ref[chunk]
            ig_row = grow[2 * d:2 * d + 1, :]
            lf_col = _log_sigmoid(gcol[:, 2 * d + 1:2 * d + 2])
            lf_row = _log_sigmoid(grow[2 * d + 1:2 * d + 2, :])
            b_col = jnp.broadcast_to(jnp.sum(keep_f * lf_row, axis=1, keepdims=True), (L, LANES))
            b_row = jnp.sum(keep_t_f * lf_col, axis=0, keepdims=True)
            dmat = jnp.where(keep, b_col - b_row + ig_row, NEG_INF)
            m_loc = jnp.broadcast_to(jnp.max(dmat, axis=1, keepdims=True), (L, LANES))
            p_loc = jnp.exp(dmat - m_loc)
            s_loc = jnp.dot(qc, ktc, preferred_element_type=F32) * p_loc
            intra = jnp.dot(s_loc.astype(BF16), v_ext, preferred_element_type=F32)

            m_t = jnp.maximum(b_col + m, m_loc)
            w_inter = jnp.exp(b_col + m - m_t)
            w_intra = jnp.exp(m_loc - m_t)
            inter = jnp.dot(qc, c_scr[s].astype(BF16), preferred_element_type=F32)
            hx = lanes(w_inter, ext_tiles) * inter + lanes(w_intra, ext_tiles) * intra
            inv = 1.0 / jnp.maximum(jnp.abs(hx[:, ML_DV:]), jnp.exp(-m_t))
            h = hx[:, :ML_DV] * lanes(inv, ML_DV // LANES)

            b_end = b_col[end:end + 1, :]
            m_loc_end = m_loc[end:end + 1, :]
            m_new = jnp.maximum(b_end + m, m_loc_end)
            w_c = jnp.exp(b_end + m - m_new)
            w_s = p_loc[end:end + 1, :] * jnp.exp(m_loc_end - m_new)
            upd = jnp.dot((ktc.astype(F32) * w_s).astype(BF16), v_ext, preferred_element_type=F32)
            c_scr[s] = lanes(w_c, ext_tiles) * c_scr[s] + upd
            ms[s] = m_new

            if d == 0:
                hs_scr[rows, :] = h
            else:
                hs = hs_scr[rows, :] + h
                mu = jnp.mean(hs, axis=1, keepdims=True)
                cen = hs - mu
                var = jnp.mean(cen * cen, axis=1, keepdims=True)
                hn = cen * lax.rsqrt(var + NORM_EPS) * nw_ref[...]
                xc = xc_ref[rows, :].astype(F32)
                z = z_ref[rows, :].astype(F32)
                o_ref[rows, :] = ((hn + skip_ref[...] * xc) * _silu(z)).astype(o_ref.dtype)

        if out_state:
            for s in range(n_seq):
                cout_ref[s, d] = c_scr[s, :, :ML_DV]
                nout_ref[s, d:d + 1, :] = c_scr[s, :, ML_DV:].T[0:1, :]
                mout_ref[s, d:d + 1, :] = ms[s]


def _ml_scan(q, kt, v, g4, g4t, xc, z, norm_w, skip, *, seq, nb, row0, init=None, final_c=None,
             n_seq=ML_SEQ_PER_STEP):
    L = ML_CHUNK
    rows = n_seq * seq
    nc = rows // L
    rb = row0 // rows
    has_init = init is not None
    in_specs = [
        pl.BlockSpec((rows, ML_DK), lambda b, h: (rb + b, h)),
        pl.BlockSpec((None, nc, ML_DK, L), lambda b, h: (h, rb + b, 0, 0)),
        pl.BlockSpec((rows, ML_DV), lambda b, h: (rb + b, h)),
        pl.BlockSpec((None, nc, L, 4), lambda b, h: (h, rb + b, 0, 0)),
        pl.BlockSpec((None, nc, 4, L), lambda b, h: (h, rb + b, 0, 0)),
        pl.BlockSpec((rows, ML_DV), lambda b, h: (rb + b, h)),
        pl.BlockSpec((rows, ML_DV), lambda b, h: (rb + b, h)),
        pl.BlockSpec((1, ML_DV), lambda b, h: (0, h)),
        pl.BlockSpec((1, ML_DV), lambda b, h: (0, h)),
    ]
    args = [q, kt, v, g4, g4t, xc, z, norm_w.reshape(1, ML_D_IN), skip.reshape(1, ML_D_IN)]
    state_n = pl.BlockSpec((n_seq, None, 2, ML_DK, LANES), lambda b, h: (b, h, 0, 0, 0))
    state_m = pl.BlockSpec((n_seq, None, 2, LANES), lambda b, h: (b, h, 0, 0))
    if has_init:
        c0, layer, n0, m0 = init
        in_specs += [pl.BlockSpec((n_seq, None, 2, None, ML_DK, ML_DV), lambda b, h: (b, layer, 0, h, 0, 0)),
                     state_n, state_m]
        args += [c0, n0, m0]
    out_specs = [pl.BlockSpec((rows, ML_DV), lambda b, h: (b, h))]
    out_shape = [jax.ShapeDtypeStruct((nb * seq, ML_D_IN), BF16)]
    aliases = {}
    if not has_init:
        layer_out, c_all = final_c
        out_specs += [pl.BlockSpec((n_seq, None, 2, None, ML_DK, ML_DV), lambda b, h: (b, layer_out, 0, h, 0, 0)),
                      pl.BlockSpec((n_seq, None, 2, ML_DK), lambda b, h: (b, h, 0, 0)), state_m]
        out_shape += [
            jax.ShapeDtypeStruct((nb, N_ML_LAYERS, 2, ML_HEADS, ML_DK, ML_DV), F32),
            jax.ShapeDtypeStruct((nb, ML_HEADS, 2, ML_DK), F32),
            jax.ShapeDtypeStruct((nb, ML_HEADS, 2, LANES), F32),
        ]
        if c_all is not None:
            in_specs.append(pl.BlockSpec(memory_space=pl.ANY))
            args.append(c_all)
            aliases = {len(args) - 1: 1}
    return pl.pallas_call(
        functools.partial(_ml_scan_kernel, seq=seq, n_seq=n_seq, has_init=has_init, out_state=not has_init,
                          n_unused_inputs=len(aliases)),
        grid=(nb // n_seq, ML_HEADS),
        in_specs=in_specs,
        out_specs=out_specs,
        out_shape=out_shape,
        scratch_shapes=[pltpu.VMEM((rows, ML_DV), F32), pltpu.VMEM((n_seq, ML_DK, ML_EXT), F32)],
        input_output_aliases=aliases,
        compiler_params=_params("arbitrary", "arbitrary"),
        name="ml_scan_init" if has_init else "ml_scan_zero",
    )(*args)


def _ml_qk_kernel(xc_ref, w_ref, q_ref, kt_ref):
    acc = jnp.dot(xc_ref[...], w_ref[...], preferred_element_type=F32)
    j = pl.program_id(1)

    @pl.when(j == 0)
    def _():
        q_ref[...] = (acc * ML_DK ** -0.5).astype(BF16)

    @pl.when(j == 1)
    def _():
        for h in range(ML_HEADS):
            kt = acc[:, h * ML_DK:(h + 1) * ML_DK].T
            for c in range(TM // ML_CHUNK):
                kt_ref[h, c] = kt[:, c * ML_CHUNK:(c + 1) * ML_CHUNK].astype(BF16)


def _ml_qk(xc, w_qk):
    n = ML_HEADS * ML_DK
    cpt = TM // ML_CHUNK
    return pl.pallas_call(
        _ml_qk_kernel,
        grid=(N_TOK // TM, 2),
        in_specs=[pl.BlockSpec((TM, ML_D_IN), lambda i, j: (i, 0)),
                  pl.BlockSpec((ML_D_IN, n), lambda i, j: (0, j))],
        out_specs=[pl.BlockSpec((TM, n), lambda i, j: (i, 0)),
                   pl.BlockSpec((ML_HEADS, cpt, ML_DK, ML_CHUNK), lambda i, j: (0, i, 0, 0))],
        out_shape=[jax.ShapeDtypeStruct((N_TOK, n), BF16),
                   jax.ShapeDtypeStruct((ML_HEADS, N_TOK // ML_CHUNK, ML_DK, ML_CHUNK), BF16)],
        compiler_params=_params("arbitrary", "arbitrary"),
        name="ml_qk",
    )(xc, w_qk)


def _mlstm_layer(x, mod, nw, p, state, final_c):
    xm, xc, z = _ml_up(x, mod, nw, p["w_up"], p["conv_w"], p["conv_b"])
    q, kt = _ml_qk(xc, p["w_qk"])
    v = _mm(xm, p["w_v"], jnp.zeros((1, ML_D_IN), F32), 1024, BF16)
    g = _mm(xm, p["w_gate"], p["b_gate"], 128, F32)[:, :4 * ML_HEADS]
    L = ML_CHUNK
    g4 = g.reshape(N_TOK // L, L, 4, ML_HEADS).transpose(3, 0, 1, 2)
    g4t = g4.transpose(0, 1, 3, 2)
    oc, c_fin, n_fin, m_fin = _ml_scan(q, kt, v, g4, g4t, xc, z, p["norm_w"], p["skip"],
                                       seq=SEQ, nb=BATCH, row0=0, final_c=final_c)
    (ol,) = _ml_scan(q, kt, v, g4, g4t, xc, z, p["norm_w"], p["skip"],
                     seq=DEC_SEQ, nb=DEC_BATCH, row0=N_CTX, init=state, n_seq=1)
    x = _mm_res(oc, ol, p["w_down"], x, mod)
    return x, c_fin, n_fin.transpose(0, 2, 1, 3), m_fin[..., 0].transpose(0, 2, 1)


def _softmax_parts(scores, sink_col):
    m = functools.reduce(jnp.maximum, [jnp.max(s, axis=1, keepdims=True) for s in scores])
    if sink_col is not None:
        m = jnp.maximum(m, sink_col)
    ps = [jnp.exp(s - m) for s in scores]
    den = functools.reduce(jnp.add, [jnp.sum(p, axis=1, keepdims=True) for p in ps])
    if sink_col is not None:
        den = den + jnp.exp(sink_col - m)
    return ps, den


def _qk(q, k):
    return lax.dot_general(q, k, (((1,), (1,)), ((), ())), preferred_element_type=F32) * (HEAD_DIM ** -0.5)


def _attend_tiles(q_tiles, parts, sinks):
    return _attend_groups([(q_tiles, parts, sinks)])[0]


def _attend_groups(groups):
    r = groups[0][0][0].shape[0]
    lo = lax.broadcasted_iota(jnp.int32, (r, LANES), 1) < HEAD_DIM
    zero = jnp.zeros((r, LANES), BF16)
    all_scores = []
    for q_tiles, parts, _ in groups:
        qs = jnp.concatenate([jnp.where(sel, t, zero) for t in q_tiles for sel in (lo, jnp.logical_not(lo))], axis=0)
        all_scores.append([post(_qk(qs, k2)) for k2, _, post in parts])
    outs = []
    for scores, (q_tiles, parts, sinks) in zip(all_scores, groups):
        n_rows = scores[0].shape[0]
        sink_col = None
        if sinks is not None:
            rows = lax.broadcasted_iota(jnp.int32, (n_rows, 1), 0)
            sink_col = jnp.full((n_rows, 1), sinks[0], F32)
            for hi in range(1, len(sinks)):
                sink_col = jnp.where(rows >= hi * r, sinks[hi], sink_col)
        ps, den = _softmax_parts(scores, sink_col)
        o = functools.reduce(jnp.add, [jnp.dot(p.astype(BF16), v2, preferred_element_type=F32)
                                       for p, (_, v2, _) in zip(ps, parts)]) / den
        outs.append([jnp.where(lo, o[2 * a * r:(2 * a + 1) * r], o[(2 * a + 1) * r:(2 * a + 2) * r])
                     for a in range(len(q_tiles))])
    return outs


def _identity(s):
    return s


def _ctx_attn_kernel(*refs, tiles_per_kv, has_sink, kv_head_stride):
    if has_sink:
        sink_ref, q_ref, k_ref, v_ref, o_ref, kout_ref, vout_ref = refs
    else:
        q_ref, k_ref, v_ref, o_ref, kout_ref, vout_ref = refs
    for h in range(kout_ref.shape[0]):
        kout_ref[h] = k_ref[:, h * kv_head_stride:h * kv_head_stride + HEAD_DIM]
        vout_ref[h] = v_ref[:, h * kv_head_stride:h * kv_head_stride + HEAD_DIM]
    groups = []
    for t in range(k_ref.shape[1] // LANES):
        k2 = k_ref[:, t * LANES:(t + 1) * LANES].astype(BF16)
        v2 = v_ref[:, t * LANES:(t + 1) * LANES].astype(BF16)
        first = t * tiles_per_kv
        q_tiles = [q_ref[:, (first + a) * LANES:(first + a + 1) * LANES] for a in range(tiles_per_kv)]
        sinks = [sink_ref[2 * first + hi] for hi in range(2 * tiles_per_kv)] if has_sink else None
        groups.append((q_tiles, [(k2, v2, _identity)], sinks))
    for t, outs in enumerate(_attend_groups(groups)):
        for a, o in enumerate(outs):
            tile = t * tiles_per_kv + a
            o_ref[:, tile * LANES:(tile + 1) * LANES] = o.astype(o_ref.dtype)


def _ctx_attn(q, kv, n_kv_cols, n_kv_heads, sink):
    nq = q.shape[1]
    has_sink = sink is not None
    cache_spec = pl.BlockSpec((None, None, n_kv_heads, SEQ, HEAD_DIM), lambda b: (b, 0, 0, 0, 0))
    cache_shape = jax.ShapeDtypeStruct((BATCH, 1, n_kv_heads, SEQ, HEAD_DIM), F32)
    in_specs = [pl.BlockSpec((SEQ, nq), lambda b: (b, 0)),
                pl.BlockSpec((SEQ, n_kv_cols), lambda b: (b, 0)),
                pl.BlockSpec((SEQ, n_kv_cols), lambda b: (b, 1))]
    args = [q, kv, kv]
    if has_sink:
        in_specs = [pl.BlockSpec(memory_space=pltpu.SMEM)] + in_specs
        args = [sink] + args
    return pl.pallas_call(
        functools.partial(_ctx_attn_kernel, tiles_per_kv=nq // n_kv_cols, has_sink=has_sink,
                          kv_head_stride=n_kv_cols // n_kv_heads),
        grid=(BATCH,),
        in_specs=in_specs,
        out_specs=[pl.BlockSpec((SEQ, nq), lambda b: (b, 0)), cache_spec, cache_spec],
        out_shape=[jax.ShapeDtypeStruct((N_CTX, nq), BF16), cache_shape, cache_shape],
        compiler_params=_params("arbitrary"),
        name="ctx_attn",
    )(*args)


SWA_SPAN = Q_BLOCK + 2 * SWA_WINDOW
SWA_KV_COLS = SWA_KV * LANES


def _swa_lat_kernel(sink_ref, q_ref, k_ref, v_ref, kc_ref, vc_ref, o_ref):
    j = pl.program_id(1)
    start = pl.multiple_of(jnp.clip((j - 1) * Q_BLOCK, 0, DEC_SEQ - SWA_SPAN), Q_BLOCK)
    rows = 4 * Q_BLOCK
    qpos = j * Q_BLOCK + (lax.broadcasted_iota(jnp.int32, (rows, SWA_SPAN), 0) & (Q_BLOCK - 1))
    kpos = start + lax.broadcasted_iota(jnp.int32, (rows, SWA_SPAN), 1)
    in_window = jnp.abs(qpos - kpos) <= SWA_WINDOW

    def window(s):
        return jnp.where(in_window, s, NEG_INF)

    groups = []
    for t in range(SWA_KV):
        cols = slice(t * LANES, (t + 1) * LANES)
        k_loc = k_ref[pl.ds(start, SWA_SPAN), cols].astype(BF16)
        v_loc = v_ref[pl.ds(start, SWA_SPAN), cols].astype(BF16)
        q_tiles = [q_ref[:, (2 * t + a) * LANES:(2 * t + a + 1) * LANES] for a in range(2)]
        sinks = [sink_ref[4 * t + hi] for hi in range(4)]
        groups.append((q_tiles, [(k_loc, v_loc, window), (kc_ref[:, cols], vc_ref[:, cols], _identity)], sinks))
    for t, outs in enumerate(_attend_groups(groups)):
        for a, o in enumerate(outs):
            o_ref[:, (2 * t + a) * LANES:(2 * t + a + 1) * LANES] = o.astype(o_ref.dtype)


def _swa_latent(q, kv, kc, vc, sink):
    nq = q.shape[1]
    rb = N_CTX // DEC_SEQ
    qb = N_CTX // Q_BLOCK
    nj = DEC_SEQ // Q_BLOCK
    cspec = pl.BlockSpec((None, SEQ, SWA_KV_COLS), lambda b, j: (b, 0, 0))
    return pl.pallas_call(
        _swa_lat_kernel,
        grid=(DEC_BATCH, nj),
        in_specs=[pl.BlockSpec(memory_space=pltpu.SMEM),
                  pl.BlockSpec((Q_BLOCK, nq), lambda b, j: (qb + b * nj + j, 0)),
                  pl.BlockSpec((DEC_SEQ, SWA_KV_COLS), lambda b, j: (rb + b, 0)),
                  pl.BlockSpec((DEC_SEQ, SWA_KV_COLS), lambda b, j: (rb + b, 1)),
                  cspec, cspec],
        out_specs=pl.BlockSpec((Q_BLOCK, nq), lambda b, j: (b * nj + j, 0)),
        out_shape=jax.ShapeDtypeStruct((N_LAT, nq), BF16),
        compiler_params=_params("arbitrary", "arbitrary"),
        name="swa_latent",
    )(sink, q, kv, kv, kc, vc)


NA_QT = 256
NA_SPAN = 768
NA_TILES = 4


def _na_start(j):
    return (j // 2) * (DEC_SEQ - NA_SPAN)


NA_ROWS = DEC_SEQ // GRID_W
NA_DR = 2 * NA_KH - 1
NA_DC = 2 * NA_KW - 1


def _na_blocks_kernel(rpb_ref, onehot_ref, valid_ref, o_ref):
    t = jnp.dot(rpb_ref[...], onehot_ref[...], preferred_element_type=F32, precision=lax.Precision.HIGHEST)
    o_ref[...] = jnp.where(valid_ref[...] > 0.5, t, NEG_INF)


def _na_bias_blocks(rpb):
    h = rpb.shape[0]
    kpad = 32
    cq, ck = np.meshgrid(np.arange(GRID_W), np.arange(GRID_W), indexing="ij")
    dc = (np.clip(ck - cq, -(NA_KW - 1), NA_KW - 1) + NA_KW - 1).reshape(-1)
    cs = np.clip(cq - NA_KW // 2, 0, GRID_W - NA_KW)
    valid = ((ck >= cs) & (ck < cs + NA_KW)).reshape(1, -1).astype(np.float32)
    onehot = (np.arange(kpad)[:, None] == dc[None, :]).astype(np.float32)
    rpb2 = jnp.pad(rpb.reshape(h * NA_DR, NA_DC), ((0, 0), (0, kpad - NA_DC)))
    n = GRID_W * GRID_W
    blocks = pl.pallas_call(
        _na_blocks_kernel,
        grid=(1,),
        in_specs=[pl.BlockSpec((h * NA_DR, kpad), lambda i: (0, 0)),
                  pl.BlockSpec((kpad, n), lambda i: (0, 0)),
                  pl.BlockSpec((1, n), lambda i: (0, 0))],
        out_specs=pl.BlockSpec((h * NA_DR, n), lambda i: (0, 0)),
        out_shape=jax.ShapeDtypeStruct((h * NA_DR, n), F32),
        compiler_params=_params("arbitrary"),
        name="na_bias_blocks",
    )(rpb2, jnp.asarray(onehot), jnp.asarray(valid))
    blocks = blocks.reshape(h, NA_DR, GRID_W, GRID_W)
    padded = jnp.pad(blocks, ((0, 0), (1, 1), (0, 0), (0, 0)), constant_values=NEG_INF)
    return jnp.concatenate([padded[:, :-1], padded[:, 1:]], axis=-1)


def _na_lat_kernel(q_ref, k_ref, v_ref, kc_ref, vc_ref, blk_ref, o_ref, bias_scr):
    j = pl.program_id(0)
    start = pl.multiple_of(_na_start(j), 256)

    @pl.when(pl.program_id(2) == 0)
    def _():
        lane_lo = lax.broadcasted_iota(jnp.int32, (GRID_W, LANES), 1) < GRID_W
        for rq_l in range(NA_QT // GRID_W):
            rq = j * (NA_QT // GRID_W) + rq_l
            rs = jnp.clip(rq - NA_KH // 2, 0, NA_ROWS - NA_KH)
            for kp in range(NA_SPAN // LANES):
                rk = start // GRID_W + 2 * kp
                idx = jnp.clip(rk - rq + NA_KH, 0, NA_DR)
                in_band = [jnp.logical_and(r >= rs, r < rs + NA_KH).astype(jnp.int32) for r in (rk, rk + 1)]
                ok = jnp.where(lane_lo, in_band[0], in_band[1]) > 0
                for hh in range(2 * NA_TILES):
                    bias_scr[hh * NA_QT + rq_l * GRID_W:hh * NA_QT + (rq_l + 1) * GRID_W,
                             kp * LANES:(kp + 1) * LANES] = jnp.where(ok, blk_ref[hh, idx], NEG_INF)

    groups = []
    for t in range(NA_TILES):
        cols = slice(t * LANES, (t + 1) * LANES)

        def add_bias(s, t=t):
            return s + bias_scr[2 * t * NA_QT:2 * (t + 1) * NA_QT, :]

        k_loc = k_ref[pl.ds(start, NA_SPAN), cols].astype(BF16)
        v_loc = v_ref[pl.ds(start, NA_SPAN), cols].astype(BF16)
        groups.append(([q_ref[:, cols]], [(k_loc, v_loc, add_bias), (kc_ref[:, cols], vc_ref[:, cols], _identity)],
                       None))
    for t, (o,) in enumerate(_attend_groups(groups)):
        o_ref[:, t * LANES:(t + 1) * LANES] = o.astype(o_ref.dtype)


def _na_latent(q, kv, kc, vc, blocks):
    nq = q.shape[1]
    width = NA_TILES * LANES
    n_steps = nq // width
    nj = DEC_SEQ // NA_QT
    rb = N_CTX // DEC_SEQ
    qb = N_CTX // NA_QT
    cspec = pl.BlockSpec((None, SEQ, width), lambda j, p, b: (b, 0, p))
    return pl.pallas_call(
        _na_lat_kernel,
        grid=(nj, n_steps, DEC_BATCH),
        in_specs=[pl.BlockSpec((NA_QT, width), lambda j, p, b: (qb + b * nj + j, p)),
                  pl.BlockSpec((DEC_SEQ, width), lambda j, p, b: (rb + b, p)),
                  pl.BlockSpec((DEC_SEQ, width), lambda j, p, b: (rb + b, n_steps + p)),
                  cspec, cspec,
                  pl.BlockSpec((2 * NA_TILES, NA_DR + 1, GRID_W, LANES), lambda j, p, b: (p, 0, 0, 0))],
        out_specs=pl.BlockSpec((NA_QT, width), lambda j, p, b: (b * nj + j, p)),
        out_shape=jax.ShapeDtypeStruct((N_LAT, nq), BF16),
        scratch_shapes=[pltpu.VMEM((2 * NA_TILES * NA_QT, NA_SPAN), F32)],
        compiler_params=_params("arbitrary", "arbitrary", "arbitrary"),
        name="na_latent",
    )(q, kv, kv, kc, vc, blocks)


def _rope_tables(width):
    quarter = HEAD_DIM // 4
    pos = np.arange(DEC_SEQ)
    inv = np.power(ROPE_BASE, -np.arange(quarter, dtype=np.float32) / quarter).astype(np.float32)
    d = np.arange(width) % HEAD_DIM
    p = np.where((d < HEAD_DIM // 2)[None, :], (pos // GRID_W)[:, None], (pos % GRID_W)[:, None]).astype(np.float32)
    ang = p * inv[d % quarter][None, :]
    sign = np.where((d // quarter) % 2 == 0, -1.0, 1.0)[None, :]
    return jnp.asarray(np.cos(ang), F32), jnp.asarray(np.sin(ang) * sign, F32)


def _cache_rows(cache, dup):
    b, h, s, hd = cache.shape
    rows = jnp.broadcast_to(cache.transpose(0, 2, 1, 3)[:, :, :, None, :], (b, s, h, dup, hd))
    return rows.reshape(b, s, h * dup * hd).astype(BF16)


def _swa_layer(x, mod, nw, w_qkv, sink, w_o, cache_k, cache_v):
    nq, nk = SWA_HEADS * HEAD_DIM, SWA_KV * HEAD_DIM

    def dup_heads(w):
        return jnp.broadcast_to(w.reshape(D_MODEL, SWA_KV, 1, HEAD_DIM),
                                (D_MODEL, SWA_KV, 2, HEAD_DIM)).reshape(D_MODEL, SWA_KV_COLS)

    w = jnp.concatenate([w_qkv[:, :nq], dup_heads(w_qkv[:, nq:nq + nk]), dup_heads(w_qkv[:, nq + nk:])], axis=1)
    q, kv = _qkv_proj(x, mod, nw, w.astype(BF16), nq, rope_cols=nq + SWA_KV_COLS)
    oc, k_new, v_new = _ctx_attn(q, kv, SWA_KV_COLS, SWA_KV, sink)
    ol = _swa_latent(q, kv, _cache_rows(cache_k, 2), _cache_rows(cache_v, 2), sink)
    return _mm_res(oc, ol, w_o, x, mod), k_new, v_new


def _na_layer(x, mod, nw, w_qkv, rpb, w_o, cache_k, cache_v):
    n = NA_HEADS * HEAD_DIM
    q, kv = _qkv_proj(x, mod, nw, w_qkv, n)
    oc, k_new, v_new = _ctx_attn(q, kv, n, NA_HEADS, None)
    ol = _na_latent(q, kv, _cache_rows(cache_k, 1), _cache_rows(cache_v, 1), _na_bias_blocks(rpb))
    return _mm_res(oc, ol, w_o, x, mod), k_new, v_new


def _final_norm_kernel(x_ref, w_ref, oc_ref, ol_ref):
    x = x_ref[...]
    y = x * lax.rsqrt(jnp.mean(x * x, axis=-1, keepdims=True) + NORM_EPS) * w_ref[...]
    is_ctx = pl.program_id(0) < N_CTX_TILES

    @pl.when(is_ctx)
    def _():
        oc_ref[...] = y

    @pl.when(jnp.logical_not(is_ctx))
    def _():
        ol_ref[...] = y


def _final_norm(x, w):
    return pl.pallas_call(
        _final_norm_kernel,
        grid=(N_TOK // TM,),
        in_specs=[pl.BlockSpec((TM, D_MODEL), lambda i: (i, 0)), pl.BlockSpec((1, D_MODEL), lambda i: (0, 0))],
        out_specs=[pl.BlockSpec((TM, D_MODEL), lambda i: (jnp.minimum(i, N_CTX_TILES - 1), 0)),
                   pl.BlockSpec((TM, D_MODEL), lambda i: (jnp.maximum(i - N_CTX_TILES, 0), 0))],
        out_shape=[jax.ShapeDtypeStruct((N_CTX, D_MODEL), F32), jax.ShapeDtypeStruct((N_LAT, D_MODEL), F32)],
        compiler_params=_params("arbitrary"),
        name="final_norm",
    )(x, w.reshape(1, D_MODEL))


def kernel(x_prompt, x_sample, state_mlstm_C, state_mlstm_n, state_mlstm_m, cache_swa_k, cache_swa_v, cache_na_k, cache_na_v, c, c_ctx, ada_w, ada_b, norm_w, final_norm_w, ffn_w_up, ffn_conv_w, ffn_conv_b, ffn_w_down, ml_w_up, ml_conv_w, ml_conv_b, ml_w_qk, ml_w_v, ml_w_gate, ml_b_gate, ml_norm_w, ml_skip, ml_w_down, swa_w_qkv, swa_sink, swa_w_o, na_w_qkv, na_rpb, na_w_o):
    x = (x_prompt.reshape(N_CTX, D_MODEL), x_sample.reshape(N_LAT, D_MODEL))
    cond = jnp.concatenate([c_ctx[None], c, jnp.zeros((MOD_ROWS - 1 - DEC_BATCH, D_MODEL), F32)], axis=0)
    mods = _ada_mod(cond, ada_w, ada_b)

    new_c, new_n, new_m = None, [], []
    new_sk = new_sv = new_nk = new_nv = None
    for i in range(DEPTH):
        kind, j = i % N_MIXERS, i // N_MIXERS
        mod = mods[i]
        if kind == 0:
            gate_w = jnp.pad(ml_w_gate[j], ((0, 0), (0, 128 - 4 * ML_HEADS))).astype(BF16)
            gate_b = jnp.pad(ml_b_gate[j], (0, 128 - 4 * ML_HEADS)).reshape(1, 128)
            p = dict(w_up=ml_w_up[j].astype(BF16), conv_w=ml_conv_w[j], conv_b=ml_conv_b[j],
                     w_qk=ml_w_qk[j].astype(BF16), w_v=ml_w_v[j].astype(BF16), w_gate=gate_w, b_gate=gate_b,
                     norm_w=ml_norm_w[j], skip=ml_skip[j], w_down=ml_w_down[j].astype(BF16))
            n0 = jnp.broadcast_to(state_mlstm_n[:, j].transpose(0, 2, 1, 3)[..., None],
                                  (DEC_BATCH, ML_HEADS, 2, ML_DK, LANES))
            m0 = jnp.broadcast_to(state_mlstm_m[:, j].transpose(0, 2, 1)[..., None], (DEC_BATCH, ML_HEADS, 2, LANES))
            x, new_c, nf, mf = _mlstm_layer(x, mod, norm_w[i, 0], p, (state_mlstm_C, j, n0, m0), (j, new_c))
            new_n.append(nf)
            new_m.append(mf)
        elif kind == 1:
            x, k_new, v_new = _swa_layer(x, mod, norm_w[i, 0], swa_w_qkv[j].astype(BF16), swa_sink[j],
                                         swa_w_o[j].astype(BF16), cache_swa_k[:, j], cache_swa_v[:, j])
            new_sk, new_sv = k_new, v_new
        else:
            x, k_new, v_new = _na_layer(x, mod, norm_w[i, 0], na_w_qkv[j].astype(BF16), na_rpb[j],
                                        na_w_o[j].astype(BF16), cache_na_k[:, j], cache_na_v[:, j])
            new_nk, new_nv = k_new, v_new
        x = _conv_ffn(x, mod, norm_w[i, 1], ffn_w_up[i].astype(BF16), ffn_conv_w[i], ffn_conv_b[i],
                      ffn_w_down[i].astype(BF16))

    y_ctx, y_lat = _final_norm(x, final_norm_w)
    return (y_ctx.reshape(BATCH, SEQ, D_MODEL), y_lat.reshape(DEC_BATCH, DEC_SEQ, D_MODEL),
            new_c, jnp.stack(new_n, axis=1), jnp.stack(new_m, axis=1),
            new_sk, new_sv, new_nk, new_nv)
```

```python
import functools

import jax
import jax.numpy as jnp
import numpy as np
from jax import lax
from jax.experimental import pallas as pl
from jax.experimental.pallas import tpu as pltpu

F32 = jnp.float32
BF16 = jnp.bfloat16

D_MODEL = 1024
BATCH = 32
SEQ = 256
DEPTH = 4
DEC_BATCH = 8
DEC_SEQ = 1024
GRID_W = 64
N_MIXERS = 3
N_ML_LAYERS = (DEPTH + 2) // 3
NORM_EPS = 1e-6
D_FF = 2816
ML_D_IN = 2 * D_MODEL
ML_HEADS = 4
ML_DK = ML_D_IN // (2 * ML_HEADS)
ML_DV = ML_D_IN // ML_HEADS
ML_CHUNK = 128
HEAD_DIM = 64
SWA_HEADS = D_MODEL // HEAD_DIM
SWA_KV = SWA_HEADS // 4
SWA_WINDOW = 128
Q_BLOCK = 128
ROPE_BASE = 10000.0
NA_HEADS = D_MODEL // HEAD_DIM
NA_KH = 8
NA_KW = 16
NEG_INF = -1e30

N_CTX = BATCH * SEQ
N_LAT = DEC_BATCH * DEC_SEQ
N_TOK = N_CTX + N_LAT
TM = 1024
N_CTX_TILES = N_CTX // TM
MOD_ROWS = 16
VMEM_LIMIT_BYTES = 56 * 1024 * 1024


def _params(*sem, vmem_limit_bytes=VMEM_LIMIT_BYTES):
    return pltpu.CompilerParams(dimension_semantics=sem, vmem_limit_bytes=vmem_limit_bytes)


def _mod_row(i):
    return jnp.where(i < N_CTX_TILES, 0, i - (N_CTX_TILES - 1))


def _silu(x):
    return x / (1.0 + jnp.exp(-x))


def _norm_mod(x, nw, shift, scale):
    y = x * lax.rsqrt(jnp.mean(x * x, axis=-1, keepdims=True) + NORM_EPS) * nw
    return y * (1.0 + scale) + shift


SUBLANES = 8
LANES = 128


def _dwconv_rows(u, cw, cb, seq):
    r, c = u.shape
    n_groups, per_seq = r // SUBLANES, seq // SUBLANES
    g = u.reshape(n_groups, SUBLANES, c)
    sub = lax.broadcasted_iota(jnp.int32, g.shape, 1)
    down = pltpu.roll(g, 1, 1)
    up = pltpu.roll(g, SUBLANES - 1, 1)
    zero = jnp.zeros((1, SUBLANES, c), F32)
    from_prev, from_next = [], []
    for s in range(0, n_groups, per_seq):
        from_prev += [zero, down[s:s + per_seq - 1]]
        from_next += [up[s + 1:s + per_seq], zero]
    prev = jnp.where(sub == 0, jnp.concatenate(from_prev, axis=0), down)
    nxt = jnp.where(sub == SUBLANES - 1, jnp.concatenate(from_next, axis=0), up)
    out = cw[0:1, :] * prev + cw[1:2, :] * g + cw[2:3, :] * nxt + cb
    return out.reshape(r, c)


def _by_tile_kind(tile, body):
    pl.when(tile < N_CTX_TILES)(functools.partial(body, SEQ))
    pl.when(tile >= N_CTX_TILES)(functools.partial(body, DEC_SEQ))


def _ada_kernel(c_ref, w_ref, b_ref, o_ref):
    s = _silu(c_ref[...]).astype(BF16)
    o_ref[...] = jnp.dot(s, w_ref[...].astype(BF16), preferred_element_type=F32) + b_ref[...]


def _ada_mod(cond, ada_w, ada_b):
    tn = 1536
    n = 6 * D_MODEL
    out = pl.pallas_call(
        _ada_kernel,
        grid=(DEPTH, n // tn),
        in_specs=[
            pl.BlockSpec((MOD_ROWS, D_MODEL), lambda l, j: (0, 0)),
            pl.BlockSpec((None, D_MODEL, tn), lambda l, j: (l, 0, j)),
            pl.BlockSpec((None, 1, tn), lambda l, j: (l, 0, j)),
        ],
        out_specs=pl.BlockSpec((None, MOD_ROWS, tn), lambda l, j: (l, 0, j)),
        out_shape=jax.ShapeDtypeStruct((DEPTH, MOD_ROWS, n), F32),
        compiler_params=_params("arbitrary", "arbitrary"),
        name="ada_mod",
    )(cond, ada_w, ada_b.reshape(DEPTH, 1, n))
    return out.reshape(DEPTH, MOD_ROWS, 6, D_MODEL)


QKV_TN = 512


def _rotate_pairs(a, cos, sin):
    lane = lax.broadcasted_iota(jnp.int32, a.shape, 1)
    first = (lane & (HEAD_DIM // 4)) == 0
    n = a.shape[1]
    partner = jnp.where(first, pltpu.roll(a, n - HEAD_DIM // 4, 1), pltpu.roll(a, HEAD_DIM // 4, 1))
    return a * cos + partner * sin


def _qkv_kernel(*refs, q_blocks, rope_blocks):
    if rope_blocks:
        x_ref, mod_ref, nw_ref, w_ref, cos_ref, sin_ref, q_ref, kv_ref, h_scr = refs
    else:
        x_ref, mod_ref, nw_ref, w_ref, q_ref, kv_ref, h_scr = refs
    i = pl.program_id(0)
    j = pl.program_id(1)

    @pl.when(j == 0)
    def _():
        h = _norm_mod(x_ref[...], nw_ref[...], mod_ref[0:1, :], mod_ref[1:2, :])
        h_scr[...] = h.astype(BF16)

    acc = jnp.dot(h_scr[...], w_ref[...], preferred_element_type=F32)

    def emit(val):
        @pl.when(j < q_blocks)
        def _():
            q_ref[...] = val.astype(BF16)

        @pl.when(j >= q_blocks)
        def _():
            kv_ref[...] = val

    if rope_blocks:
        rotate = jnp.logical_and(i >= N_CTX_TILES, j < rope_blocks)

        @pl.when(rotate)
        def _():
            emit(_rotate_pairs(acc, cos_ref[...], sin_ref[...]))

        @pl.when(jnp.logical_not(rotate))
        def _():
            emit(acc)
    else:
        emit(acc)


def _qkv_proj(x, mod, nw, w, n_q, rope_cols=0):
    n = w.shape[1]
    q_blocks = n_q // QKV_TN
    rope_blocks = rope_cols // QKV_TN
    in_specs = [
        pl.BlockSpec((TM, D_MODEL), lambda i, j: (i, 0)),
        pl.BlockSpec((None, 6, D_MODEL), lambda i, j: (_mod_row(i), 0, 0)),
        pl.BlockSpec((1, D_MODEL), lambda i, j: (0, 0)),
        pl.BlockSpec((D_MODEL, QKV_TN), lambda i, j: (0, j)),
    ]
    args = [x, mod, nw.reshape(1, D_MODEL), w]
    if rope_blocks:
        tab = pl.BlockSpec((DEC_SEQ, QKV_TN), lambda i, j: (0, 0))
        in_specs += [tab, tab]
        args += list(_rope_tables(QKV_TN))
    return pl.pallas_call(
        functools.partial(_qkv_kernel, q_blocks=q_blocks, rope_blocks=rope_blocks),
        grid=(N_TOK // TM, n // QKV_TN),
        in_specs=in_specs,
        out_specs=[pl.BlockSpec((TM, QKV_TN), lambda i, j: (i, jnp.minimum(j, q_blocks - 1))),
                   pl.BlockSpec((TM, QKV_TN), lambda i, j: (i, jnp.maximum(j - q_blocks, 0)))],
        out_shape=[jax.ShapeDtypeStruct((N_TOK, n_q), BF16), jax.ShapeDtypeStruct((N_TOK, n - n_q), F32)],
        scratch_shapes=[pltpu.VMEM((TM, D_MODEL), BF16)],
        compiler_params=_params("arbitrary", "arbitrary"),
        name="qkv_proj",
    )(*args)


def _mm_kernel(a_ref, w_ref, b_ref, o_ref):
    acc = jnp.dot(a_ref[...], w_ref[...], preferred_element_type=F32)
    o_ref[...] = (acc + b_ref[...]).astype(o_ref.dtype)


def _mm(a, w, bias, tn, out_dtype):
    m, k = a.shape
    n = w.shape[1]
    return pl.pallas_call(
        _mm_kernel,
        grid=(m // TM, n // tn),
        in_specs=[
            pl.BlockSpec((TM, k), lambda i, j: (i, 0)),
            pl.BlockSpec((k, tn), lambda i, j: (0, j)),
            pl.BlockSpec((1, tn), lambda i, j: (0, j)),
        ],
        out_specs=pl.BlockSpec((TM, tn), lambda i, j: (i, j)),
        out_shape=jax.ShapeDtypeStruct((m, n), out_dtype),
        compiler_params=_params("arbitrary", "arbitrary"),
        name="mm",
    )(a, w, bias)


def _row_sources(x, width):
    (ctx, ctx_first), (lat, lat_first) = (((x[0], 0), (x[1], 0)) if isinstance(x, tuple)
                                          else ((x, 0), (x, N_CTX_TILES)))

    def spec(first, lo, hi):
        return pl.BlockSpec((TM, width), lambda i, *_: (jnp.clip(i, lo, hi) - lo + first, 0))

    return ([spec(ctx_first, 0, N_CTX_TILES - 1), spec(lat_first, N_CTX_TILES, N_TOK // TM - 1)], [ctx, lat])


def _mm_res_kernel(ac_ref, al_ref, w_ref, xc_ref, xl_ref, mod_ref, o_ref):
    def emit(a_ref, x_ref):
        acc = jnp.dot(a_ref[...], w_ref[...], preferred_element_type=F32)
        o_ref[...] = x_ref[...] + mod_ref[2:3, :] * acc

    is_ctx = pl.program_id(0) < N_CTX_TILES
    pl.when(is_ctx)(functools.partial(emit, ac_ref, xc_ref))
    pl.when(jnp.logical_not(is_ctx))(functools.partial(emit, al_ref, xl_ref))


def _mm_res(a_ctx, a_lat, w, x, mod):
    k = a_ctx.shape[1]
    a_specs, a_arrays = _row_sources((a_ctx, a_lat), k)
    x_specs, x_arrays = _row_sources(x, D_MODEL)
    return pl.pallas_call(
        _mm_res_kernel,
        grid=(N_TOK // TM,),
        in_specs=a_specs + [pl.BlockSpec((k, D_MODEL), lambda i: (0, 0))] + x_specs
        + [pl.BlockSpec((None, 6, D_MODEL), lambda i: (_mod_row(i), 0, 0))],
        out_specs=pl.BlockSpec((TM, D_MODEL), lambda i: (i, 0)),
        out_shape=jax.ShapeDtypeStruct((N_TOK, D_MODEL), F32),
        compiler_params=_params("arbitrary"),
        name="mm_res",
    )(*a_arrays, w, *x_arrays, mod)


FFN_TF = 256
FFN_BLOCKS = D_FF // FFN_TF


def _ffn_kernel(x_ref, mod_ref, nw_ref, wup_ref, cw_ref, cb_ref, wd_ref, o_ref, h_scr, acc_scr, raw_a, raw_b):
    nb = FFN_BLOCKS
    h_scr[...] = _norm_mod(x_ref[...], nw_ref[...], mod_ref[3:4, :], mod_ref[4:5, :]).astype(BF16)
    acc_scr[...] = jnp.zeros_like(acc_scr)

    def run(seq):
        def project(k, raw):
            raw[...] = jnp.dot(h_scr[...], wup_ref[k], preferred_element_type=F32)

        def consume(k, raw):
            u = _dwconv_rows(raw[...], cw_ref[k], cb_ref[k], seq)
            a = (_silu(u[:, :FFN_TF]) * u[:, FFN_TF:]).astype(BF16)
            acc_scr[...] += jnp.dot(a, wd_ref[k], preferred_element_type=F32)

        project(0, raw_a)

        def two_stages(t, carry):
            k = 2 * t
            project(k + 1, raw_b)
            consume(k, raw_a)
            project(k + 2, raw_a)
            consume(k + 1, raw_b)
            return carry

        lax.fori_loop(0, (nb - 1) // 2, two_stages, 0)
        consume(nb - 1, raw_a)

    _by_tile_kind(pl.program_id(0), run)
    o_ref[...] = x_ref[...] + mod_ref[5:6, :] * acc_scr[...]


def _conv_ffn(x, mod, nw, w_up, conv_w, conv_b, w_down):
    nb = FFN_BLOCKS

    def blocks(a):
        r = a.shape[0]
        return a.reshape(r, 2, nb, FFN_TF).transpose(2, 0, 1, 3).reshape(nb, r, 2 * FFN_TF)

    w_up = blocks(w_up).astype(BF16)
    conv_w = blocks(conv_w)
    conv_b = blocks(conv_b.reshape(1, 2 * D_FF))
    w_down = w_down.reshape(nb, FFN_TF, D_MODEL).astype(BF16)

    def resident(shape):
        return pl.BlockSpec(shape, lambda i: (0,) * len(shape), pipeline_mode=pl.Buffered(1))

    return pl.pallas_call(
        _ffn_kernel,
        grid=(N_TOK // TM,),
        in_specs=[
            pl.BlockSpec((TM, D_MODEL), lambda i: (i, 0)),
            pl.BlockSpec((None, 6, D_MODEL), lambda i: (_mod_row(i), 0, 0)),
            pl.BlockSpec((1, D_MODEL), lambda i: (0, 0)),
            resident((nb, D_MODEL, 2 * FFN_TF)),
            resident((nb, 3, 2 * FFN_TF)),
            resident((nb, 1, 2 * FFN_TF)),
            resident((nb, FFN_TF, D_MODEL)),
        ],
        out_specs=pl.BlockSpec((TM, D_MODEL), lambda i: (i, 0)),
        out_shape=jax.ShapeDtypeStruct((N_TOK, D_MODEL), F32),
        scratch_shapes=[pltpu.VMEM((TM, D_MODEL), BF16), pltpu.VMEM((TM, D_MODEL), F32),
                        pltpu.VMEM((TM, 2 * FFN_TF), F32), pltpu.VMEM((TM, 2 * FFN_TF), F32)],
        compiler_params=_params("arbitrary"),
        name="conv_ffn",
    )(x, mod, nw.reshape(1, D_MODEL), w_up, conv_w, conv_b, w_down)


ML_TN = 512
ML_EXT = ML_DV + LANES
ML_SEQ_PER_STEP = 2


ML_BLOCKS = ML_D_IN // ML_TN
ML_UP_VMEM_LIMIT_BYTES = 58 * 1024 * 1024


def _ml_up_kernel(xc_src_ref, xl_src_ref, mod_ref, nw_ref, w_ref, cw_ref, cb_ref, xm_ref, xc_ref, z_ref,
                  h_scr, raw_a, raw_b):
    nb = ML_BLOCKS
    i = pl.program_id(0)

    def normalise(x_ref):
        h = _norm_mod(x_ref[...], nw_ref[...], mod_ref[0:1, :], mod_ref[1:2, :])
        h_scr[...] = h.astype(BF16)

    is_ctx = i < N_CTX_TILES
    pl.when(is_ctx)(functools.partial(normalise, xc_src_ref))
    pl.when(jnp.logical_not(is_ctx))(functools.partial(normalise, xl_src_ref))

    def run(seq):
        def project(k, raw):
            h = h_scr[...]
            raw[...] = jnp.dot(h, w_ref[k], preferred_element_type=F32)
            z_ref[:, k * ML_TN:(k + 1) * ML_TN] = jnp.dot(h, w_ref[nb + k], preferred_element_type=F32).astype(BF16)

        def consume(k, raw):
            cols = slice(k * ML_TN, (k + 1) * ML_TN)
            xm = raw[...]
            xm_ref[:, cols] = xm.astype(BF16)
            xc_ref[:, cols] = _silu(_dwconv_rows(xm, cw_ref[:, cols], cb_ref[:, cols], seq)).astype(BF16)

        raws = (raw_a, raw_b)
        project(0, raws[0])
        for k in range(nb):
            if k + 1 < nb:
                project(k + 1, raws[(k + 1) % 2])
            consume(k, raws[k % 2])

    _by_tile_kind(i, run)


def _ml_up(x, mod, nw, w_up, conv_w, conv_b):
    nb = ML_BLOCKS
    x_specs, x_arrays = _row_sources(x, D_MODEL)
    w_blocks = w_up.reshape(D_MODEL, 2 * nb, ML_TN).transpose(1, 0, 2).astype(BF16)
    rows = pl.BlockSpec((TM, ML_D_IN), lambda i: (i, 0))
    shp = jax.ShapeDtypeStruct((N_TOK, ML_D_IN), BF16)
    return pl.pallas_call(
        _ml_up_kernel,
        grid=(N_TOK // TM,),
        in_specs=x_specs + [
            pl.BlockSpec((None, 6, D_MODEL), lambda i: (_mod_row(i), 0, 0)),
            pl.BlockSpec((1, D_MODEL), lambda i: (0, 0)),
            pl.BlockSpec((2 * nb, D_MODEL, ML_TN), lambda i: (0, 0, 0), pipeline_mode=pl.Buffered(1)),
            pl.BlockSpec((3, ML_D_IN), lambda i: (0, 0)),
            pl.BlockSpec((1, ML_D_IN), lambda i: (0, 0)),
        ],
        out_specs=[rows, rows, rows],
        out_shape=[shp, shp, shp],
        scratch_shapes=[pltpu.VMEM((TM, D_MODEL), BF16), pltpu.VMEM((TM, ML_TN), F32), pltpu.VMEM((TM, ML_TN), F32)],
        compiler_params=_params("arbitrary", vmem_limit_bytes=ML_UP_VMEM_LIMIT_BYTES),
        name="ml_up",
    )(*x_arrays, mod, nw.reshape(1, D_MODEL), w_blocks, conv_w, conv_b.reshape(1, ML_D_IN))


def _log_sigmoid(x):
    return jnp.minimum(x, 0.0) - jnp.log(1.0 + jnp.exp(-jnp.abs(x)))


def _ml_scan_kernel(*refs, seq, n_seq, has_init, out_state, n_unused_inputs, fill_layer=None):
    q_ref, kt_ref, v_ref, g_ref, gt_ref, xc_ref, z_ref, nw_ref, skip_ref = refs[:9]
    pos = 9
    if has_init:
        c0_ref, n0_ref, m0_ref = refs[pos:pos + 3]
        pos += 3
    pos += n_unused_inputs
    o_ref = refs[pos]
    pos += 1
    if out_state:
        cout_ref, nout_ref, mout_ref = refs[pos:pos + 3]
        pos += 3
    hs_scr, c_scr = refs[pos:pos + 2]

    L = ML_CHUNK
    nc = seq // L
    ext_tiles = ML_EXT // LANES
    row_i = lax.broadcasted_iota(jnp.int32, (L, L), 0)
    col_i = lax.broadcasted_iota(jnp.int32, (L, L), 1)
    ones_tile = jnp.ones((L, LANES), BF16)

    def lanes(a, n):
        return jnp.concatenate([a] * n, axis=1)

    for d in (0, 1):
        keep = (col_i <= row_i) if d == 0 else (col_i >= row_i)
        keep_f = keep.astype(F32)
        keep_t_f = ((row_i <= col_i) if d == 0 else (row_i >= col_i)).astype(F32)
        end = L - 1 if d == 0 else 0

        ms = []
        for s in range(n_seq):
            if has_init:
                c_scr[s, :, :ML_DV] = c0_ref[s, d]
                c_scr[s, :, ML_DV:] = n0_ref[s, d]
                ms.append(m0_ref[s, d:d + 1, :])
            else:
                c_scr[s] = jnp.zeros((ML_DK, ML_EXT), F32)
                ms.append(jnp.zeros((1, LANES), F32))

        for c, s in [(c, s) for c in range(nc) for s in range(n_seq)]:
            cc = c if d == 0 else nc - 1 - c
            chunk = s * nc + cc
            rows = slice(chunk * L, (chunk + 1) * L)
            m = ms[s]
            qc = q_ref[rows, :]
            ktc = kt_ref[chunk]
            v_ext = jnp.concatenate([v_ref[rows, :], ones_tile], axis=1)
            gcol = g_ref[chunk]
            grow = gt_ref[chunk]
            ig_row = grow[2 * d:2 * d + 1, :]
            lf_col = _log_sigmoid(gcol[:, 2 * d + 1:2 * d + 2])
            lf_row = _log_sigmoid(grow[2 * d + 1:2 * d + 2, :])
            b_col = jnp.broadcast_to(jnp.sum(keep_f * lf_row, axis=1, keepdims=True), (L, LANES))
            b_row = jnp.sum(keep_t_f * lf_col, axis=0, keepdims=True)
            dmat = jnp.where(keep, b_col - b_row + ig_row, NEG_INF)
            m_loc = jnp.broadcast_to(jnp.max(dmat, axis=1, keepdims=True), (L, LANES))
            p_loc = jnp.exp(dmat - m_loc)
            s_loc = jnp.dot(qc, ktc, preferred_element_type=F32) * p_loc
            intra = jnp.dot(s_loc.astype(BF16), v_ext, preferred_element_type=F32)

            m_t = jnp.maximum(b_col + m, m_loc)
            w_inter = jnp.exp(b_col + m - m_t)
            w_intra = jnp.exp(m_loc - m_t)
            inter = jnp.dot(qc, c_scr[s].astype(BF16), preferred_element_type=F32)
            hx = lanes(w_inter, ext_tiles) * inter + lanes(w_intra, ext_tiles) * intra
            inv = 1.0 / jnp.maximum(jnp.abs(hx[:, ML_DV:]), jnp.exp(-m_t))
            h = hx[:, :ML_DV] * lanes(inv, ML_DV // LANES)

            b_end = b_col[end:end + 1, :]
            m_loc_end = m_loc[end:end + 1, :]
            m_new = jnp.maximum(b_end + m, m_loc_end)
            w_c = jnp.exp(b_end + m - m_new)
            w_s = p_loc[end:end + 1, :] * jnp.exp(m_loc_end - m_new)
            upd = jnp.dot((ktc.astype(F32) * w_s).astype(BF16), v_ext, preferred_element_type=F32)
            c_scr[s] = lanes(w_c, ext_tiles) * c_scr[s] + upd
            ms[s] = m_new

            if d == 0:
                hs_scr[rows, :] = h
            else:
                hs = hs_scr[rows, :] + h
                mu = jnp.mean(hs, axis=1, keepdims=True)
                cen = hs - mu
                var = jnp.mean(cen * cen, axis=1, keepdims=True)
                hn = cen * lax.rsqrt(var + NORM_EPS) * nw_ref[...]
                xc = xc_ref[rows, :].astype(F32)
                z = z_ref[rows, :].astype(F32)
                o_ref[rows, :] = ((hn + skip_ref[...] * xc) * _silu(z)).astype(o_ref.dtype)

        if out_state:
            for s in range(n_seq):
                if fill_layer is None:
                    cout_ref[s, d] = c_scr[s, :, :ML_DV]
                else:
                    for layer in range(cout_ref.shape[1]):
                        cout_ref[s, layer, d] = (c_scr[s, :, :ML_DV] if layer == fill_layer
                                                 else jnp.zeros((ML_DK, ML_DV), F32))
                nout_ref[s, d:d + 1, :] = c_scr[s, :, ML_DV:].T[0:1, :]
                mout_ref[s, d:d + 1, :] = ms[s]


def _ml_scan(q, kt, v, g4, g4t, xc, z, norm_w, skip, *, seq, nb, row0, init=None, final_c=None,
             n_seq=ML_SEQ_PER_STEP):
    L = ML_CHUNK
    rows = n_seq * seq
    nc = rows // L
    rb = row0 // rows
    has_init = init is not None
    in_specs = [
        pl.BlockSpec((rows, ML_DK), lambda b, h: (rb + b, h)),
        pl.BlockSpec((None, nc, ML_DK, L), lambda b, h: (h, rb + b, 0, 0)),
        pl.BlockSpec((rows, ML_DV), lambda b, h: (rb + b, h)),
        pl.BlockSpec((None, nc, L, 4), lambda b, h: (h, rb + b, 0, 0)),
        pl.BlockSpec((None, nc, 4, L), lambda b, h: (h, rb + b, 0, 0)),
        pl.BlockSpec((rows, ML_DV), lambda b, h: (rb + b, h)),
        pl.BlockSpec((rows, ML_DV), lambda b, h: (rb + b, h)),
        pl.BlockSpec((1, ML_DV), lambda b, h: (0, h)),
        pl.BlockSpec((1, ML_DV), lambda b, h: (0, h)),
    ]
    args = [q, kt, v, g4, g4t, xc, z, norm_w.reshape(1, ML_D_IN), skip.reshape(1, ML_D_IN)]
    state_n = pl.BlockSpec((n_seq, None, 2, ML_DK, LANES), lambda b, h: (b, h, 0, 0, 0))
    state_m = pl.BlockSpec((n_seq, None, 2, LANES), lambda b, h: (b, h, 0, 0))
    if has_init:
        c0, layer, n0, m0 = init
        in_specs += [pl.BlockSpec((n_seq, None, 2, None, ML_DK, ML_DV), lambda b, h: (b, layer, 0, h, 0, 0)),
                     state_n, state_m]
        args += [c0, n0, m0]
    out_specs = [pl.BlockSpec((rows, ML_DV), lambda b, h: (b, h))]
    out_shape = [jax.ShapeDtypeStruct((nb * seq, ML_D_IN), BF16)]
    aliases = {}
    if not has_init:
        layer_out, c_all = final_c
        if c_all is None:
            c_spec = pl.BlockSpec((n_seq, N_ML_LAYERS, 2, None, ML_DK, ML_DV), lambda b, h: (b, 0, 0, h, 0, 0))
        else:
            c_spec = pl.BlockSpec((n_seq, None, 2, None, ML_DK, ML_DV), lambda b, h: (b, layer_out, 0, h, 0, 0))
        out_specs += [c_spec, pl.BlockSpec((n_seq, None, 2, ML_DK), lambda b, h: (b, h, 0, 0)), state_m]
        out_shape += [
            jax.ShapeDtypeStruct((nb, N_ML_LAYERS, 2, ML_HEADS, ML_DK, ML_DV), F32),
            jax.ShapeDtypeStruct((nb, ML_HEADS, 2, ML_DK), F32),
            jax.ShapeDtypeStruct((nb, ML_HEADS, 2, LANES), F32),
        ]
        if c_all is not None:
            in_specs.append(pl.BlockSpec(memory_space=pl.ANY))
            args.append(c_all)
            aliases = {len(args) - 1: 1}
    return pl.pallas_call(
        functools.partial(_ml_scan_kernel, seq=seq, n_seq=n_seq, has_init=has_init, out_state=not has_init,
                          n_unused_inputs=len(aliases),
                          fill_layer=final_c[0] if (not has_init and final_c[1] is None) else None),
        grid=(nb // n_seq, ML_HEADS),
        in_specs=in_specs,
        out_specs=out_specs,
        out_shape=out_shape,
        scratch_shapes=[pltpu.VMEM((rows, ML_DV), F32), pltpu.VMEM((n_seq, ML_DK, ML_EXT), F32)],
        input_output_aliases=aliases,
        compiler_params=_params("arbitrary", "arbitrary"),
        name="ml_scan_init" if has_init else "ml_scan_zero",
    )(*args)


def _ml_qk_kernel(xc_ref, w_ref, q_ref, kt_ref):
    acc = jnp.dot(xc_ref[...], w_ref[...], preferred_element_type=F32)
    j = pl.program_id(1)

    @pl.when(j == 0)
    def _():
        q_ref[...] = (acc * ML_DK ** -0.5).astype(BF16)

    @pl.when(j == 1)
    def _():
        for h in range(ML_HEADS):
            kt = acc[:, h * ML_DK:(h + 1) * ML_DK].T
            for c in range(TM // ML_CHUNK):
                kt_ref[h, c] = kt[:, c * ML_CHUNK:(c + 1) * ML_CHUNK].astype(BF16)


def _ml_qk(xc, w_qk):
    n = ML_HEADS * ML_DK
    cpt = TM // ML_CHUNK
    return pl.pallas_call(
        _ml_qk_kernel,
        grid=(N_TOK // TM, 2),
        in_specs=[pl.BlockSpec((TM, ML_D_IN), lambda i, j: (i, 0)),
                  pl.BlockSpec((ML_D_IN, n), lambda i, j: (0, j))],
        out_specs=[pl.BlockSpec((TM, n), lambda i, j: (i, 0)),
                   pl.BlockSpec((ML_HEADS, cpt, ML_DK, ML_CHUNK), lambda i, j: (0, i, 0, 0))],
        out_shape=[jax.ShapeDtypeStruct((N_TOK, n), BF16),
                   jax.ShapeDtypeStruct((ML_HEADS, N_TOK // ML_CHUNK, ML_DK, ML_CHUNK), BF16)],
        compiler_params=_params("arbitrary", "arbitrary"),
        name="ml_qk",
    )(xc, w_qk)


def _mlstm_layer(x, mod, nw, p, state, final_c):
    xm, xc, z = _ml_up(x, mod, nw, p["w_up"], p["conv_w"], p["conv_b"])
    q, kt = _ml_qk(xc, p["w_qk"])
    v = _mm(xm, p["w_v"], jnp.zeros((1, ML_D_IN), F32), 1024, BF16)
    g = _mm(xm, p["w_gate"], p["b_gate"], 128, F32)[:, :4 * ML_HEADS]
    L = ML_CHUNK
    g4 = g.reshape(N_TOK // L, L, 4, ML_HEADS).transpose(3, 0, 1, 2)
    g4t = g4.transpose(0, 1, 3, 2)
    oc, c_fin, n_fin, m_fin = _ml_scan(q, kt, v, g4, g4t, xc, z, p["norm_w"], p["skip"],
                                       seq=SEQ, nb=BATCH, row0=0, final_c=final_c)
    (ol,) = _ml_scan(q, kt, v, g4, g4t, xc, z, p["norm_w"], p["skip"],
                     seq=DEC_SEQ, nb=DEC_BATCH, row0=N_CTX, init=state, n_seq=1)
    x = _mm_res(oc, ol, p["w_down"], x, mod)
    return x, c_fin, n_fin.transpose(0, 2, 1, 3), m_fin[..., 0].transpose(0, 2, 1)


def _softmax_parts(scores, sink_col):
    m = functools.reduce(jnp.maximum, [jnp.max(s, axis=1, keepdims=True) for s in scores])
    if sink_col is not None:
        m = jnp.maximum(m, sink_col)
    ps = [jnp.exp(s - m) for s in scores]
    den = functools.reduce(jnp.add, [jnp.sum(p, axis=1, keepdims=True) for p in ps])
    if sink_col is not None:
        den = den + jnp.exp(sink_col - m)
    return ps, den


def _qk(q, k):
    return lax.dot_general(q, k, (((1,), (1,)), ((), ())), preferred_element_type=F32) * (HEAD_DIM ** -0.5)


def _attend_tiles(q_tiles, parts, sinks):
    return _attend_groups([(q_tiles, parts, sinks)])[0]


def _attend_groups(groups):
    r = groups[0][0][0].shape[0]
    lo = lax.broadcasted_iota(jnp.int32, (r, LANES), 1) < HEAD_DIM
    zero = jnp.zeros((r, LANES), BF16)
    all_scores = []
    for q_tiles, parts, _ in groups:
        qs = jnp.concatenate([jnp.where(sel, t, zero) for t in q_tiles for sel in (lo, jnp.logical_not(lo))], axis=0)
        all_scores.append([post(_qk(qs, k2)) for k2, _, post in parts])
    outs = []
    for scores, (q_tiles, parts, sinks) in zip(all_scores, groups):
        n_rows = scores[0].shape[0]
        sink_col = None
        if sinks is not None:
            rows = lax.broadcasted_iota(jnp.int32, (n_rows, 1), 0)
            sink_col = jnp.full((n_rows, 1), sinks[0], F32)
            for hi in range(1, len(sinks)):
                sink_col = jnp.where(rows >= hi * r, sinks[hi], sink_col)
        ps, den = _softmax_parts(scores, sink_col)
        o = functools.reduce(jnp.add, [jnp.dot(p.astype(BF16), v2, preferred_element_type=F32)
                                       for p, (_, v2, _) in zip(ps, parts)]) / den
        outs.append([jnp.where(lo, o[2 * a * r:(2 * a + 1) * r], o[(2 * a + 1) * r:(2 * a + 2) * r])
                     for a in range(len(q_tiles))])
    return outs


def _identity(s):
    return s


def _ctx_attn_kernel(*refs, tiles_per_kv, has_sink, kv_head_stride):
    if has_sink:
        sink_ref, q_ref, k_ref, v_ref, o_ref, kout_ref, vout_ref = refs
    else:
        q_ref, k_ref, v_ref, o_ref, kout_ref, vout_ref = refs
    for h in range(kout_ref.shape[0]):
        kout_ref[h] = k_ref[:, h * kv_head_stride:h * kv_head_stride + HEAD_DIM]
        vout_ref[h] = v_ref[:, h * kv_head_stride:h * kv_head_stride + HEAD_DIM]
    groups = []
    for t in range(k_ref.shape[1] // LANES):
        k2 = k_ref[:, t * LANES:(t + 1) * LANES].astype(BF16)
        v2 = v_ref[:, t * LANES:(t + 1) * LANES].astype(BF16)
        first = t * tiles_per_kv
        q_tiles = [q_ref[:, (first + a) * LANES:(first + a + 1) * LANES] for a in range(tiles_per_kv)]
        sinks = [sink_ref[2 * first + hi] for hi in range(2 * tiles_per_kv)] if has_sink else None
        groups.append((q_tiles, [(k2, v2, _identity)], sinks))
    for t, outs in enumerate(_attend_groups(groups)):
        for a, o in enumerate(outs):
            tile = t * tiles_per_kv + a
            o_ref[:, tile * LANES:(tile + 1) * LANES] = o.astype(o_ref.dtype)


def _ctx_attn(q, kv, n_kv_cols, n_kv_heads, sink):
    nq = q.shape[1]
    has_sink = sink is not None
    cache_spec = pl.BlockSpec((None, None, n_kv_heads, SEQ, HEAD_DIM), lambda b: (b, 0, 0, 0, 0))
    cache_shape = jax.ShapeDtypeStruct((BATCH, 1, n_kv_heads, SEQ, HEAD_DIM), F32)
    in_specs = [pl.BlockSpec((SEQ, nq), lambda b: (b, 0)),
                pl.BlockSpec((SEQ, n_kv_cols), lambda b: (b, 0)),
                pl.BlockSpec((SEQ, n_kv_cols), lambda b: (b, 1))]
    args = [q, kv, kv]
    if has_sink:
        in_specs = [pl.BlockSpec(memory_space=pltpu.SMEM)] + in_specs
        args = [sink] + args
    return pl.pallas_call(
        functools.partial(_ctx_attn_kernel, tiles_per_kv=nq // n_kv_cols, has_sink=has_sink,
                          kv_head_stride=n_kv_cols // n_kv_heads),
        grid=(BATCH,),
        in_specs=in_specs,
        out_specs=[pl.BlockSpec((SEQ, nq), lambda b: (b, 0)), cache_spec, cache_spec],
        out_shape=[jax.ShapeDtypeStruct((N_CTX, nq), BF16), cache_shape, cache_shape],
        compiler_params=_params("arbitrary"),
        name="ctx_attn",
    )(*args)


SWA_SPAN = Q_BLOCK + 2 * SWA_WINDOW
SWA_KV_COLS = SWA_KV * LANES


def _swa_lat_kernel(sink_ref, q_ref, k_ref, v_ref, kc_ref, vc_ref, o_ref):
    j = pl.program_id(1)
    start = pl.multiple_of(jnp.clip((j - 1) * Q_BLOCK, 0, DEC_SEQ - SWA_SPAN), Q_BLOCK)
    rows = 4 * Q_BLOCK
    qpos = j * Q_BLOCK + (lax.broadcasted_iota(jnp.int32, (rows, SWA_SPAN), 0) & (Q_BLOCK - 1))
    kpos = start + lax.broadcasted_iota(jnp.int32, (rows, SWA_SPAN), 1)
    in_window = jnp.abs(qpos - kpos) <= SWA_WINDOW

    def window(s):
        return jnp.where(in_window, s, NEG_INF)

    groups = []
    for t in range(SWA_KV):
        cols = slice(t * LANES, (t + 1) * LANES)
        k_loc = k_ref[pl.ds(start, SWA_SPAN), cols].astype(BF16)
        v_loc = v_ref[pl.ds(start, SWA_SPAN), cols].astype(BF16)
        q_tiles = [q_ref[:, (2 * t + a) * LANES:(2 * t + a + 1) * LANES] for a in range(2)]
        sinks = [sink_ref[4 * t + hi] for hi in range(4)]
        groups.append((q_tiles, [(k_loc, v_loc, window), (kc_ref[:, cols], vc_ref[:, cols], _identity)], sinks))
    for t, outs in enumerate(_attend_groups(groups)):
        for a, o in enumerate(outs):
            o_ref[:, (2 * t + a) * LANES:(2 * t + a + 1) * LANES] = o.astype(o_ref.dtype)


def _swa_latent(q, kv, kc, vc, sink):
    nq = q.shape[1]
    rb = N_CTX // DEC_SEQ
    qb = N_CTX // Q_BLOCK
    nj = DEC_SEQ // Q_BLOCK
    cspec = pl.BlockSpec((None, SEQ, SWA_KV_COLS), lambda b, j: (b, 0, 0))
    return pl.pallas_call(
        _swa_lat_kernel,
        grid=(DEC_BATCH, nj),
        in_specs=[pl.BlockSpec(memory_space=pltpu.SMEM),
                  pl.BlockSpec((Q_BLOCK, nq), lambda b, j: (qb + b * nj + j, 0)),
                  pl.BlockSpec((DEC_SEQ, SWA_KV_COLS), lambda b, j: (rb + b, 0)),
                  pl.BlockSpec((DEC_SEQ, SWA_KV_COLS), lambda b, j: (rb + b, 1)),
                  cspec, cspec],
        out_specs=pl.BlockSpec((Q_BLOCK, nq), lambda b, j: (b * nj + j, 0)),
        out_shape=jax.ShapeDtypeStruct((N_LAT, nq), BF16),
        compiler_params=_params("arbitrary", "arbitrary"),
        name="swa_latent",
    )(sink, q, kv, kv, kc, vc)


NA_QT = 256
NA_SPAN = 768
NA_TILES = 4


def _na_start(j):
    return (j // 2) * (DEC_SEQ - NA_SPAN)


NA_ROWS = DEC_SEQ // GRID_W
NA_DR = 2 * NA_KH - 1
NA_DC = 2 * NA_KW - 1


def _na_blocks_kernel(rpb_ref, onehot_ref, valid_ref, o_ref):
    t = jnp.dot(rpb_ref[...], onehot_ref[...], preferred_element_type=F32, precision=lax.Precision.HIGHEST)
    o_ref[...] = jnp.where(valid_ref[...] > 0.5, t, NEG_INF)


def _na_bias_blocks(rpb):
    h = rpb.shape[0]
    kpad = 32
    cq, ck = np.meshgrid(np.arange(GRID_W), np.arange(GRID_W), indexing="ij")
    dc = (np.clip(ck - cq, -(NA_KW - 1), NA_KW - 1) + NA_KW - 1).reshape(-1)
    cs = np.clip(cq - NA_KW // 2, 0, GRID_W - NA_KW)
    valid = ((ck >= cs) & (ck < cs + NA_KW)).reshape(1, -1).astype(np.float32)
    onehot = (np.arange(kpad)[:, None] == dc[None, :]).astype(np.float32)
    rpb2 = jnp.pad(rpb.reshape(h * NA_DR, NA_DC), ((0, 0), (0, kpad - NA_DC)))
    n = GRID_W * GRID_W
    blocks = pl.pallas_call(
        _na_blocks_kernel,
        grid=(1,),
        in_specs=[pl.BlockSpec((h * NA_DR, kpad), lambda i: (0, 0)),
                  pl.BlockSpec((kpad, n), lambda i: (0, 0)),
                  pl.BlockSpec((1, n), lambda i: (0, 0))],
        out_specs=pl.BlockSpec((h * NA_DR, n), lambda i: (0, 0)),
        out_shape=jax.ShapeDtypeStruct((h * NA_DR, n), F32),
        compiler_params=_params("arbitrary"),
        name="na_bias_blocks",
    )(rpb2, jnp.asarray(onehot), jnp.asarray(valid))
    blocks = blocks.reshape(h, NA_DR, GRID_W, GRID_W)
    padded = jnp.pad(blocks, ((0, 0), (1, 1), (0, 0), (0, 0)), constant_values=NEG_INF)
    return jnp.concatenate([padded[:, :-1], padded[:, 1:]], axis=-1)


def _na_lat_kernel(q_ref, k_ref, v_ref, kc_ref, vc_ref, blk_ref, o_ref, bias_scr):
    j = pl.program_id(0)
    start = pl.multiple_of(_na_start(j), 256)

    @pl.when(pl.program_id(2) == 0)
    def _():
        lane_lo = lax.broadcasted_iota(jnp.int32, (GRID_W, LANES), 1) < GRID_W
        for rq_l in range(NA_QT // GRID_W):
            rq = j * (NA_QT // GRID_W) + rq_l
            rs = jnp.clip(rq - NA_KH // 2, 0, NA_ROWS - NA_KH)
            for kp in range(NA_SPAN // LANES):
                rk = start // GRID_W + 2 * kp
                idx = jnp.clip(rk - rq + NA_KH, 0, NA_DR)
                in_band = [jnp.logical_and(r >= rs, r < rs + NA_KH).astype(jnp.int32) for r in (rk, rk + 1)]
                ok = jnp.where(lane_lo, in_band[0], in_band[1]) > 0
                for hh in range(2 * NA_TILES):
                    bias_scr[hh * NA_QT + rq_l * GRID_W:hh * NA_QT + (rq_l + 1) * GRID_W,
                             kp * LANES:(kp + 1) * LANES] = jnp.where(ok, blk_ref[hh, idx], NEG_INF)

    groups = []
    for t in range(NA_TILES):
        cols = slice(t * LANES, (t + 1) * LANES)

        def add_bias(s, t=t):
            return s + bias_scr[2 * t * NA_QT:2 * (t + 1) * NA_QT, :]

        k_loc = k_ref[pl.ds(start, NA_SPAN), cols].astype(BF16)
        v_loc = v_ref[pl.ds(start, NA_SPAN), cols].astype(BF16)
        groups.append(([q_ref[:, cols]], [(k_loc, v_loc, add_bias), (kc_ref[:, cols], vc_ref[:, cols], _identity)],
                       None))
    for t, (o,) in enumerate(_attend_groups(groups)):
        o_ref[:, t * LANES:(t + 1) * LANES] = o.astype(o_ref.dtype)


def _na_latent(q, kv, kc, vc, blocks):
    nq = q.shape[1]
    width = NA_TILES * LANES
    n_steps = nq // width
    nj = DEC_SEQ // NA_QT
    rb = N_CTX // DEC_SEQ
    qb = N_CTX // NA_QT
    cspec = pl.BlockSpec((None, SEQ, width), lambda j, p, b: (b, 0, p))
    return pl.pallas_call(
        _na_lat_kernel,
        grid=(nj, n_steps, DEC_BATCH),
        in_specs=[pl.BlockSpec((NA_QT, width), lambda j, p, b: (qb + b * nj + j, p)),
                  pl.BlockSpec((DEC_SEQ, width), lambda j, p, b: (rb + b, p)),
                  pl.BlockSpec((DEC_SEQ, width), lambda j, p, b: (rb + b, n_steps + p)),
                  cspec, cspec,
                  pl.BlockSpec((2 * NA_TILES, NA_DR + 1, GRID_W, LANES), lambda j, p, b: (p, 0, 0, 0))],
        out_specs=pl.BlockSpec((NA_QT, width), lambda j, p, b: (b * nj + j, p)),
        out_shape=jax.ShapeDtypeStruct((N_LAT, nq), BF16),
        scratch_shapes=[pltpu.VMEM((2 * NA_TILES * NA_QT, NA_SPAN), F32)],
        compiler_params=_params("arbitrary", "arbitrary", "arbitrary"),
        name="na_latent",
    )(q, kv, kv, kc, vc, blocks)


def _rope_tables(width):
    quarter = HEAD_DIM // 4
    pos = np.arange(DEC_SEQ)
    inv = np.power(ROPE_BASE, -np.arange(quarter, dtype=np.float32) / quarter).astype(np.float32)
    d = np.arange(width) % HEAD_DIM
    p = np.where((d < HEAD_DIM // 2)[None, :], (pos // GRID_W)[:, None], (pos % GRID_W)[:, None]).astype(np.float32)
    ang = p * inv[d % quarter][None, :]
    sign = np.where((d // quarter) % 2 == 0, -1.0, 1.0)[None, :]
    return jnp.asarray(np.cos(ang), F32), jnp.asarray(np.sin(ang) * sign, F32)


def _cache_rows(cache, dup):
    b, h, s, hd = cache.shape
    rows = jnp.broadcast_to(cache.transpose(0, 2, 1, 3)[:, :, :, None, :], (b, s, h, dup, hd))
    return rows.reshape(b, s, h * dup * hd).astype(BF16)


def _swa_layer(x, mod, nw, w_qkv, sink, w_o, cache_k, cache_v):
    nq, nk = SWA_HEADS * HEAD_DIM, SWA_KV * HEAD_DIM

    def dup_heads(w):
        return jnp.broadcast_to(w.reshape(D_MODEL, SWA_KV, 1, HEAD_DIM),
                                (D_MODEL, SWA_KV, 2, HEAD_DIM)).reshape(D_MODEL, SWA_KV_COLS)

    w = jnp.concatenate([w_qkv[:, :nq], dup_heads(w_qkv[:, nq:nq + nk]), dup_heads(w_qkv[:, nq + nk:])], axis=1)
    q, kv = _qkv_proj(x, mod, nw, w.astype(BF16), nq, rope_cols=nq + SWA_KV_COLS)
    oc, k_new, v_new = _ctx_attn(q, kv, SWA_KV_COLS, SWA_KV, sink)
    ol = _swa_latent(q, kv, _cache_rows(cache_k, 2), _cache_rows(cache_v, 2), sink)
    return _mm_res(oc, ol, w_o, x, mod), k_new, v_new


def _na_layer(x, mod, nw, w_qkv, rpb, w_o, cache_k, cache_v):
    n = NA_HEADS * HEAD_DIM
    q, kv = _qkv_proj(x, mod, nw, w_qkv, n)
    oc, k_new, v_new = _ctx_attn(q, kv, n, NA_HEADS, None)
    ol = _na_latent(q, kv, _cache_rows(cache_k, 1), _cache_rows(cache_v, 1), _na_bias_blocks(rpb))
    return _mm_res(oc, ol, w_o, x, mod), k_new, v_new


def _final_norm_kernel(x_ref, w_ref, oc_ref, ol_ref):
    x = x_ref[...]
    y = x * lax.rsqrt(jnp.mean(x * x, axis=-1, keepdims=True) + NORM_EPS) * w_ref[...]
    is_ctx = pl.program_id(0) < N_CTX_TILES

    @pl.when(is_ctx)
    def _():
        oc_ref[...] = y

    @pl.when(jnp.logical_not(is_ctx))
    def _():
        ol_ref[...] = y


def _final_norm(x, w):
    return pl.pallas_call(
        _final_norm_kernel,
        grid=(N_TOK // TM,),
        in_specs=[pl.BlockSpec((TM, D_MODEL), lambda i: (i, 0)), pl.BlockSpec((1, D_MODEL), lambda i: (0, 0))],
        out_specs=[pl.BlockSpec((TM, D_MODEL), lambda i: (jnp.minimum(i, N_CTX_TILES - 1), 0)),
                   pl.BlockSpec((TM, D_MODEL), lambda i: (jnp.maximum(i - N_CTX_TILES, 0), 0))],
        out_shape=[jax.ShapeDtypeStruct((N_CTX, D_MODEL), F32), jax.ShapeDtypeStruct((N_LAT, D_MODEL), F32)],
        compiler_params=_params("arbitrary"),
        name="final_norm",
    )(x, w.reshape(1, D_MODEL))


def kernel(x_prompt, x_sample, state_mlstm_C, state_mlstm_n, state_mlstm_m, cache_swa_k, cache_swa_v, cache_na_k, cache_na_v, c, c_ctx, ada_w, ada_b, norm_w, final_norm_w, ffn_w_up, ffn_conv_w, ffn_conv_b, ffn_w_down, ml_w_up, ml_conv_w, ml_conv_b, ml_w_qk, ml_w_v, ml_w_gate, ml_b_gate, ml_norm_w, ml_skip, ml_w_down, swa_w_qkv, swa_sink, swa_w_o, na_w_qkv, na_rpb, na_w_o):
    x = (x_prompt.reshape(N_CTX, D_MODEL), x_sample.reshape(N_LAT, D_MODEL))
    cond = jnp.concatenate([c_ctx[None], c, jnp.zeros((MOD_ROWS - 1 - DEC_BATCH, D_MODEL), F32)], axis=0)
    mods = _ada_mod(cond, ada_w, ada_b)

    new_c, new_n, new_m = None, [], []
    new_sk = new_sv = new_nk = new_nv = None
    for i in range(DEPTH):
        kind, j = i % N_MIXERS, i // N_MIXERS
        mod = mods[i]
        if kind == 0:
            gate_w = jnp.pad(ml_w_gate[j], ((0, 0), (0, 128 - 4 * ML_HEADS))).astype(BF16)
            gate_b = jnp.pad(ml_b_gate[j], (0, 128 - 4 * ML_HEADS)).reshape(1, 128)
            p = dict(w_up=ml_w_up[j], conv_w=ml_conv_w[j], conv_b=ml_conv_b[j],
                     w_qk=ml_w_qk[j].astype(BF16), w_v=ml_w_v[j].astype(BF16), w_gate=gate_w, b_gate=gate_b,
                     norm_w=ml_norm_w[j], skip=ml_skip[j], w_down=ml_w_down[j].astype(BF16))
            n0 = jnp.broadcast_to(state_mlstm_n[:, j].transpose(0, 2, 1, 3)[..., None],
                                  (DEC_BATCH, ML_HEADS, 2, ML_DK, LANES))
            m0 = jnp.broadcast_to(state_mlstm_m[:, j].transpose(0, 2, 1)[..., None], (DEC_BATCH, ML_HEADS, 2, LANES))
            x, new_c, nf, mf = _mlstm_layer(x, mod, norm_w[i, 0], p, (state_mlstm_C, j, n0, m0), (j, new_c))
            new_n.append(nf)
            new_m.append(mf)
        elif kind == 1:
            x, k_new, v_new = _swa_layer(x, mod, norm_w[i, 0], swa_w_qkv[j].astype(BF16), swa_sink[j],
                                         swa_w_o[j].astype(BF16), cache_swa_k[:, j], cache_swa_v[:, j])
            new_sk, new_sv = k_new, v_new
        else:
            x, k_new, v_new = _na_layer(x, mod, norm_w[i, 0], na_w_qkv[j].astype(BF16), na_rpb[j],
                                        na_w_o[j].astype(BF16), cache_na_k[:, j], cache_na_v[:, j])
            new_nk, new_nv = k_new, v_new
        x = _conv_ffn(x, mod, norm_w[i, 1], ffn_w_up[i], ffn_conv_w[i], ffn_conv_b[i], ffn_w_down[i])

    y_ctx, y_lat = _final_norm(x, final_norm_w)
    return (y_ctx.reshape(BATCH, SEQ, D_MODEL), y_lat.reshape(DEC_BATCH, DEC_SEQ, D_MODEL),
            new_c, jnp.stack(new_n, axis=1), jnp.stack(new_m, axis=1),
            new_sk, new_sv, new_nk, new_nv)
```

```python
import functools

import jax
import jax.numpy as jnp
import numpy as np
from jax import lax
from jax.experimental import pallas as pl
from jax.experimental.pallas import tpu as pltpu

F32 = jnp.float32
BF16 = jnp.bfloat16

D_MODEL = 1024
BATCH = 32
SEQ = 256
DEPTH = 4
DEC_BATCH = 8
DEC_SEQ = 1024
GRID_W = 64
N_MIXERS = 3
N_ML_LAYERS = (DEPTH + 2) // 3
NORM_EPS = 1e-6
D_FF = 2816
ML_D_IN = 2 * D_MODEL
ML_HEADS = 4
ML_DK = ML_D_IN // (2 * ML_HEADS)
ML_DV = ML_D_IN // ML_HEADS
ML_CHUNK = 128
HEAD_DIM = 64
SWA_HEADS = D_MODEL // HEAD_DIM
SWA_KV = SWA_HEADS // 4
SWA_WINDOW = 128
Q_BLOCK = 128
ROPE_BASE = 10000.0
NA_HEADS = D_MODEL // HEAD_DIM
NA_KH = 8
NA_KW = 16
NEG_INF = -1e30

N_CTX = BATCH * SEQ
N_LAT = DEC_BATCH * DEC_SEQ
N_TOK = N_CTX + N_LAT
TM = 1024
N_CTX_TILES = N_CTX // TM
MOD_ROWS = 16
VMEM_LIMIT_BYTES = 56 * 1024 * 1024


def _params(*sem, vmem_limit_bytes=VMEM_LIMIT_BYTES):
    return pltpu.CompilerParams(dimension_semantics=sem, vmem_limit_bytes=vmem_limit_bytes)


def _mod_row(i):
    return jnp.where(i < N_CTX_TILES, 0, i - (N_CTX_TILES - 1))


def _silu(x):
    return x / (1.0 + jnp.exp(-x))


def _norm_mod(x, nw, shift, scale):
    y = x * lax.rsqrt(jnp.mean(x * x, axis=-1, keepdims=True) + NORM_EPS) * nw
    return y * (1.0 + scale) + shift


SUBLANES = 8
LANES = 128


def _dwconv_rows(u, cw, cb, seq):
    r, c = u.shape
    n_groups, per_seq = r // SUBLANES, seq // SUBLANES
    g = u.reshape(n_groups, SUBLANES, c)
    sub = lax.broadcasted_iota(jnp.int32, g.shape, 1)
    down = pltpu.roll(g, 1, 1)
    up = pltpu.roll(g, SUBLANES - 1, 1)
    zero = jnp.zeros((1, SUBLANES, c), F32)
    from_prev, from_next = [], []
    for s in range(0, n_groups, per_seq):
        from_prev += [zero, down[s:s + per_seq - 1]]
        from_next += [up[s + 1:s + per_seq], zero]
    prev = jnp.where(sub == 0, jnp.concatenate(from_prev, axis=0), down)
    nxt = jnp.where(sub == SUBLANES - 1, jnp.concatenate(from_next, axis=0), up)
    out = cw[0:1, :] * prev + cw[1:2, :] * g + cw[2:3, :] * nxt + cb
    return out.reshape(r, c)


def _by_tile_kind(tile, body):
    pl.when(tile < N_CTX_TILES)(functools.partial(body, SEQ))
    pl.when(tile >= N_CTX_TILES)(functools.partial(body, DEC_SEQ))


def _ada_kernel(c_ref, w_ref, b_ref, o_ref):
    s = _silu(c_ref[...]).astype(BF16)
    o_ref[...] = jnp.dot(s, w_ref[...].astype(BF16), preferred_element_type=F32) + b_ref[...]


def _ada_mod(cond, ada_w, ada_b):
    tn = 1536
    n = 6 * D_MODEL
    out = pl.pallas_call(
        _ada_kernel,
        grid=(DEPTH, n // tn),
        in_specs=[
            pl.BlockSpec((MOD_ROWS, D_MODEL), lambda l, j: (0, 0)),
            pl.BlockSpec((None, D_MODEL, tn), lambda l, j: (l, 0, j)),
            pl.BlockSpec((None, 1, tn), lambda l, j: (l, 0, j)),
        ],
        out_specs=pl.BlockSpec((None, MOD_ROWS, tn), lambda l, j: (l, 0, j)),
        out_shape=jax.ShapeDtypeStruct((DEPTH, MOD_ROWS, n), F32),
        compiler_params=_params("arbitrary", "arbitrary"),
        name="ada_mod",
    )(cond, ada_w, ada_b.reshape(DEPTH, 1, n))
    return out.reshape(DEPTH, MOD_ROWS, 6, D_MODEL)


QKV_TN = 512


def _rotate_pairs(a, cos, sin):
    lane = lax.broadcasted_iota(jnp.int32, a.shape, 1)
    first = (lane & (HEAD_DIM // 4)) == 0
    n = a.shape[1]
    partner = jnp.where(first, pltpu.roll(a, n - HEAD_DIM // 4, 1), pltpu.roll(a, HEAD_DIM // 4, 1))
    return a * cos + partner * sin


def _qkv_kernel(*refs, q_blocks, rope_blocks):
    if rope_blocks:
        x_ref, mod_ref, nw_ref, w_ref, cos_ref, sin_ref, q_ref, kv_ref, h_scr = refs
    else:
        x_ref, mod_ref, nw_ref, w_ref, q_ref, kv_ref, h_scr = refs
    i = pl.program_id(0)
    j = pl.program_id(1)

    @pl.when(j == 0)
    def _():
        h = _norm_mod(x_ref[...], nw_ref[...], mod_ref[0:1, :], mod_ref[1:2, :])
        h_scr[...] = h.astype(BF16)

    acc = jnp.dot(h_scr[...], w_ref[...], preferred_element_type=F32)

    def emit(val):
        @pl.when(j < q_blocks)
        def _():
            q_ref[...] = val.astype(BF16)

        @pl.when(j >= q_blocks)
        def _():
            kv_ref[...] = val

    if rope_blocks:
        rotate = jnp.logical_and(i >= N_CTX_TILES, j < rope_blocks)

        @pl.when(rotate)
        def _():
            emit(_rotate_pairs(acc, cos_ref[...], sin_ref[...]))

        @pl.when(jnp.logical_not(rotate))
        def _():
            emit(acc)
    else:
        emit(acc)


def _qkv_proj(x, mod, nw, w, n_q, rope_cols=0):
    n = w.shape[1]
    q_blocks = n_q // QKV_TN
    rope_blocks = rope_cols // QKV_TN
    in_specs = [
        pl.BlockSpec((TM, D_MODEL), lambda i, j: (i, 0)),
        pl.BlockSpec((None, 6, D_MODEL), lambda i, j: (_mod_row(i), 0, 0)),
        pl.BlockSpec((1, D_MODEL), lambda i, j: (0, 0)),
        pl.BlockSpec((D_MODEL, QKV_TN), lambda i, j: (0, j)),
    ]
    args = [x, mod, nw.reshape(1, D_MODEL), w]
    if rope_blocks:
        tab = pl.BlockSpec((DEC_SEQ, QKV_TN), lambda i, j: (0, 0))
        in_specs += [tab, tab]
        args += list(_rope_tables(QKV_TN))
    return pl.pallas_call(
        functools.partial(_qkv_kernel, q_blocks=q_blocks, rope_blocks=rope_blocks),
        grid=(N_TOK // TM, n // QKV_TN),
        in_specs=in_specs,
        out_specs=[pl.BlockSpec((TM, QKV_TN), lambda i, j: (i, jnp.minimum(j, q_blocks - 1))),
                   pl.BlockSpec((TM, QKV_TN), lambda i, j: (i, jnp.maximum(j - q_blocks, 0)))],
        out_shape=[jax.ShapeDtypeStruct((N_TOK, n_q), BF16), jax.ShapeDtypeStruct((N_TOK, n - n_q), F32)],
        scratch_shapes=[pltpu.VMEM((TM, D_MODEL), BF16)],
        compiler_params=_params("arbitrary", "arbitrary"),
        name="qkv_proj",
    )(*args)


def _mm_kernel(a_ref, w_ref, b_ref, o_ref):
    acc = jnp.dot(a_ref[...], w_ref[...], preferred_element_type=F32)
    o_ref[...] = (acc + b_ref[...]).astype(o_ref.dtype)


def _weight(w, block, index):
    if isinstance(w, tuple):
        stacked, layer = w
        return pl.BlockSpec((None,) + block, lambda *g: (layer,) + index(*g)), stacked
    return pl.BlockSpec(block, index), w


def _weight_cols(w):
    return w[0].shape[2] if isinstance(w, tuple) else w.shape[1]


def _mm(a, w, bias, tn, out_dtype):
    m, k = a.shape
    n = _weight_cols(w)
    w_spec, w = _weight(w, (k, tn), lambda i, j: (0, j))
    return pl.pallas_call(
        _mm_kernel,
        grid=(m // TM, n // tn),
        in_specs=[
            pl.BlockSpec((TM, k), lambda i, j: (i, 0)),
            w_spec,
            pl.BlockSpec((1, tn), lambda i, j: (0, j)),
        ],
        out_specs=pl.BlockSpec((TM, tn), lambda i, j: (i, j)),
        out_shape=jax.ShapeDtypeStruct((m, n), out_dtype),
        compiler_params=_params("arbitrary", "arbitrary"),
        name="mm",
    )(a, w, bias)


def _row_sources(x, width):
    (ctx, ctx_first), (lat, lat_first) = (((x[0], 0), (x[1], 0)) if isinstance(x, tuple)
                                          else ((x, 0), (x, N_CTX_TILES)))

    def spec(first, lo, hi):
        return pl.BlockSpec((TM, width), lambda i, *_: (jnp.clip(i, lo, hi) - lo + first, 0))

    return ([spec(ctx_first, 0, N_CTX_TILES - 1), spec(lat_first, N_CTX_TILES, N_TOK // TM - 1)], [ctx, lat])


def _mm_res_kernel(ac_ref, al_ref, w_ref, xc_ref, xl_ref, mod_ref, o_ref):
    def emit(a_ref, x_ref):
        acc = jnp.dot(a_ref[...], w_ref[...], preferred_element_type=F32)
        o_ref[...] = x_ref[...] + mod_ref[2:3, :] * acc

    is_ctx = pl.program_id(0) < N_CTX_TILES
    pl.when(is_ctx)(functools.partial(emit, ac_ref, xc_ref))
    pl.when(jnp.logical_not(is_ctx))(functools.partial(emit, al_ref, xl_ref))


def _mm_res(a_ctx, a_lat, w, x, mod):
    k = a_ctx.shape[1]
    a_specs, a_arrays = _row_sources((a_ctx, a_lat), k)
    x_specs, x_arrays = _row_sources(x, D_MODEL)
    w_spec, w = _weight(w, (k, D_MODEL), lambda i: (0, 0))
    return pl.pallas_call(
        _mm_res_kernel,
        grid=(N_TOK // TM,),
        in_specs=a_specs + [w_spec] + x_specs
        + [pl.BlockSpec((None, 6, D_MODEL), lambda i: (_mod_row(i), 0, 0))],
        out_specs=pl.BlockSpec((TM, D_MODEL), lambda i: (i, 0)),
        out_shape=jax.ShapeDtypeStruct((N_TOK, D_MODEL), F32),
        compiler_params=_params("arbitrary"),
        name="mm_res",
    )(*a_arrays, w, *x_arrays, mod)


FFN_TF = 256
FFN_BLOCKS = D_FF // FFN_TF


def _ffn_kernel(x_ref, mod_ref, nw_ref, wup_ref, cw_ref, cb_ref, wd_ref, o_ref, h_scr, acc_scr, raw_a, raw_b):
    nb = FFN_BLOCKS
    h_scr[...] = _norm_mod(x_ref[...], nw_ref[...], mod_ref[3:4, :], mod_ref[4:5, :]).astype(BF16)
    acc_scr[...] = jnp.zeros_like(acc_scr)

    def run(seq):
        def project(k, raw):
            raw[...] = jnp.dot(h_scr[...], wup_ref[k], preferred_element_type=F32)

        def consume(k, raw):
            u = _dwconv_rows(raw[...], cw_ref[k], cb_ref[k], seq)
            a = (_silu(u[:, :FFN_TF]) * u[:, FFN_TF:]).astype(BF16)
            acc_scr[...] += jnp.dot(a, wd_ref[k], preferred_element_type=F32)

        project(0, raw_a)

        def two_stages(t, carry):
            k = 2 * t
            project(k + 1, raw_b)
            consume(k, raw_a)
            project(k + 2, raw_a)
            consume(k + 1, raw_b)
            return carry

        lax.fori_loop(0, (nb - 1) // 2, two_stages, 0)
        consume(nb - 1, raw_a)

    _by_tile_kind(pl.program_id(0), run)
    o_ref[...] = x_ref[...] + mod_ref[5:6, :] * acc_scr[...]


def _ffn_weight_blocks(w_up, conv_w, conv_b, w_down):
    nb = FFN_BLOCKS

    def blocks(a):
        r = a.shape[1]
        return a.reshape(DEPTH, r, 2, nb, FFN_TF).transpose(0, 3, 1, 2, 4).reshape(DEPTH, nb, r, 2 * FFN_TF)

    return (blocks(w_up).astype(BF16), blocks(conv_w), blocks(conv_b.reshape(DEPTH, 1, 2 * D_FF)),
            w_down.reshape(DEPTH, nb, FFN_TF, D_MODEL).astype(BF16))


def _conv_ffn(x, mod, nw, layer, w_up, conv_w, conv_b, w_down):
    nb = FFN_BLOCKS

    def resident(shape):
        return pl.BlockSpec((None,) + shape, lambda i: (layer,) + (0,) * len(shape), pipeline_mode=pl.Buffered(1))

    return pl.pallas_call(
        _ffn_kernel,
        grid=(N_TOK // TM,),
        in_specs=[
            pl.BlockSpec((TM, D_MODEL), lambda i: (i, 0)),
            pl.BlockSpec((None, 6, D_MODEL), lambda i: (_mod_row(i), 0, 0)),
            pl.BlockSpec((1, D_MODEL), lambda i: (0, 0)),
            resident((nb, D_MODEL, 2 * FFN_TF)),
            resident((nb, 3, 2 * FFN_TF)),
            resident((nb, 1, 2 * FFN_TF)),
            resident((nb, FFN_TF, D_MODEL)),
        ],
        out_specs=pl.BlockSpec((TM, D_MODEL), lambda i: (i, 0)),
        out_shape=jax.ShapeDtypeStruct((N_TOK, D_MODEL), F32),
        scratch_shapes=[pltpu.VMEM((TM, D_MODEL), BF16), pltpu.VMEM((TM, D_MODEL), F32),
                        pltpu.VMEM((TM, 2 * FFN_TF), F32), pltpu.VMEM((TM, 2 * FFN_TF), F32)],
        compiler_params=_params("arbitrary"),
        name="conv_ffn",
    )(x, mod, nw.reshape(1, D_MODEL), w_up, conv_w, conv_b, w_down)


ML_TN = 512
ML_EXT = ML_DV + LANES
ML_SEQ_PER_STEP = 2


ML_BLOCKS = ML_D_IN // ML_TN
ML_UP_VMEM_LIMIT_BYTES = 58 * 1024 * 1024


def _ml_up_kernel(xc_src_ref, xl_src_ref, mod_ref, nw_ref, w_ref, cw_ref, cb_ref, xm_ref, xc_ref, z_ref,
                  h_scr, raw_a, raw_b):
    nb = ML_BLOCKS
    i = pl.program_id(0)

    def normalise(x_ref):
        h = _norm_mod(x_ref[...], nw_ref[...], mod_ref[0:1, :], mod_ref[1:2, :])
        h_scr[...] = h.astype(BF16)

    is_ctx = i < N_CTX_TILES
    pl.when(is_ctx)(functools.partial(normalise, xc_src_ref))
    pl.when(jnp.logical_not(is_ctx))(functools.partial(normalise, xl_src_ref))

    def run(seq):
        def project(k, raw):
            h = h_scr[...]
            raw[...] = jnp.dot(h, w_ref[k], preferred_element_type=F32)
            z_ref[:, k * ML_TN:(k + 1) * ML_TN] = jnp.dot(h, w_ref[nb + k], preferred_element_type=F32).astype(BF16)

        def consume(k, raw):
            cols = slice(k * ML_TN, (k + 1) * ML_TN)
            xm = raw[...]
            xm_ref[:, cols] = xm.astype(BF16)
            xc_ref[:, cols] = _silu(_dwconv_rows(xm, cw_ref[:, cols], cb_ref[:, cols], seq)).astype(BF16)

        raws = (raw_a, raw_b)
        project(0, raws[0])
        for k in range(nb):
            if k + 1 < nb:
                project(k + 1, raws[(k + 1) % 2])
            consume(k, raws[k % 2])

    _by_tile_kind(i, run)


def _ml_up_weight_blocks(w_up):
    layers = w_up.shape[0]
    return w_up.reshape(layers, D_MODEL, 2 * ML_BLOCKS, ML_TN).transpose(0, 2, 1, 3).astype(BF16)


def _ml_up(x, mod, nw, w_up, conv_w, conv_b):
    nb = ML_BLOCKS
    x_specs, x_arrays = _row_sources(x, D_MODEL)
    w_blocks, layer = w_up
    rows = pl.BlockSpec((TM, ML_D_IN), lambda i: (i, 0))
    shp = jax.ShapeDtypeStruct((N_TOK, ML_D_IN), BF16)
    return pl.pallas_call(
        _ml_up_kernel,
        grid=(N_TOK // TM,),
        in_specs=x_specs + [
            pl.BlockSpec((None, 6, D_MODEL), lambda i: (_mod_row(i), 0, 0)),
            pl.BlockSpec((1, D_MODEL), lambda i: (0, 0)),
            pl.BlockSpec((None, 2 * nb, D_MODEL, ML_TN), lambda i: (layer, 0, 0, 0), pipeline_mode=pl.Buffered(1)),
            pl.BlockSpec((3, ML_D_IN), lambda i: (0, 0)),
            pl.BlockSpec((1, ML_D_IN), lambda i: (0, 0)),
        ],
        out_specs=[rows, rows, rows],
        out_shape=[shp, shp, shp],
        scratch_shapes=[pltpu.VMEM((TM, D_MODEL), BF16), pltpu.VMEM((TM, ML_TN), F32), pltpu.VMEM((TM, ML_TN), F32)],
        compiler_params=_params("arbitrary", vmem_limit_bytes=ML_UP_VMEM_LIMIT_BYTES),
        name="ml_up",
    )(*x_arrays, mod, nw.reshape(1, D_MODEL), w_blocks, conv_w, conv_b.reshape(1, ML_D_IN))


def _log_sigmoid(x):
    return jnp.minimum(x, 0.0) - jnp.log(1.0 + jnp.exp(-jnp.abs(x)))


def _ml_scan_kernel(*refs, seq, n_seq, has_init, out_state, n_unused_inputs, fill_layer=None):
    q_ref, kt_ref, v_ref, g_ref, gt_ref, xc_ref, z_ref, nw_ref, skip_ref = refs[:9]
    pos = 9
    if has_init:
        c0_ref, n0_ref, m0_ref = refs[pos:pos + 3]
        pos += 3
    pos += n_unused_inputs
    o_ref = refs[pos]
    pos += 1
    if out_state:
        cout_ref, nout_ref, mout_ref = refs[pos:pos + 3]
        pos += 3
    hs_scr, c_scr = refs[pos:pos + 2]

    L = ML_CHUNK
    nc = seq // L
    ext_tiles = ML_EXT // LANES
    row_i = lax.broadcasted_iota(jnp.int32, (L, L), 0)
    col_i = lax.broadcasted_iota(jnp.int32, (L, L), 1)
    ones_tile = jnp.ones((L, LANES), BF16)

    def lanes(a, n):
        return jnp.concatenate([a] * n, axis=1)

    for d in (0, 1):
        keep = (col_i <= row_i) if d == 0 else (col_i >= row_i)
        keep_f = keep.astype(F32)
        keep_t_f = ((row_i <= col_i) if d == 0 else (row_i >= col_i)).astype(F32)
        end = L - 1 if d == 0 else 0

        ms = []
        for s in range(n_seq):
            if has_init:
                c_scr[s, :, :ML_DV] = c0_ref[s, d]
                c_scr[s, :, ML_DV:] = n0_ref[s, d]
                ms.append(m0_ref[s, d:d + 1, :])
            else:
                c_scr[s] = jnp.zeros((ML_DK, ML_EXT), F32)
                ms.append(jnp.zeros((1, LANES), F32))

        for c, s in [(c, s) for c in range(nc) for s in range(n_seq)]:
            cc = c if d == 0 else nc - 1 - c
            chunk = s * nc + cc
            rows = slice(chunk * L, (chunk + 1) * L)
            m = ms[s]
            qc = q_ref[rows, :]
            ktc = kt_ref[chunk]
            v_ext = jnp.concatenate([v_ref[rows, :], ones_tile], axis=1)
            gcol = g_ref[chunk]
            grow = gt_ref[chunk]
            ig_row = grow[2 * d:2 * d + 1, :]
            lf_col = _log_sigmoid(gcol[:, 2 * d + 1:2 * d + 2])
            lf_row = _log_sigmoid(grow[2 * d + 1:2 * d + 2, :])
            b_col = jnp.broadcast_to(jnp.sum(keep_f * lf_row, axis=1, keepdims=True), (L, LANES))
            b_row = jnp.sum(keep_t_f * lf_col, axis=0, keepdims=True)
            dmat = jnp.where(keep, b_col - b_row + ig_row, NEG_INF)
            m_loc = jnp.broadcast_to(jnp.max(dmat, axis=1, keepdims=True), (L, LANES))
            p_loc = jnp.exp(dmat - m_loc)
            s_loc = jnp.dot(qc, ktc, preferred_element_type=F32) * p_loc
            intra = jnp.dot(s_loc.astype(BF16), v_ext, preferred_element_type=F32)

            m_t = jnp.maximum(b_col + m, m_loc)
            w_inter = jnp.exp(b_col + m - m_t)
            w_intra = jnp.exp(m_loc - m_t)
            inter = jnp.dot(qc, c_scr[s].astype(BF16), preferred_element_type=F32)
            hx = lanes(w_inter, ext_tiles) * inter + lanes(w_intra, ext_tiles) * intra
            inv = 1.0 / jnp.maximum(jnp.abs(hx[:, ML_DV:]), jnp.exp(-m_t))
            h = hx[:, :ML_DV] * lanes(inv, ML_DV // LANES)

            b_end = b_col[end:end + 1, :]
            m_loc_end = m_loc[end:end + 1, :]
            m_new = jnp.maximum(b_end + m, m_loc_end)
            w_c = jnp.exp(b_end + m - m_new)
            w_s = p_loc[end:end + 1, :] * jnp.exp(m_loc_end - m_new)
            upd = jnp.dot((ktc.astype(F32) * w_s).astype(BF16), v_ext, preferred_element_type=F32)
            c_scr[s] = lanes(w_c, ext_tiles) * c_scr[s] + upd
            ms[s] = m_new

            if d == 0:
                hs_scr[rows, :] = h
            else:
                hs = hs_scr[rows, :] + h
                mu = jnp.mean(hs, axis=1, keepdims=True)
                cen = hs - mu
                var = jnp.mean(cen * cen, axis=1, keepdims=True)
                hn = cen * lax.rsqrt(var + NORM_EPS) * nw_ref[...]
                xc = xc_ref[rows, :].astype(F32)
                z = z_ref[rows, :].astype(F32)
                o_ref[rows, :] = ((hn + skip_ref[...] * xc) * _silu(z)).astype(o_ref.dtype)

        if out_state:
            for s in range(n_seq):
                if fill_layer is None:
                    cout_ref[s, d] = c_scr[s, :, :ML_DV]
                else:
                    for layer in range(cout_ref.shape[1]):
                        cout_ref[s, layer, d] = (c_scr[s, :, :ML_DV] if layer == fill_layer
                                                 else jnp.zeros((ML_DK, ML_DV), F32))
                nout_ref[s, d:d + 1, :] = c_scr[s, :, ML_DV:].T[0:1, :]
                mout_ref[s, d:d + 1, :] = ms[s]


def _ml_scan(q, kt, v, g4, g4t, xc, z, norm_w, skip, *, seq, nb, row0, init=None, final_c=None,
             n_seq=ML_SEQ_PER_STEP):
    L = ML_CHUNK
    rows = n_seq * seq
    nc = rows // L
    rb = row0 // rows
    has_init = init is not None
    in_specs = [
        pl.BlockSpec((rows, ML_DK), lambda b, h: (rb + b, h)),
        pl.BlockSpec((None, nc, ML_DK, L), lambda b, h: (h, rb + b, 0, 0)),
        pl.BlockSpec((rows, ML_DV), lambda b, h: (rb + b, h)),
        pl.BlockSpec((None, nc, L, 4), lambda b, h: (h, rb + b, 0, 0)),
        pl.BlockSpec((None, nc, 4, L), lambda b, h: (h, rb + b, 0, 0)),
        pl.BlockSpec((rows, ML_DV), lambda b, h: (rb + b, h)),
        pl.BlockSpec((rows, ML_DV), lambda b, h: (rb + b, h)),
        pl.BlockSpec((1, ML_DV), lambda b, h: (0, h)),
        pl.BlockSpec((1, ML_DV), lambda b, h: (0, h)),
    ]
    args = [q, kt, v, g4, g4t, xc, z, norm_w.reshape(1, ML_D_IN), skip.reshape(1, ML_D_IN)]
    state_n = pl.BlockSpec((n_seq, None, 2, ML_DK, LANES), lambda b, h: (b, h, 0, 0, 0))
    state_m = pl.BlockSpec((n_seq, None, 2, LANES), lambda b, h: (b, h, 0, 0))
    if has_init:
        c0, layer, n0, m0 = init
        in_specs += [pl.BlockSpec((n_seq, None, 2, None, ML_DK, ML_DV), lambda b, h: (b, layer, 0, h, 0, 0)),
                     state_n, state_m]
        args += [c0, n0, m0]
    out_specs = [pl.BlockSpec((rows, ML_DV), lambda b, h: (b, h))]
    out_shape = [jax.ShapeDtypeStruct((nb * seq, ML_D_IN), BF16)]
    aliases = {}
    if not has_init:
        layer_out, c_all = final_c
        if c_all is None:
            c_spec = pl.BlockSpec((n_seq, N_ML_LAYERS, 2, None, ML_DK, ML_DV), lambda b, h: (b, 0, 0, h, 0, 0))
        else:
            c_spec = pl.BlockSpec((n_seq, None, 2, None, ML_DK, ML_DV), lambda b, h: (b, layer_out, 0, h, 0, 0))
        out_specs += [c_spec, pl.BlockSpec((n_seq, None, 2, ML_DK), lambda b, h: (b, h, 0, 0)), state_m]
        out_shape += [
            jax.ShapeDtypeStruct((nb, N_ML_LAYERS, 2, ML_HEADS, ML_DK, ML_DV), F32),
            jax.ShapeDtypeStruct((nb, ML_HEADS, 2, ML_DK), F32),
            jax.ShapeDtypeStruct((nb, ML_HEADS, 2, LANES), F32),
        ]
        if c_all is not None:
            in_specs.append(pl.BlockSpec(memory_space=pl.ANY))
            args.append(c_all)
            aliases = {len(args) - 1: 1}
    return pl.pallas_call(
        functools.partial(_ml_scan_kernel, seq=seq, n_seq=n_seq, has_init=has_init, out_state=not has_init,
                          n_unused_inputs=len(aliases),
                          fill_layer=final_c[0] if (not has_init and final_c[1] is None) else None),
        grid=(nb // n_seq, ML_HEADS),
        in_specs=in_specs,
        out_specs=out_specs,
        out_shape=out_shape,
        scratch_shapes=[pltpu.VMEM((rows, ML_DV), F32), pltpu.VMEM((n_seq, ML_DK, ML_EXT), F32)],
        input_output_aliases=aliases,
        compiler_params=_params("arbitrary", "arbitrary"),
        name="ml_scan_init" if has_init else "ml_scan_zero",
    )(*args)


def _ml_qk_kernel(xc_ref, w_ref, q_ref, kt_ref):
    acc = jnp.dot(xc_ref[...], w_ref[...], preferred_element_type=F32)
    j = pl.program_id(1)

    @pl.when(j == 0)
    def _():
        q_ref[...] = (acc * ML_DK ** -0.5).astype(BF16)

    @pl.when(j == 1)
    def _():
        for h in range(ML_HEADS):
            kt = acc[:, h * ML_DK:(h + 1) * ML_DK].T
            for c in range(TM // ML_CHUNK):
                kt_ref[h, c] = kt[:, c * ML_CHUNK:(c + 1) * ML_CHUNK].astype(BF16)


def _ml_qk(xc, w_qk):
    n = ML_HEADS * ML_DK
    cpt = TM // ML_CHUNK
    w_spec, w_qk = _weight(w_qk, (ML_D_IN, n), lambda i, j: (0, j))
    return pl.pallas_call(
        _ml_qk_kernel,
        grid=(N_TOK // TM, 2),
        in_specs=[pl.BlockSpec((TM, ML_D_IN), lambda i, j: (i, 0)), w_spec],
        out_specs=[pl.BlockSpec((TM, n), lambda i, j: (i, 0)),
                   pl.BlockSpec((ML_HEADS, cpt, ML_DK, ML_CHUNK), lambda i, j: (0, i, 0, 0))],
        out_shape=[jax.ShapeDtypeStruct((N_TOK, n), BF16),
                   jax.ShapeDtypeStruct((ML_HEADS, N_TOK // ML_CHUNK, ML_DK, ML_CHUNK), BF16)],
        compiler_params=_params("arbitrary", "arbitrary"),
        name="ml_qk",
    )(xc, w_qk)


def _mlstm_layer(x, mod, nw, p, state, final_c):
    xm, xc, z = _ml_up(x, mod, nw, p["w_up"], p["conv_w"], p["conv_b"])
    q, kt = _ml_qk(xc, p["w_qk"])
    v = _mm(xm, p["w_v"], jnp.zeros((1, ML_D_IN), F32), 1024, BF16)
    g = _mm(xm, p["w_gate"], p["b_gate"], 128, F32)[:, :4 * ML_HEADS]
    L = ML_CHUNK
    g4 = g.reshape(N_TOK // L, L, 4, ML_HEADS).transpose(3, 0, 1, 2)
    g4t = g4.transpose(0, 1, 3, 2)
    oc, c_fin, n_fin, m_fin = _ml_scan(q, kt, v, g4, g4t, xc, z, p["norm_w"], p["skip"],
                                       seq=SEQ, nb=BATCH, row0=0, final_c=final_c)
    (ol,) = _ml_scan(q, kt, v, g4, g4t, xc, z, p["norm_w"], p["skip"],
                     seq=DEC_SEQ, nb=DEC_BATCH, row0=N_CTX, init=state, n_seq=1)
    x = _mm_res(oc, ol, p["w_down"], x, mod)
    return x, c_fin, n_fin.transpose(0, 2, 1, 3), m_fin[..., 0].transpose(0, 2, 1)


def _softmax_parts(scores, sink_col):
    m = functools.reduce(jnp.maximum, [jnp.max(s, axis=1, keepdims=True) for s in scores])
    if sink_col is not None:
        m = jnp.maximum(m, sink_col)
    ps = [jnp.exp(s - m) for s in scores]
    den = functools.reduce(jnp.add, [jnp.sum(p, axis=1, keepdims=True) for p in ps])
    if sink_col is not None:
        den = den + jnp.exp(sink_col - m)
    return ps, den


def _qk(q, k):
    return lax.dot_general(q, k, (((1,), (1,)), ((), ())), preferred_element_type=F32) * (HEAD_DIM ** -0.5)


def _attend_tiles(q_tiles, parts, sinks):
    return _attend_groups([(q_tiles, parts, sinks)])[0]


def _attend_groups(groups):
    r = groups[0][0][0].shape[0]
    lo = lax.broadcasted_iota(jnp.int32, (r, LANES), 1) < HEAD_DIM
    zero = jnp.zeros((r, LANES), BF16)
    all_scores = []
    for q_tiles, parts, _ in groups:
        qs = jnp.concatenate([jnp.where(sel, t, zero) for t in q_tiles for sel in (lo, jnp.logical_not(lo))], axis=0)
        all_scores.append([post(_qk(qs, k2)) for k2, _, post in parts])
    outs = []
    for scores, (q_tiles, parts, sinks) in zip(all_scores, groups):
        n_rows = scores[0].shape[0]
        sink_col = None
        if sinks is not None:
            rows = lax.broadcasted_iota(jnp.int32, (n_rows, 1), 0)
            sink_col = jnp.full((n_rows, 1), sinks[0], F32)
            for hi in range(1, len(sinks)):
                sink_col = jnp.where(rows >= hi * r, sinks[hi], sink_col)
        ps, den = _softmax_parts(scores, sink_col)
        o = functools.reduce(jnp.add, [jnp.dot(p.astype(BF16), v2, preferred_element_type=F32)
                                       for p, (_, v2, _) in zip(ps, parts)]) / den
        outs.append([jnp.where(lo, o[2 * a * r:(2 * a + 1) * r], o[(2 * a + 1) * r:(2 * a + 2) * r])
                     for a in range(len(q_tiles))])
    return outs


def _identity(s):
    return s


def _ctx_attn_kernel(*refs, tiles_per_kv, has_sink, kv_head_stride):
    if has_sink:
        sink_ref, q_ref, k_ref, v_ref, o_ref, kout_ref, vout_ref = refs
    else:
        q_ref, k_ref, v_ref, o_ref, kout_ref, vout_ref = refs
    for h in range(kout_ref.shape[0]):
        kout_ref[h] = k_ref[:, h * kv_head_stride:h * kv_head_stride + HEAD_DIM]
        vout_ref[h] = v_ref[:, h * kv_head_stride:h * kv_head_stride + HEAD_DIM]
    groups = []
    for t in range(k_ref.shape[1] // LANES):
        k2 = k_ref[:, t * LANES:(t + 1) * LANES].astype(BF16)
        v2 = v_ref[:, t * LANES:(t + 1) * LANES].astype(BF16)
        first = t * tiles_per_kv
        q_tiles = [q_ref[:, (first + a) * LANES:(first + a + 1) * LANES] for a in range(tiles_per_kv)]
        sinks = [sink_ref[2 * first + hi] for hi in range(2 * tiles_per_kv)] if has_sink else None
        groups.append((q_tiles, [(k2, v2, _identity)], sinks))
    for t, outs in enumerate(_attend_groups(groups)):
        for a, o in enumerate(outs):
            tile = t * tiles_per_kv + a
            o_ref[:, tile * LANES:(tile + 1) * LANES] = o.astype(o_ref.dtype)


def _ctx_attn(q, kv, n_kv_cols, n_kv_heads, sink):
    nq = q.shape[1]
    has_sink = sink is not None
    cache_spec = pl.BlockSpec((None, None, n_kv_heads, SEQ, HEAD_DIM), lambda b: (b, 0, 0, 0, 0))
    cache_shape = jax.ShapeDtypeStruct((BATCH, 1, n_kv_heads, SEQ, HEAD_DIM), F32)
    in_specs = [pl.BlockSpec((SEQ, nq), lambda b: (b, 0)),
                pl.BlockSpec((SEQ, n_kv_cols), lambda b: (b, 0)),
                pl.BlockSpec((SEQ, n_kv_cols), lambda b: (b, 1))]
    args = [q, kv, kv]
    if has_sink:
        in_specs = [pl.BlockSpec(memory_space=pltpu.SMEM)] + in_specs
        args = [sink] + args
    return pl.pallas_call(
        functools.partial(_ctx_attn_kernel, tiles_per_kv=nq // n_kv_cols, has_sink=has_sink,
                          kv_head_stride=n_kv_cols // n_kv_heads),
        grid=(BATCH,),
        in_specs=in_specs,
        out_specs=[pl.BlockSpec((SEQ, nq), lambda b: (b, 0)), cache_spec, cache_spec],
        out_shape=[jax.ShapeDtypeStruct((N_CTX, nq), BF16), cache_shape, cache_shape],
        compiler_params=_params("arbitrary"),
        name="ctx_attn",
    )(*args)


SWA_SPAN = Q_BLOCK + 2 * SWA_WINDOW
SWA_KV_COLS = SWA_KV * LANES


def _swa_lat_kernel(sink_ref, q_ref, k_ref, v_ref, kc_ref, vc_ref, o_ref):
    j = pl.program_id(1)
    start = pl.multiple_of(jnp.clip((j - 1) * Q_BLOCK, 0, DEC_SEQ - SWA_SPAN), Q_BLOCK)
    rows = 4 * Q_BLOCK
    qpos = j * Q_BLOCK + (lax.broadcasted_iota(jnp.int32, (rows, SWA_SPAN), 0) & (Q_BLOCK - 1))
    kpos = start + lax.broadcasted_iota(jnp.int32, (rows, SWA_SPAN), 1)
    in_window = jnp.abs(qpos - kpos) <= SWA_WINDOW

    def window(s):
        return jnp.where(in_window, s, NEG_INF)

    groups = []
    for t in range(SWA_KV):
        cols = slice(t * LANES, (t + 1) * LANES)
        k_loc = k_ref[pl.ds(start, SWA_SPAN), cols].astype(BF16)
        v_loc = v_ref[pl.ds(start, SWA_SPAN), cols].astype(BF16)
        q_tiles = [q_ref[:, (2 * t + a) * LANES:(2 * t + a + 1) * LANES] for a in range(2)]
        sinks = [sink_ref[4 * t + hi] for hi in range(4)]
        groups.append((q_tiles, [(k_loc, v_loc, window), (kc_ref[:, cols], vc_ref[:, cols], _identity)], sinks))
    for t, outs in enumerate(_attend_groups(groups)):
        for a, o in enumerate(outs):
            o_ref[:, (2 * t + a) * LANES:(2 * t + a + 1) * LANES] = o.astype(o_ref.dtype)


def _swa_latent(q, kv, kc, vc, sink):
    nq = q.shape[1]
    rb = N_CTX // DEC_SEQ
    qb = N_CTX // Q_BLOCK
    nj = DEC_SEQ // Q_BLOCK
    cspec = pl.BlockSpec((None, SEQ, SWA_KV_COLS), lambda b, j: (b, 0, 0))
    return pl.pallas_call(
        _swa_lat_kernel,
        grid=(DEC_BATCH, nj),
        in_specs=[pl.BlockSpec(memory_space=pltpu.SMEM),
                  pl.BlockSpec((Q_BLOCK, nq), lambda b, j: (qb + b * nj + j, 0)),
                  pl.BlockSpec((DEC_SEQ, SWA_KV_COLS), lambda b, j: (rb + b, 0)),
                  pl.BlockSpec((DEC_SEQ, SWA_KV_COLS), lambda b, j: (rb + b, 1)),
                  cspec, cspec],
        out_specs=pl.BlockSpec((Q_BLOCK, nq), lambda b, j: (b * nj + j, 0)),
        out_shape=jax.ShapeDtypeStruct((N_LAT, nq), BF16),
        compiler_params=_params("arbitrary", "arbitrary"),
        name="swa_latent",
    )(sink, q, kv, kv, kc, vc)


NA_QT = 256
NA_SPAN = 768
NA_TILES = 4


def _na_start(j):
    return (j // 2) * (DEC_SEQ - NA_SPAN)


NA_ROWS = DEC_SEQ // GRID_W
NA_DR = 2 * NA_KH - 1
NA_DC = 2 * NA_KW - 1


def _na_blocks_kernel(rpb_ref, onehot_ref, valid_ref, o_ref):
    t = jnp.dot(rpb_ref[...], onehot_ref[...], preferred_element_type=F32, precision=lax.Precision.HIGHEST)
    o_ref[...] = jnp.where(valid_ref[...] > 0.5, t, NEG_INF)


def _na_bias_blocks(rpb):
    h = rpb.shape[0]
    kpad = 32
    cq, ck = np.meshgrid(np.arange(GRID_W), np.arange(GRID_W), indexing="ij")
    dc = (np.clip(ck - cq, -(NA_KW - 1), NA_KW - 1) + NA_KW - 1).reshape(-1)
    cs = np.clip(cq - NA_KW // 2, 0, GRID_W - NA_KW)
    valid = ((ck >= cs) & (ck < cs + NA_KW)).reshape(1, -1).astype(np.float32)
    onehot = (np.arange(kpad)[:, None] == dc[None, :]).astype(np.float32)
    rpb2 = jnp.pad(rpb.reshape(h * NA_DR, NA_DC), ((0, 0), (0, kpad - NA_DC)))
    n = GRID_W * GRID_W
    blocks = pl.pallas_call(
        _na_blocks_kernel,
        grid=(1,),
        in_specs=[pl.BlockSpec((h * NA_DR, kpad), lambda i: (0, 0)),
                  pl.BlockSpec((kpad, n), lambda i: (0, 0)),
                  pl.BlockSpec((1, n), lambda i: (0, 0))],
        out_specs=pl.BlockSpec((h * NA_DR, n), lambda i: (0, 0)),
        out_shape=jax.ShapeDtypeStruct((h * NA_DR, n), F32),
        compiler_params=_params("arbitrary"),
        name="na_bias_blocks",
    )(rpb2, jnp.asarray(onehot), jnp.asarray(valid))
    blocks = blocks.reshape(h, NA_DR, GRID_W, GRID_W)
    padded = jnp.pad(blocks, ((0, 0), (1, 1), (0, 0), (0, 0)), constant_values=NEG_INF)
    return jnp.concatenate([padded[:, :-1], padded[:, 1:]], axis=-1)


def _na_lat_kernel(q_ref, k_ref, v_ref, kc_ref, vc_ref, blk_ref, o_ref, bias_scr):
    j = pl.program_id(0)
    start = pl.multiple_of(_na_start(j), 256)

    @pl.when(pl.program_id(2) == 0)
    def _():
        lane_lo = lax.broadcasted_iota(jnp.int32, (GRID_W, LANES), 1) < GRID_W
        for rq_l in range(NA_QT // GRID_W):
            rq = j * (NA_QT // GRID_W) + rq_l
            rs = jnp.clip(rq - NA_KH // 2, 0, NA_ROWS - NA_KH)
            for kp in range(NA_SPAN // LANES):
                rk = start // GRID_W + 2 * kp
                idx = jnp.clip(rk - rq + NA_KH, 0, NA_DR)
                in_band = [jnp.logical_and(r >= rs, r < rs + NA_KH).astype(jnp.int32) for r in (rk, rk + 1)]
                ok = jnp.where(lane_lo, in_band[0], in_band[1]) > 0
                for hh in range(2 * NA_TILES):
                    bias_scr[hh * NA_QT + rq_l * GRID_W:hh * NA_QT + (rq_l + 1) * GRID_W,
                             kp * LANES:(kp + 1) * LANES] = jnp.where(ok, blk_ref[hh, idx], NEG_INF)

    groups = []
    for t in range(NA_TILES):
        cols = slice(t * LANES, (t + 1) * LANES)

        def add_bias(s, t=t):
            return s + bias_scr[2 * t * NA_QT:2 * (t + 1) * NA_QT, :]

        k_loc = k_ref[pl.ds(start, NA_SPAN), cols].astype(BF16)
        v_loc = v_ref[pl.ds(start, NA_SPAN), cols].astype(BF16)
        groups.append(([q_ref[:, cols]], [(k_loc, v_loc, add_bias), (kc_ref[:, cols], vc_ref[:, cols], _identity)],
                       None))
    for t, (o,) in enumerate(_attend_groups(groups)):
        o_ref[:, t * LANES:(t + 1) * LANES] = o.astype(o_ref.dtype)


def _na_latent(q, kv, kc, vc, blocks):
    nq = q.shape[1]
    width = NA_TILES * LANES
    n_steps = nq // width
    nj = DEC_SEQ // NA_QT
    rb = N_CTX // DEC_SEQ
    qb = N_CTX // NA_QT
    cspec = pl.BlockSpec((None, SEQ, width), lambda j, p, b: (b, 0, p))
    return pl.pallas_call(
        _na_lat_kernel,
        grid=(nj, n_steps, DEC_BATCH),
        in_specs=[pl.BlockSpec((NA_QT, width), lambda j, p, b: (qb + b * nj + j, p)),
                  pl.BlockSpec((DEC_SEQ, width), lambda j, p, b: (rb + b, p)),
                  pl.BlockSpec((DEC_SEQ, width), lambda j, p, b: (rb + b, n_steps + p)),
                  cspec, cspec,
                  pl.BlockSpec((2 * NA_TILES, NA_DR + 1, GRID_W, LANES), lambda j, p, b: (p, 0, 0, 0))],
        out_specs=pl.BlockSpec((NA_QT, width), lambda j, p, b: (b * nj + j, p)),
        out_shape=jax.ShapeDtypeStruct((N_LAT, nq), BF16),
        scratch_shapes=[pltpu.VMEM((2 * NA_TILES * NA_QT, NA_SPAN), F32)],
        compiler_params=_params("arbitrary", "arbitrary", "arbitrary"),
        name="na_latent",
    )(q, kv, kv, kc, vc, blocks)


def _rope_tables(width):
    quarter = HEAD_DIM // 4
    pos = np.arange(DEC_SEQ)
    inv = np.power(ROPE_BASE, -np.arange(quarter, dtype=np.float32) / quarter).astype(np.float32)
    d = np.arange(width) % HEAD_DIM
    p = np.where((d < HEAD_DIM // 2)[None, :], (pos // GRID_W)[:, None], (pos % GRID_W)[:, None]).astype(np.float32)
    ang = p * inv[d % quarter][None, :]
    sign = np.where((d // quarter) % 2 == 0, -1.0, 1.0)[None, :]
    return jnp.asarray(np.cos(ang), F32), jnp.asarray(np.sin(ang) * sign, F32)


def _cache_rows(cache, dup):
    b, h, s, hd = cache.shape
    rows = jnp.broadcast_to(cache.transpose(0, 2, 1, 3)[:, :, :, None, :], (b, s, h, dup, hd))
    return rows.reshape(b, s, h * dup * hd).astype(BF16)


def _swa_layer(x, mod, nw, w_qkv, sink, w_o, cache_k, cache_v):
    nq, nk = SWA_HEADS * HEAD_DIM, SWA_KV * HEAD_DIM

    def dup_heads(w):
        return jnp.broadcast_to(w.reshape(D_MODEL, SWA_KV, 1, HEAD_DIM),
                                (D_MODEL, SWA_KV, 2, HEAD_DIM)).reshape(D_MODEL, SWA_KV_COLS)

    w = jnp.concatenate([w_qkv[:, :nq], dup_heads(w_qkv[:, nq:nq + nk]), dup_heads(w_qkv[:, nq + nk:])], axis=1)
    q, kv = _qkv_proj(x, mod, nw, w.astype(BF16), nq, rope_cols=nq + SWA_KV_COLS)
    oc, k_new, v_new = _ctx_attn(q, kv, SWA_KV_COLS, SWA_KV, sink)
    ol = _swa_latent(q, kv, _cache_rows(cache_k, 2), _cache_rows(cache_v, 2), sink)
    return _mm_res(oc, ol, w_o, x, mod), k_new, v_new


def _na_layer(x, mod, nw, w_qkv, rpb, w_o, cache_k, cache_v):
    n = NA_HEADS * HEAD_DIM
    q, kv = _qkv_proj(x, mod, nw, w_qkv, n)
    oc, k_new, v_new = _ctx_attn(q, kv, n, NA_HEADS, None)
    ol = _na_latent(q, kv, _cache_rows(cache_k, 1), _cache_rows(cache_v, 1), _na_bias_blocks(rpb))
    return _mm_res(oc, ol, w_o, x, mod), k_new, v_new


def _final_norm_kernel(x_ref, w_ref, oc_ref, ol_ref):
    x = x_ref[...]
    y = x * lax.rsqrt(jnp.mean(x * x, axis=-1, keepdims=True) + NORM_EPS) * w_ref[...]
    is_ctx = pl.program_id(0) < N_CTX_TILES

    @pl.when(is_ctx)
    def _():
        oc_ref[...] = y

    @pl.when(jnp.logical_not(is_ctx))
    def _():
        ol_ref[...] = y


def _final_norm(x, w):
    return pl.pallas_call(
        _final_norm_kernel,
        grid=(N_TOK // TM,),
        in_specs=[pl.BlockSpec((TM, D_MODEL), lambda i: (i, 0)), pl.BlockSpec((1, D_MODEL), lambda i: (0, 0))],
        out_specs=[pl.BlockSpec((TM, D_MODEL), lambda i: (jnp.minimum(i, N_CTX_TILES - 1), 0)),
                   pl.BlockSpec((TM, D_MODEL), lambda i: (jnp.maximum(i - N_CTX_TILES, 0), 0))],
        out_shape=[jax.ShapeDtypeStruct((N_CTX, D_MODEL), F32), jax.ShapeDtypeStruct((N_LAT, D_MODEL), F32)],
        compiler_params=_params("arbitrary"),
        name="final_norm",
    )(x, w.reshape(1, D_MODEL))


def kernel(x_prompt, x_sample, state_mlstm_C, state_mlstm_n, state_mlstm_m, cache_swa_k, cache_swa_v, cache_na_k, cache_na_v, c, c_ctx, ada_w, ada_b, norm_w, final_norm_w, ffn_w_up, ffn_conv_w, ffn_conv_b, ffn_w_down, ml_w_up, ml_conv_w, ml_conv_b, ml_w_qk, ml_w_v, ml_w_gate, ml_b_gate, ml_norm_w, ml_skip, ml_w_down, swa_w_qkv, swa_sink, swa_w_o, na_w_qkv, na_rpb, na_w_o):
    x = (x_prompt.reshape(N_CTX, D_MODEL), x_sample.reshape(N_LAT, D_MODEL))
    cond = jnp.concatenate([c_ctx[None], c, jnp.zeros((MOD_ROWS - 1 - DEC_BATCH, D_MODEL), F32)], axis=0)
    mods = _ada_mod(cond, ada_w, ada_b)
    ffn_weights = _ffn_weight_blocks(ffn_w_up, ffn_conv_w, ffn_conv_b, ffn_w_down)
    ml_up_blocks = _ml_up_weight_blocks(ml_w_up)
    ml_qk_bf16, ml_v_bf16, ml_down_bf16 = ml_w_qk.astype(BF16), ml_w_v.astype(BF16), ml_w_down.astype(BF16)

    new_c, new_n, new_m = None, [], []
    new_sk = new_sv = new_nk = new_nv = None
    for i in range(DEPTH):
        kind, j = i % N_MIXERS, i // N_MIXERS
        mod = mods[i]
        if kind == 0:
            gate_w = jnp.pad(ml_w_gate[j], ((0, 0), (0, 128 - 4 * ML_HEADS))).astype(BF16)
            gate_b = jnp.pad(ml_b_gate[j], (0, 128 - 4 * ML_HEADS)).reshape(1, 128)
            p = dict(w_up=(ml_up_blocks, j), conv_w=ml_conv_w[j], conv_b=ml_conv_b[j],
                     w_qk=(ml_qk_bf16, j), w_v=(ml_v_bf16, j), w_gate=gate_w, b_gate=gate_b,
                     norm_w=ml_norm_w[j], skip=ml_skip[j], w_down=(ml_down_bf16, j))
            n0 = jnp.broadcast_to(state_mlstm_n[:, j].transpose(0, 2, 1, 3)[..., None],
                                  (DEC_BATCH, ML_HEADS, 2, ML_DK, LANES))
            m0 = jnp.broadcast_to(state_mlstm_m[:, j].transpose(0, 2, 1)[..., None], (DEC_BATCH, ML_HEADS, 2, LANES))
            x, new_c, nf, mf = _mlstm_layer(x, mod, norm_w[i, 0], p, (state_mlstm_C, j, n0, m0), (j, new_c))
            new_n.append(nf)
            new_m.append(mf)
        elif kind == 1:
            x, k_new, v_new = _swa_layer(x, mod, norm_w[i, 0], swa_w_qkv[j].astype(BF16), swa_sink[j],
                                         swa_w_o[j].astype(BF16), cache_swa_k[:, j], cache_swa_v[:, j])
            new_sk, new_sv = k_new, v_new
        else:
            x, k_new, v_new = _na_layer(x, mod, norm_w[i, 0], na_w_qkv[j].astype(BF16), na_rpb[j],
                                        na_w_o[j].astype(BF16), cache_na_k[:, j], cache_na_v[:, j])
            new_nk, new_nv = k_new, v_new
        x = _conv_ffn(x, mod, norm_w[i, 1], i, *ffn_weights)

    y_ctx, y_lat = _final_norm(x, final_norm_w)
    return (y_ctx.reshape(BATCH, SEQ, D_MODEL), y_lat.reshape(DEC_BATCH, DEC_SEQ, D_MODEL),
            new_c, jnp.stack(new_n, axis=1), jnp.stack(new_m, axis=1),
            new_sk, new_sv, new_nk, new_nv)
```

```python
import functools

import jax
import jax.numpy as jnp
import numpy as np
from jax import lax
from jax.experimental import pallas as pl
from jax.experimental.pallas import tpu as pltpu

F32 = jnp.float32
BF16 = jnp.bfloat16

D_MODEL = 1024
BATCH = 32
SEQ = 256
DEPTH = 4
DEC_BATCH = 8
DEC_SEQ = 1024
GRID_W = 64
N_MIXERS = 3
N_ML_LAYERS = (DEPTH + 2) // 3
NORM_EPS = 1e-6
D_FF = 2816
ML_D_IN = 2 * D_MODEL
ML_HEADS = 4
ML_DK = ML_D_IN // (2 * ML_HEADS)
ML_DV = ML_D_IN // ML_HEADS
ML_CHUNK = 128
HEAD_DIM = 64
SWA_HEADS = D_MODEL // HEAD_DIM
SWA_KV = SWA_HEADS // 4
SWA_WINDOW = 128
Q_BLOCK = 128
ROPE_BASE = 10000.0
NA_HEADS = D_MODEL // HEAD_DIM
NA_KH = 8
NA_KW = 16
NEG_INF = -1e30

N_CTX = BATCH * SEQ
N_LAT = DEC_BATCH * DEC_SEQ
N_TOK = N_CTX + N_LAT
TM = 1024
N_CTX_TILES = N_CTX // TM
MOD_ROWS = 16
VMEM_LIMIT_BYTES = 56 * 1024 * 1024


def _params(*sem, vmem_limit_bytes=VMEM_LIMIT_BYTES):
    return pltpu.CompilerParams(dimension_semantics=sem, vmem_limit_bytes=vmem_limit_bytes)


def _mod_row(i):
    return jnp.where(i < N_CTX_TILES, 0, i - (N_CTX_TILES - 1))


def _silu(x):
    return x / (1.0 + jnp.exp(-x))


def _norm_mod(x, nw, shift, scale):
    y = x * lax.rsqrt(jnp.mean(x * x, axis=-1, keepdims=True) + NORM_EPS) * nw
    return y * (1.0 + scale) + shift


SUBLANES = 8
LANES = 128


def _dwconv_rows(u, cw, cb, seq):
    r, c = u.shape
    n_groups, per_seq = r // SUBLANES, seq // SUBLANES
    g = u.reshape(n_groups, SUBLANES, c)
    sub = lax.broadcasted_iota(jnp.int32, g.shape, 1)
    down = pltpu.roll(g, 1, 1)
    up = pltpu.roll(g, SUBLANES - 1, 1)
    zero = jnp.zeros((1, SUBLANES, c), F32)
    from_prev, from_next = [], []
    for s in range(0, n_groups, per_seq):
        from_prev += [zero, down[s:s + per_seq - 1]]
        from_next += [up[s + 1:s + per_seq], zero]
    prev = jnp.where(sub == 0, jnp.concatenate(from_prev, axis=0), down)
    nxt = jnp.where(sub == SUBLANES - 1, jnp.concatenate(from_next, axis=0), up)
    out = cw[0:1, :] * prev + cw[1:2, :] * g + cw[2:3, :] * nxt + cb
    return out.reshape(r, c)


def _by_tile_kind(tile, body):
    pl.when(tile < N_CTX_TILES)(functools.partial(body, SEQ))
    pl.when(tile >= N_CTX_TILES)(functools.partial(body, DEC_SEQ))


def _ada_kernel(c_ref, w_ref, b_ref, o_ref):
    s = _silu(c_ref[...]).astype(BF16)
    o_ref[...] = jnp.dot(s, w_ref[...].astype(BF16), preferred_element_type=F32) + b_ref[...]


def _ada_mod(cond, ada_w, ada_b):
    tn = 3072
    n = 6 * D_MODEL
    out = pl.pallas_call(
        _ada_kernel,
        grid=(DEPTH, n // tn),
        in_specs=[
            pl.BlockSpec((MOD_ROWS, D_MODEL), lambda l, j: (0, 0)),
            pl.BlockSpec((None, D_MODEL, tn), lambda l, j: (l, 0, j)),
            pl.BlockSpec((None, 1, tn), lambda l, j: (l, 0, j)),
        ],
        out_specs=pl.BlockSpec((None, MOD_ROWS, tn), lambda l, j: (l, 0, j)),
        out_shape=jax.ShapeDtypeStruct((DEPTH, MOD_ROWS, n), F32),
        compiler_params=_params("arbitrary", "arbitrary"),
        name="ada_mod",
    )(cond, ada_w, ada_b.reshape(DEPTH, 1, n))
    return out.reshape(DEPTH, MOD_ROWS, 6, D_MODEL)


QKV_TN = 512


def _rotate_pairs(a, cos, sin):
    lane = lax.broadcasted_iota(jnp.int32, a.shape, 1)
    first = (lane & (HEAD_DIM // 4)) == 0
    n = a.shape[1]
    partner = jnp.where(first, pltpu.roll(a, n - HEAD_DIM // 4, 1), pltpu.roll(a, HEAD_DIM // 4, 1))
    return a * cos + partner * sin


def _qkv_kernel(*refs, q_blocks, rope_blocks):
    if rope_blocks:
        x_ref, mod_ref, nw_ref, w_ref, cos_ref, sin_ref, q_ref, kv_ref, h_scr = refs
    else:
        x_ref, mod_ref, nw_ref, w_ref, q_ref, kv_ref, h_scr = refs
    i = pl.program_id(0)
    j = pl.program_id(1)

    @pl.when(j == 0)
    def _():
        h = _norm_mod(x_ref[...], nw_ref[...], mod_ref[0:1, :], mod_ref[1:2, :])
        h_scr[...] = h.astype(BF16)

    acc = jnp.dot(h_scr[...], w_ref[...], preferred_element_type=F32)

    def emit(val):
        @pl.when(j < q_blocks)
        def _():
            q_ref[...] = val.astype(BF16)

        @pl.when(j >= q_blocks)
        def _():
            kv_ref[...] = val

    if rope_blocks:
        rotate = jnp.logical_and(i >= N_CTX_TILES, j < rope_blocks)

        @pl.when(rotate)
        def _():
            emit(_rotate_pairs(acc, cos_ref[...], sin_ref[...]))

        @pl.when(jnp.logical_not(rotate))
        def _():
            emit(acc)
    else:
        emit(acc)


def _qkv_proj(x, mod, nw, w, n_q, rope_cols=0):
    n = w.shape[1]
    q_blocks = n_q // QKV_TN
    rope_blocks = rope_cols // QKV_TN
    in_specs = [
        pl.BlockSpec((TM, D_MODEL), lambda i, j: (i, 0)),
        pl.BlockSpec((None, 6, D_MODEL), lambda i, j: (_mod_row(i), 0, 0)),
        pl.BlockSpec((1, D_MODEL), lambda i, j: (0, 0)),
        pl.BlockSpec((D_MODEL, QKV_TN), lambda i, j: (0, j)),
    ]
    args = [x, mod, nw.reshape(1, D_MODEL), w]
    if rope_blocks:
        tab = pl.BlockSpec((DEC_SEQ, QKV_TN), lambda i, j: (0, 0))
        in_specs += [tab, tab]
        args += list(_rope_tables(QKV_TN))
    return pl.pallas_call(
        functools.partial(_qkv_kernel, q_blocks=q_blocks, rope_blocks=rope_blocks),
        grid=(N_TOK // TM, n // QKV_TN),
        in_specs=in_specs,
        out_specs=[pl.BlockSpec((TM, QKV_TN), lambda i, j: (i, jnp.minimum(j, q_blocks - 1))),
                   pl.BlockSpec((TM, QKV_TN), lambda i, j: (i, jnp.maximum(j - q_blocks, 0)))],
        out_shape=[jax.ShapeDtypeStruct((N_TOK, n_q), BF16), jax.ShapeDtypeStruct((N_TOK, n - n_q), F32)],
        scratch_shapes=[pltpu.VMEM((TM, D_MODEL), BF16)],
        compiler_params=_params("arbitrary", "arbitrary"),
        name="qkv_proj",
    )(*args)


def _mm_kernel(a_ref, w_ref, b_ref, o_ref):
    acc = jnp.dot(a_ref[...], w_ref[...], preferred_element_type=F32)
    o_ref[...] = (acc + b_ref[...]).astype(o_ref.dtype)


def _weight(w, block, index):
    if isinstance(w, tuple):
        stacked, layer = w
        return pl.BlockSpec((None,) + block, lambda *g: (layer,) + index(*g)), stacked
    return pl.BlockSpec(block, index), w


def _weight_cols(w):
    return w[0].shape[2] if isinstance(w, tuple) else w.shape[1]


def _mm(a, w, bias, tn, out_dtype):
    m, k = a.shape
    n = _weight_cols(w)
    w_spec, w = _weight(w, (k, tn), lambda i, j: (0, j))
    return pl.pallas_call(
        _mm_kernel,
        grid=(m // TM, n // tn),
        in_specs=[
            pl.BlockSpec((TM, k), lambda i, j: (i, 0)),
            w_spec,
            pl.BlockSpec((1, tn), lambda i, j: (0, j)),
        ],
        out_specs=pl.BlockSpec((TM, tn), lambda i, j: (i, j)),
        out_shape=jax.ShapeDtypeStruct((m, n), out_dtype),
        compiler_params=_params("arbitrary", "arbitrary"),
        name="mm",
    )(a, w, bias)


def _row_sources(x, width):
    (ctx, ctx_first), (lat, lat_first) = (((x[0], 0), (x[1], 0)) if isinstance(x, tuple)
                                          else ((x, 0), (x, N_CTX_TILES)))

    def spec(first, lo, hi):
        return pl.BlockSpec((TM, width), lambda i, *_: (jnp.clip(i, lo, hi) - lo + first, 0))

    return ([spec(ctx_first, 0, N_CTX_TILES - 1), spec(lat_first, N_CTX_TILES, N_TOK // TM - 1)], [ctx, lat])


def _mm_res_kernel(ac_ref, al_ref, w_ref, xc_ref, xl_ref, mod_ref, o_ref):
    def emit(a_ref, x_ref):
        acc = jnp.dot(a_ref[...], w_ref[...], preferred_element_type=F32)
        o_ref[...] = x_ref[...] + mod_ref[2:3, :] * acc

    is_ctx = pl.program_id(0) < N_CTX_TILES
    pl.when(is_ctx)(functools.partial(emit, ac_ref, xc_ref))
    pl.when(jnp.logical_not(is_ctx))(functools.partial(emit, al_ref, xl_ref))


def _mm_res(a_ctx, a_lat, w, x, mod):
    k = a_ctx.shape[1]
    a_specs, a_arrays = _row_sources((a_ctx, a_lat), k)
    x_specs, x_arrays = _row_sources(x, D_MODEL)
    w_spec, w = _weight(w, (k, D_MODEL), lambda i: (0, 0))
    return pl.pallas_call(
        _mm_res_kernel,
        grid=(N_TOK // TM,),
        in_specs=a_specs + [w_spec] + x_specs
        + [pl.BlockSpec((None, 6, D_MODEL), lambda i: (_mod_row(i), 0, 0))],
        out_specs=pl.BlockSpec((TM, D_MODEL), lambda i: (i, 0)),
        out_shape=jax.ShapeDtypeStruct((N_TOK, D_MODEL), F32),
        compiler_params=_params("arbitrary"),
        name="mm_res",
    )(*a_arrays, w, *x_arrays, mod)


FFN_TF = 256
FFN_BLOCKS = D_FF // FFN_TF


def _ffn_kernel(x_ref, mod_ref, nw_ref, wup_ref, cw_ref, cb_ref, wd_ref, o_ref, h_scr, acc_scr, raw_a, raw_b):
    nb = FFN_BLOCKS
    h_scr[...] = _norm_mod(x_ref[...], nw_ref[...], mod_ref[3:4, :], mod_ref[4:5, :]).astype(BF16)
    acc_scr[...] = jnp.zeros_like(acc_scr)

    def run(seq):
        def project(k, raw):
            raw[...] = jnp.dot(h_scr[...], wup_ref[k], preferred_element_type=F32)

        def consume(k, raw):
            u = _dwconv_rows(raw[...], cw_ref[k], cb_ref[k], seq)
            a = (_silu(u[:, :FFN_TF]) * u[:, FFN_TF:]).astype(BF16)
            acc_scr[...] += jnp.dot(a, wd_ref[k], preferred_element_type=F32)

        project(0, raw_a)

        def two_stages(t, carry):
            k = 2 * t
            project(k + 1, raw_b)
            consume(k, raw_a)
            project(k + 2, raw_a)
            consume(k + 1, raw_b)
            return carry

        lax.fori_loop(0, (nb - 1) // 2, two_stages, 0)
        consume(nb - 1, raw_a)

    _by_tile_kind(pl.program_id(0), run)
    o_ref[...] = x_ref[...] + mod_ref[5:6, :] * acc_scr[...]


def _column_blocks_kernel(*refs):
    *in_refs, o_ref = refs
    o_ref[...] = jnp.concatenate([r[...] for r in in_refs], axis=1).astype(o_ref.dtype)


def _column_blocks_bf16(w, width, groups):
    layers, r, cols = w.shape
    nb = cols // (groups * width)
    in_specs = [pl.BlockSpec((None, r, width), functools.partial(lambda l, k, g: (l, 0, g * nb + k), g=g))
                for g in range(groups)]
    return pl.pallas_call(
        _column_blocks_kernel,
        grid=(layers, nb),
        in_specs=in_specs,
        out_specs=pl.BlockSpec((None, None, r, groups * width), lambda l, k: (l, k, 0, 0)),
        out_shape=jax.ShapeDtypeStruct((layers, nb, r, groups * width), BF16),
        compiler_params=_params("arbitrary", "arbitrary"),
        name="column_blocks",
    )(*([w] * groups))


def _ffn_weight_blocks(w_up, conv_w, conv_b, w_down):
    nb = FFN_BLOCKS

    def blocks(a):
        r = a.shape[1]
        return a.reshape(DEPTH, r, 2, nb, FFN_TF).transpose(0, 3, 1, 2, 4).reshape(DEPTH, nb, r, 2 * FFN_TF)

    return (_column_blocks_bf16(w_up, FFN_TF, 2), blocks(conv_w), blocks(conv_b.reshape(DEPTH, 1, 2 * D_FF)),
            w_down.reshape(DEPTH, nb, FFN_TF, D_MODEL).astype(BF16))


def _conv_ffn(x, mod, nw, layer, w_up, conv_w, conv_b, w_down):
    nb = FFN_BLOCKS

    def resident(shape):
        return pl.BlockSpec((None,) + shape, lambda i: (layer,) + (0,) * len(shape), pipeline_mode=pl.Buffered(1))

    return pl.pallas_call(
        _ffn_kernel,
        grid=(N_TOK // TM,),
        in_specs=[
            pl.BlockSpec((TM, D_MODEL), lambda i: (i, 0)),
            pl.BlockSpec((None, 6, D_MODEL), lambda i: (_mod_row(i), 0, 0)),
            pl.BlockSpec((1, D_MODEL), lambda i: (0, 0)),
            resident((nb, D_MODEL, 2 * FFN_TF)),
            resident((nb, 3, 2 * FFN_TF)),
            resident((nb, 1, 2 * FFN_TF)),
            resident((nb, FFN_TF, D_MODEL)),
        ],
        out_specs=pl.BlockSpec((TM, D_MODEL), lambda i: (i, 0)),
        out_shape=jax.ShapeDtypeStruct((N_TOK, D_MODEL), F32),
        scratch_shapes=[pltpu.VMEM((TM, D_MODEL), BF16), pltpu.VMEM((TM, D_MODEL), F32),
                        pltpu.VMEM((TM, 2 * FFN_TF), F32), pltpu.VMEM((TM, 2 * FFN_TF), F32)],
        compiler_params=_params("arbitrary"),
        name="conv_ffn",
    )(x, mod, nw.reshape(1, D_MODEL), w_up, conv_w, conv_b, w_down)


ML_TN = 512
ML_EXT = ML_DV + LANES
ML_SEQ_PER_STEP = 2


ML_BLOCKS = ML_D_IN // ML_TN
ML_UP_VMEM_LIMIT_BYTES = 58 * 1024 * 1024


def _ml_up_kernel(xc_src_ref, xl_src_ref, mod_ref, nw_ref, w_ref, cw_ref, cb_ref, xm_ref, xc_ref, z_ref,
                  h_scr, raw_a, raw_b):
    nb = ML_BLOCKS
    i = pl.program_id(0)

    def normalise(x_ref):
        h = _norm_mod(x_ref[...], nw_ref[...], mod_ref[0:1, :], mod_ref[1:2, :])
        h_scr[...] = h.astype(BF16)

    is_ctx = i < N_CTX_TILES
    pl.when(is_ctx)(functools.partial(normalise, xc_src_ref))
    pl.when(jnp.logical_not(is_ctx))(functools.partial(normalise, xl_src_ref))

    def run(seq):
        def project(k, raw):
            h = h_scr[...]
            raw[...] = jnp.dot(h, w_ref[k], preferred_element_type=F32)
            z_ref[:, k * ML_TN:(k + 1) * ML_TN] = jnp.dot(h, w_ref[nb + k], preferred_element_type=F32).astype(BF16)

        def consume(k, raw):
            cols = slice(k * ML_TN, (k + 1) * ML_TN)
            xm = raw[...]
            xm_ref[:, cols] = xm.astype(BF16)
            xc_ref[:, cols] = _silu(_dwconv_rows(xm, cw_ref[:, cols], cb_ref[:, cols], seq)).astype(BF16)

        raws = (raw_a, raw_b)
        project(0, raws[0])
        for k in range(nb):
            if k + 1 < nb:
                project(k + 1, raws[(k + 1) % 2])
            consume(k, raws[k % 2])

    _by_tile_kind(i, run)


def _ml_up_weight_blocks(w_up):
    return _column_blocks_bf16(w_up, ML_TN, 1)


def _ml_up(x, mod, nw, w_up, conv_w, conv_b):
    nb = ML_BLOCKS
    x_specs, x_arrays = _row_sources(x, D_MODEL)
    w_blocks, layer = w_up
    rows = pl.BlockSpec((TM, ML_D_IN), lambda i: (i, 0))
    shp = jax.ShapeDtypeStruct((N_TOK, ML_D_IN), BF16)
    return pl.pallas_call(
        _ml_up_kernel,
        grid=(N_TOK // TM,),
        in_specs=x_specs + [
            pl.BlockSpec((None, 6, D_MODEL), lambda i: (_mod_row(i), 0, 0)),
            pl.BlockSpec((1, D_MODEL), lambda i: (0, 0)),
            pl.BlockSpec((None, 2 * nb, D_MODEL, ML_TN), lambda i: (layer, 0, 0, 0), pipeline_mode=pl.Buffered(1)),
            pl.BlockSpec((3, ML_D_IN), lambda i: (0, 0)),
            pl.BlockSpec((1, ML_D_IN), lambda i: (0, 0)),
        ],
        out_specs=[rows, rows, rows],
        out_shape=[shp, shp, shp],
        scratch_shapes=[pltpu.VMEM((TM, D_MODEL), BF16), pltpu.VMEM((TM, ML_TN), F32), pltpu.VMEM((TM, ML_TN), F32)],
        compiler_params=_params("arbitrary", vmem_limit_bytes=ML_UP_VMEM_LIMIT_BYTES),
        name="ml_up",
    )(*x_arrays, mod, nw.reshape(1, D_MODEL), w_blocks, conv_w, conv_b.reshape(1, ML_D_IN))


def _log_sigmoid(x):
    return jnp.minimum(x, 0.0) - jnp.log(1.0 + jnp.exp(-jnp.abs(x)))


def _ml_scan_kernel(*refs, seq, n_seq, has_init, out_state, n_unused_inputs, fill_layer=None):
    q_ref, kt_ref, v_ref, g_ref, gt_ref, xc_ref, z_ref, nw_ref, skip_ref = refs[:9]
    pos = 9
    if has_init:
        c0_ref, n0_ref, m0_ref = refs[pos:pos + 3]
        pos += 3
    pos += n_unused_inputs
    o_ref = refs[pos]
    pos += 1
    if out_state:
        cout_ref, nout_ref, mout_ref = refs[pos:pos + 3]
        pos += 3
    hs_scr, c_scr = refs[pos:pos + 2]

    L = ML_CHUNK
    nc = seq // L
    ext_tiles = ML_EXT // LANES
    row_i = lax.broadcasted_iota(jnp.int32, (L, L), 0)
    col_i = lax.broadcasted_iota(jnp.int32, (L, L), 1)
    ones_tile = jnp.ones((L, LANES), BF16)

    def lanes(a, n):
        return jnp.concatenate([a] * n, axis=1)

    for d in (0, 1):
        keep = (col_i <= row_i) if d == 0 else (col_i >= row_i)
        keep_f = keep.astype(F32)
        keep_t_f = ((row_i <= col_i) if d == 0 else (row_i >= col_i)).astype(F32)
        end = L - 1 if d == 0 else 0

        ms = []
        for s in range(n_seq):
            if has_init:
                c_scr[s, :, :ML_DV] = c0_ref[s, d]
                c_scr[s, :, ML_DV:] = n0_ref[s, d]
                ms.append(m0_ref[s, d:d + 1, :])
            else:
                c_scr[s] = jnp.zeros((ML_DK, ML_EXT), F32)
                ms.append(jnp.zeros((1, LANES), F32))

        for c, s in [(c, s) for c in range(nc) for s in range(n_seq)]:
            cc = c if d == 0 else nc - 1 - c
            chunk = s * nc + cc
            rows = slice(chunk * L, (chunk + 1) * L)
            m = ms[s]
            qc = q_ref[rows, :]
            ktc = kt_ref[chunk]
            v_ext = jnp.concatenate([v_ref[rows, :], ones_tile], axis=1)
            gcol = g_ref[chunk]
            grow = gt_ref[chunk]
            ig_row = grow[2 * d:2 * d + 1, :]
            lf_col = _log_sigmoid(gcol[:, 2 * d + 1:2 * d + 2])
            lf_row = _log_sigmoid(grow[2 * d + 1:2 * d + 2, :])
            b_col = jnp.broadcast_to(jnp.sum(keep_f * lf_row, axis=1, keepdims=True), (L, LANES))
            b_row = jnp.sum(keep_t_f * lf_col, axis=0, keepdims=True)
            dmat = jnp.where(keep, b_col - b_row + ig_row, NEG_INF)
            m_loc = jnp.broadcast_to(jnp.max(dmat, axis=1, keepdims=True), (L, LANES))
            p_loc = jnp.exp(dmat - m_loc)
            s_loc = jnp.dot(qc, ktc, preferred_element_type=F32) * p_loc
            intra = jnp.dot(s_loc.astype(BF16), v_ext, preferred_element_type=F32)

            m_t = jnp.maximum(b_col + m, m_loc)
            w_inter = jnp.exp(b_col + m - m_t)
            w_intra = jnp.exp(m_loc - m_t)
            inter = jnp.dot(qc, c_scr[s].astype(BF16), preferred_element_type=F32)
            hx = lanes(w_inter, ext_tiles) * inter + lanes(w_intra, ext_tiles) * intra
            inv = 1.0 / jnp.maximum(jnp.abs(hx[:, ML_DV:]), jnp.exp(-m_t))
            h = hx[:, :ML_DV] * lanes(inv, ML_DV // LANES)

            b_end = b_col[end:end + 1, :]
            m_loc_end = m_loc[end:end + 1, :]
            m_new = jnp.maximum(b_end + m, m_loc_end)
            w_c = jnp.exp(b_end + m - m_new)
            w_s = p_loc[end:end + 1, :] * jnp.exp(m_loc_end - m_new)
            upd = jnp.dot((ktc.astype(F32) * w_s).astype(BF16), v_ext, preferred_element_type=F32)
            c_scr[s] = lanes(w_c, ext_tiles) * c_scr[s] + upd
            ms[s] = m_new

            if d == 0:
                hs_scr[rows, :] = h
            else:
                hs = hs_scr[rows, :] + h
                mu = jnp.mean(hs, axis=1, keepdims=True)
                cen = hs - mu
                var = jnp.mean(cen * cen, axis=1, keepdims=True)
                hn = cen * lax.rsqrt(var + NORM_EPS) * nw_ref[...]
                xc = xc_ref[rows, :].astype(F32)
                z = z_ref[rows, :].astype(F32)
                o_ref[rows, :] = ((hn + skip_ref[...] * xc) * _silu(z)).astype(o_ref.dtype)

        if out_state:
            for s in range(n_seq):
                if fill_layer is None:
                    cout_ref[s, d] = c_scr[s, :, :ML_DV]
                else:
                    for layer in range(cout_ref.shape[1]):
                        cout_ref[s, layer, d] = (c_scr[s, :, :ML_DV] if layer == fill_layer
                                                 else jnp.zeros((ML_DK, ML_DV), F32))
                nout_ref[s, d:d + 1, :] = c_scr[s, :, ML_DV:].T[0:1, :]
                mout_ref[s, d:d + 1, :] = ms[s]


def _ml_scan(q, kt, v, g4, g4t, xc, z, norm_w, skip, *, seq, nb, row0, init=None, final_c=None,
             n_seq=ML_SEQ_PER_STEP):
    L = ML_CHUNK
    rows = n_seq * seq
    nc = rows // L
    rb = row0 // rows
    has_init = init is not None
    in_specs = [
        pl.BlockSpec((rows, ML_DK), lambda b, h: (rb + b, h)),
        pl.BlockSpec((None, nc, ML_DK, L), lambda b, h: (h, rb + b, 0, 0)),
        pl.BlockSpec((rows, ML_DV), lambda b, h: (rb + b, h)),
        pl.BlockSpec((None, nc, L, 4), lambda b, h: (h, rb + b, 0, 0)),
        pl.BlockSpec((None, nc, 4, L), lambda b, h: (h, rb + b, 0, 0)),
        pl.BlockSpec((rows, ML_DV), lambda b, h: (rb + b, h)),
        pl.BlockSpec((rows, ML_DV), lambda b, h: (rb + b, h)),
        pl.BlockSpec((1, ML_DV), lambda b, h: (0, h)),
        pl.BlockSpec((1, ML_DV), lambda b, h: (0, h)),
    ]
    args = [q, kt, v, g4, g4t, xc, z, norm_w.reshape(1, ML_D_IN), skip.reshape(1, ML_D_IN)]
    state_n = pl.BlockSpec((n_seq, None, 2, ML_DK, LANES), lambda b, h: (b, h, 0, 0, 0))
    state_m = pl.BlockSpec((n_seq, None, 2, LANES), lambda b, h: (b, h, 0, 0))
    if has_init:
        c0, layer, n0, m0 = init
        in_specs += [pl.BlockSpec((n_seq, None, 2, None, ML_DK, ML_DV), lambda b, h: (b, layer, 0, h, 0, 0)),
                     state_n, state_m]
        args += [c0, n0, m0]
    out_specs = [pl.BlockSpec((rows, ML_DV), lambda b, h: (b, h))]
    out_shape = [jax.ShapeDtypeStruct((nb * seq, ML_D_IN), BF16)]
    aliases = {}
    if not has_init:
        layer_out, c_all = final_c
        if c_all is None:
            c_spec = pl.BlockSpec((n_seq, N_ML_LAYERS, 2, None, ML_DK, ML_DV), lambda b, h: (b, 0, 0, h, 0, 0))
        else:
            c_spec = pl.BlockSpec((n_seq, None, 2, None, ML_DK, ML_DV), lambda b, h: (b, layer_out, 0, h, 0, 0))
        out_specs += [c_spec, pl.BlockSpec((n_seq, None, 2, ML_DK), lambda b, h: (b, h, 0, 0)), state_m]
        out_shape += [
            jax.ShapeDtypeStruct((nb, N_ML_LAYERS, 2, ML_HEADS, ML_DK, ML_DV), F32),
            jax.ShapeDtypeStruct((nb, ML_HEADS, 2, ML_DK), F32),
            jax.ShapeDtypeStruct((nb, ML_HEADS, 2, LANES), F32),
        ]
        if c_all is not None:
            in_specs.append(pl.BlockSpec(memory_space=pl.ANY))
            args.append(c_all)
            aliases = {len(args) - 1: 1}
    return pl.pallas_call(
        functools.partial(_ml_scan_kernel, seq=seq, n_seq=n_seq, has_init=has_init, out_state=not has_init,
                          n_unused_inputs=len(aliases),
                          fill_layer=final_c[0] if (not has_init and final_c[1] is None) else None),
        grid=(nb // n_seq, ML_HEADS),
        in_specs=in_specs,
        out_specs=out_specs,
        out_shape=out_shape,
        scratch_shapes=[pltpu.VMEM((rows, ML_DV), F32), pltpu.VMEM((n_seq, ML_DK, ML_EXT), F32)],
        input_output_aliases=aliases,
        compiler_params=_params("arbitrary", "arbitrary"),
        name="ml_scan_init" if has_init else "ml_scan_zero",
    )(*args)


def _ml_qk_kernel(xc_ref, w_ref, q_ref, kt_ref):
    acc = jnp.dot(xc_ref[...], w_ref[...], preferred_element_type=F32)
    j = pl.program_id(1)

    @pl.when(j == 0)
    def _():
        q_ref[...] = (acc * ML_DK ** -0.5).astype(BF16)

    @pl.when(j == 1)
    def _():
        for h in range(ML_HEADS):
            kt = acc[:, h * ML_DK:(h + 1) * ML_DK].T
            for c in range(TM // ML_CHUNK):
                kt_ref[h, c] = kt[:, c * ML_CHUNK:(c + 1) * ML_CHUNK].astype(BF16)


def _ml_qk(xc, w_qk):
    n = ML_HEADS * ML_DK
    cpt = TM // ML_CHUNK
    w_spec, w_qk = _weight(w_qk, (ML_D_IN, n), lambda i, j: (0, j))
    return pl.pallas_call(
        _ml_qk_kernel,
        grid=(N_TOK // TM, 2),
        in_specs=[pl.BlockSpec((TM, ML_D_IN), lambda i, j: (i, 0)), w_spec],
        out_specs=[pl.BlockSpec((TM, n), lambda i, j: (i, 0)),
                   pl.BlockSpec((ML_HEADS, cpt, ML_DK, ML_CHUNK), lambda i, j: (0, i, 0, 0))],
        out_shape=[jax.ShapeDtypeStruct((N_TOK, n), BF16),
                   jax.ShapeDtypeStruct((ML_HEADS, N_TOK // ML_CHUNK, ML_DK, ML_CHUNK), BF16)],
        compiler_params=_params("arbitrary", "arbitrary"),
        name="ml_qk",
    )(xc, w_qk)


def _mlstm_layer(x, mod, nw, p, state, final_c):
    xm, xc, z = _ml_up(x, mod, nw, p["w_up"], p["conv_w"], p["conv_b"])
    q, kt = _ml_qk(xc, p["w_qk"])
    v = _mm(xm, p["w_v"], jnp.zeros((1, ML_D_IN), F32), 1024, BF16)
    g = _mm(xm, p["w_gate"], p["b_gate"], 128, F32)[:, :4 * ML_HEADS]
    L = ML_CHUNK
    g4 = g.reshape(N_TOK // L, L, 4, ML_HEADS).transpose(3, 0, 1, 2)
    g4t = g4.transpose(0, 1, 3, 2)
    oc, c_fin, n_fin, m_fin = _ml_scan(q, kt, v, g4, g4t, xc, z, p["norm_w"], p["skip"],
                                       seq=SEQ, nb=BATCH, row0=0, final_c=final_c)
    (ol,) = _ml_scan(q, kt, v, g4, g4t, xc, z, p["norm_w"], p["skip"],
                     seq=DEC_SEQ, nb=DEC_BATCH, row0=N_CTX, init=state, n_seq=1)
    x = _mm_res(oc, ol, p["w_down"], x, mod)
    return x, c_fin, n_fin.transpose(0, 2, 1, 3), m_fin[..., 0].transpose(0, 2, 1)


def _softmax_parts(scores, sink_col):
    m = functools.reduce(jnp.maximum, [jnp.max(s, axis=1, keepdims=True) for s in scores])
    if sink_col is not None:
        m = jnp.maximum(m, sink_col)
    ps = [jnp.exp(s - m) for s in scores]
    den = functools.reduce(jnp.add, [jnp.sum(p, axis=1, keepdims=True) for p in ps])
    if sink_col is not None:
        den = den + jnp.exp(sink_col - m)
    return ps, den


def _qk(q, k):
    return lax.dot_general(q, k, (((1,), (1,)), ((), ())), preferred_element_type=F32) * (HEAD_DIM ** -0.5)


def _attend_tiles(q_tiles, parts, sinks):
    return _attend_groups([(q_tiles, parts, sinks)])[0]


def _attend_groups(groups):
    r = groups[0][0][0].shape[0]
    lo = lax.broadcasted_iota(jnp.int32, (r, LANES), 1) < HEAD_DIM
    zero = jnp.zeros((r, LANES), BF16)
    all_scores = []
    for q_tiles, parts, _ in groups:
        qs = jnp.concatenate([jnp.where(sel, t, zero) for t in q_tiles for sel in (lo, jnp.logical_not(lo))], axis=0)
        all_scores.append([post(_qk(qs, k2)) for k2, _, post in parts])
    outs = []
    for scores, (q_tiles, parts, sinks) in zip(all_scores, groups):
        n_rows = scores[0].shape[0]
        sink_col = None
        if sinks is not None:
            rows = lax.broadcasted_iota(jnp.int32, (n_rows, 1), 0)
            sink_col = jnp.full((n_rows, 1), sinks[0], F32)
            for hi in range(1, len(sinks)):
                sink_col = jnp.where(rows >= hi * r, sinks[hi], sink_col)
        ps, den = _softmax_parts(scores, sink_col)
        o = functools.reduce(jnp.add, [jnp.dot(p.astype(BF16), v2, preferred_element_type=F32)
                                       for p, (_, v2, _) in zip(ps, parts)]) / den
        outs.append([jnp.where(lo, o[2 * a * r:(2 * a + 1) * r], o[(2 * a + 1) * r:(2 * a + 2) * r])
                     for a in range(len(q_tiles))])
    return outs


def _identity(s):
    return s


def _ctx_attn_kernel(*refs, tiles_per_kv, has_sink, kv_head_stride):
    if has_sink:
        sink_ref, q_ref, k_ref, v_ref, o_ref, kout_ref, vout_ref = refs
    else:
        q_ref, k_ref, v_ref, o_ref, kout_ref, vout_ref = refs
    for h in range(kout_ref.shape[0]):
        kout_ref[h] = k_ref[:, h * kv_head_stride:h * kv_head_stride + HEAD_DIM]
        vout_ref[h] = v_ref[:, h * kv_head_stride:h * kv_head_stride + HEAD_DIM]
    groups = []
    for t in range(k_ref.shape[1] // LANES):
        k2 = k_ref[:, t * LANES:(t + 1) * LANES].astype(BF16)
        v2 = v_ref[:, t * LANES:(t + 1) * LANES].astype(BF16)
        first = t * tiles_per_kv
        q_tiles = [q_ref[:, (first + a) * LANES:(first + a + 1) * LANES] for a in range(tiles_per_kv)]
        sinks = [sink_ref[2 * first + hi] for hi in range(2 * tiles_per_kv)] if has_sink else None
        groups.append((q_tiles, [(k2, v2, _identity)], sinks))
    for t, outs in enumerate(_attend_groups(groups)):
        for a, o in enumerate(outs):
            tile = t * tiles_per_kv + a
            o_ref[:, tile * LANES:(tile + 1) * LANES] = o.astype(o_ref.dtype)


def _ctx_attn(q, kv, n_kv_cols, n_kv_heads, sink):
    nq = q.shape[1]
    has_sink = sink is not None
    cache_spec = pl.BlockSpec((None, None, n_kv_heads, SEQ, HEAD_DIM), lambda b: (b, 0, 0, 0, 0))
    cache_shape = jax.ShapeDtypeStruct((BATCH, 1, n_kv_heads, SEQ, HEAD_DIM), F32)
    in_specs = [pl.BlockSpec((SEQ, nq), lambda b: (b, 0)),
                pl.BlockSpec((SEQ, n_kv_cols), lambda b: (b, 0)),
                pl.BlockSpec((SEQ, n_kv_cols), lambda b: (b, 1))]
    args = [q, kv, kv]
    if has_sink:
        in_specs = [pl.BlockSpec(memory_space=pltpu.SMEM)] + in_specs
        args = [sink] + args
    return pl.pallas_call(
        functools.partial(_ctx_attn_kernel, tiles_per_kv=nq // n_kv_cols, has_sink=has_sink,
                          kv_head_stride=n_kv_cols // n_kv_heads),
        grid=(BATCH,),
        in_specs=in_specs,
        out_specs=[pl.BlockSpec((SEQ, nq), lambda b: (b, 0)), cache_spec, cache_spec],
        out_shape=[jax.ShapeDtypeStruct((N_CTX, nq), BF16), cache_shape, cache_shape],
        compiler_params=_params("arbitrary"),
        name="ctx_attn",
    )(*args)


SWA_SPAN = Q_BLOCK + 2 * SWA_WINDOW
SWA_KV_COLS = SWA_KV * LANES


def _swa_lat_kernel(sink_ref, q_ref, k_ref, v_ref, kc_ref, vc_ref, o_ref):
    j = pl.program_id(1)
    start = pl.multiple_of(jnp.clip((j - 1) * Q_BLOCK, 0, DEC_SEQ - SWA_SPAN), Q_BLOCK)
    rows = 4 * Q_BLOCK
    qpos = j * Q_BLOCK + (lax.broadcasted_iota(jnp.int32, (rows, SWA_SPAN), 0) & (Q_BLOCK - 1))
    kpos = start + lax.broadcasted_iota(jnp.int32, (rows, SWA_SPAN), 1)
    in_window = jnp.abs(qpos - kpos) <= SWA_WINDOW

    def window(s):
        return jnp.where(in_window, s, NEG_INF)

    groups = []
    for t in range(SWA_KV):
        cols = slice(t * LANES, (t + 1) * LANES)
        k_loc = k_ref[pl.ds(start, SWA_SPAN), cols].astype(BF16)
        v_loc = v_ref[pl.ds(start, SWA_SPAN), cols].astype(BF16)
        q_tiles = [q_ref[:, (2 * t + a) * LANES:(2 * t + a + 1) * LANES] for a in range(2)]
        sinks = [sink_ref[4 * t + hi] for hi in range(4)]
        groups.append((q_tiles, [(k_loc, v_loc, window), (kc_ref[:, cols], vc_ref[:, cols], _identity)], sinks))
    for t, outs in enumerate(_attend_groups(groups)):
        for a, o in enumerate(outs):
            o_ref[:, (2 * t + a) * LANES:(2 * t + a + 1) * LANES] = o.astype(o_ref.dtype)


def _swa_latent(q, kv, kc, vc, sink):
    nq = q.shape[1]
    rb = N_CTX // DEC_SEQ
    qb = N_CTX // Q_BLOCK
    nj = DEC_SEQ // Q_BLOCK
    cspec = pl.BlockSpec((None, SEQ, SWA_KV_COLS), lambda b, j: (b, 0, 0))
    return pl.pallas_call(
        _swa_lat_kernel,
        grid=(DEC_BATCH, nj),
        in_specs=[pl.BlockSpec(memory_space=pltpu.SMEM),
                  pl.BlockSpec((Q_BLOCK, nq), lambda b, j: (qb + b * nj + j, 0)),
                  pl.BlockSpec((DEC_SEQ, SWA_KV_COLS), lambda b, j: (rb + b, 0)),
                  pl.BlockSpec((DEC_SEQ, SWA_KV_COLS), lambda b, j: (rb + b, 1)),
                  cspec, cspec],
        out_specs=pl.BlockSpec((Q_BLOCK, nq), lambda b, j: (b * nj + j, 0)),
        out_shape=jax.ShapeDtypeStruct((N_LAT, nq), BF16),
        compiler_params=_params("arbitrary", "arbitrary"),
        name="swa_latent",
    )(sink, q, kv, kv, kc, vc)


NA_QT = 256
NA_SPAN = 768
NA_TILES = 4


def _na_start(j):
    return (j // 2) * (DEC_SEQ - NA_SPAN)


NA_ROWS = DEC_SEQ // GRID_W
NA_DR = 2 * NA_KH - 1
NA_DC = 2 * NA_KW - 1


def _na_blocks_kernel(rpb_ref, onehot_ref, valid_ref, o_ref):
    t = jnp.dot(rpb_ref[...], onehot_ref[...], preferred_element_type=F32, precision=lax.Precision.HIGHEST)
    o_ref[...] = jnp.where(valid_ref[...] > 0.5, t, NEG_INF)


def _na_bias_blocks(rpb):
    h = rpb.shape[0]
    kpad = 32
    cq, ck = np.meshgrid(np.arange(GRID_W), np.arange(GRID_W), indexing="ij")
    dc = (np.clip(ck - cq, -(NA_KW - 1), NA_KW - 1) + NA_KW - 1).reshape(-1)
    cs = np.clip(cq - NA_KW // 2, 0, GRID_W - NA_KW)
    valid = ((ck >= cs) & (ck < cs + NA_KW)).reshape(1, -1).astype(np.float32)
    onehot = (np.arange(kpad)[:, None] == dc[None, :]).astype(np.float32)
    rpb2 = jnp.pad(rpb.reshape(h * NA_DR, NA_DC), ((0, 0), (0, kpad - NA_DC)))
    n = GRID_W * GRID_W
    blocks = pl.pallas_call(
        _na_blocks_kernel,
        grid=(1,),
        in_specs=[pl.BlockSpec((h * NA_DR, kpad), lambda i: (0, 0)),
                  pl.BlockSpec((kpad, n), lambda i: (0, 0)),
                  pl.BlockSpec((1, n), lambda i: (0, 0))],
        out_specs=pl.BlockSpec((h * NA_DR, n), lambda i: (0, 0)),
        out_shape=jax.ShapeDtypeStruct((h * NA_DR, n), F32),
        compiler_params=_params("arbitrary"),
        name="na_bias_blocks",
    )(rpb2, jnp.asarray(onehot), jnp.asarray(valid))
    blocks = blocks.reshape(h, NA_DR, GRID_W, GRID_W)
    padded = jnp.pad(blocks, ((0, 0), (1, 1), (0, 0), (0, 0)), constant_values=NEG_INF)
    return jnp.concatenate([padded[:, :-1], padded[:, 1:]], axis=-1)


def _na_lat_kernel(q_ref, k_ref, v_ref, kc_ref, vc_ref, blk_ref, o_ref, bias_scr):
    j = pl.program_id(0)
    start = pl.multiple_of(_na_start(j), 256)

    @pl.when(pl.program_id(2) == 0)
    def _():
        lane_lo = lax.broadcasted_iota(jnp.int32, (GRID_W, LANES), 1) < GRID_W
        for rq_l in range(NA_QT // GRID_W):
            rq = j * (NA_QT // GRID_W) + rq_l
            rs = jnp.clip(rq - NA_KH // 2, 0, NA_ROWS - NA_KH)
            for kp in range(NA_SPAN // LANES):
                rk = start // GRID_W + 2 * kp
                idx = jnp.clip(rk - rq + NA_KH, 0, NA_DR)
                in_band = [jnp.logical_and(r >= rs, r < rs + NA_KH).astype(jnp.int32) for r in (rk, rk + 1)]
                ok = jnp.where(lane_lo, in_band[0], in_band[1]) > 0
                for hh in range(2 * NA_TILES):
                    bias_scr[hh * NA_QT + rq_l * GRID_W:hh * NA_QT + (rq_l + 1) * GRID_W,
                             kp * LANES:(kp + 1) * LANES] = jnp.where(ok, blk_ref[hh, idx], NEG_INF)

    groups = []
    for t in range(NA_TILES):
        cols = slice(t * LANES, (t + 1) * LANES)

        def add_bias(s, t=t):
            return s + bias_scr[2 * t * NA_QT:2 * (t + 1) * NA_QT, :]

        k_loc = k_ref[pl.ds(start, NA_SPAN), cols].astype(BF16)
        v_loc = v_ref[pl.ds(start, NA_SPAN), cols].astype(BF16)
        groups.append(([q_ref[:, cols]], [(k_loc, v_loc, add_bias), (kc_ref[:, cols], vc_ref[:, cols], _identity)],
                       None))
    for t, (o,) in enumerate(_attend_groups(groups)):
        o_ref[:, t * LANES:(t + 1) * LANES] = o.astype(o_ref.dtype)


def _na_latent(q, kv, kc, vc, blocks):
    nq = q.shape[1]
    width = NA_TILES * LANES
    n_steps = nq // width
    nj = DEC_SEQ // NA_QT
    rb = N_CTX // DEC_SEQ
    qb = N_CTX // NA_QT
    cspec = pl.BlockSpec((None, SEQ, width), lambda j, p, b: (b, 0, p))
    return pl.pallas_call(
        _na_lat_kernel,
        grid=(nj, n_steps, DEC_BATCH),
        in_specs=[pl.BlockSpec((NA_QT, width), lambda j, p, b: (qb + b * nj + j, p)),
                  pl.BlockSpec((DEC_SEQ, width), lambda j, p, b: (rb + b, p)),
                  pl.BlockSpec((DEC_SEQ, width), lambda j, p, b: (rb + b, n_steps + p)),
                  cspec, cspec,
                  pl.BlockSpec((2 * NA_TILES, NA_DR + 1, GRID_W, LANES), lambda j, p, b: (p, 0, 0, 0))],
        out_specs=pl.BlockSpec((NA_QT, width), lambda j, p, b: (b * nj + j, p)),
        out_shape=jax.ShapeDtypeStruct((N_LAT, nq), BF16),
        scratch_shapes=[pltpu.VMEM((2 * NA_TILES * NA_QT, NA_SPAN), F32)],
        compiler_params=_params("arbitrary", "arbitrary", "arbitrary"),
        name="na_latent",
    )(q, kv, kv, kc, vc, blocks)


def _rope_tables(width):
    quarter = HEAD_DIM // 4
    pos = np.arange(DEC_SEQ)
    inv = np.power(ROPE_BASE, -np.arange(quarter, dtype=np.float32) / quarter).astype(np.float32)
    d = np.arange(width) % HEAD_DIM
    p = np.where((d < HEAD_DIM // 2)[None, :], (pos // GRID_W)[:, None], (pos % GRID_W)[:, None]).astype(np.float32)
    ang = p * inv[d % quarter][None, :]
    sign = np.where((d // quarter) % 2 == 0, -1.0, 1.0)[None, :]
    return jnp.asarray(np.cos(ang), F32), jnp.asarray(np.sin(ang) * sign, F32)


def _cache_rows(cache, dup):
    b, h, s, hd = cache.shape
    rows = jnp.broadcast_to(cache.transpose(0, 2, 1, 3)[:, :, :, None, :], (b, s, h, dup, hd))
    return rows.reshape(b, s, h * dup * hd).astype(BF16)


def _swa_layer(x, mod, nw, w_qkv, sink, w_o, cache_k, cache_v):
    nq, nk = SWA_HEADS * HEAD_DIM, SWA_KV * HEAD_DIM

    def dup_heads(w):
        return jnp.broadcast_to(w.reshape(D_MODEL, SWA_KV, 1, HEAD_DIM),
                                (D_MODEL, SWA_KV, 2, HEAD_DIM)).reshape(D_MODEL, SWA_KV_COLS)

    w = jnp.concatenate([w_qkv[:, :nq], dup_heads(w_qkv[:, nq:nq + nk]), dup_heads(w_qkv[:, nq + nk:])], axis=1)
    q, kv = _qkv_proj(x, mod, nw, w.astype(BF16), nq, rope_cols=nq + SWA_KV_COLS)
    oc, k_new, v_new = _ctx_attn(q, kv, SWA_KV_COLS, SWA_KV, sink)
    ol = _swa_latent(q, kv, _cache_rows(cache_k, 2), _cache_rows(cache_v, 2), sink)
    return _mm_res(oc, ol, w_o, x, mod), k_new, v_new


def _na_layer(x, mod, nw, w_qkv, rpb, w_o, cache_k, cache_v):
    n = NA_HEADS * HEAD_DIM
    q, kv = _qkv_proj(x, mod, nw, w_qkv, n)
    oc, k_new, v_new = _ctx_attn(q, kv, n, NA_HEADS, None)
    ol = _na_latent(q, kv, _cache_rows(cache_k, 1), _cache_rows(cache_v, 1), _na_bias_blocks(rpb))
    return _mm_res(oc, ol, w_o, x, mod), k_new, v_new


def _final_norm_kernel(x_ref, w_ref, oc_ref, ol_ref):
    x = x_ref[...]
    y = x * lax.rsqrt(jnp.mean(x * x, axis=-1, keepdims=True) + NORM_EPS) * w_ref[...]
    is_ctx = pl.program_id(0) < N_CTX_TILES

    @pl.when(is_ctx)
    def _():
        oc_ref[...] = y

    @pl.when(jnp.logical_not(is_ctx))
    def _():
        ol_ref[...] = y


def _final_norm(x, w):
    return pl.pallas_call(
        _final_norm_kernel,
        grid=(N_TOK // TM,),
        in_specs=[pl.BlockSpec((TM, D_MODEL), lambda i: (i, 0)), pl.BlockSpec((1, D_MODEL), lambda i: (0, 0))],
        out_specs=[pl.BlockSpec((TM, D_MODEL), lambda i: (jnp.minimum(i, N_CTX_TILES - 1), 0)),
                   pl.BlockSpec((TM, D_MODEL), lambda i: (jnp.maximum(i - N_CTX_TILES, 0), 0))],
        out_shape=[jax.ShapeDtypeStruct((N_CTX, D_MODEL), F32), jax.ShapeDtypeStruct((N_LAT, D_MODEL), F32)],
        compiler_params=_params("arbitrary"),
        name="final_norm",
    )(x, w.reshape(1, D_MODEL))


def kernel(x_prompt, x_sample, state_mlstm_C, state_mlstm_n, state_mlstm_m, cache_swa_k, cache_swa_v, cache_na_k, cache_na_v, c, c_ctx, ada_w, ada_b, norm_w, final_norm_w, ffn_w_up, ffn_conv_w, ffn_conv_b, ffn_w_down, ml_w_up, ml_conv_w, ml_conv_b, ml_w_qk, ml_w_v, ml_w_gate, ml_b_gate, ml_norm_w, ml_skip, ml_w_down, swa_w_qkv, swa_sink, swa_w_o, na_w_qkv, na_rpb, na_w_o):
    x = (x_prompt.reshape(N_CTX, D_MODEL), x_sample.reshape(N_LAT, D_MODEL))
    cond = jnp.concatenate([c_ctx[None], c, jnp.zeros((MOD_ROWS - 1 - DEC_BATCH, D_MODEL), F32)], axis=0)
    mods = _ada_mod(cond, ada_w, ada_b)
    ffn_weights = _ffn_weight_blocks(ffn_w_up, ffn_conv_w, ffn_conv_b, ffn_w_down)
    ml_up_blocks = _ml_up_weight_blocks(ml_w_up)
    ml_qk_bf16, ml_v_bf16, ml_down_bf16 = ml_w_qk.astype(BF16), ml_w_v.astype(BF16), ml_w_down.astype(BF16)

    new_c, new_n, new_m = None, [], []
    new_sk = new_sv = new_nk = new_nv = None
    for i in range(DEPTH):
        kind, j = i % N_MIXERS, i // N_MIXERS
        mod = mods[i]
        if kind == 0:
            gate_w = jnp.pad(ml_w_gate[j], ((0, 0), (0, 128 - 4 * ML_HEADS))).astype(BF16)
            gate_b = jnp.pad(ml_b_gate[j], (0, 128 - 4 * ML_HEADS)).reshape(1, 128)
            p = dict(w_up=(ml_up_blocks, j), conv_w=ml_conv_w[j], conv_b=ml_conv_b[j],
                     w_qk=(ml_qk_bf16, j), w_v=(ml_v_bf16, j), w_gate=gate_w, b_gate=gate_b,
                     norm_w=ml_norm_w[j], skip=ml_skip[j], w_down=(ml_down_bf16, j))
            n0 = jnp.broadcast_to(state_mlstm_n[:, j].transpose(0, 2, 1, 3)[..., None],
                                  (DEC_BATCH, ML_HEADS, 2, ML_DK, LANES))
            m0 = jnp.broadcast_to(state_mlstm_m[:, j].transpose(0, 2, 1)[..., None], (DEC_BATCH, ML_HEADS, 2, LANES))
            x, new_c, nf, mf = _mlstm_layer(x, mod, norm_w[i, 0], p, (state_mlstm_C, j, n0, m0), (j, new_c))
            new_n.append(nf)
            new_m.append(mf)
        elif kind == 1:
            x, k_new, v_new = _swa_layer(x, mod, norm_w[i, 0], swa_w_qkv[j].astype(BF16), swa_sink[j],
                                         swa_w_o[j].astype(BF16), cache_swa_k[:, j], cache_swa_v[:, j])
            new_sk, new_sv = k_new, v_new
        else:
            x, k_new, v_new = _na_layer(x, mod, norm_w[i, 0], na_w_qkv[j].astype(BF16), na_rpb[j],
                                        na_w_o[j].astype(BF16), cache_na_k[:, j], cache_na_v[:, j])
            new_nk, new_nv = k_new, v_new
        x = _conv_ffn(x, mod, norm_w[i, 1], i, *ffn_weights)

    y_ctx, y_lat = _final_norm(x, final_norm_w)
    return (y_ctx.reshape(BATCH, SEQ, D_MODEL), y_lat.reshape(DEC_BATCH, DEC_SEQ, D_MODEL),
            new_c, jnp.stack(new_n, axis=1), jnp.stack(new_m, axis=1),
            new_sk, new_sv, new_nk, new_nv)
```

```python
import functools

import jax
import jax.numpy as jnp
import numpy as np
from jax import lax
from jax.experimental import pallas as pl
from jax.experimental.pallas import tpu as pltpu

F32 = jnp.float32
BF16 = jnp.bfloat16

D_MODEL = 1024
BATCH = 32
SEQ = 256
DEPTH = 4
DEC_BATCH = 8
DEC_SEQ = 1024
GRID_W = 64
N_MIXERS = 3
N_ML_LAYERS = (DEPTH + 2) // 3
NORM_EPS = 1e-6
D_FF = 2816
ML_D_IN = 2 * D_MODEL
ML_HEADS = 4
ML_DK = ML_D_IN // (2 * ML_HEADS)
ML_DV = ML_D_IN // ML_HEADS
ML_CHUNK = 128
HEAD_DIM = 64
SWA_HEADS = D_MODEL // HEAD_DIM
SWA_KV = SWA_HEADS // 4
SWA_WINDOW = 128
Q_BLOCK = 128
ROPE_BASE = 10000.0
NA_HEADS = D_MODEL // HEAD_DIM
NA_KH = 8
NA_KW = 16
NEG_INF = -1e30

N_CTX = BATCH * SEQ
N_LAT = DEC_BATCH * DEC_SEQ
N_TOK = N_CTX + N_LAT
TM = 1024
N_CTX_TILES = N_CTX // TM
MOD_ROWS = 16
VMEM_LIMIT_BYTES = 56 * 1024 * 1024


def _params(*sem, vmem_limit_bytes=VMEM_LIMIT_BYTES):
    return pltpu.CompilerParams(dimension_semantics=sem, vmem_limit_bytes=vmem_limit_bytes)


def _mod_row(i):
    return jnp.where(i < N_CTX_TILES, 0, i - (N_CTX_TILES - 1))


def _silu(x):
    return x / (1.0 + jnp.exp(-x))


def _norm_mod(x, nw, shift, scale):
    y = x * lax.rsqrt(jnp.mean(x * x, axis=-1, keepdims=True) + NORM_EPS) * nw
    return y * (1.0 + scale) + shift


SUBLANES = 8
LANES = 128


def _dwconv_rows(u, cw, cb, seq):
    r, c = u.shape
    n_groups, per_seq = r // SUBLANES, seq // SUBLANES
    g = u.reshape(n_groups, SUBLANES, c)
    sub = lax.broadcasted_iota(jnp.int32, g.shape, 1)
    down = pltpu.roll(g, 1, 1)
    up = pltpu.roll(g, SUBLANES - 1, 1)
    zero = jnp.zeros((1, SUBLANES, c), F32)
    from_prev, from_next = [], []
    for s in range(0, n_groups, per_seq):
        from_prev += [zero, down[s:s + per_seq - 1]]
        from_next += [up[s + 1:s + per_seq], zero]
    prev = jnp.where(sub == 0, jnp.concatenate(from_prev, axis=0), down)
    nxt = jnp.where(sub == SUBLANES - 1, jnp.concatenate(from_next, axis=0), up)
    out = cw[0:1, :] * prev + cw[1:2, :] * g + cw[2:3, :] * nxt + cb
    return out.reshape(r, c)


def _by_tile_kind(tile, body):
    pl.when(tile < N_CTX_TILES)(functools.partial(body, SEQ))
    pl.when(tile >= N_CTX_TILES)(functools.partial(body, DEC_SEQ))


def _ada_kernel(c_ref, w_ref, b_ref, o_ref):
    s = _silu(c_ref[...]).astype(BF16)
    o_ref[...] = jnp.dot(s, w_ref[...].astype(BF16), preferred_element_type=F32) + b_ref[...]


def _ada_mod(cond, ada_w, ada_b):
    tn = 3072
    n = 6 * D_MODEL
    out = pl.pallas_call(
        _ada_kernel,
        grid=(DEPTH, n // tn),
        in_specs=[
            pl.BlockSpec((MOD_ROWS, D_MODEL), lambda l, j: (0, 0)),
            pl.BlockSpec((None, D_MODEL, tn), lambda l, j: (l, 0, j)),
            pl.BlockSpec((None, 1, tn), lambda l, j: (l, 0, j)),
        ],
        out_specs=pl.BlockSpec((None, MOD_ROWS, tn), lambda l, j: (l, 0, j)),
        out_shape=jax.ShapeDtypeStruct((DEPTH, MOD_ROWS, n), F32),
        compiler_params=_params("arbitrary", "arbitrary"),
        name="ada_mod",
    )(cond, ada_w, ada_b.reshape(DEPTH, 1, n))
    return out.reshape(DEPTH, MOD_ROWS, 6, D_MODEL)


QKV_TN = 512


def _rotate_pairs(a, cos, sin):
    lane = lax.broadcasted_iota(jnp.int32, a.shape, 1)
    first = (lane & (HEAD_DIM // 4)) == 0
    n = a.shape[1]
    partner = jnp.where(first, pltpu.roll(a, n - HEAD_DIM // 4, 1), pltpu.roll(a, HEAD_DIM // 4, 1))
    return a * cos + partner * sin


def _qkv_kernel(*refs, q_blocks, rope_blocks):
    if rope_blocks:
        x_ref, mod_ref, nw_ref, w_ref, cos_ref, sin_ref, q_ref, kv_ref, h_scr = refs
    else:
        x_ref, mod_ref, nw_ref, w_ref, q_ref, kv_ref, h_scr = refs
    i = pl.program_id(0)
    j = pl.program_id(1)

    @pl.when(j == 0)
    def _():
        h = _norm_mod(x_ref[...], nw_ref[...], mod_ref[0:1, :], mod_ref[1:2, :])
        h_scr[...] = h.astype(BF16)

    acc = jnp.dot(h_scr[...], w_ref[...], preferred_element_type=F32)

    def emit(val):
        @pl.when(j < q_blocks)
        def _():
            q_ref[...] = val.astype(BF16)

        @pl.when(j >= q_blocks)
        def _():
            kv_ref[...] = val

    if rope_blocks:
        rotate = jnp.logical_and(i >= N_CTX_TILES, j < rope_blocks)

        @pl.when(rotate)
        def _():
            emit(_rotate_pairs(acc, cos_ref[...], sin_ref[...]))

        @pl.when(jnp.logical_not(rotate))
        def _():
            emit(acc)
    else:
        emit(acc)


def _qkv_proj(x, mod, nw, w, n_q, rope_cols=0):
    n = w.shape[1]
    q_blocks = n_q // QKV_TN
    rope_blocks = rope_cols // QKV_TN
    in_specs = [
        pl.BlockSpec((TM, D_MODEL), lambda i, j: (i, 0)),
        pl.BlockSpec((None, 6, D_MODEL), lambda i, j: (_mod_row(i), 0, 0)),
        pl.BlockSpec((1, D_MODEL), lambda i, j: (0, 0)),
        pl.BlockSpec((D_MODEL, QKV_TN), lambda i, j: (0, j)),
    ]
    args = [x, mod, nw.reshape(1, D_MODEL), w]
    if rope_blocks:
        tab = pl.BlockSpec((DEC_SEQ, QKV_TN), lambda i, j: (0, 0))
        in_specs += [tab, tab]
        args += list(_rope_tables(QKV_TN))
    return pl.pallas_call(
        functools.partial(_qkv_kernel, q_blocks=q_blocks, rope_blocks=rope_blocks),
        grid=(N_TOK // TM, n // QKV_TN),
        in_specs=in_specs,
        out_specs=[pl.BlockSpec((TM, QKV_TN), lambda i, j: (i, jnp.minimum(j, q_blocks - 1))),
                   pl.BlockSpec((TM, QKV_TN), lambda i, j: (i, jnp.maximum(j - q_blocks, 0)))],
        out_shape=[jax.ShapeDtypeStruct((N_TOK, n_q), BF16), jax.ShapeDtypeStruct((N_TOK, n - n_q), F32)],
        scratch_shapes=[pltpu.VMEM((TM, D_MODEL), BF16)],
        compiler_params=_params("arbitrary", "arbitrary"),
        name="qkv_proj",
    )(*args)


def _ml_vg_kernel(a_ref, wv_ref, wg_ref, bg_ref, v_ref, g_ref):
    a = a_ref[...]
    v_ref[...] = jnp.dot(a, wv_ref[...], preferred_element_type=F32).astype(v_ref.dtype)

    @pl.when(pl.program_id(1) == 0)
    def _():
        g_ref[...] = jnp.dot(a, wg_ref[...], preferred_element_type=F32) + bg_ref[...]


def _ml_vg(xm, w_v, w_gate, b_gate, tn):
    m, k = xm.shape
    n = _weight_cols(w_v)
    wv_spec, w_v = _weight(w_v, (k, tn), lambda i, j: (0, j))
    return pl.pallas_call(
        _ml_vg_kernel,
        grid=(m // TM, n // tn),
        in_specs=[pl.BlockSpec((TM, k), lambda i, j: (i, 0)), wv_spec,
                  pl.BlockSpec((k, LANES), lambda i, j: (0, 0)),
                  pl.BlockSpec((1, LANES), lambda i, j: (0, 0))],
        out_specs=[pl.BlockSpec((TM, tn), lambda i, j: (i, j)), pl.BlockSpec((TM, LANES), lambda i, j: (i, 0))],
        out_shape=[jax.ShapeDtypeStruct((m, n), BF16), jax.ShapeDtypeStruct((m, LANES), F32)],
        compiler_params=_params("arbitrary", "arbitrary"),
        name="ml_vg",
    )(xm, w_v, w_gate, b_gate)


def _weight(w, block, index):
    if isinstance(w, tuple):
        stacked, layer = w
        return pl.BlockSpec((None,) + block, lambda *g: (layer,) + index(*g)), stacked
    return pl.BlockSpec(block, index), w


def _weight_cols(w):
    return w[0].shape[2] if isinstance(w, tuple) else w.shape[1]


def _row_sources(x, width):
    (ctx, ctx_first), (lat, lat_first) = (((x[0], 0), (x[1], 0)) if isinstance(x, tuple)
                                          else ((x, 0), (x, N_CTX_TILES)))

    def spec(first, lo, hi):
        return pl.BlockSpec((TM, width), lambda i, *_: (jnp.clip(i, lo, hi) - lo + first, 0))

    return ([spec(ctx_first, 0, N_CTX_TILES - 1), spec(lat_first, N_CTX_TILES, N_TOK // TM - 1)], [ctx, lat])


def _mm_res_kernel(ac_ref, al_ref, w_ref, xc_ref, xl_ref, mod_ref, o_ref):
    def emit(a_ref, x_ref):
        acc = jnp.dot(a_ref[...], w_ref[...], preferred_element_type=F32)
        o_ref[...] = x_ref[...] + mod_ref[2:3, :] * acc

    is_ctx = pl.program_id(0) < N_CTX_TILES
    pl.when(is_ctx)(functools.partial(emit, ac_ref, xc_ref))
    pl.when(jnp.logical_not(is_ctx))(functools.partial(emit, al_ref, xl_ref))


def _mm_res(a_ctx, a_lat, w, x, mod):
    k = a_ctx.shape[1]
    a_specs, a_arrays = _row_sources((a_ctx, a_lat), k)
    x_specs, x_arrays = _row_sources(x, D_MODEL)
    w_spec, w = _weight(w, (k, D_MODEL), lambda i: (0, 0))
    return pl.pallas_call(
        _mm_res_kernel,
        grid=(N_TOK // TM,),
        in_specs=a_specs + [w_spec] + x_specs
        + [pl.BlockSpec((None, 6, D_MODEL), lambda i: (_mod_row(i), 0, 0))],
        out_specs=pl.BlockSpec((TM, D_MODEL), lambda i: (i, 0)),
        out_shape=jax.ShapeDtypeStruct((N_TOK, D_MODEL), F32),
        compiler_params=_params("arbitrary"),
        name="mm_res",
    )(*a_arrays, w, *x_arrays, mod)


FFN_TF = 256
FFN_BLOCKS = D_FF // FFN_TF


def _ffn_kernel(x_ref, mod_ref, nw_ref, wup_ref, cw_ref, cb_ref, wd_ref, o_ref, h_scr, acc_scr, raw_a, raw_b):
    nb = FFN_BLOCKS
    h_scr[...] = _norm_mod(x_ref[...], nw_ref[...], mod_ref[3:4, :], mod_ref[4:5, :]).astype(BF16)
    acc_scr[...] = jnp.zeros_like(acc_scr)

    def run(seq):
        def project(k, raw):
            raw[...] = jnp.dot(h_scr[...], wup_ref[k], preferred_element_type=F32)

        def consume(k, raw):
            u = _dwconv_rows(raw[...], cw_ref[k], cb_ref[k], seq)
            a = (_silu(u[:, :FFN_TF]) * u[:, FFN_TF:]).astype(BF16)
            acc_scr[...] += jnp.dot(a, wd_ref[k], preferred_element_type=F32)

        project(0, raw_a)

        def two_stages(t, carry):
            k = 2 * t
            project(k + 1, raw_b)
            consume(k, raw_a)
            project(k + 2, raw_a)
            consume(k + 1, raw_b)
            return carry

        lax.fori_loop(0, (nb - 1) // 2, two_stages, 0)
        consume(nb - 1, raw_a)

    _by_tile_kind(pl.program_id(0), run)
    o_ref[...] = x_ref[...] + mod_ref[5:6, :] * acc_scr[...]


def _column_blocks_kernel(*refs):
    *in_refs, o_ref = refs
    o_ref[...] = jnp.concatenate([r[...] for r in in_refs], axis=1).astype(o_ref.dtype)


def _column_blocks_bf16(w, width, groups):
    layers, r, cols = w.shape
    nb = cols // (groups * width)
    in_specs = [pl.BlockSpec((None, r, width), functools.partial(lambda l, k, g: (l, 0, g * nb + k), g=g))
                for g in range(groups)]
    return pl.pallas_call(
        _column_blocks_kernel,
        grid=(layers, nb),
        in_specs=in_specs,
        out_specs=pl.BlockSpec((None, None, r, groups * width), lambda l, k: (l, k, 0, 0)),
        out_shape=jax.ShapeDtypeStruct((layers, nb, r, groups * width), BF16),
        compiler_params=_params("arbitrary", "arbitrary"),
        name="column_blocks",
    )(*([w] * groups))


def _ffn_weight_blocks(w_up, conv_w, conv_b, w_down):
    nb = FFN_BLOCKS

    def blocks(a):
        r = a.shape[1]
        return a.reshape(DEPTH, r, 2, nb, FFN_TF).transpose(0, 3, 1, 2, 4).reshape(DEPTH, nb, r, 2 * FFN_TF)

    return (_column_blocks_bf16(w_up, FFN_TF, 2), blocks(conv_w), blocks(conv_b.reshape(DEPTH, 1, 2 * D_FF)),
            w_down.reshape(DEPTH, nb, FFN_TF, D_MODEL).astype(BF16))


def _conv_ffn(x, mod, nw, layer, w_up, conv_w, conv_b, w_down):
    nb = FFN_BLOCKS

    def resident(shape):
        return pl.BlockSpec((None,) + shape, lambda i: (layer,) + (0,) * len(shape), pipeline_mode=pl.Buffered(1))

    return pl.pallas_call(
        _ffn_kernel,
        grid=(N_TOK // TM,),
        in_specs=[
            pl.BlockSpec((TM, D_MODEL), lambda i: (i, 0)),
            pl.BlockSpec((None, 6, D_MODEL), lambda i: (_mod_row(i), 0, 0)),
            pl.BlockSpec((1, D_MODEL), lambda i: (0, 0)),
            resident((nb, D_MODEL, 2 * FFN_TF)),
            resident((nb, 3, 2 * FFN_TF)),
            resident((nb, 1, 2 * FFN_TF)),
            resident((nb, FFN_TF, D_MODEL)),
        ],
        out_specs=pl.BlockSpec((TM, D_MODEL), lambda i: (i, 0)),
        out_shape=jax.ShapeDtypeStruct((N_TOK, D_MODEL), F32),
        scratch_shapes=[pltpu.VMEM((TM, D_MODEL), BF16), pltpu.VMEM((TM, D_MODEL), F32),
                        pltpu.VMEM((TM, 2 * FFN_TF), F32), pltpu.VMEM((TM, 2 * FFN_TF), F32)],
        compiler_params=_params("arbitrary"),
        name="conv_ffn",
    )(x, mod, nw.reshape(1, D_MODEL), w_up, conv_w, conv_b, w_down)


ML_TN = 512
ML_EXT = ML_DV + LANES
ML_SEQ_PER_STEP = 2


ML_BLOCKS = ML_D_IN // ML_TN
ML_UP_VMEM_LIMIT_BYTES = 58 * 1024 * 1024


def _ml_up_kernel(xc_src_ref, xl_src_ref, mod_ref, nw_ref, w_ref, cw_ref, cb_ref, xm_ref, xc_ref, z_ref,
                  h_scr, raw_a, raw_b):
    nb = ML_BLOCKS
    i = pl.program_id(0)

    def normalise(x_ref):
        h = _norm_mod(x_ref[...], nw_ref[...], mod_ref[0:1, :], mod_ref[1:2, :])
        h_scr[...] = h.astype(BF16)

    is_ctx = i < N_CTX_TILES
    pl.when(is_ctx)(functools.partial(normalise, xc_src_ref))
    pl.when(jnp.logical_not(is_ctx))(functools.partial(normalise, xl_src_ref))

    def run(seq):
        def project(k, raw):
            h = h_scr[...]
            raw[...] = jnp.dot(h, w_ref[k], preferred_element_type=F32)
            z_ref[:, k * ML_TN:(k + 1) * ML_TN] = jnp.dot(h, w_ref[nb + k], preferred_element_type=F32).astype(BF16)

        def consume(k, raw):
            cols = slice(k * ML_TN, (k + 1) * ML_TN)
            xm = raw[...]
            xm_ref[:, cols] = xm.astype(BF16)
            xc_ref[:, cols] = _silu(_dwconv_rows(xm, cw_ref[:, cols], cb_ref[:, cols], seq)).astype(BF16)

        raws = (raw_a, raw_b)
        project(0, raws[0])
        for k in range(nb):
            if k + 1 < nb:
                project(k + 1, raws[(k + 1) % 2])
            consume(k, raws[k % 2])

    _by_tile_kind(i, run)


def _ml_up_weight_blocks(w_up):
    return _column_blocks_bf16(w_up, ML_TN, 1)


def _ml_up(x, mod, nw, w_up, conv_w, conv_b):
    nb = ML_BLOCKS
    x_specs, x_arrays = _row_sources(x, D_MODEL)
    w_blocks, layer = w_up
    rows = pl.BlockSpec((TM, ML_D_IN), lambda i: (i, 0))
    shp = jax.ShapeDtypeStruct((N_TOK, ML_D_IN), BF16)
    return pl.pallas_call(
        _ml_up_kernel,
        grid=(N_TOK // TM,),
        in_specs=x_specs + [
            pl.BlockSpec((None, 6, D_MODEL), lambda i: (_mod_row(i), 0, 0)),
            pl.BlockSpec((1, D_MODEL), lambda i: (0, 0)),
            pl.BlockSpec((None, 2 * nb, D_MODEL, ML_TN), lambda i: (layer, 0, 0, 0), pipeline_mode=pl.Buffered(1)),
            pl.BlockSpec((3, ML_D_IN), lambda i: (0, 0)),
            pl.BlockSpec((1, ML_D_IN), lambda i: (0, 0)),
        ],
        out_specs=[rows, rows, rows],
        out_shape=[shp, shp, shp],
        scratch_shapes=[pltpu.VMEM((TM, D_MODEL), BF16), pltpu.VMEM((TM, ML_TN), F32), pltpu.VMEM((TM, ML_TN), F32)],
        compiler_params=_params("arbitrary", vmem_limit_bytes=ML_UP_VMEM_LIMIT_BYTES),
        name="ml_up",
    )(*x_arrays, mod, nw.reshape(1, D_MODEL), w_blocks, conv_w, conv_b.reshape(1, ML_D_IN))


def _log_sigmoid(x):
    return jnp.minimum(x, 0.0) - jnp.log(1.0 + jnp.exp(-jnp.abs(x)))


def _ml_scan_kernel(*refs, seq, n_seq, has_init, out_state, n_unused_inputs, fill_layer=None):
    q_ref, kt_ref, v_ref, g_ref, gt_ref, xc_ref, z_ref, nw_ref, skip_ref = refs[:9]
    pos = 9
    if has_init:
        c0_ref, n0_ref, m0_ref = refs[pos:pos + 3]
        pos += 3
    pos += n_unused_inputs
    o_ref = refs[pos]
    pos += 1
    if out_state:
        cout_ref, nout_ref, mout_ref = refs[pos:pos + 3]
        pos += 3
    hs_scr, c_scr = refs[pos:pos + 2]

    L = ML_CHUNK
    nc = seq // L
    ext_tiles = ML_EXT // LANES
    row_i = lax.broadcasted_iota(jnp.int32, (L, L), 0)
    col_i = lax.broadcasted_iota(jnp.int32, (L, L), 1)
    ones_tile = jnp.ones((L, LANES), BF16)

    def lanes(a, n):
        return jnp.concatenate([a] * n, axis=1)

    for d in (0, 1):
        keep = (col_i <= row_i) if d == 0 else (col_i >= row_i)
        keep_f = keep.astype(F32)
        keep_t_f = ((row_i <= col_i) if d == 0 else (row_i >= col_i)).astype(F32)
        end = L - 1 if d == 0 else 0

        ms = []
        for s in range(n_seq):
            if has_init:
                c_scr[s, :, :ML_DV] = c0_ref[s, d]
                c_scr[s, :, ML_DV:] = n0_ref[s, d]
                ms.append(m0_ref[s, d:d + 1, :])
            else:
                c_scr[s] = jnp.zeros((ML_DK, ML_EXT), F32)
                ms.append(jnp.zeros((1, LANES), F32))

        for c, s in [(c, s) for c in range(nc) for s in range(n_seq)]:
            cc = c if d == 0 else nc - 1 - c
            chunk = s * nc + cc
            rows = slice(chunk * L, (chunk + 1) * L)
            m = ms[s]
            qc = q_ref[rows, :]
            ktc = kt_ref[chunk]
            v_ext = jnp.concatenate([v_ref[rows, :], ones_tile], axis=1)
            gcol = g_ref[chunk]
            grow = gt_ref[chunk]
            ig_row = grow[2 * d:2 * d + 1, :]
            lf_col = _log_sigmoid(gcol[:, 2 * d + 1:2 * d + 2])
            lf_row = _log_sigmoid(grow[2 * d + 1:2 * d + 2, :])
            b_col = jnp.broadcast_to(jnp.sum(keep_f * lf_row, axis=1, keepdims=True), (L, LANES))
            b_row = jnp.sum(keep_t_f * lf_col, axis=0, keepdims=True)
            dmat = jnp.where(keep, b_col - b_row + ig_row, NEG_INF)
            m_loc = jnp.broadcast_to(jnp.max(dmat, axis=1, keepdims=True), (L, LANES))
            p_loc = jnp.exp(dmat - m_loc)
            s_loc = jnp.dot(qc, ktc, preferred_element_type=F32) * p_loc
            intra = jnp.dot(s_loc.astype(BF16), v_ext, preferred_element_type=F32)

            m_t = jnp.maximum(b_col + m, m_loc)
            w_inter = jnp.exp(b_col + m - m_t)
            w_intra = jnp.exp(m_loc - m_t)
            inter = jnp.dot(qc, c_scr[s].astype(BF16), preferred_element_type=F32)
            hx = lanes(w_inter, ext_tiles) * inter + lanes(w_intra, ext_tiles) * intra
            inv = 1.0 / jnp.maximum(jnp.abs(hx[:, ML_DV:]), jnp.exp(-m_t))
            h = hx[:, :ML_DV] * lanes(inv, ML_DV // LANES)

            b_end = b_col[end:end + 1, :]
            m_loc_end = m_loc[end:end + 1, :]
            m_new = jnp.maximum(b_end + m, m_loc_end)
            w_c = jnp.exp(b_end + m - m_new)
            w_s = p_loc[end:end + 1, :] * jnp.exp(m_loc_end - m_new)
            upd = jnp.dot((ktc.astype(F32) * w_s).astype(BF16), v_ext, preferred_element_type=F32)
            c_scr[s] = lanes(w_c, ext_tiles) * c_scr[s] + upd
            ms[s] = m_new

            if d == 0:
                hs_scr[rows, :] = h
            else:
                hs = hs_scr[rows, :] + h
                mu = jnp.mean(hs, axis=1, keepdims=True)
                cen = hs - mu
                var = jnp.mean(cen * cen, axis=1, keepdims=True)
                hn = cen * lax.rsqrt(var + NORM_EPS) * nw_ref[...]
                xc = xc_ref[rows, :].astype(F32)
                z = z_ref[rows, :].astype(F32)
                o_ref[rows, :] = ((hn + skip_ref[...] * xc) * _silu(z)).astype(o_ref.dtype)

        if out_state:
            for s in range(n_seq):
                if fill_layer is None:
                    cout_ref[s, d] = c_scr[s, :, :ML_DV]
                else:
                    for layer in range(cout_ref.shape[1]):
                        cout_ref[s, layer, d] = (c_scr[s, :, :ML_DV] if layer == fill_layer
                                                 else jnp.zeros((ML_DK, ML_DV), F32))
                nout_ref[s, d:d + 1, :] = c_scr[s, :, ML_DV:].T[0:1, :]
                mout_ref[s, d:d + 1, :] = ms[s]


def _ml_scan(q, kt, v, g4, g4t, xc, z, norm_w, skip, *, seq, nb, row0, init=None, final_c=None,
             n_seq=ML_SEQ_PER_STEP):
    L = ML_CHUNK
    rows = n_seq * seq
    nc = rows // L
    rb = row0 // rows
    has_init = init is not None
    in_specs = [
        pl.BlockSpec((rows, ML_DK), lambda b, h: (rb + b, h)),
        pl.BlockSpec((None, nc, ML_DK, L), lambda b, h: (h, rb + b, 0, 0)),
        pl.BlockSpec((rows, ML_DV), lambda b, h: (rb + b, h)),
        pl.BlockSpec((None, nc, L, 4), lambda b, h: (h, rb + b, 0, 0)),
        pl.BlockSpec((None, nc, 4, L), lambda b, h: (h, rb + b, 0, 0)),
        pl.BlockSpec((rows, ML_DV), lambda b, h: (rb + b, h)),
        pl.BlockSpec((rows, ML_DV), lambda b, h: (rb + b, h)),
        pl.BlockSpec((1, ML_DV), lambda b, h: (0, h)),
        pl.BlockSpec((1, ML_DV), lambda b, h: (0, h)),
    ]
    args = [q, kt, v, g4, g4t, xc, z, norm_w.reshape(1, ML_D_IN), skip.reshape(1, ML_D_IN)]
    state_n = pl.BlockSpec((n_seq, None, 2, ML_DK, LANES), lambda b, h: (b, h, 0, 0, 0))
    state_m = pl.BlockSpec((n_seq, None, 2, LANES), lambda b, h: (b, h, 0, 0))
    if has_init:
        c0, layer, n0, m0 = init
        in_specs += [pl.BlockSpec((n_seq, None, 2, None, ML_DK, ML_DV), lambda b, h: (b, layer, 0, h, 0, 0)),
                     state_n, state_m]
        args += [c0, n0, m0]
    out_specs = [pl.BlockSpec((rows, ML_DV), lambda b, h: (b, h))]
    out_shape = [jax.ShapeDtypeStruct((nb * seq, ML_D_IN), BF16)]
    aliases = {}
    if not has_init:
        layer_out, c_all = final_c
        if c_all is None:
            c_spec = pl.BlockSpec((n_seq, N_ML_LAYERS, 2, None, ML_DK, ML_DV), lambda b, h: (b, 0, 0, h, 0, 0))
        else:
            c_spec = pl.BlockSpec((n_seq, None, 2, None, ML_DK, ML_DV), lambda b, h: (b, layer_out, 0, h, 0, 0))
        out_specs += [c_spec, pl.BlockSpec((n_seq, None, 2, ML_DK), lambda b, h: (b, h, 0, 0)), state_m]
        out_shape += [
            jax.ShapeDtypeStruct((nb, N_ML_LAYERS, 2, ML_HEADS, ML_DK, ML_DV), F32),
            jax.ShapeDtypeStruct((nb, ML_HEADS, 2, ML_DK), F32),
            jax.ShapeDtypeStruct((nb, ML_HEADS, 2, LANES), F32),
        ]
        if c_all is not None:
            in_specs.append(pl.BlockSpec(memory_space=pl.ANY))
            args.append(c_all)
            aliases = {len(args) - 1: 1}
    return pl.pallas_call(
        functools.partial(_ml_scan_kernel, seq=seq, n_seq=n_seq, has_init=has_init, out_state=not has_init,
                          n_unused_inputs=len(aliases),
                          fill_layer=final_c[0] if (not has_init and final_c[1] is None) else None),
        grid=(nb // n_seq, ML_HEADS),
        in_specs=in_specs,
        out_specs=out_specs,
        out_shape=out_shape,
        scratch_shapes=[pltpu.VMEM((rows, ML_DV), F32), pltpu.VMEM((n_seq, ML_DK, ML_EXT), F32)],
        input_output_aliases=aliases,
        compiler_params=_params("arbitrary", "arbitrary"),
        name="ml_scan_init" if has_init else "ml_scan_zero",
    )(*args)


def _ml_qk_kernel(xc_ref, w_ref, q_ref, kt_ref):
    acc = jnp.dot(xc_ref[...], w_ref[...], preferred_element_type=F32)
    j = pl.program_id(1)

    @pl.when(j == 0)
    def _():
        q_ref[...] = (acc * ML_DK ** -0.5).astype(BF16)

    @pl.when(j == 1)
    def _():
        for h in range(ML_HEADS):
            kt = acc[:, h * ML_DK:(h + 1) * ML_DK].T
            for c in range(TM // ML_CHUNK):
                kt_ref[h, c] = kt[:, c * ML_CHUNK:(c + 1) * ML_CHUNK].astype(BF16)


def _ml_qk(xc, w_qk):
    n = ML_HEADS * ML_DK
    cpt = TM // ML_CHUNK
    w_spec, w_qk = _weight(w_qk, (ML_D_IN, n), lambda i, j: (0, j))
    return pl.pallas_call(
        _ml_qk_kernel,
        grid=(N_TOK // TM, 2),
        in_specs=[pl.BlockSpec((TM, ML_D_IN), lambda i, j: (i, 0)), w_spec],
        out_specs=[pl.BlockSpec((TM, n), lambda i, j: (i, 0)),
                   pl.BlockSpec((ML_HEADS, cpt, ML_DK, ML_CHUNK), lambda i, j: (0, i, 0, 0))],
        out_shape=[jax.ShapeDtypeStruct((N_TOK, n), BF16),
                   jax.ShapeDtypeStruct((ML_HEADS, N_TOK // ML_CHUNK, ML_DK, ML_CHUNK), BF16)],
        compiler_params=_params("arbitrary", "arbitrary"),
        name="ml_qk",
    )(xc, w_qk)


def _mlstm_layer(x, mod, nw, p, state, final_c):
    xm, xc, z = _ml_up(x, mod, nw, p["w_up"], p["conv_w"], p["conv_b"])
    q, kt = _ml_qk(xc, p["w_qk"])
    v, g = _ml_vg(xm, p["w_v"], p["w_gate"], p["b_gate"], 1024)
    g = g[:, :4 * ML_HEADS]
    L = ML_CHUNK
    g4 = g.reshape(N_TOK // L, L, 4, ML_HEADS).transpose(3, 0, 1, 2)
    g4t = g4.transpose(0, 1, 3, 2)
    oc, c_fin, n_fin, m_fin = _ml_scan(q, kt, v, g4, g4t, xc, z, p["norm_w"], p["skip"],
                                       seq=SEQ, nb=BATCH, row0=0, final_c=final_c)
    (ol,) = _ml_scan(q, kt, v, g4, g4t, xc, z, p["norm_w"], p["skip"],
                     seq=DEC_SEQ, nb=DEC_BATCH, row0=N_CTX, init=state, n_seq=1)
    x = _mm_res(oc, ol, p["w_down"], x, mod)
    return x, c_fin, n_fin.transpose(0, 2, 1, 3), m_fin[..., 0].transpose(0, 2, 1)


def _softmax_parts(scores, sink_col):
    m = functools.reduce(jnp.maximum, [jnp.max(s, axis=1, keepdims=True) for s in scores])
    if sink_col is not None:
        m = jnp.maximum(m, sink_col)
    ps = [jnp.exp(s - m) for s in scores]
    den = functools.reduce(jnp.add, [jnp.sum(p, axis=1, keepdims=True) for p in ps])
    if sink_col is not None:
        den = den + jnp.exp(sink_col - m)
    return ps, den


def _qk(q, k):
    return lax.dot_general(q, k, (((1,), (1,)), ((), ())), preferred_element_type=F32) * (HEAD_DIM ** -0.5)


def _attend_groups(groups):
    r = groups[0][0][0].shape[0]
    lo = lax.broadcasted_iota(jnp.int32, (r, LANES), 1) < HEAD_DIM
    zero = jnp.zeros((r, LANES), BF16)
    all_scores = []
    for q_tiles, parts, _ in groups:
        qs = jnp.concatenate([jnp.where(sel, t, zero) for t in q_tiles for sel in (lo, jnp.logical_not(lo))], axis=0)
        all_scores.append([post(_qk(qs, k2)) for k2, _, post in parts])
    outs = []
    for scores, (q_tiles, parts, sinks) in zip(all_scores, groups):
        n_rows = scores[0].shape[0]
        sink_col = None
        if sinks is not None:
            rows = lax.broadcasted_iota(jnp.int32, (n_rows, 1), 0)
            sink_col = jnp.full((n_rows, 1), sinks[0], F32)
            for hi in range(1, len(sinks)):
                sink_col = jnp.where(rows >= hi * r, sinks[hi], sink_col)
        ps, den = _softmax_parts(scores, sink_col)
        o = functools.reduce(jnp.add, [jnp.dot(p.astype(BF16), v2, preferred_element_type=F32)
                                       for p, (_, v2, _) in zip(ps, parts)]) / den
        outs.append([jnp.where(lo, o[2 * a * r:(2 * a + 1) * r], o[(2 * a + 1) * r:(2 * a + 2) * r])
                     for a in range(len(q_tiles))])
    return outs


def _identity(s):
    return s


def _ctx_attn_kernel(*refs, tiles_per_kv, has_sink, kv_head_stride):
    if has_sink:
        sink_ref, q_ref, k_ref, v_ref, o_ref, kout_ref, vout_ref = refs
    else:
        q_ref, k_ref, v_ref, o_ref, kout_ref, vout_ref = refs
    for h in range(kout_ref.shape[0]):
        kout_ref[h] = k_ref[:, h * kv_head_stride:h * kv_head_stride + HEAD_DIM]
        vout_ref[h] = v_ref[:, h * kv_head_stride:h * kv_head_stride + HEAD_DIM]
    groups = []
    for t in range(k_ref.shape[1] // LANES):
        k2 = k_ref[:, t * LANES:(t + 1) * LANES].astype(BF16)
        v2 = v_ref[:, t * LANES:(t + 1) * LANES].astype(BF16)
        first = t * tiles_per_kv
        q_tiles = [q_ref[:, (first + a) * LANES:(first + a + 1) * LANES] for a in range(tiles_per_kv)]
        sinks = [sink_ref[2 * first + hi] for hi in range(2 * tiles_per_kv)] if has_sink else None
        groups.append((q_tiles, [(k2, v2, _identity)], sinks))
    for t, outs in enumerate(_attend_groups(groups)):
        for a, o in enumerate(outs):
            tile = t * tiles_per_kv + a
            o_ref[:, tile * LANES:(tile + 1) * LANES] = o.astype(o_ref.dtype)


def _ctx_attn(q, kv, n_kv_cols, n_kv_heads, sink):
    nq = q.shape[1]
    has_sink = sink is not None
    cache_spec = pl.BlockSpec((None, None, n_kv_heads, SEQ, HEAD_DIM), lambda b: (b, 0, 0, 0, 0))
    cache_shape = jax.ShapeDtypeStruct((BATCH, 1, n_kv_heads, SEQ, HEAD_DIM), F32)
    in_specs = [pl.BlockSpec((SEQ, nq), lambda b: (b, 0)),
                pl.BlockSpec((SEQ, n_kv_cols), lambda b: (b, 0)),
                pl.BlockSpec((SEQ, n_kv_cols), lambda b: (b, 1))]
    args = [q, kv, kv]
    if has_sink:
        in_specs = [pl.BlockSpec(memory_space=pltpu.SMEM)] + in_specs
        args = [sink] + args
    return pl.pallas_call(
        functools.partial(_ctx_attn_kernel, tiles_per_kv=nq // n_kv_cols, has_sink=has_sink,
                          kv_head_stride=n_kv_cols // n_kv_heads),
        grid=(BATCH,),
        in_specs=in_specs,
        out_specs=[pl.BlockSpec((SEQ, nq), lambda b: (b, 0)), cache_spec, cache_spec],
        out_shape=[jax.ShapeDtypeStruct((N_CTX, nq), BF16), cache_shape, cache_shape],
        compiler_params=_params("arbitrary"),
        name="ctx_attn",
    )(*args)


SWA_SPAN = Q_BLOCK + 2 * SWA_WINDOW
SWA_KV_COLS = SWA_KV * LANES


def _swa_lat_kernel(sink_ref, q_ref, k_ref, v_ref, kc_ref, vc_ref, o_ref):
    j = pl.program_id(1)
    start = pl.multiple_of(jnp.clip((j - 1) * Q_BLOCK, 0, DEC_SEQ - SWA_SPAN), Q_BLOCK)
    rows = 4 * Q_BLOCK
    qpos = j * Q_BLOCK + (lax.broadcasted_iota(jnp.int32, (rows, SWA_SPAN), 0) & (Q_BLOCK - 1))
    kpos = start + lax.broadcasted_iota(jnp.int32, (rows, SWA_SPAN), 1)
    in_window = jnp.abs(qpos - kpos) <= SWA_WINDOW

    def window(s):
        return jnp.where(in_window, s, NEG_INF)

    groups = []
    for t in range(SWA_KV):
        cols = slice(t * LANES, (t + 1) * LANES)
        k_loc = k_ref[pl.ds(start, SWA_SPAN), cols].astype(BF16)
        v_loc = v_ref[pl.ds(start, SWA_SPAN), cols].astype(BF16)
        q_tiles = [q_ref[:, (2 * t + a) * LANES:(2 * t + a + 1) * LANES] for a in range(2)]
        sinks = [sink_ref[4 * t + hi] for hi in range(4)]
        groups.append((q_tiles, [(k_loc, v_loc, window), (kc_ref[:, cols], vc_ref[:, cols], _identity)], sinks))
    for t, outs in enumerate(_attend_groups(groups)):
        for a, o in enumerate(outs):
            o_ref[:, (2 * t + a) * LANES:(2 * t + a + 1) * LANES] = o.astype(o_ref.dtype)


def _swa_latent(q, kv, kc, vc, sink):
    nq = q.shape[1]
    rb = N_CTX // DEC_SEQ
    qb = N_CTX // Q_BLOCK
    nj = DEC_SEQ // Q_BLOCK
    cspec = pl.BlockSpec((None, SEQ, SWA_KV_COLS), lambda b, j: (b, 0, 0))
    return pl.pallas_call(
        _swa_lat_kernel,
        grid=(DEC_BATCH, nj),
        in_specs=[pl.BlockSpec(memory_space=pltpu.SMEM),
                  pl.BlockSpec((Q_BLOCK, nq), lambda b, j: (qb + b * nj + j, 0)),
                  pl.BlockSpec((DEC_SEQ, SWA_KV_COLS), lambda b, j: (rb + b, 0)),
                  pl.BlockSpec((DEC_SEQ, SWA_KV_COLS), lambda b, j: (rb + b, 1)),
                  cspec, cspec],
        out_specs=pl.BlockSpec((Q_BLOCK, nq), lambda b, j: (b * nj + j, 0)),
        out_shape=jax.ShapeDtypeStruct((N_LAT, nq), BF16),
        compiler_params=_params("arbitrary", "arbitrary"),
        name="swa_latent",
    )(sink, q, kv, kv, kc, vc)


NA_QT = 256
NA_SPAN = 768
NA_TILES = 4


def _na_start(j):
    return (j // 2) * (DEC_SEQ - NA_SPAN)


NA_ROWS = DEC_SEQ // GRID_W
NA_DR = 2 * NA_KH - 1
NA_DC = 2 * NA_KW - 1


def _na_blocks_kernel(rpb_ref, onehot_ref, valid_ref, o_ref):
    t = jnp.dot(rpb_ref[...], onehot_ref[...], preferred_element_type=F32, precision=lax.Precision.HIGHEST)
    o_ref[...] = jnp.where(valid_ref[...] > 0.5, t, NEG_INF)


def _na_bias_blocks(rpb):
    h = rpb.shape[0]
    kpad = 32
    cq, ck = np.meshgrid(np.arange(GRID_W), np.arange(GRID_W), indexing="ij")
    dc = (np.clip(ck - cq, -(NA_KW - 1), NA_KW - 1) + NA_KW - 1).reshape(-1)
    cs = np.clip(cq - NA_KW // 2, 0, GRID_W - NA_KW)
    valid = ((ck >= cs) & (ck < cs + NA_KW)).reshape(1, -1).astype(np.float32)
    onehot = (np.arange(kpad)[:, None] == dc[None, :]).astype(np.float32)
    rpb2 = jnp.pad(rpb.reshape(h * NA_DR, NA_DC), ((0, 0), (0, kpad - NA_DC)))
    n = GRID_W * GRID_W
    blocks = pl.pallas_call(
        _na_blocks_kernel,
        grid=(1,),
        in_specs=[pl.BlockSpec((h * NA_DR, kpad), lambda i: (0, 0)),
                  pl.BlockSpec((kpad, n), lambda i: (0, 0)),
                  pl.BlockSpec((1, n), lambda i: (0, 0))],
        out_specs=pl.BlockSpec((h * NA_DR, n), lambda i: (0, 0)),
        out_shape=jax.ShapeDtypeStruct((h * NA_DR, n), F32),
        compiler_params=_params("arbitrary"),
        name="na_bias_blocks",
    )(rpb2, jnp.asarray(onehot), jnp.asarray(valid))
    blocks = blocks.reshape(h, NA_DR, GRID_W, GRID_W)
    padded = jnp.pad(blocks, ((0, 0), (1, 1), (0, 0), (0, 0)), constant_values=NEG_INF)
    return jnp.concatenate([padded[:, :-1], padded[:, 1:]], axis=-1)


def _na_lat_kernel(q_ref, k_ref, v_ref, kc_ref, vc_ref, blk_ref, o_ref, bias_scr):
    j = pl.program_id(0)
    start = pl.multiple_of(_na_start(j), 256)

    @pl.when(pl.program_id(2) == 0)
    def _():
        lane_lo = lax.broadcasted_iota(jnp.int32, (GRID_W, LANES), 1) < GRID_W
        for rq_l in range(NA_QT // GRID_W):
            rq = j * (NA_QT // GRID_W) + rq_l
            rs = jnp.clip(rq - NA_KH // 2, 0, NA_ROWS - NA_KH)
            for kp in range(NA_SPAN // LANES):
                rk = start // GRID_W + 2 * kp
                idx = jnp.clip(rk - rq + NA_KH, 0, NA_DR)
                in_band = [jnp.logical_and(r >= rs, r < rs + NA_KH).astype(jnp.int32) for r in (rk, rk + 1)]
                ok = jnp.where(lane_lo, in_band[0], in_band[1]) > 0
                for hh in range(2 * NA_TILES):
                    bias_scr[hh * NA_QT + rq_l * GRID_W:hh * NA_QT + (rq_l + 1) * GRID_W,
                             kp * LANES:(kp + 1) * LANES] = jnp.where(ok, blk_ref[hh, idx], NEG_INF)

    groups = []
    for t in range(NA_TILES):
        cols = slice(t * LANES, (t + 1) * LANES)

        def add_bias(s, t=t):
            return s + bias_scr[2 * t * NA_QT:2 * (t + 1) * NA_QT, :]

        k_loc = k_ref[pl.ds(start, NA_SPAN), cols].astype(BF16)
        v_loc = v_ref[pl.ds(start, NA_SPAN), cols].astype(BF16)
        groups.append(([q_ref[:, cols]], [(k_loc, v_loc, add_bias), (kc_ref[:, cols], vc_ref[:, cols], _identity)],
                       None))
    for t, (o,) in enumerate(_attend_groups(groups)):
        o_ref[:, t * LANES:(t + 1) * LANES] = o.astype(o_ref.dtype)


def _na_latent(q, kv, kc, vc, blocks):
    nq = q.shape[1]
    width = NA_TILES * LANES
    n_steps = nq // width
    nj = DEC_SEQ // NA_QT
    rb = N_CTX // DEC_SEQ
    qb = N_CTX // NA_QT
    cspec = pl.BlockSpec((None, SEQ, width), lambda j, p, b: (b, 0, p))
    return pl.pallas_call(
        _na_lat_kernel,
        grid=(nj, n_steps, DEC_BATCH),
        in_specs=[pl.BlockSpec((NA_QT, width), lambda j, p, b: (qb + b * nj + j, p)),
                  pl.BlockSpec((DEC_SEQ, width), lambda j, p, b: (rb + b, p)),
                  pl.BlockSpec((DEC_SEQ, width), lambda j, p, b: (rb + b, n_steps + p)),
                  cspec, cspec,
                  pl.BlockSpec((2 * NA_TILES, NA_DR + 1, GRID_W, LANES), lambda j, p, b: (p, 0, 0, 0))],
        out_specs=pl.BlockSpec((NA_QT, width), lambda j, p, b: (b * nj + j, p)),
        out_shape=jax.ShapeDtypeStruct((N_LAT, nq), BF16),
        scratch_shapes=[pltpu.VMEM((2 * NA_TILES * NA_QT, NA_SPAN), F32)],
        compiler_params=_params("arbitrary", "arbitrary", "arbitrary"),
        name="na_latent",
    )(q, kv, kv, kc, vc, blocks)


def _rope_tables(width):
    quarter = HEAD_DIM // 4
    pos = np.arange(DEC_SEQ)
    inv = np.power(ROPE_BASE, -np.arange(quarter, dtype=np.float32) / quarter).astype(np.float32)
    d = np.arange(width) % HEAD_DIM
    p = np.where((d < HEAD_DIM // 2)[None, :], (pos // GRID_W)[:, None], (pos % GRID_W)[:, None]).astype(np.float32)
    ang = p * inv[d % quarter][None, :]
    sign = np.where((d // quarter) % 2 == 0, -1.0, 1.0)[None, :]
    return jnp.asarray(np.cos(ang), F32), jnp.asarray(np.sin(ang) * sign, F32)


def _cache_rows(cache, dup):
    b, h, s, hd = cache.shape
    rows = jnp.broadcast_to(cache.transpose(0, 2, 1, 3)[:, :, :, None, :], (b, s, h, dup, hd))
    return rows.reshape(b, s, h * dup * hd).astype(BF16)


def _swa_layer(x, mod, nw, w_qkv, sink, w_o, cache_k, cache_v):
    nq, nk = SWA_HEADS * HEAD_DIM, SWA_KV * HEAD_DIM

    def dup_heads(w):
        return jnp.broadcast_to(w.reshape(D_MODEL, SWA_KV, 1, HEAD_DIM),
                                (D_MODEL, SWA_KV, 2, HEAD_DIM)).reshape(D_MODEL, SWA_KV_COLS)

    w = jnp.concatenate([w_qkv[:, :nq], dup_heads(w_qkv[:, nq:nq + nk]), dup_heads(w_qkv[:, nq + nk:])], axis=1)
    q, kv = _qkv_proj(x, mod, nw, w.astype(BF16), nq, rope_cols=nq + SWA_KV_COLS)
    oc, k_new, v_new = _ctx_attn(q, kv, SWA_KV_COLS, SWA_KV, sink)
    ol = _swa_latent(q, kv, _cache_rows(cache_k, 2), _cache_rows(cache_v, 2), sink)
    return _mm_res(oc, ol, w_o, x, mod), k_new, v_new


def _na_layer(x, mod, nw, w_qkv, rpb, w_o, cache_k, cache_v):
    n = NA_HEADS * HEAD_DIM
    q, kv = _qkv_proj(x, mod, nw, w_qkv, n)
    oc, k_new, v_new = _ctx_attn(q, kv, n, NA_HEADS, None)
    ol = _na_latent(q, kv, _cache_rows(cache_k, 1), _cache_rows(cache_v, 1), _na_bias_blocks(rpb))
    return _mm_res(oc, ol, w_o, x, mod), k_new, v_new


def _final_norm_kernel(x_ref, w_ref, oc_ref, ol_ref):
    x = x_ref[...]
    y = x * lax.rsqrt(jnp.mean(x * x, axis=-1, keepdims=True) + NORM_EPS) * w_ref[...]
    is_ctx = pl.program_id(0) < N_CTX_TILES

    @pl.when(is_ctx)
    def _():
        oc_ref[...] = y

    @pl.when(jnp.logical_not(is_ctx))
    def _():
        ol_ref[...] = y


def _final_norm(x, w):
    return pl.pallas_call(
        _final_norm_kernel,
        grid=(N_TOK // TM,),
        in_specs=[pl.BlockSpec((TM, D_MODEL), lambda i: (i, 0)), pl.BlockSpec((1, D_MODEL), lambda i: (0, 0))],
        out_specs=[pl.BlockSpec((TM, D_MODEL), lambda i: (jnp.minimum(i, N_CTX_TILES - 1), 0)),
                   pl.BlockSpec((TM, D_MODEL), lambda i: (jnp.maximum(i - N_CTX_TILES, 0), 0))],
        out_shape=[jax.ShapeDtypeStruct((N_CTX, D_MODEL), F32), jax.ShapeDtypeStruct((N_LAT, D_MODEL), F32)],
        compiler_params=_params("arbitrary"),
        name="final_norm",
    )(x, w.reshape(1, D_MODEL))


def kernel(x_prompt, x_sample, state_mlstm_C, state_mlstm_n, state_mlstm_m, cache_swa_k, cache_swa_v, cache_na_k, cache_na_v, c, c_ctx, ada_w, ada_b, norm_w, final_norm_w, ffn_w_up, ffn_conv_w, ffn_conv_b, ffn_w_down, ml_w_up, ml_conv_w, ml_conv_b, ml_w_qk, ml_w_v, ml_w_gate, ml_b_gate, ml_norm_w, ml_skip, ml_w_down, swa_w_qkv, swa_sink, swa_w_o, na_w_qkv, na_rpb, na_w_o):
    x = (x_prompt.reshape(N_CTX, D_MODEL), x_sample.reshape(N_LAT, D_MODEL))
    cond = jnp.concatenate([c_ctx[None], c, jnp.zeros((MOD_ROWS - 1 - DEC_BATCH, D_MODEL), F32)], axis=0)
    mods = _ada_mod(cond, ada_w, ada_b)
    ffn_weights = _ffn_weight_blocks(ffn_w_up, ffn_conv_w, ffn_conv_b, ffn_w_down)
    ml_up_blocks = _ml_up_weight_blocks(ml_w_up)
    ml_qk_bf16, ml_v_bf16, ml_down_bf16 = ml_w_qk.astype(BF16), ml_w_v.astype(BF16), ml_w_down.astype(BF16)

    new_c, new_n, new_m = None, [], []
    new_sk = new_sv = new_nk = new_nv = None
    for i in range(DEPTH):
        kind, j = i % N_MIXERS, i // N_MIXERS
        mod = mods[i]
        if kind == 0:
            gate_w = jnp.pad(ml_w_gate[j], ((0, 0), (0, 128 - 4 * ML_HEADS))).astype(BF16)
            gate_b = jnp.pad(ml_b_gate[j], (0, 128 - 4 * ML_HEADS)).reshape(1, 128)
            p = dict(w_up=(ml_up_blocks, j), conv_w=ml_conv_w[j], conv_b=ml_conv_b[j],
                     w_qk=(ml_qk_bf16, j), w_v=(ml_v_bf16, j), w_gate=gate_w, b_gate=gate_b,
                     norm_w=ml_norm_w[j], skip=ml_skip[j], w_down=(ml_down_bf16, j))
            n0 = jnp.broadcast_to(state_mlstm_n[:, j].transpose(0, 2, 1, 3)[..., None],
                                  (DEC_BATCH, ML_HEADS, 2, ML_DK, LANES))
            m0 = jnp.broadcast_to(state_mlstm_m[:, j].transpose(0, 2, 1)[..., None], (DEC_BATCH, ML_HEADS, 2, LANES))
            x, new_c, nf, mf = _mlstm_layer(x, mod, norm_w[i, 0], p, (state_mlstm_C, j, n0, m0), (j, new_c))
            new_n.append(nf)
            new_m.append(mf)
        elif kind == 1:
            x, k_new, v_new = _swa_layer(x, mod, norm_w[i, 0], swa_w_qkv[j].astype(BF16), swa_sink[j],
                                         swa_w_o[j].astype(BF16), cache_swa_k[:, j], cache_swa_v[:, j])
            new_sk, new_sv = k_new, v_new
        else:
            x, k_new, v_new = _na_layer(x, mod, norm_w[i, 0], na_w_qkv[j].astype(BF16), na_rpb[j],
                                        na_w_o[j].astype(BF16), cache_na_k[:, j], cache_na_v[:, j])
            new_nk, new_nv = k_new, v_new
        x = _conv_ffn(x, mod, norm_w[i, 1], i, *ffn_weights)

    y_ctx, y_lat = _final_norm(x, final_norm_w)
    return (y_ctx.reshape(BATCH, SEQ, D_MODEL), y_lat.reshape(DEC_BATCH, DEC_SEQ, D_MODEL),
            new_c, jnp.stack(new_n, axis=1), jnp.stack(new_m, axis=1),
            new_sk, new_sv, new_nk, new_nv)
```

```python
import functools

import jax
import jax.numpy as jnp
import numpy as np
from jax import lax
from jax.experimental import pallas as pl
from jax.experimental.pallas import tpu as pltpu

F32 = jnp.float32
BF16 = jnp.bfloat16

D_MODEL = 1024
BATCH = 32
SEQ = 256
DEPTH = 4
DEC_BATCH = 8
DEC_SEQ = 1024
GRID_W = 64
N_MIXERS = 3
N_ML_LAYERS = (DEPTH + 2) // 3
NORM_EPS = 1e-6
D_FF = 2816
ML_D_IN = 2 * D_MODEL
ML_HEADS = 4
ML_DK = ML_D_IN // (2 * ML_HEADS)
ML_DV = ML_D_IN // ML_HEADS
ML_CHUNK = 128
HEAD_DIM = 64
SWA_HEADS = D_MODEL // HEAD_DIM
SWA_KV = SWA_HEADS // 4
SWA_WINDOW = 128
Q_BLOCK = 128
ROPE_BASE = 10000.0
NA_HEADS = D_MODEL // HEAD_DIM
NA_KH = 8
NA_KW = 16
NEG_INF = -1e30

N_CTX = BATCH * SEQ
N_LAT = DEC_BATCH * DEC_SEQ
N_TOK = N_CTX + N_LAT
TM = 1024
N_CTX_TILES = N_CTX // TM
MOD_ROWS = 16
VMEM_LIMIT_BYTES = 56 * 1024 * 1024


def _params(*sem, vmem_limit_bytes=VMEM_LIMIT_BYTES):
    return pltpu.CompilerParams(dimension_semantics=sem, vmem_limit_bytes=vmem_limit_bytes)


def _mod_row(i):
    return jnp.where(i < N_CTX_TILES, 0, i - (N_CTX_TILES - 1))


def _silu(x):
    return x / (1.0 + jnp.exp(-x))


def _norm_mod(x, nw, shift, scale):
    y = x * lax.rsqrt(jnp.mean(x * x, axis=-1, keepdims=True) + NORM_EPS) * nw
    return y * (1.0 + scale) + shift


SUBLANES = 8
LANES = 128


def _dwconv_rows(u, cw, cb, seq):
    r, c = u.shape
    n_groups, per_seq = r // SUBLANES, seq // SUBLANES
    g = u.reshape(n_groups, SUBLANES, c)
    sub = lax.broadcasted_iota(jnp.int32, g.shape, 1)
    down = pltpu.roll(g, 1, 1)
    up = pltpu.roll(g, SUBLANES - 1, 1)
    zero = jnp.zeros((1, SUBLANES, c), F32)
    from_prev, from_next = [], []
    for s in range(0, n_groups, per_seq):
        from_prev += [zero, down[s:s + per_seq - 1]]
        from_next += [up[s + 1:s + per_seq], zero]
    prev = jnp.where(sub == 0, jnp.concatenate(from_prev, axis=0), down)
    nxt = jnp.where(sub == SUBLANES - 1, jnp.concatenate(from_next, axis=0), up)
    out = cw[0:1, :] * prev + cw[1:2, :] * g + cw[2:3, :] * nxt + cb
    return out.reshape(r, c)


def _by_tile_kind(tile, body):
    pl.when(tile < N_CTX_TILES)(functools.partial(body, SEQ))
    pl.when(tile >= N_CTX_TILES)(functools.partial(body, DEC_SEQ))


def _ada_kernel(c_ref, w_ref, b_ref, o_ref):
    s = _silu(c_ref[...]).astype(BF16)
    o_ref[...] = jnp.dot(s, w_ref[...].astype(BF16), preferred_element_type=F32) + b_ref[...]


def _ada_mod(cond, ada_w, ada_b):
    tn = 3072
    n = 6 * D_MODEL
    out = pl.pallas_call(
        _ada_kernel,
        grid=(DEPTH, n // tn),
        in_specs=[
            pl.BlockSpec((MOD_ROWS, D_MODEL), lambda l, j: (0, 0)),
            pl.BlockSpec((None, D_MODEL, tn), lambda l, j: (l, 0, j)),
            pl.BlockSpec((None, 1, tn), lambda l, j: (l, 0, j)),
        ],
        out_specs=pl.BlockSpec((None, MOD_ROWS, tn), lambda l, j: (l, 0, j)),
        out_shape=jax.ShapeDtypeStruct((DEPTH, MOD_ROWS, n), F32),
        compiler_params=_params("arbitrary", "arbitrary"),
        name="ada_mod",
    )(cond, ada_w, ada_b.reshape(DEPTH, 1, n))
    return out.reshape(DEPTH, MOD_ROWS, 6, D_MODEL)


QKV_TN = 512


def _rotate_pairs(a, cos, sin):
    lane = lax.broadcasted_iota(jnp.int32, a.shape, 1)
    first = (lane & (HEAD_DIM // 4)) == 0
    n = a.shape[1]
    partner = jnp.where(first, pltpu.roll(a, n - HEAD_DIM // 4, 1), pltpu.roll(a, HEAD_DIM // 4, 1))
    return a * cos + partner * sin


def _qkv_kernel(*refs, q_blocks, rope_blocks):
    if rope_blocks:
        x_ref, mod_ref, nw_ref, w_ref, cos_ref, sin_ref, q_ref, kv_ref, h_scr = refs
    else:
        x_ref, mod_ref, nw_ref, w_ref, q_ref, kv_ref, h_scr = refs
    i = pl.program_id(0)
    j = pl.program_id(1)

    @pl.when(j == 0)
    def _():
        h = _norm_mod(x_ref[...], nw_ref[...], mod_ref[0:1, :], mod_ref[1:2, :])
        h_scr[...] = h.astype(BF16)

    acc = jnp.dot(h_scr[...], w_ref[...], preferred_element_type=F32)

    def emit(val):
        @pl.when(j < q_blocks)
        def _():
            q_ref[...] = val.astype(BF16)

        @pl.when(j >= q_blocks)
        def _():
            kv_ref[...] = val

    if rope_blocks:
        rotate = jnp.logical_and(i >= N_CTX_TILES, j < rope_blocks)

        @pl.when(rotate)
        def _():
            emit(_rotate_pairs(acc, cos_ref[...], sin_ref[...]))

        @pl.when(jnp.logical_not(rotate))
        def _():
            emit(acc)
    else:
        emit(acc)


def _qkv_proj(x, mod, nw, w, n_q, rope_cols=0):
    n = w.shape[1]
    q_blocks = n_q // QKV_TN
    rope_blocks = rope_cols // QKV_TN
    in_specs = [
        pl.BlockSpec((TM, D_MODEL), lambda i, j: (i, 0)),
        pl.BlockSpec((None, 6, D_MODEL), lambda i, j: (_mod_row(i), 0, 0)),
        pl.BlockSpec((1, D_MODEL), lambda i, j: (0, 0)),
        pl.BlockSpec((D_MODEL, QKV_TN), lambda i, j: (0, j)),
    ]
    args = [x, mod, nw.reshape(1, D_MODEL), w]
    if rope_blocks:
        tab = pl.BlockSpec((DEC_SEQ, QKV_TN), lambda i, j: (0, 0))
        in_specs += [tab, tab]
        args += list(_rope_tables(QKV_TN))
    return pl.pallas_call(
        functools.partial(_qkv_kernel, q_blocks=q_blocks, rope_blocks=rope_blocks),
        grid=(N_TOK // TM, n // QKV_TN),
        in_specs=in_specs,
        out_specs=[pl.BlockSpec((TM, QKV_TN), lambda i, j: (i, jnp.minimum(j, q_blocks - 1))),
                   pl.BlockSpec((TM, QKV_TN), lambda i, j: (i, jnp.maximum(j - q_blocks, 0)))],
        out_shape=[jax.ShapeDtypeStruct((N_TOK, n_q), BF16), jax.ShapeDtypeStruct((N_TOK, n - n_q), F32)],
        scratch_shapes=[pltpu.VMEM((TM, D_MODEL), BF16)],
        compiler_params=_params("arbitrary", "arbitrary"),
        name="qkv_proj",
    )(*args)


def _ml_vg_kernel(a_ref, wv_ref, wg_ref, bg_ref, v_ref, g_ref):
    a = a_ref[...]
    v_ref[...] = jnp.dot(a, wv_ref[...], preferred_element_type=F32).astype(v_ref.dtype)

    @pl.when(pl.program_id(1) == 0)
    def _():
        g_ref[...] = jnp.dot(a, wg_ref[...], preferred_element_type=F32) + bg_ref[...]


def _ml_vg(xm, w_v, w_gate, b_gate, tn):
    m, k = xm.shape
    n = _weight_cols(w_v)
    wv_spec, w_v = _weight(w_v, (k, tn), lambda i, j: (0, j))
    return pl.pallas_call(
        _ml_vg_kernel,
        grid=(m // TM, n // tn),
        in_specs=[pl.BlockSpec((TM, k), lambda i, j: (i, 0)), wv_spec,
                  pl.BlockSpec((k, LANES), lambda i, j: (0, 0)),
                  pl.BlockSpec((1, LANES), lambda i, j: (0, 0))],
        out_specs=[pl.BlockSpec((TM, tn), lambda i, j: (i, j)), pl.BlockSpec((TM, LANES), lambda i, j: (i, 0))],
        out_shape=[jax.ShapeDtypeStruct((m, n), BF16), jax.ShapeDtypeStruct((m, LANES), F32)],
        compiler_params=_params("arbitrary", "arbitrary"),
        name="ml_vg",
    )(xm, w_v, w_gate, b_gate)


def _weight(w, block, index):
    if isinstance(w, tuple):
        stacked, layer = w
        return pl.BlockSpec((None,) + block, lambda *g: (layer,) + index(*g)), stacked
    return pl.BlockSpec(block, index), w


def _weight_cols(w):
    return w[0].shape[2] if isinstance(w, tuple) else w.shape[1]


def _row_sources(x, width):
    (ctx, ctx_first), (lat, lat_first) = (((x[0], 0), (x[1], 0)) if isinstance(x, tuple)
                                          else ((x, 0), (x, N_CTX_TILES)))

    def spec(first, lo, hi):
        return pl.BlockSpec((TM, width), lambda i, *_: (jnp.clip(i, lo, hi) - lo + first, 0))

    return ([spec(ctx_first, 0, N_CTX_TILES - 1), spec(lat_first, N_CTX_TILES, N_TOK // TM - 1)], [ctx, lat])


def _mm_res_kernel(ac_ref, al_ref, w_ref, xc_ref, xl_ref, mod_ref, o_ref):
    def emit(a_ref, x_ref):
        acc = jnp.dot(a_ref[...], w_ref[...], preferred_element_type=F32)
        o_ref[...] = x_ref[...] + mod_ref[2:3, :] * acc

    is_ctx = pl.program_id(0) < N_CTX_TILES
    pl.when(is_ctx)(functools.partial(emit, ac_ref, xc_ref))
    pl.when(jnp.logical_not(is_ctx))(functools.partial(emit, al_ref, xl_ref))


def _mm_res(a_ctx, a_lat, w, x, mod):
    k = a_ctx.shape[1]
    a_specs, a_arrays = _row_sources((a_ctx, a_lat), k)
    x_specs, x_arrays = _row_sources(x, D_MODEL)
    w_spec, w = _weight(w, (k, D_MODEL), lambda i: (0, 0))
    return pl.pallas_call(
        _mm_res_kernel,
        grid=(N_TOK // TM,),
        in_specs=a_specs + [w_spec] + x_specs
        + [pl.BlockSpec((None, 6, D_MODEL), lambda i: (_mod_row(i), 0, 0))],
        out_specs=pl.BlockSpec((TM, D_MODEL), lambda i: (i, 0)),
        out_shape=jax.ShapeDtypeStruct((N_TOK, D_MODEL), F32),
        compiler_params=_params("arbitrary"),
        name="mm_res",
    )(*a_arrays, w, *x_arrays, mod)


FFN_TF = 256
FFN_BLOCKS = D_FF // FFN_TF


def _ffn_kernel(*refs, final_norm):
    if final_norm:
        (x_ref, mod_ref, nw_ref, wup_ref, cw_ref, cb_ref, wd_ref, fw_ref, oc_ref, ol_ref,
         h_scr, acc_scr, raw_a, raw_b) = refs
    else:
        x_ref, mod_ref, nw_ref, wup_ref, cw_ref, cb_ref, wd_ref, o_ref, h_scr, acc_scr, raw_a, raw_b = refs
    nb = FFN_BLOCKS
    h_scr[...] = _norm_mod(x_ref[...], nw_ref[...], mod_ref[3:4, :], mod_ref[4:5, :]).astype(BF16)
    acc_scr[...] = jnp.zeros_like(acc_scr)

    def run(seq):
        def project(k, raw):
            raw[...] = jnp.dot(h_scr[...], wup_ref[k], preferred_element_type=F32)

        def consume(k, raw):
            u = _dwconv_rows(raw[...], cw_ref[k], cb_ref[k], seq)
            a = (_silu(u[:, :FFN_TF]) * u[:, FFN_TF:]).astype(BF16)
            acc_scr[...] += jnp.dot(a, wd_ref[k], preferred_element_type=F32)

        project(0, raw_a)

        def two_stages(t, carry):
            k = 2 * t
            project(k + 1, raw_b)
            consume(k, raw_a)
            project(k + 2, raw_a)
            consume(k + 1, raw_b)
            return carry

        lax.fori_loop(0, (nb - 1) // 2, two_stages, 0)
        consume(nb - 1, raw_a)

    _by_tile_kind(pl.program_id(0), run)
    y = x_ref[...] + mod_ref[5:6, :] * acc_scr[...]
    if not final_norm:
        o_ref[...] = y
    else:
        y = y * lax.rsqrt(jnp.mean(y * y, axis=-1, keepdims=True) + NORM_EPS) * fw_ref[...]
        is_ctx = pl.program_id(0) < N_CTX_TILES

        @pl.when(is_ctx)
        def _():
            oc_ref[...] = y

        @pl.when(jnp.logical_not(is_ctx))
        def _():
            ol_ref[...] = y


def _column_blocks_kernel(*refs):
    *in_refs, o_ref = refs
    o_ref[...] = jnp.concatenate([r[...] for r in in_refs], axis=1).astype(o_ref.dtype)


def _column_blocks_bf16(w, width, groups):
    layers, r, cols = w.shape
    nb = cols // (groups * width)
    in_specs = [pl.BlockSpec((None, r, width), functools.partial(lambda l, k, g: (l, 0, g * nb + k), g=g))
                for g in range(groups)]
    return pl.pallas_call(
        _column_blocks_kernel,
        grid=(layers, nb),
        in_specs=in_specs,
        out_specs=pl.BlockSpec((None, None, r, groups * width), lambda l, k: (l, k, 0, 0)),
        out_shape=jax.ShapeDtypeStruct((layers, nb, r, groups * width), BF16),
        compiler_params=_params("arbitrary", "arbitrary"),
        name="column_blocks",
    )(*([w] * groups))


def _ffn_weight_blocks(w_up, conv_w, conv_b, w_down):
    nb = FFN_BLOCKS

    def blocks(a):
        r = a.shape[1]
        return a.reshape(DEPTH, r, 2, nb, FFN_TF).transpose(0, 3, 1, 2, 4).reshape(DEPTH, nb, r, 2 * FFN_TF)

    return (_column_blocks_bf16(w_up, FFN_TF, 2), blocks(conv_w), blocks(conv_b.reshape(DEPTH, 1, 2 * D_FF)),
            w_down.reshape(DEPTH, nb, FFN_TF, D_MODEL).astype(BF16))


FFN_FINAL_VMEM_LIMIT_BYTES = 58 * 1024 * 1024


def _conv_ffn(x, mod, nw, layer, w_up, conv_w, conv_b, w_down, final_w=None):
    nb = FFN_BLOCKS
    final = final_w is not None

    def resident(shape):
        return pl.BlockSpec((None,) + shape, lambda i: (layer,) + (0,) * len(shape), pipeline_mode=pl.Buffered(1))

    in_specs = [
        pl.BlockSpec((TM, D_MODEL), lambda i: (i, 0)),
        pl.BlockSpec((None, 6, D_MODEL), lambda i: (_mod_row(i), 0, 0)),
        pl.BlockSpec((1, D_MODEL), lambda i: (0, 0)),
        resident((nb, D_MODEL, 2 * FFN_TF)),
        resident((nb, 3, 2 * FFN_TF)),
        resident((nb, 1, 2 * FFN_TF)),
        resident((nb, FFN_TF, D_MODEL)),
    ]
    args = [x, mod, nw.reshape(1, D_MODEL), w_up, conv_w, conv_b, w_down]
    if final:
        in_specs.append(pl.BlockSpec((1, D_MODEL), lambda i: (0, 0)))
        args.append(final_w.reshape(1, D_MODEL))
        out_specs = [pl.BlockSpec((TM, D_MODEL), lambda i: (jnp.minimum(i, N_CTX_TILES - 1), 0)),
                     pl.BlockSpec((TM, D_MODEL), lambda i: (jnp.maximum(i - N_CTX_TILES, 0), 0))]
        out_shape = [jax.ShapeDtypeStruct((N_CTX, D_MODEL), F32), jax.ShapeDtypeStruct((N_LAT, D_MODEL), F32)]
    else:
        out_specs = pl.BlockSpec((TM, D_MODEL), lambda i: (i, 0))
        out_shape = jax.ShapeDtypeStruct((N_TOK, D_MODEL), F32)
    return pl.pallas_call(
        functools.partial(_ffn_kernel, final_norm=final),
        grid=(N_TOK // TM,),
        in_specs=in_specs,
        out_specs=out_specs,
        out_shape=out_shape,
        scratch_shapes=[pltpu.VMEM((TM, D_MODEL), BF16), pltpu.VMEM((TM, D_MODEL), F32),
                        pltpu.VMEM((TM, 2 * FFN_TF), F32), pltpu.VMEM((TM, 2 * FFN_TF), F32)],
        compiler_params=_params("arbitrary",
                                vmem_limit_bytes=FFN_FINAL_VMEM_LIMIT_BYTES if final else VMEM_LIMIT_BYTES),
        name="conv_ffn",
    )(*args)


ML_TN = 512
ML_EXT = ML_DV + LANES
ML_SEQ_PER_STEP = 2


ML_BLOCKS = ML_D_IN // ML_TN
ML_UP_VMEM_LIMIT_BYTES = 58 * 1024 * 1024


def _ml_up_kernel(xc_src_ref, xl_src_ref, mod_ref, nw_ref, w_ref, cw_ref, cb_ref, xm_ref, xc_ref, z_ref,
                  h_scr, raw_a, raw_b):
    nb = ML_BLOCKS
    i = pl.program_id(0)

    def normalise(x_ref):
        h = _norm_mod(x_ref[...], nw_ref[...], mod_ref[0:1, :], mod_ref[1:2, :])
        h_scr[...] = h.astype(BF16)

    is_ctx = i < N_CTX_TILES
    pl.when(is_ctx)(functools.partial(normalise, xc_src_ref))
    pl.when(jnp.logical_not(is_ctx))(functools.partial(normalise, xl_src_ref))

    def run(seq):
        def project(k, raw):
            h = h_scr[...]
            raw[...] = jnp.dot(h, w_ref[k], preferred_element_type=F32)
            z_ref[:, k * ML_TN:(k + 1) * ML_TN] = jnp.dot(h, w_ref[nb + k], preferred_element_type=F32).astype(BF16)

        def consume(k, raw):
            cols = slice(k * ML_TN, (k + 1) * ML_TN)
            xm = raw[...]
            xm_ref[:, cols] = xm.astype(BF16)
            xc_ref[:, cols] = _silu(_dwconv_rows(xm, cw_ref[:, cols], cb_ref[:, cols], seq)).astype(BF16)

        raws = (raw_a, raw_b)
        project(0, raws[0])
        for k in range(nb):
            if k + 1 < nb:
                project(k + 1, raws[(k + 1) % 2])
            consume(k, raws[k % 2])

    _by_tile_kind(i, run)


def _ml_up_weight_blocks(w_up):
    return _column_blocks_bf16(w_up, ML_TN, 1)


def _ml_up(x, mod, nw, w_up, conv_w, conv_b):
    nb = ML_BLOCKS
    x_specs, x_arrays = _row_sources(x, D_MODEL)
    w_blocks, layer = w_up
    rows = pl.BlockSpec((TM, ML_D_IN), lambda i: (i, 0))
    shp = jax.ShapeDtypeStruct((N_TOK, ML_D_IN), BF16)
    return pl.pallas_call(
        _ml_up_kernel,
        grid=(N_TOK // TM,),
        in_specs=x_specs + [
            pl.BlockSpec((None, 6, D_MODEL), lambda i: (_mod_row(i), 0, 0)),
            pl.BlockSpec((1, D_MODEL), lambda i: (0, 0)),
            pl.BlockSpec((None, 2 * nb, D_MODEL, ML_TN), lambda i: (layer, 0, 0, 0), pipeline_mode=pl.Buffered(1)),
            pl.BlockSpec((3, ML_D_IN), lambda i: (0, 0)),
            pl.BlockSpec((1, ML_D_IN), lambda i: (0, 0)),
        ],
        out_specs=[rows, rows, rows],
        out_shape=[shp, shp, shp],
        scratch_shapes=[pltpu.VMEM((TM, D_MODEL), BF16), pltpu.VMEM((TM, ML_TN), F32), pltpu.VMEM((TM, ML_TN), F32)],
        compiler_params=_params("arbitrary", vmem_limit_bytes=ML_UP_VMEM_LIMIT_BYTES),
        name="ml_up",
    )(*x_arrays, mod, nw.reshape(1, D_MODEL), w_blocks, conv_w, conv_b.reshape(1, ML_D_IN))


def _log_sigmoid(x):
    return jnp.minimum(x, 0.0) - jnp.log(1.0 + jnp.exp(-jnp.abs(x)))


def _ml_scan_kernel(*refs, seq, n_seq, has_init, out_state, n_unused_inputs, fill_layer=None):
    q_ref, kt_ref, v_ref, g_ref, gt_ref, xc_ref, z_ref, nw_ref, skip_ref = refs[:9]
    pos = 9
    if has_init:
        c0_ref, n0_ref, m0_ref = refs[pos:pos + 3]
        pos += 3
    pos += n_unused_inputs
    o_ref = refs[pos]
    pos += 1
    if out_state:
        cout_ref, nout_ref, mout_ref = refs[pos:pos + 3]
        pos += 3
    hs_scr, c_scr = refs[pos:pos + 2]

    L = ML_CHUNK
    nc = seq // L
    ext_tiles = ML_EXT // LANES
    row_i = lax.broadcasted_iota(jnp.int32, (L, L), 0)
    col_i = lax.broadcasted_iota(jnp.int32, (L, L), 1)
    ones_tile = jnp.ones((L, LANES), BF16)

    def lanes(a, n):
        return jnp.concatenate([a] * n, axis=1)

    for d in (0, 1):
        keep = (col_i <= row_i) if d == 0 else (col_i >= row_i)
        keep_f = keep.astype(F32)
        keep_t_f = ((row_i <= col_i) if d == 0 else (row_i >= col_i)).astype(F32)
        end = L - 1 if d == 0 else 0

        ms = []
        for s in range(n_seq):
            if has_init:
                c_scr[s, :, :ML_DV] = c0_ref[s, d]
                c_scr[s, :, ML_DV:] = n0_ref[s, d]
                ms.append(m0_ref[s, d:d + 1, :])
            else:
                c_scr[s] = jnp.zeros((ML_DK, ML_EXT), F32)
                ms.append(jnp.zeros((1, LANES), F32))

        for c, s in [(c, s) for c in range(nc) for s in range(n_seq)]:
            cc = c if d == 0 else nc - 1 - c
            chunk = s * nc + cc
            rows = slice(chunk * L, (chunk + 1) * L)
            m = ms[s]
            qc = q_ref[rows, :]
            ktc = kt_ref[chunk]
            v_ext = jnp.concatenate([v_ref[rows, :], ones_tile], axis=1)
            gcol = g_ref[chunk]
            grow = gt_ref[chunk]
            ig_row = grow[2 * d:2 * d + 1, :]
            lf_col = _log_sigmoid(gcol[:, 2 * d + 1:2 * d + 2])
            lf_row = _log_sigmoid(grow[2 * d + 1:2 * d + 2, :])
            b_col = jnp.broadcast_to(jnp.sum(keep_f * lf_row, axis=1, keepdims=True), (L, LANES))
            b_row = jnp.sum(keep_t_f * lf_col, axis=0, keepdims=True)
            dmat = jnp.where(keep, b_col - b_row + ig_row, NEG_INF)
            m_loc = jnp.broadcast_to(jnp.max(dmat, axis=1, keepdims=True), (L, LANES))
            p_loc = jnp.exp(dmat - m_loc)
            s_loc = jnp.dot(qc, ktc, preferred_element_type=F32) * p_loc
            intra = jnp.dot(s_loc.astype(BF16), v_ext, preferred_element_type=F32)

            m_t = jnp.maximum(b_col + m, m_loc)
            w_inter = jnp.exp(b_col + m - m_t)
            w_intra = jnp.exp(m_loc - m_t)
            inter = jnp.dot(qc, c_scr[s].astype(BF16), preferred_element_type=F32)
            hx = lanes(w_inter, ext_tiles) * inter + lanes(w_intra, ext_tiles) * intra
            inv = 1.0 / jnp.maximum(jnp.abs(hx[:, ML_DV:]), jnp.exp(-m_t))
            h = hx[:, :ML_DV] * lanes(inv, ML_DV // LANES)

            b_end = b_col[end:end + 1, :]
            m_loc_end = m_loc[end:end + 1, :]
            m_new = jnp.maximum(b_end + m, m_loc_end)
            w_c = jnp.exp(b_end + m - m_new)
            w_s = p_loc[end:end + 1, :] * jnp.exp(m_loc_end - m_new)
            upd = jnp.dot((ktc.astype(F32) * w_s).astype(BF16), v_ext, preferred_element_type=F32)
            c_scr[s] = lanes(w_c, ext_tiles) * c_scr[s] + upd
            ms[s] = m_new

            if d == 0:
                hs_scr[rows, :] = h
            else:
                hs = hs_scr[rows, :] + h
                mu = jnp.mean(hs, axis=1, keepdims=True)
                cen = hs - mu
                var = jnp.mean(cen * cen, axis=1, keepdims=True)
                hn = cen * lax.rsqrt(var + NORM_EPS) * nw_ref[...]
                xc = xc_ref[rows, :].astype(F32)
                z = z_ref[rows, :].astype(F32)
                o_ref[rows, :] = ((hn + skip_ref[...] * xc) * _silu(z)).astype(o_ref.dtype)

        if out_state:
            for s in range(n_seq):
                if fill_layer is None:
                    cout_ref[s, d] = c_scr[s, :, :ML_DV]
                else:
                    for layer in range(cout_ref.shape[1]):
                        cout_ref[s, layer, d] = (c_scr[s, :, :ML_DV] if layer == fill_layer
                                                 else jnp.zeros((ML_DK, ML_DV), F32))
                nout_ref[s, d:d + 1, :] = c_scr[s, :, ML_DV:].T[0:1, :]
                mout_ref[s, d:d + 1, :] = ms[s]


def _ml_scan(q, kt, v, g4, g4t, xc, z, norm_w, skip, *, seq, nb, row0, init=None, final_c=None,
             n_seq=ML_SEQ_PER_STEP):
    L = ML_CHUNK
    rows = n_seq * seq
    nc = rows // L
    rb = row0 // rows
    has_init = init is not None
    in_specs = [
        pl.BlockSpec((rows, ML_DK), lambda b, h: (rb + b, h)),
        pl.BlockSpec((None, nc, ML_DK, L), lambda b, h: (h, rb + b, 0, 0)),
        pl.BlockSpec((rows, ML_DV), lambda b, h: (rb + b, h)),
        pl.BlockSpec((None, nc, L, 4), lambda b, h: (h, rb + b, 0, 0)),
        pl.BlockSpec((None, nc, 4, L), lambda b, h: (h, rb + b, 0, 0)),
        pl.BlockSpec((rows, ML_DV), lambda b, h: (rb + b, h)),
        pl.BlockSpec((rows, ML_DV), lambda b, h: (rb + b, h)),
        pl.BlockSpec((1, ML_DV), lambda b, h: (0, h)),
        pl.BlockSpec((1, ML_DV), lambda b, h: (0, h)),
    ]
    args = [q, kt, v, g4, g4t, xc, z, norm_w.reshape(1, ML_D_IN), skip.reshape(1, ML_D_IN)]
    state_n = pl.BlockSpec((n_seq, None, 2, ML_DK, LANES), lambda b, h: (b, h, 0, 0, 0))
    state_m = pl.BlockSpec((n_seq, None, 2, LANES), lambda b, h: (b, h, 0, 0))
    if has_init:
        c0, layer, n0, m0 = init
        in_specs += [pl.BlockSpec((n_seq, None, 2, None, ML_DK, ML_DV), lambda b, h: (b, layer, 0, h, 0, 0)),
                     state_n, state_m]
        args += [c0, n0, m0]
    out_specs = [pl.BlockSpec((rows, ML_DV), lambda b, h: (b, h))]
    out_shape = [jax.ShapeDtypeStruct((nb * seq, ML_D_IN), BF16)]
    aliases = {}
    if not has_init:
        layer_out, c_all = final_c
        if c_all is None:
            c_spec = pl.BlockSpec((n_seq, N_ML_LAYERS, 2, None, ML_DK, ML_DV), lambda b, h: (b, 0, 0, h, 0, 0))
        else:
            c_spec = pl.BlockSpec((n_seq, None, 2, None, ML_DK, ML_DV), lambda b, h: (b, layer_out, 0, h, 0, 0))
        out_specs += [c_spec, pl.BlockSpec((n_seq, None, 2, ML_DK), lambda b, h: (b, h, 0, 0)), state_m]
        out_shape += [
            jax.ShapeDtypeStruct((nb, N_ML_LAYERS, 2, ML_HEADS, ML_DK, ML_DV), F32),
            jax.ShapeDtypeStruct((nb, ML_HEADS, 2, ML_DK), F32),
            jax.ShapeDtypeStruct((nb, ML_HEADS, 2, LANES), F32),
        ]
        if c_all is not None:
            in_specs.append(pl.BlockSpec(memory_space=pl.ANY))
            args.append(c_all)
            aliases = {len(args) - 1: 1}
    return pl.pallas_call(
        functools.partial(_ml_scan_kernel, seq=seq, n_seq=n_seq, has_init=has_init, out_state=not has_init,
                          n_unused_inputs=len(aliases),
                          fill_layer=final_c[0] if (not has_init and final_c[1] is None) else None),
        grid=(nb // n_seq, ML_HEADS),
        in_specs=in_specs,
        out_specs=out_specs,
        out_shape=out_shape,
        scratch_shapes=[pltpu.VMEM((rows, ML_DV), F32), pltpu.VMEM((n_seq, ML_DK, ML_EXT), F32)],
        input_output_aliases=aliases,
        compiler_params=_params("arbitrary", "arbitrary"),
        name="ml_scan_init" if has_init else "ml_scan_zero",
    )(*args)


def _ml_qk_kernel(xc_ref, w_ref, q_ref, kt_ref):
    acc = jnp.dot(xc_ref[...], w_ref[...], preferred_element_type=F32)
    j = pl.program_id(1)

    @pl.when(j == 0)
    def _():
        q_ref[...] = (acc * ML_DK ** -0.5).astype(BF16)

    @pl.when(j == 1)
    def _():
        for h in range(ML_HEADS):
            kt = acc[:, h * ML_DK:(h + 1) * ML_DK].T
            for c in range(TM // ML_CHUNK):
                kt_ref[h, c] = kt[:, c * ML_CHUNK:(c + 1) * ML_CHUNK].astype(BF16)


def _ml_qk(xc, w_qk):
    n = ML_HEADS * ML_DK
    cpt = TM // ML_CHUNK
    w_spec, w_qk = _weight(w_qk, (ML_D_IN, n), lambda i, j: (0, j))
    return pl.pallas_call(
        _ml_qk_kernel,
        grid=(N_TOK // TM, 2),
        in_specs=[pl.BlockSpec((TM, ML_D_IN), lambda i, j: (i, 0)), w_spec],
        out_specs=[pl.BlockSpec((TM, n), lambda i, j: (i, 0)),
                   pl.BlockSpec((ML_HEADS, cpt, ML_DK, ML_CHUNK), lambda i, j: (0, i, 0, 0))],
        out_shape=[jax.ShapeDtypeStruct((N_TOK, n), BF16),
                   jax.ShapeDtypeStruct((ML_HEADS, N_TOK // ML_CHUNK, ML_DK, ML_CHUNK), BF16)],
        compiler_params=_params("arbitrary", "arbitrary"),
        name="ml_qk",
    )(xc, w_qk)


def _mlstm_layer(x, mod, nw, p, state, final_c):
    xm, xc, z = _ml_up(x, mod, nw, p["w_up"], p["conv_w"], p["conv_b"])
    q, kt = _ml_qk(xc, p["w_qk"])
    v, g = _ml_vg(xm, p["w_v"], p["w_gate"], p["b_gate"], 1024)
    g = g[:, :4 * ML_HEADS]
    L = ML_CHUNK
    g4 = g.reshape(N_TOK // L, L, 4, ML_HEADS).transpose(3, 0, 1, 2)
    g4t = g4.transpose(0, 1, 3, 2)
    oc, c_fin, n_fin, m_fin = _ml_scan(q, kt, v, g4, g4t, xc, z, p["norm_w"], p["skip"],
                                       seq=SEQ, nb=BATCH, row0=0, final_c=final_c)
    (ol,) = _ml_scan(q, kt, v, g4, g4t, xc, z, p["norm_w"], p["skip"],
                     seq=DEC_SEQ, nb=DEC_BATCH, row0=N_CTX, init=state, n_seq=1)
    x = _mm_res(oc, ol, p["w_down"], x, mod)
    return x, c_fin, n_fin.transpose(0, 2, 1, 3), m_fin[..., 0].transpose(0, 2, 1)


def _softmax_parts(scores, sink_col):
    m = functools.reduce(jnp.maximum, [jnp.max(s, axis=1, keepdims=True) for s in scores])
    if sink_col is not None:
        m = jnp.maximum(m, sink_col)
    ps = [jnp.exp(s - m) for s in scores]
    den = functools.reduce(jnp.add, [jnp.sum(p, axis=1, keepdims=True) for p in ps])
    if sink_col is not None:
        den = den + jnp.exp(sink_col - m)
    return ps, den


def _qk(q, k):
    return lax.dot_general(q, k, (((1,), (1,)), ((), ())), preferred_element_type=F32) * (HEAD_DIM ** -0.5)


def _attend_groups(groups):
    r = groups[0][0][0].shape[0]
    lo = lax.broadcasted_iota(jnp.int32, (r, LANES), 1) < HEAD_DIM
    zero = jnp.zeros((r, LANES), BF16)
    all_scores = []
    for q_tiles, parts, _ in groups:
        qs = jnp.concatenate([jnp.where(sel, t, zero) for t in q_tiles for sel in (lo, jnp.logical_not(lo))], axis=0)
        all_scores.append([post(_qk(qs, k2)) for k2, _, post in parts])
    outs = []
    for scores, (q_tiles, parts, sinks) in zip(all_scores, groups):
        n_rows = scores[0].shape[0]
        sink_col = None
        if sinks is not None:
            rows = lax.broadcasted_iota(jnp.int32, (n_rows, 1), 0)
            sink_col = jnp.full((n_rows, 1), sinks[0], F32)
            for hi in range(1, len(sinks)):
                sink_col = jnp.where(rows >= hi * r, sinks[hi], sink_col)
        ps, den = _softmax_parts(scores, sink_col)
        o = functools.reduce(jnp.add, [jnp.dot(p.astype(BF16), v2, preferred_element_type=F32)
                                       for p, (_, v2, _) in zip(ps, parts)]) / den
        outs.append([jnp.where(lo, o[2 * a * r:(2 * a + 1) * r], o[(2 * a + 1) * r:(2 * a + 2) * r])
                     for a in range(len(q_tiles))])
    return outs


def _identity(s):
    return s


def _ctx_attn_kernel(*refs, tiles_per_kv, has_sink, kv_head_stride):
    if has_sink:
        sink_ref, q_ref, k_ref, v_ref, o_ref, kout_ref, vout_ref = refs
    else:
        q_ref, k_ref, v_ref, o_ref, kout_ref, vout_ref = refs
    for h in range(kout_ref.shape[0]):
        kout_ref[h] = k_ref[:, h * kv_head_stride:h * kv_head_stride + HEAD_DIM]
        vout_ref[h] = v_ref[:, h * kv_head_stride:h * kv_head_stride + HEAD_DIM]
    groups = []
    for t in range(k_ref.shape[1] // LANES):
        k2 = k_ref[:, t * LANES:(t + 1) * LANES].astype(BF16)
        v2 = v_ref[:, t * LANES:(t + 1) * LANES].astype(BF16)
        first = t * tiles_per_kv
        q_tiles = [q_ref[:, (first + a) * LANES:(first + a + 1) * LANES] for a in range(tiles_per_kv)]
        sinks = [sink_ref[2 * first + hi] for hi in range(2 * tiles_per_kv)] if has_sink else None
        groups.append((q_tiles, [(k2, v2, _identity)], sinks))
    for t, outs in enumerate(_attend_groups(groups)):
        for a, o in enumerate(outs):
            tile = t * tiles_per_kv + a
            o_ref[:, tile * LANES:(tile + 1) * LANES] = o.astype(o_ref.dtype)


def _ctx_attn(q, kv, n_kv_cols, n_kv_heads, sink):
    nq = q.shape[1]
    has_sink = sink is not None
    cache_spec = pl.BlockSpec((None, None, n_kv_heads, SEQ, HEAD_DIM), lambda b: (b, 0, 0, 0, 0))
    cache_shape = jax.ShapeDtypeStruct((BATCH, 1, n_kv_heads, SEQ, HEAD_DIM), F32)
    in_specs = [pl.BlockSpec((SEQ, nq), lambda b: (b, 0)),
                pl.BlockSpec((SEQ, n_kv_cols), lambda b: (b, 0)),
                pl.BlockSpec((SEQ, n_kv_cols), lambda b: (b, 1))]
    args = [q, kv, kv]
    if has_sink:
        in_specs = [pl.BlockSpec(memory_space=pltpu.SMEM)] + in_specs
        args = [sink] + args
    return pl.pallas_call(
        functools.partial(_ctx_attn_kernel, tiles_per_kv=nq // n_kv_cols, has_sink=has_sink,
                          kv_head_stride=n_kv_cols // n_kv_heads),
        grid=(BATCH,),
        in_specs=in_specs,
        out_specs=[pl.BlockSpec((SEQ, nq), lambda b: (b, 0)), cache_spec, cache_spec],
        out_shape=[jax.ShapeDtypeStruct((N_CTX, nq), BF16), cache_shape, cache_shape],
        compiler_params=_params("arbitrary"),
        name="ctx_attn",
    )(*args)


SWA_SPAN = Q_BLOCK + 2 * SWA_WINDOW
SWA_KV_COLS = SWA_KV * LANES


def _swa_lat_kernel(sink_ref, q_ref, k_ref, v_ref, kc_ref, vc_ref, o_ref):
    j = pl.program_id(1)
    start = pl.multiple_of(jnp.clip((j - 1) * Q_BLOCK, 0, DEC_SEQ - SWA_SPAN), Q_BLOCK)
    rows = 4 * Q_BLOCK
    qpos = j * Q_BLOCK + (lax.broadcasted_iota(jnp.int32, (rows, SWA_SPAN), 0) & (Q_BLOCK - 1))
    kpos = start + lax.broadcasted_iota(jnp.int32, (rows, SWA_SPAN), 1)
    in_window = jnp.abs(qpos - kpos) <= SWA_WINDOW

    def window(s):
        return jnp.where(in_window, s, NEG_INF)

    groups = []
    for t in range(SWA_KV):
        cols = slice(t * LANES, (t + 1) * LANES)
        k_loc = k_ref[pl.ds(start, SWA_SPAN), cols].astype(BF16)
        v_loc = v_ref[pl.ds(start, SWA_SPAN), cols].astype(BF16)
        q_tiles = [q_ref[:, (2 * t + a) * LANES:(2 * t + a + 1) * LANES] for a in range(2)]
        sinks = [sink_ref[4 * t + hi] for hi in range(4)]
        groups.append((q_tiles, [(k_loc, v_loc, window), (kc_ref[:, cols], vc_ref[:, cols], _identity)], sinks))
    for t, outs in enumerate(_attend_groups(groups)):
        for a, o in enumerate(outs):
            o_ref[:, (2 * t + a) * LANES:(2 * t + a + 1) * LANES] = o.astype(o_ref.dtype)


def _swa_latent(q, kv, kc, vc, sink):
    nq = q.shape[1]
    rb = N_CTX // DEC_SEQ
    qb = N_CTX // Q_BLOCK
    nj = DEC_SEQ // Q_BLOCK
    cspec = pl.BlockSpec((None, SEQ, SWA_KV_COLS), lambda b, j: (b, 0, 0))
    return pl.pallas_call(
        _swa_lat_kernel,
        grid=(DEC_BATCH, nj),
        in_specs=[pl.BlockSpec(memory_space=pltpu.SMEM),
                  pl.BlockSpec((Q_BLOCK, nq), lambda b, j: (qb + b * nj + j, 0)),
                  pl.BlockSpec((DEC_SEQ, SWA_KV_COLS), lambda b, j: (rb + b, 0)),
                  pl.BlockSpec((DEC_SEQ, SWA_KV_COLS), lambda b, j: (rb + b, 1)),
                  cspec, cspec],
        out_specs=pl.BlockSpec((Q_BLOCK, nq), lambda b, j: (b * nj + j, 0)),
        out_shape=jax.ShapeDtypeStruct((N_LAT, nq), BF16),
        compiler_params=_params("arbitrary", "arbitrary"),
        name="swa_latent",
    )(sink, q, kv, kv, kc, vc)


NA_QT = 256
NA_SPAN = 768
NA_TILES = 4


def _na_start(j):
    return (j // 2) * (DEC_SEQ - NA_SPAN)


NA_ROWS = DEC_SEQ // GRID_W
NA_DR = 2 * NA_KH - 1
NA_DC = 2 * NA_KW - 1


def _na_blocks_kernel(rpb_ref, onehot_ref, valid_ref, o_ref):
    t = jnp.dot(rpb_ref[...], onehot_ref[...], preferred_element_type=F32, precision=lax.Precision.HIGHEST)
    o_ref[...] = jnp.where(valid_ref[...] > 0.5, t, NEG_INF)


def _na_bias_blocks(rpb):
    h = rpb.shape[0]
    kpad = 32
    cq, ck = np.meshgrid(np.arange(GRID_W), np.arange(GRID_W), indexing="ij")
    dc = (np.clip(ck - cq, -(NA_KW - 1), NA_KW - 1) + NA_KW - 1).reshape(-1)
    cs = np.clip(cq - NA_KW // 2, 0, GRID_W - NA_KW)
    valid = ((ck >= cs) & (ck < cs + NA_KW)).reshape(1, -1).astype(np.float32)
    onehot = (np.arange(kpad)[:, None] == dc[None, :]).astype(np.float32)
    rpb2 = jnp.pad(rpb.reshape(h * NA_DR, NA_DC), ((0, 0), (0, kpad - NA_DC)))
    n = GRID_W * GRID_W
    blocks = pl.pallas_call(
        _na_blocks_kernel,
        grid=(1,),
        in_specs=[pl.BlockSpec((h * NA_DR, kpad), lambda i: (0, 0)),
                  pl.BlockSpec((kpad, n), lambda i: (0, 0)),
                  pl.BlockSpec((1, n), lambda i: (0, 0))],
        out_specs=pl.BlockSpec((h * NA_DR, n), lambda i: (0, 0)),
        out_shape=jax.ShapeDtypeStruct((h * NA_DR, n), F32),
        compiler_params=_params("arbitrary"),
        name="na_bias_blocks",
    )(rpb2, jnp.asarray(onehot), jnp.asarray(valid))
    blocks = blocks.reshape(h, NA_DR, GRID_W, GRID_W)
    padded = jnp.pad(blocks, ((0, 0), (1, 1), (0, 0), (0, 0)), constant_values=NEG_INF)
    return jnp.concatenate([padded[:, :-1], padded[:, 1:]], axis=-1)


def _na_lat_kernel(q_ref, k_ref, v_ref, kc_ref, vc_ref, blk_ref, o_ref, bias_scr):
    j = pl.program_id(0)
    start = pl.multiple_of(_na_start(j), 256)

    @pl.when(pl.program_id(2) == 0)
    def _():
        lane_lo = lax.broadcasted_iota(jnp.int32, (GRID_W, LANES), 1) < GRID_W
        for rq_l in range(NA_QT // GRID_W):
            rq = j * (NA_QT // GRID_W) + rq_l
            rs = jnp.clip(rq - NA_KH // 2, 0, NA_ROWS - NA_KH)
            for kp in range(NA_SPAN // LANES):
                rk = start // GRID_W + 2 * kp
                idx = jnp.clip(rk - rq + NA_KH, 0, NA_DR)
                in_band = [jnp.logical_and(r >= rs, r < rs + NA_KH).astype(jnp.int32) for r in (rk, rk + 1)]
                ok = jnp.where(lane_lo, in_band[0], in_band[1]) > 0
                for hh in range(2 * NA_TILES):
                    bias_scr[hh * NA_QT + rq_l * GRID_W:hh * NA_QT + (rq_l + 1) * GRID_W,
                             kp * LANES:(kp + 1) * LANES] = jnp.where(ok, blk_ref[hh, idx], NEG_INF)

    groups = []
    for t in range(NA_TILES):
        cols = slice(t * LANES, (t + 1) * LANES)

        def add_bias(s, t=t):
            return s + bias_scr[2 * t * NA_QT:2 * (t + 1) * NA_QT, :]

        k_loc = k_ref[pl.ds(start, NA_SPAN), cols].astype(BF16)
        v_loc = v_ref[pl.ds(start, NA_SPAN), cols].astype(BF16)
        groups.append(([q_ref[:, cols]], [(k_loc, v_loc, add_bias), (kc_ref[:, cols], vc_ref[:, cols], _identity)],
                       None))
    for t, (o,) in enumerate(_attend_groups(groups)):
        o_ref[:, t * LANES:(t + 1) * LANES] = o.astype(o_ref.dtype)


def _na_latent(q, kv, kc, vc, blocks):
    nq = q.shape[1]
    width = NA_TILES * LANES
    n_steps = nq // width
    nj = DEC_SEQ // NA_QT
    rb = N_CTX // DEC_SEQ
    qb = N_CTX // NA_QT
    cspec = pl.BlockSpec((None, SEQ, width), lambda j, p, b: (b, 0, p))
    return pl.pallas_call(
        _na_lat_kernel,
        grid=(nj, n_steps, DEC_BATCH),
        in_specs=[pl.BlockSpec((NA_QT, width), lambda j, p, b: (qb + b * nj + j, p)),
                  pl.BlockSpec((DEC_SEQ, width), lambda j, p, b: (rb + b, p)),
                  pl.BlockSpec((DEC_SEQ, width), lambda j, p, b: (rb + b, n_steps + p)),
                  cspec, cspec,
                  pl.BlockSpec((2 * NA_TILES, NA_DR + 1, GRID_W, LANES), lambda j, p, b: (p, 0, 0, 0))],
        out_specs=pl.BlockSpec((NA_QT, width), lambda j, p, b: (b * nj + j, p)),
        out_shape=jax.ShapeDtypeStruct((N_LAT, nq), BF16),
        scratch_shapes=[pltpu.VMEM((2 * NA_TILES * NA_QT, NA_SPAN), F32)],
        compiler_params=_params("arbitrary", "arbitrary", "arbitrary"),
        name="na_latent",
    )(q, kv, kv, kc, vc, blocks)


def _rope_tables(width):
    quarter = HEAD_DIM // 4
    pos = np.arange(DEC_SEQ)
    inv = np.power(ROPE_BASE, -np.arange(quarter, dtype=np.float32) / quarter).astype(np.float32)
    d = np.arange(width) % HEAD_DIM
    p = np.where((d < HEAD_DIM // 2)[None, :], (pos // GRID_W)[:, None], (pos % GRID_W)[:, None]).astype(np.float32)
    ang = p * inv[d % quarter][None, :]
    sign = np.where((d // quarter) % 2 == 0, -1.0, 1.0)[None, :]
    return jnp.asarray(np.cos(ang), F32), jnp.asarray(np.sin(ang) * sign, F32)


def _cache_rows(cache, dup):
    b, h, s, hd = cache.shape
    rows = jnp.broadcast_to(cache.transpose(0, 2, 1, 3)[:, :, :, None, :], (b, s, h, dup, hd))
    return rows.reshape(b, s, h * dup * hd).astype(BF16)


def _swa_layer(x, mod, nw, w_qkv, sink, w_o, cache_k, cache_v):
    nq, nk = SWA_HEADS * HEAD_DIM, SWA_KV * HEAD_DIM

    def dup_heads(w):
        return jnp.broadcast_to(w.reshape(D_MODEL, SWA_KV, 1, HEAD_DIM),
                                (D_MODEL, SWA_KV, 2, HEAD_DIM)).reshape(D_MODEL, SWA_KV_COLS)

    w = jnp.concatenate([w_qkv[:, :nq], dup_heads(w_qkv[:, nq:nq + nk]), dup_heads(w_qkv[:, nq + nk:])], axis=1)
    q, kv = _qkv_proj(x, mod, nw, w.astype(BF16), nq, rope_cols=nq + SWA_KV_COLS)
    oc, k_new, v_new = _ctx_attn(q, kv, SWA_KV_COLS, SWA_KV, sink)
    ol = _swa_latent(q, kv, _cache_rows(cache_k, 2), _cache_rows(cache_v, 2), sink)
    return _mm_res(oc, ol, w_o, x, mod), k_new, v_new


def _na_layer(x, mod, nw, w_qkv, rpb, w_o, cache_k, cache_v):
    n = NA_HEADS * HEAD_DIM
    q, kv = _qkv_proj(x, mod, nw, w_qkv, n)
    oc, k_new, v_new = _ctx_attn(q, kv, n, NA_HEADS, None)
    ol = _na_latent(q, kv, _cache_rows(cache_k, 1), _cache_rows(cache_v, 1), _na_bias_blocks(rpb))
    return _mm_res(oc, ol, w_o, x, mod), k_new, v_new


def kernel(x_prompt, x_sample, state_mlstm_C, state_mlstm_n, state_mlstm_m, cache_swa_k, cache_swa_v, cache_na_k, cache_na_v, c, c_ctx, ada_w, ada_b, norm_w, final_norm_w, ffn_w_up, ffn_conv_w, ffn_conv_b, ffn_w_down, ml_w_up, ml_conv_w, ml_conv_b, ml_w_qk, ml_w_v, ml_w_gate, ml_b_gate, ml_norm_w, ml_skip, ml_w_down, swa_w_qkv, swa_sink, swa_w_o, na_w_qkv, na_rpb, na_w_o):
    x = (x_prompt.reshape(N_CTX, D_MODEL), x_sample.reshape(N_LAT, D_MODEL))
    cond = jnp.concatenate([c_ctx[None], c, jnp.zeros((MOD_ROWS - 1 - DEC_BATCH, D_MODEL), F32)], axis=0)
    mods = _ada_mod(cond, ada_w, ada_b)
    ffn_weights = _ffn_weight_blocks(ffn_w_up, ffn_conv_w, ffn_conv_b, ffn_w_down)
    ml_up_blocks = _ml_up_weight_blocks(ml_w_up)
    ml_qk_bf16, ml_v_bf16, ml_down_bf16 = ml_w_qk.astype(BF16), ml_w_v.astype(BF16), ml_w_down.astype(BF16)

    new_c, new_n, new_m = None, [], []
    new_sk = new_sv = new_nk = new_nv = None
    for i in range(DEPTH):
        kind, j = i % N_MIXERS, i // N_MIXERS
        mod = mods[i]
        if kind == 0:
            gate_w = jnp.pad(ml_w_gate[j], ((0, 0), (0, 128 - 4 * ML_HEADS))).astype(BF16)
            gate_b = jnp.pad(ml_b_gate[j], (0, 128 - 4 * ML_HEADS)).reshape(1, 128)
            p = dict(w_up=(ml_up_blocks, j), conv_w=ml_conv_w[j], conv_b=ml_conv_b[j],
                     w_qk=(ml_qk_bf16, j), w_v=(ml_v_bf16, j), w_gate=gate_w, b_gate=gate_b,
                     norm_w=ml_norm_w[j], skip=ml_skip[j], w_down=(ml_down_bf16, j))
            n0 = jnp.broadcast_to(state_mlstm_n[:, j].transpose(0, 2, 1, 3)[..., None],
                                  (DEC_BATCH, ML_HEADS, 2, ML_DK, LANES))
            m0 = jnp.broadcast_to(state_mlstm_m[:, j].transpose(0, 2, 1)[..., None], (DEC_BATCH, ML_HEADS, 2, LANES))
            x, new_c, nf, mf = _mlstm_layer(x, mod, norm_w[i, 0], p, (state_mlstm_C, j, n0, m0), (j, new_c))
            new_n.append(nf)
            new_m.append(mf)
        elif kind == 1:
            x, k_new, v_new = _swa_layer(x, mod, norm_w[i, 0], swa_w_qkv[j].astype(BF16), swa_sink[j],
                                         swa_w_o[j].astype(BF16), cache_swa_k[:, j], cache_swa_v[:, j])
            new_sk, new_sv = k_new, v_new
        else:
            x, k_new, v_new = _na_layer(x, mod, norm_w[i, 0], na_w_qkv[j].astype(BF16), na_rpb[j],
                                        na_w_o[j].astype(BF16), cache_na_k[:, j], cache_na_v[:, j])
            new_nk, new_nv = k_new, v_new
        x = _conv_ffn(x, mod, norm_w[i, 1], i, *ffn_weights, final_w=final_norm_w if i == DEPTH - 1 else None)

    y_ctx, y_lat = x
    return (y_ctx.reshape(BATCH, SEQ, D_MODEL), y_lat.reshape(DEC_BATCH, DEC_SEQ, D_MODEL),
            new_c, jnp.stack(new_n, axis=1), jnp.stack(new_m, axis=1),
            new_sk, new_sv, new_nk, new_nv)
```

```python
import functools

import jax
import jax.numpy as jnp
import numpy as np
from jax import lax
from jax.experimental import pallas as pl
from jax.experimental.pallas import tpu as pltpu

F32 = jnp.float32
BF16 = jnp.bfloat16

D_MODEL = 1024
BATCH = 32
SEQ = 256
DEPTH = 4
DEC_BATCH = 8
DEC_SEQ = 1024
GRID_W = 64
N_MIXERS = 3
N_ML_LAYERS = (DEPTH + 2) // 3
NORM_EPS = 1e-6
D_FF = 2816
ML_D_IN = 2 * D_MODEL
ML_HEADS = 4
ML_DK = ML_D_IN // (2 * ML_HEADS)
ML_DV = ML_D_IN // ML_HEADS
ML_CHUNK = 128
HEAD_DIM = 64
SWA_HEADS = D_MODEL // HEAD_DIM
SWA_KV = SWA_HEADS // 4
SWA_WINDOW = 128
Q_BLOCK = 128
ROPE_BASE = 10000.0
NA_HEADS = D_MODEL // HEAD_DIM
NA_KH = 8
NA_KW = 16
NEG_INF = -1e30

N_CTX = BATCH * SEQ
N_LAT = DEC_BATCH * DEC_SEQ
N_TOK = N_CTX + N_LAT
TM = 1024
N_CTX_TILES = N_CTX // TM
MOD_ROWS = 16
VMEM_LIMIT_BYTES = 56 * 1024 * 1024


def _params(*sem, vmem_limit_bytes=VMEM_LIMIT_BYTES):
    return pltpu.CompilerParams(dimension_semantics=sem, vmem_limit_bytes=vmem_limit_bytes)


def _mod_row(i):
    return jnp.where(i < N_CTX_TILES, 0, i - (N_CTX_TILES - 1))


def _silu(x):
    return x / (1.0 + jnp.exp(-x))


def _norm_mod(x, nw, shift, scale):
    y = x * lax.rsqrt(jnp.mean(x * x, axis=-1, keepdims=True) + NORM_EPS) * nw
    return y * (1.0 + scale) + shift


SUBLANES = 8
LANES = 128


def _dwconv_rows(u, cw, cb, seq):
    r, c = u.shape
    n_groups, per_seq = r // SUBLANES, seq // SUBLANES
    g = u.reshape(n_groups, SUBLANES, c)
    sub = lax.broadcasted_iota(jnp.int32, g.shape, 1)
    down = pltpu.roll(g, 1, 1)
    up = pltpu.roll(g, SUBLANES - 1, 1)
    zero = jnp.zeros((1, SUBLANES, c), F32)
    from_prev, from_next = [], []
    for s in range(0, n_groups, per_seq):
        from_prev += [zero, down[s:s + per_seq - 1]]
        from_next += [up[s + 1:s + per_seq], zero]
    prev = jnp.where(sub == 0, jnp.concatenate(from_prev, axis=0), down)
    nxt = jnp.where(sub == SUBLANES - 1, jnp.concatenate(from_next, axis=0), up)
    out = cw[0:1, :] * prev + cw[1:2, :] * g + cw[2:3, :] * nxt + cb
    return out.reshape(r, c)


def _by_tile_kind(tile, body):
    pl.when(tile < N_CTX_TILES)(functools.partial(body, SEQ))
    pl.when(tile >= N_CTX_TILES)(functools.partial(body, DEC_SEQ))


def _ada_kernel(c_ref, w_ref, b_ref, o_ref):
    s = _silu(c_ref[...]).astype(BF16)
    o_ref[...] = jnp.dot(s, w_ref[...].astype(BF16), preferred_element_type=F32) + b_ref[...]


def _ada_mod(cond, ada_w, ada_b):
    tn = 3072
    n = 6 * D_MODEL
    out = pl.pallas_call(
        _ada_kernel,
        grid=(DEPTH, n // tn),
        in_specs=[
            pl.BlockSpec((MOD_ROWS, D_MODEL), lambda l, j: (0, 0)),
            pl.BlockSpec((None, D_MODEL, tn), lambda l, j: (l, 0, j)),
            pl.BlockSpec((None, 1, tn), lambda l, j: (l, 0, j)),
        ],
        out_specs=pl.BlockSpec((None, MOD_ROWS, tn), lambda l, j: (l, 0, j)),
        out_shape=jax.ShapeDtypeStruct((DEPTH, MOD_ROWS, n), F32),
        compiler_params=_params("arbitrary", "arbitrary"),
        name="ada_mod",
    )(cond, ada_w, ada_b.reshape(DEPTH, 1, n))
    return out.reshape(DEPTH, MOD_ROWS, 6, D_MODEL)


QKV_TN = 512
NA_QKV_TN = 1024


def _rotate_pairs(a, cos, sin):
    lane = lax.broadcasted_iota(jnp.int32, a.shape, 1)
    first = (lane & (HEAD_DIM // 4)) == 0
    n = a.shape[1]
    partner = jnp.where(first, pltpu.roll(a, n - HEAD_DIM // 4, 1), pltpu.roll(a, HEAD_DIM // 4, 1))
    return a * cos + partner * sin


def _qkv_kernel(*refs, q_blocks, rope_blocks):
    if rope_blocks:
        x_ref, mod_ref, nw_ref, w_ref, cos_ref, sin_ref, q_ref, kv_ref, h_scr = refs
    else:
        x_ref, mod_ref, nw_ref, w_ref, q_ref, kv_ref, h_scr = refs
    i = pl.program_id(0)
    j = pl.program_id(1)

    @pl.when(j == 0)
    def _():
        h = _norm_mod(x_ref[...], nw_ref[...], mod_ref[0:1, :], mod_ref[1:2, :])
        h_scr[...] = h.astype(BF16)

    acc = jnp.dot(h_scr[...], w_ref[...], preferred_element_type=F32)

    def emit(val):
        @pl.when(j < q_blocks)
        def _():
            q_ref[...] = val.astype(BF16)

        @pl.when(j >= q_blocks)
        def _():
            kv_ref[...] = val

    if rope_blocks:
        rotate = jnp.logical_and(i >= N_CTX_TILES, j < rope_blocks)

        @pl.when(rotate)
        def _():
            emit(_rotate_pairs(acc, cos_ref[...], sin_ref[...]))

        @pl.when(jnp.logical_not(rotate))
        def _():
            emit(acc)
    else:
        emit(acc)


def _qkv_proj(x, mod, nw, w, n_q, rope_cols=0, tn=QKV_TN):
    n = w.shape[1]
    q_blocks = n_q // tn
    rope_blocks = rope_cols // tn
    in_specs = [
        pl.BlockSpec((TM, D_MODEL), lambda i, j: (i, 0)),
        pl.BlockSpec((None, 6, D_MODEL), lambda i, j: (_mod_row(i), 0, 0)),
        pl.BlockSpec((1, D_MODEL), lambda i, j: (0, 0)),
        pl.BlockSpec((D_MODEL, tn), lambda i, j: (0, j)),
    ]
    args = [x, mod, nw.reshape(1, D_MODEL), w]
    if rope_blocks:
        tab = pl.BlockSpec((DEC_SEQ, tn), lambda i, j: (0, 0))
        in_specs += [tab, tab]
        args += list(_rope_tables(tn))
    return pl.pallas_call(
        functools.partial(_qkv_kernel, q_blocks=q_blocks, rope_blocks=rope_blocks),
        grid=(N_TOK // TM, n // tn),
        in_specs=in_specs,
        out_specs=[pl.BlockSpec((TM, tn), lambda i, j: (i, jnp.minimum(j, q_blocks - 1))),
                   pl.BlockSpec((TM, tn), lambda i, j: (i, jnp.maximum(j - q_blocks, 0)))],
        out_shape=[jax.ShapeDtypeStruct((N_TOK, n_q), BF16), jax.ShapeDtypeStruct((N_TOK, n - n_q), F32)],
        scratch_shapes=[pltpu.VMEM((TM, D_MODEL), BF16)],
        compiler_params=_params("arbitrary", "arbitrary"),
        name="qkv_proj",
    )(*args)


def _ml_vg_kernel(a_ref, wv_ref, wg_ref, bg_ref, v_ref, g_ref):
    a = a_ref[...]
    v_ref[...] = jnp.dot(a, wv_ref[...], preferred_element_type=F32).astype(v_ref.dtype)

    @pl.when(pl.program_id(1) == 0)
    def _():
        g_ref[...] = jnp.dot(a, wg_ref[...], preferred_element_type=F32) + bg_ref[...]


def _ml_vg(xm, w_v, w_gate, b_gate, tn):
    m, k = xm.shape
    n = _weight_cols(w_v)
    wv_spec, w_v = _weight(w_v, (k, tn), lambda i, j: (0, j))
    return pl.pallas_call(
        _ml_vg_kernel,
        grid=(m // TM, n // tn),
        in_specs=[pl.BlockSpec((TM, k), lambda i, j: (i, 0)), wv_spec,
                  pl.BlockSpec((k, LANES), lambda i, j: (0, 0)),
                  pl.BlockSpec((1, LANES), lambda i, j: (0, 0))],
        out_specs=[pl.BlockSpec((TM, tn), lambda i, j: (i, j)), pl.BlockSpec((TM, LANES), lambda i, j: (i, 0))],
        out_shape=[jax.ShapeDtypeStruct((m, n), BF16), jax.ShapeDtypeStruct((m, LANES), F32)],
        compiler_params=_params("arbitrary", "arbitrary"),
        name="ml_vg",
    )(xm, w_v, w_gate, b_gate)


def _weight(w, block, index):
    if isinstance(w, tuple):
        stacked, layer = w
        return pl.BlockSpec((None,) + block, lambda *g: (layer,) + index(*g)), stacked
    return pl.BlockSpec(block, index), w


def _weight_cols(w):
    return w[0].shape[2] if isinstance(w, tuple) else w.shape[1]


def _row_sources(x, width):
    (ctx, ctx_first), (lat, lat_first) = (((x[0], 0), (x[1], 0)) if isinstance(x, tuple)
                                          else ((x, 0), (x, N_CTX_TILES)))

    def spec(first, lo, hi):
        return pl.BlockSpec((TM, width), lambda i, *_: (jnp.clip(i, lo, hi) - lo + first, 0))

    return ([spec(ctx_first, 0, N_CTX_TILES - 1), spec(lat_first, N_CTX_TILES, N_TOK // TM - 1)], [ctx, lat])


def _mm_res_kernel(ac_ref, al_ref, w_ref, xc_ref, xl_ref, mod_ref, o_ref):
    def emit(a_ref, x_ref):
        acc = jnp.dot(a_ref[...], w_ref[...], preferred_element_type=F32)
        o_ref[...] = x_ref[...] + mod_ref[2:3, :] * acc

    is_ctx = pl.program_id(0) < N_CTX_TILES
    pl.when(is_ctx)(functools.partial(emit, ac_ref, xc_ref))
    pl.when(jnp.logical_not(is_ctx))(functools.partial(emit, al_ref, xl_ref))


def _mm_res(a_ctx, a_lat, w, x, mod):
    k = a_ctx.shape[1]
    a_specs, a_arrays = _row_sources((a_ctx, a_lat), k)
    x_specs, x_arrays = _row_sources(x, D_MODEL)
    w_spec, w = _weight(w, (k, D_MODEL), lambda i: (0, 0))
    return pl.pallas_call(
        _mm_res_kernel,
        grid=(N_TOK // TM,),
        in_specs=a_specs + [w_spec] + x_specs
        + [pl.BlockSpec((None, 6, D_MODEL), lambda i: (_mod_row(i), 0, 0))],
        out_specs=pl.BlockSpec((TM, D_MODEL), lambda i: (i, 0)),
        out_shape=jax.ShapeDtypeStruct((N_TOK, D_MODEL), F32),
        compiler_params=_params("arbitrary"),
        name="mm_res",
    )(*a_arrays, w, *x_arrays, mod)


FFN_TF = 256
FFN_BLOCKS = D_FF // FFN_TF


def _ffn_kernel(*refs, final_norm):
    if final_norm:
        (x_ref, mod_ref, nw_ref, wup_ref, cw_ref, cb_ref, wd_ref, fw_ref, oc_ref, ol_ref,
         h_scr, acc_scr, raw_a, raw_b) = refs
    else:
        x_ref, mod_ref, nw_ref, wup_ref, cw_ref, cb_ref, wd_ref, o_ref, h_scr, acc_scr, raw_a, raw_b = refs
    nb = FFN_BLOCKS
    h_scr[...] = _norm_mod(x_ref[...], nw_ref[...], mod_ref[3:4, :], mod_ref[4:5, :]).astype(BF16)
    acc_scr[...] = jnp.zeros_like(acc_scr)

    def run(seq):
        def project(k, raw):
            raw[...] = jnp.dot(h_scr[...], wup_ref[k], preferred_element_type=F32)

        def consume(k, raw):
            u = _dwconv_rows(raw[...], cw_ref[k], cb_ref[k], seq)
            a = (_silu(u[:, :FFN_TF]) * u[:, FFN_TF:]).astype(BF16)
            acc_scr[...] += jnp.dot(a, wd_ref[k], preferred_element_type=F32)

        project(0, raw_a)

        def two_stages(t, carry):
            k = 2 * t
            project(k + 1, raw_b)
            consume(k, raw_a)
            project(k + 2, raw_a)
            consume(k + 1, raw_b)
            return carry

        lax.fori_loop(0, (nb - 1) // 2, two_stages, 0)
        consume(nb - 1, raw_a)

    _by_tile_kind(pl.program_id(0), run)
    y = x_ref[...] + mod_ref[5:6, :] * acc_scr[...]
    if not final_norm:
        o_ref[...] = y
    else:
        y = y * lax.rsqrt(jnp.mean(y * y, axis=-1, keepdims=True) + NORM_EPS) * fw_ref[...]
        is_ctx = pl.program_id(0) < N_CTX_TILES

        @pl.when(is_ctx)
        def _():
            oc_ref[...] = y

        @pl.when(jnp.logical_not(is_ctx))
        def _():
            ol_ref[...] = y


def _column_blocks_kernel(*refs):
    *in_refs, o_ref = refs
    o_ref[...] = jnp.concatenate([r[...] for r in in_refs], axis=1).astype(o_ref.dtype)


def _column_blocks_bf16(w, width, groups):
    layers, r, cols = w.shape
    nb = cols // (groups * width)
    in_specs = [pl.BlockSpec((None, r, width), functools.partial(lambda l, k, g: (l, 0, g * nb + k), g=g))
                for g in range(groups)]
    return pl.pallas_call(
        _column_blocks_kernel,
        grid=(layers, nb),
        in_specs=in_specs,
        out_specs=pl.BlockSpec((None, None, r, groups * width), lambda l, k: (l, k, 0, 0)),
        out_shape=jax.ShapeDtypeStruct((layers, nb, r, groups * width), BF16),
        compiler_params=_params("arbitrary", "arbitrary"),
        name="column_blocks",
    )(*([w] * groups))


def _ffn_weight_blocks(w_up, conv_w, conv_b, w_down):
    nb = FFN_BLOCKS

    def blocks(a):
        r = a.shape[1]
        return a.reshape(DEPTH, r, 2, nb, FFN_TF).transpose(0, 3, 1, 2, 4).reshape(DEPTH, nb, r, 2 * FFN_TF)

    return (_column_blocks_bf16(w_up, FFN_TF, 2), blocks(conv_w), blocks(conv_b.reshape(DEPTH, 1, 2 * D_FF)),
            w_down.reshape(DEPTH, nb, FFN_TF, D_MODEL).astype(BF16))


FFN_FINAL_VMEM_LIMIT_BYTES = 58 * 1024 * 1024


def _conv_ffn(x, mod, nw, layer, w_up, conv_w, conv_b, w_down, final_w=None):
    nb = FFN_BLOCKS
    final = final_w is not None

    def resident(shape):
        return pl.BlockSpec((None,) + shape, lambda i: (layer,) + (0,) * len(shape), pipeline_mode=pl.Buffered(1))

    in_specs = [
        pl.BlockSpec((TM, D_MODEL), lambda i: (i, 0)),
        pl.BlockSpec((None, 6, D_MODEL), lambda i: (_mod_row(i), 0, 0)),
        pl.BlockSpec((1, D_MODEL), lambda i: (0, 0)),
        resident((nb, D_MODEL, 2 * FFN_TF)),
        resident((nb, 3, 2 * FFN_TF)),
        resident((nb, 1, 2 * FFN_TF)),
        resident((nb, FFN_TF, D_MODEL)),
    ]
    args = [x, mod, nw.reshape(1, D_MODEL), w_up, conv_w, conv_b, w_down]
    if final:
        in_specs.append(pl.BlockSpec((1, D_MODEL), lambda i: (0, 0)))
        args.append(final_w.reshape(1, D_MODEL))
        out_specs = [pl.BlockSpec((TM, D_MODEL), lambda i: (jnp.minimum(i, N_CTX_TILES - 1), 0)),
                     pl.BlockSpec((TM, D_MODEL), lambda i: (jnp.maximum(i - N_CTX_TILES, 0), 0))]
        out_shape = [jax.ShapeDtypeStruct((N_CTX, D_MODEL), F32), jax.ShapeDtypeStruct((N_LAT, D_MODEL), F32)]
    else:
        out_specs = pl.BlockSpec((TM, D_MODEL), lambda i: (i, 0))
        out_shape = jax.ShapeDtypeStruct((N_TOK, D_MODEL), F32)
    return pl.pallas_call(
        functools.partial(_ffn_kernel, final_norm=final),
        grid=(N_TOK // TM,),
        in_specs=in_specs,
        out_specs=out_specs,
        out_shape=out_shape,
        scratch_shapes=[pltpu.VMEM((TM, D_MODEL), BF16), pltpu.VMEM((TM, D_MODEL), F32),
                        pltpu.VMEM((TM, 2 * FFN_TF), F32), pltpu.VMEM((TM, 2 * FFN_TF), F32)],
        compiler_params=_params("arbitrary",
                                vmem_limit_bytes=FFN_FINAL_VMEM_LIMIT_BYTES if final else VMEM_LIMIT_BYTES),
        name="conv_ffn",
    )(*args)


ML_TN = 512
ML_EXT = ML_DV + LANES
ML_SEQ_PER_STEP = 2


ML_BLOCKS = ML_D_IN // ML_TN
ML_UP_VMEM_LIMIT_BYTES = 58 * 1024 * 1024


def _ml_up_kernel(xc_src_ref, xl_src_ref, mod_ref, nw_ref, w_ref, cw_ref, cb_ref, xm_ref, xc_ref, z_ref,
                  h_scr, raw_a, raw_b):
    nb = ML_BLOCKS
    i = pl.program_id(0)

    def normalise(x_ref):
        h = _norm_mod(x_ref[...], nw_ref[...], mod_ref[0:1, :], mod_ref[1:2, :])
        h_scr[...] = h.astype(BF16)

    is_ctx = i < N_CTX_TILES
    pl.when(is_ctx)(functools.partial(normalise, xc_src_ref))
    pl.when(jnp.logical_not(is_ctx))(functools.partial(normalise, xl_src_ref))

    def run(seq):
        def project(k, raw):
            h = h_scr[...]
            raw[...] = jnp.dot(h, w_ref[k], preferred_element_type=F32)
            z_ref[:, k * ML_TN:(k + 1) * ML_TN] = jnp.dot(h, w_ref[nb + k], preferred_element_type=F32).astype(BF16)

        def consume(k, raw):
            cols = slice(k * ML_TN, (k + 1) * ML_TN)
            xm = raw[...]
            xm_ref[:, cols] = xm.astype(BF16)
            xc_ref[:, cols] = _silu(_dwconv_rows(xm, cw_ref[:, cols], cb_ref[:, cols], seq)).astype(BF16)

        raws = (raw_a, raw_b)
        project(0, raws[0])
        for k in range(nb):
            if k + 1 < nb:
                project(k + 1, raws[(k + 1) % 2])
            consume(k, raws[k % 2])

    _by_tile_kind(i, run)


def _ml_up_weight_blocks(w_up):
    return _column_blocks_bf16(w_up, ML_TN, 1)


def _ml_up(x, mod, nw, w_up, conv_w, conv_b):
    nb = ML_BLOCKS
    x_specs, x_arrays = _row_sources(x, D_MODEL)
    w_blocks, layer = w_up
    rows = pl.BlockSpec((TM, ML_D_IN), lambda i: (i, 0))
    shp = jax.ShapeDtypeStruct((N_TOK, ML_D_IN), BF16)
    return pl.pallas_call(
        _ml_up_kernel,
        grid=(N_TOK // TM,),
        in_specs=x_specs + [
            pl.BlockSpec((None, 6, D_MODEL), lambda i: (_mod_row(i), 0, 0)),
            pl.BlockSpec((1, D_MODEL), lambda i: (0, 0)),
            pl.BlockSpec((None, 2 * nb, D_MODEL, ML_TN), lambda i: (layer, 0, 0, 0), pipeline_mode=pl.Buffered(1)),
            pl.BlockSpec((3, ML_D_IN), lambda i: (0, 0)),
            pl.BlockSpec((1, ML_D_IN), lambda i: (0, 0)),
        ],
        out_specs=[rows, rows, rows],
        out_shape=[shp, shp, shp],
        scratch_shapes=[pltpu.VMEM((TM, D_MODEL), BF16), pltpu.VMEM((TM, ML_TN), F32), pltpu.VMEM((TM, ML_TN), F32)],
        compiler_params=_params("arbitrary", vmem_limit_bytes=ML_UP_VMEM_LIMIT_BYTES),
        name="ml_up",
    )(*x_arrays, mod, nw.reshape(1, D_MODEL), w_blocks, conv_w, conv_b.reshape(1, ML_D_IN))


def _log_sigmoid(x):
    return jnp.minimum(x, 0.0) - jnp.log(1.0 + jnp.exp(-jnp.abs(x)))


def _ml_scan_kernel(*refs, seq, n_seq, has_init, out_state, n_unused_inputs, fill_layer=None):
    q_ref, kt_ref, v_ref, g_ref, gt_ref, xc_ref, z_ref, nw_ref, skip_ref = refs[:9]
    pos = 9
    if has_init:
        c0_ref, n0_ref, m0_ref = refs[pos:pos + 3]
        pos += 3
    pos += n_unused_inputs
    o_ref = refs[pos]
    pos += 1
    if out_state:
        cout_ref, nout_ref, mout_ref = refs[pos:pos + 3]
        pos += 3
    hs_scr, c_scr = refs[pos:pos + 2]

    L = ML_CHUNK
    nc = seq // L
    ext_tiles = ML_EXT // LANES
    row_i = lax.broadcasted_iota(jnp.int32, (L, L), 0)
    col_i = lax.broadcasted_iota(jnp.int32, (L, L), 1)
    ones_tile = jnp.ones((L, LANES), BF16)

    def lanes(a, n):
        return jnp.concatenate([a] * n, axis=1)

    for d in (0, 1):
        keep = (col_i <= row_i) if d == 0 else (col_i >= row_i)
        keep_f = keep.astype(F32)
        keep_t_f = ((row_i <= col_i) if d == 0 else (row_i >= col_i)).astype(F32)
        end = L - 1 if d == 0 else 0

        ms = []
        for s in range(n_seq):
            if has_init:
                c_scr[s, :, :ML_DV] = c0_ref[s, d]
                c_scr[s, :, ML_DV:] = n0_ref[s, d]
                ms.append(m0_ref[s, d:d + 1, :])
            else:
                c_scr[s] = jnp.zeros((ML_DK, ML_EXT), F32)
                ms.append(jnp.zeros((1, LANES), F32))

        for c, s in [(c, s) for c in range(nc) for s in range(n_seq)]:
            cc = c if d == 0 else nc - 1 - c
            chunk = s * nc + cc
            rows = slice(chunk * L, (chunk + 1) * L)
            m = ms[s]
            qc = q_ref[rows, :]
            ktc = kt_ref[chunk]
            v_ext = jnp.concatenate([v_ref[rows, :], ones_tile], axis=1)
            gcol = g_ref[chunk]
            grow = gt_ref[chunk]
            ig_row = grow[2 * d:2 * d + 1, :]
            lf_col = _log_sigmoid(gcol[:, 2 * d + 1:2 * d + 2])
            lf_row = _log_sigmoid(grow[2 * d + 1:2 * d + 2, :])
            b_col = jnp.broadcast_to(jnp.sum(keep_f * lf_row, axis=1, keepdims=True), (L, LANES))
            b_row = jnp.sum(keep_t_f * lf_col, axis=0, keepdims=True)
            dmat = jnp.where(keep, b_col - b_row + ig_row, NEG_INF)
            m_loc = jnp.broadcast_to(jnp.max(dmat, axis=1, keepdims=True), (L, LANES))
            p_loc = jnp.exp(dmat - m_loc)
            s_loc = jnp.dot(qc, ktc, preferred_element_type=F32) * p_loc
            intra = jnp.dot(s_loc.astype(BF16), v_ext, preferred_element_type=F32)

            m_t = jnp.maximum(b_col + m, m_loc)
            w_inter = jnp.exp(b_col + m - m_t)
            w_intra = jnp.exp(m_loc - m_t)
            inter = jnp.dot(qc, c_scr[s].astype(BF16), preferred_element_type=F32)
            hx = lanes(w_inter, ext_tiles) * inter + lanes(w_intra, ext_tiles) * intra
            inv = 1.0 / jnp.maximum(jnp.abs(hx[:, ML_DV:]), jnp.exp(-m_t))
            h = hx[:, :ML_DV] * lanes(inv, ML_DV // LANES)

            b_end = b_col[end:end + 1, :]
            m_loc_end = m_loc[end:end + 1, :]
            m_new = jnp.maximum(b_end + m, m_loc_end)
            w_c = jnp.exp(b_end + m - m_new)
            w_s = p_loc[end:end + 1, :] * jnp.exp(m_loc_end - m_new)
            upd = jnp.dot((ktc.astype(F32) * w_s).astype(BF16), v_ext, preferred_element_type=F32)
            c_scr[s] = lanes(w_c, ext_tiles) * c_scr[s] + upd
            ms[s] = m_new

            if d == 0:
                hs_scr[rows, :] = h
            else:
                hs = hs_scr[rows, :] + h
                mu = jnp.mean(hs, axis=1, keepdims=True)
                cen = hs - mu
                var = jnp.mean(cen * cen, axis=1, keepdims=True)
                hn = cen * lax.rsqrt(var + NORM_EPS) * nw_ref[...]
                xc = xc_ref[rows, :].astype(F32)
                z = z_ref[rows, :].astype(F32)
                o_ref[rows, :] = ((hn + skip_ref[...] * xc) * _silu(z)).astype(o_ref.dtype)

        if out_state:
            for s in range(n_seq):
                if fill_layer is None:
                    cout_ref[s, d] = c_scr[s, :, :ML_DV]
                else:
                    for layer in range(cout_ref.shape[1]):
                        cout_ref[s, layer, d] = (c_scr[s, :, :ML_DV] if layer == fill_layer
                                                 else jnp.zeros((ML_DK, ML_DV), F32))
                nout_ref[s, d:d + 1, :] = c_scr[s, :, ML_DV:].T[0:1, :]
                mout_ref[s, d:d + 1, :] = ms[s]


def _ml_scan(q, kt, v, g4, g4t, xc, z, norm_w, skip, *, seq, nb, row0, init=None, final_c=None,
             n_seq=ML_SEQ_PER_STEP):
    L = ML_CHUNK
    rows = n_seq * seq
    nc = rows // L
    rb = row0 // rows
    has_init = init is not None
    in_specs = [
        pl.BlockSpec((rows, ML_DK), lambda b, h: (rb + b, h)),
        pl.BlockSpec((None, nc, ML_DK, L), lambda b, h: (h, rb + b, 0, 0)),
        pl.BlockSpec((rows, ML_DV), lambda b, h: (rb + b, h)),
        pl.BlockSpec((None, nc, L, 4), lambda b, h: (h, rb + b, 0, 0)),
        pl.BlockSpec((None, nc, 4, L), lambda b, h: (h, rb + b, 0, 0)),
        pl.BlockSpec((rows, ML_DV), lambda b, h: (rb + b, h)),
        pl.BlockSpec((rows, ML_DV), lambda b, h: (rb + b, h)),
        pl.BlockSpec((1, ML_DV), lambda b, h: (0, h)),
        pl.BlockSpec((1, ML_DV), lambda b, h: (0, h)),
    ]
    args = [q, kt, v, g4, g4t, xc, z, norm_w.reshape(1, ML_D_IN), skip.reshape(1, ML_D_IN)]
    state_n = pl.BlockSpec((n_seq, None, 2, ML_DK, LANES), lambda b, h: (b, h, 0, 0, 0))
    state_m = pl.BlockSpec((n_seq, None, 2, LANES), lambda b, h: (b, h, 0, 0))
    if has_init:
        c0, layer, n0, m0 = init
        in_specs += [pl.BlockSpec((n_seq, None, 2, None, ML_DK, ML_DV), lambda b, h: (b, layer, 0, h, 0, 0)),
                     state_n, state_m]
        args += [c0, n0, m0]
    out_specs = [pl.BlockSpec((rows, ML_DV), lambda b, h: (b, h))]
    out_shape = [jax.ShapeDtypeStruct((nb * seq, ML_D_IN), BF16)]
    aliases = {}
    if not has_init:
        layer_out, c_all = final_c
        if c_all is None:
            c_spec = pl.BlockSpec((n_seq, N_ML_LAYERS, 2, None, ML_DK, ML_DV), lambda b, h: (b, 0, 0, h, 0, 0))
        else:
            c_spec = pl.BlockSpec((n_seq, None, 2, None, ML_DK, ML_DV), lambda b, h: (b, layer_out, 0, h, 0, 0))
        out_specs += [c_spec, pl.BlockSpec((n_seq, None, 2, ML_DK), lambda b, h: (b, h, 0, 0)), state_m]
        out_shape += [
            jax.ShapeDtypeStruct((nb, N_ML_LAYERS, 2, ML_HEADS, ML_DK, ML_DV), F32),
            jax.ShapeDtypeStruct((nb, ML_HEADS, 2, ML_DK), F32),
            jax.ShapeDtypeStruct((nb, ML_HEADS, 2, LANES), F32),
        ]
        if c_all is not None:
            in_specs.append(pl.BlockSpec(memory_space=pl.ANY))
            args.append(c_all)
            aliases = {len(args) - 1: 1}
    return pl.pallas_call(
        functools.partial(_ml_scan_kernel, seq=seq, n_seq=n_seq, has_init=has_init, out_state=not has_init,
                          n_unused_inputs=len(aliases),
                          fill_layer=final_c[0] if (not has_init and final_c[1] is None) else None),
        grid=(nb // n_seq, ML_HEADS),
        in_specs=in_specs,
        out_specs=out_specs,
        out_shape=out_shape,
        scratch_shapes=[pltpu.VMEM((rows, ML_DV), F32), pltpu.VMEM((n_seq, ML_DK, ML_EXT), F32)],
        input_output_aliases=aliases,
        compiler_params=_params("arbitrary", "arbitrary"),
        name="ml_scan_init" if has_init else "ml_scan_zero",
    )(*args)


def _ml_qk_kernel(xc_ref, w_ref, q_ref, kt_ref):
    acc = jnp.dot(xc_ref[...], w_ref[...], preferred_element_type=F32)
    j = pl.program_id(1)

    @pl.when(j == 0)
    def _():
        q_ref[...] = (acc * ML_DK ** -0.5).astype(BF16)

    @pl.when(j == 1)
    def _():
        for h in range(ML_HEADS):
            kt = acc[:, h * ML_DK:(h + 1) * ML_DK].T
            for c in range(TM // ML_CHUNK):
                kt_ref[h, c] = kt[:, c * ML_CHUNK:(c + 1) * ML_CHUNK].astype(BF16)


def _ml_qk(xc, w_qk):
    n = ML_HEADS * ML_DK
    cpt = TM // ML_CHUNK
    w_spec, w_qk = _weight(w_qk, (ML_D_IN, n), lambda i, j: (0, j))
    return pl.pallas_call(
        _ml_qk_kernel,
        grid=(N_TOK // TM, 2),
        in_specs=[pl.BlockSpec((TM, ML_D_IN), lambda i, j: (i, 0)), w_spec],
        out_specs=[pl.BlockSpec((TM, n), lambda i, j: (i, 0)),
                   pl.BlockSpec((ML_HEADS, cpt, ML_DK, ML_CHUNK), lambda i, j: (0, i, 0, 0))],
        out_shape=[jax.ShapeDtypeStruct((N_TOK, n), BF16),
                   jax.ShapeDtypeStruct((ML_HEADS, N_TOK // ML_CHUNK, ML_DK, ML_CHUNK), BF16)],
        compiler_params=_params("arbitrary", "arbitrary"),
        name="ml_qk",
    )(xc, w_qk)


def _mlstm_layer(x, mod, nw, p, state, final_c):
    xm, xc, z = _ml_up(x, mod, nw, p["w_up"], p["conv_w"], p["conv_b"])
    q, kt = _ml_qk(xc, p["w_qk"])
    v, g = _ml_vg(xm, p["w_v"], p["w_gate"], p["b_gate"], 1024)
    g = g[:, :4 * ML_HEADS]
    L = ML_CHUNK
    g4 = g.reshape(N_TOK // L, L, 4, ML_HEADS).transpose(3, 0, 1, 2)
    g4t = g4.transpose(0, 1, 3, 2)
    oc, c_fin, n_fin, m_fin = _ml_scan(q, kt, v, g4, g4t, xc, z, p["norm_w"], p["skip"],
                                       seq=SEQ, nb=BATCH, row0=0, final_c=final_c)
    (ol,) = _ml_scan(q, kt, v, g4, g4t, xc, z, p["norm_w"], p["skip"],
                     seq=DEC_SEQ, nb=DEC_BATCH, row0=N_CTX, init=state, n_seq=1)
    x = _mm_res(oc, ol, p["w_down"], x, mod)
    return x, c_fin, n_fin.transpose(0, 2, 1, 3), m_fin[..., 0].transpose(0, 2, 1)


def _softmax_parts(scores, sink_col):
    m = functools.reduce(jnp.maximum, [jnp.max(s, axis=1, keepdims=True) for s in scores])
    if sink_col is not None:
        m = jnp.maximum(m, sink_col)
    ps = [jnp.exp(s - m) for s in scores]
    den = functools.reduce(jnp.add, [jnp.sum(p, axis=1, keepdims=True) for p in ps])
    if sink_col is not None:
        den = den + jnp.exp(sink_col - m)
    return ps, den


def _qk(q, k):
    return lax.dot_general(q, k, (((1,), (1,)), ((), ())), preferred_element_type=F32) * (HEAD_DIM ** -0.5)


def _attend_groups(groups):
    r = groups[0][0][0].shape[0]
    lo = lax.broadcasted_iota(jnp.int32, (r, LANES), 1) < HEAD_DIM
    zero = jnp.zeros((r, LANES), BF16)
    all_scores = []
    for q_tiles, parts, _ in groups:
        qs = jnp.concatenate([jnp.where(sel, t, zero) for t in q_tiles for sel in (lo, jnp.logical_not(lo))], axis=0)
        all_scores.append([post(_qk(qs, k2)) for k2, _, post in parts])
    outs = []
    for scores, (q_tiles, parts, sinks) in zip(all_scores, groups):
        n_rows = scores[0].shape[0]
        sink_col = None
        if sinks is not None:
            rows = lax.broadcasted_iota(jnp.int32, (n_rows, 1), 0)
            sink_col = jnp.full((n_rows, 1), sinks[0], F32)
            for hi in range(1, len(sinks)):
                sink_col = jnp.where(rows >= hi * r, sinks[hi], sink_col)
        ps, den = _softmax_parts(scores, sink_col)
        o = functools.reduce(jnp.add, [jnp.dot(p.astype(BF16), v2, preferred_element_type=F32)
                                       for p, (_, v2, _) in zip(ps, parts)]) / den
        outs.append([jnp.where(lo, o[2 * a * r:(2 * a + 1) * r], o[(2 * a + 1) * r:(2 * a + 2) * r])
                     for a in range(len(q_tiles))])
    return outs


def _identity(s):
    return s


def _ctx_attn_kernel(*refs, tiles_per_kv, has_sink, kv_head_stride):
    if has_sink:
        sink_ref, q_ref, k_ref, v_ref, o_ref, kout_ref, vout_ref = refs
    else:
        q_ref, k_ref, v_ref, o_ref, kout_ref, vout_ref = refs
    for h in range(kout_ref.shape[0]):
        kout_ref[h] = k_ref[:, h * kv_head_stride:h * kv_head_stride + HEAD_DIM]
        vout_ref[h] = v_ref[:, h * kv_head_stride:h * kv_head_stride + HEAD_DIM]
    groups = []
    for t in range(k_ref.shape[1] // LANES):
        k2 = k_ref[:, t * LANES:(t + 1) * LANES].astype(BF16)
        v2 = v_ref[:, t * LANES:(t + 1) * LANES].astype(BF16)
        first = t * tiles_per_kv
        q_tiles = [q_ref[:, (first + a) * LANES:(first + a + 1) * LANES] for a in range(tiles_per_kv)]
        sinks = [sink_ref[2 * first + hi] for hi in range(2 * tiles_per_kv)] if has_sink else None
        groups.append((q_tiles, [(k2, v2, _identity)], sinks))
    for t, outs in enumerate(_attend_groups(groups)):
        for a, o in enumerate(outs):
            tile = t * tiles_per_kv + a
            o_ref[:, tile * LANES:(tile + 1) * LANES] = o.astype(o_ref.dtype)


def _ctx_attn(q, kv, n_kv_cols, n_kv_heads, sink):
    nq = q.shape[1]
    has_sink = sink is not None
    cache_spec = pl.BlockSpec((None, None, n_kv_heads, SEQ, HEAD_DIM), lambda b: (b, 0, 0, 0, 0))
    cache_shape = jax.ShapeDtypeStruct((BATCH, 1, n_kv_heads, SEQ, HEAD_DIM), F32)
    in_specs = [pl.BlockSpec((SEQ, nq), lambda b: (b, 0)),
                pl.BlockSpec((SEQ, n_kv_cols), lambda b: (b, 0)),
                pl.BlockSpec((SEQ, n_kv_cols), lambda b: (b, 1))]
    args = [q, kv, kv]
    if has_sink:
        in_specs = [pl.BlockSpec(memory_space=pltpu.SMEM)] + in_specs
        args = [sink] + args
    return pl.pallas_call(
        functools.partial(_ctx_attn_kernel, tiles_per_kv=nq // n_kv_cols, has_sink=has_sink,
                          kv_head_stride=n_kv_cols // n_kv_heads),
        grid=(BATCH,),
        in_specs=in_specs,
        out_specs=[pl.BlockSpec((SEQ, nq), lambda b: (b, 0)), cache_spec, cache_spec],
        out_shape=[jax.ShapeDtypeStruct((N_CTX, nq), BF16), cache_shape, cache_shape],
        compiler_params=_params("arbitrary"),
        name="ctx_attn",
    )(*args)


SWA_SPAN = Q_BLOCK + 2 * SWA_WINDOW
SWA_KV_COLS = SWA_KV * LANES


def _swa_lat_kernel(sink_ref, q_ref, k_ref, v_ref, kc_ref, vc_ref, o_ref):
    j = pl.program_id(1)
    start = pl.multiple_of(jnp.clip((j - 1) * Q_BLOCK, 0, DEC_SEQ - SWA_SPAN), Q_BLOCK)
    rows = 4 * Q_BLOCK
    qpos = j * Q_BLOCK + (lax.broadcasted_iota(jnp.int32, (rows, SWA_SPAN), 0) & (Q_BLOCK - 1))
    kpos = start + lax.broadcasted_iota(jnp.int32, (rows, SWA_SPAN), 1)
    in_window = jnp.abs(qpos - kpos) <= SWA_WINDOW

    def window(s):
        return jnp.where(in_window, s, NEG_INF)

    groups = []
    for t in range(SWA_KV):
        cols = slice(t * LANES, (t + 1) * LANES)
        k_loc = k_ref[pl.ds(start, SWA_SPAN), cols].astype(BF16)
        v_loc = v_ref[pl.ds(start, SWA_SPAN), cols].astype(BF16)
        q_tiles = [q_ref[:, (2 * t + a) * LANES:(2 * t + a + 1) * LANES] for a in range(2)]
        sinks = [sink_ref[4 * t + hi] for hi in range(4)]
        groups.append((q_tiles, [(k_loc, v_loc, window), (kc_ref[:, cols], vc_ref[:, cols], _identity)], sinks))
    for t, outs in enumerate(_attend_groups(groups)):
        for a, o in enumerate(outs):
            o_ref[:, (2 * t + a) * LANES:(2 * t + a + 1) * LANES] = o.astype(o_ref.dtype)


def _swa_latent(q, kv, kc, vc, sink):
    nq = q.shape[1]
    rb = N_CTX // DEC_SEQ
    qb = N_CTX // Q_BLOCK
    nj = DEC_SEQ // Q_BLOCK
    cspec = pl.BlockSpec((None, SEQ, SWA_KV_COLS), lambda b, j: (b, 0, 0))
    return pl.pallas_call(
        _swa_lat_kernel,
        grid=(DEC_BATCH, nj),
        in_specs=[pl.BlockSpec(memory_space=pltpu.SMEM),
                  pl.BlockSpec((Q_BLOCK, nq), lambda b, j: (qb + b * nj + j, 0)),
                  pl.BlockSpec((DEC_SEQ, SWA_KV_COLS), lambda b, j: (rb + b, 0)),
                  pl.BlockSpec((DEC_SEQ, SWA_KV_COLS), lambda b, j: (rb + b, 1)),
                  cspec, cspec],
        out_specs=pl.BlockSpec((Q_BLOCK, nq), lambda b, j: (b * nj + j, 0)),
        out_shape=jax.ShapeDtypeStruct((N_LAT, nq), BF16),
        compiler_params=_params("arbitrary", "arbitrary"),
        name="swa_latent",
    )(sink, q, kv, kv, kc, vc)


NA_QT = 256
NA_SPAN = 768
NA_TILES = 4


def _na_start(j):
    return (j // 2) * (DEC_SEQ - NA_SPAN)


NA_ROWS = DEC_SEQ // GRID_W
NA_DR = 2 * NA_KH - 1
NA_DC = 2 * NA_KW - 1


def _na_blocks_kernel(rpb_ref, onehot_ref, valid_ref, o_ref):
    t = jnp.dot(rpb_ref[...], onehot_ref[...], preferred_element_type=F32, precision=lax.Precision.HIGHEST)
    o_ref[...] = jnp.where(valid_ref[...] > 0.5, t, NEG_INF)


def _na_bias_blocks(rpb):
    h = rpb.shape[0]
    kpad = 32
    cq, ck = np.meshgrid(np.arange(GRID_W), np.arange(GRID_W), indexing="ij")
    dc = (np.clip(ck - cq, -(NA_KW - 1), NA_KW - 1) + NA_KW - 1).reshape(-1)
    cs = np.clip(cq - NA_KW // 2, 0, GRID_W - NA_KW)
    valid = ((ck >= cs) & (ck < cs + NA_KW)).reshape(1, -1).astype(np.float32)
    onehot = (np.arange(kpad)[:, None] == dc[None, :]).astype(np.float32)
    rpb2 = jnp.pad(rpb.reshape(h * NA_DR, NA_DC), ((0, 0), (0, kpad - NA_DC)))
    n = GRID_W * GRID_W
    blocks = pl.pallas_call(
        _na_blocks_kernel,
        grid=(1,),
        in_specs=[pl.BlockSpec((h * NA_DR, kpad), lambda i: (0, 0)),
                  pl.BlockSpec((kpad, n), lambda i: (0, 0)),
                  pl.BlockSpec((1, n), lambda i: (0, 0))],
        out_specs=pl.BlockSpec((h * NA_DR, n), lambda i: (0, 0)),
        out_shape=jax.ShapeDtypeStruct((h * NA_DR, n), F32),
        compiler_params=_params("arbitrary"),
        name="na_bias_blocks",
    )(rpb2, jnp.asarray(onehot), jnp.asarray(valid))
    blocks = blocks.reshape(h, NA_DR, GRID_W, GRID_W)
    padded = jnp.pad(blocks, ((0, 0), (1, 1), (0, 0), (0, 0)), constant_values=NEG_INF)
    return jnp.concatenate([padded[:, :-1], padded[:, 1:]], axis=-1)


def _na_lat_kernel(q_ref, k_ref, v_ref, kc_ref, vc_ref, blk_ref, o_ref, bias_scr):
    j = pl.program_id(0)
    start = pl.multiple_of(_na_start(j), 256)

    @pl.when(pl.program_id(2) == 0)
    def _():
        lane_lo = lax.broadcasted_iota(jnp.int32, (GRID_W, LANES), 1) < GRID_W
        for rq_l in range(NA_QT // GRID_W):
            rq = j * (NA_QT // GRID_W) + rq_l
            rs = jnp.clip(rq - NA_KH // 2, 0, NA_ROWS - NA_KH)
            for kp in range(NA_SPAN // LANES):
                rk = start // GRID_W + 2 * kp
                idx = jnp.clip(rk - rq + NA_KH, 0, NA_DR)
                in_band = [jnp.logical_and(r >= rs, r < rs + NA_KH).astype(jnp.int32) for r in (rk, rk + 1)]
                ok = jnp.where(lane_lo, in_band[0], in_band[1]) > 0
                for hh in range(2 * NA_TILES):
                    bias_scr[hh * NA_QT + rq_l * GRID_W:hh * NA_QT + (rq_l + 1) * GRID_W,
                             kp * LANES:(kp + 1) * LANES] = jnp.where(ok, blk_ref[hh, idx], NEG_INF)

    groups = []
    for t in range(NA_TILES):
        cols = slice(t * LANES, (t + 1) * LANES)

        def add_bias(s, t=t):
            return s + bias_scr[2 * t * NA_QT:2 * (t + 1) * NA_QT, :]

        k_loc = k_ref[pl.ds(start, NA_SPAN), cols].astype(BF16)
        v_loc = v_ref[pl.ds(start, NA_SPAN), cols].astype(BF16)
        groups.append(([q_ref[:, cols]], [(k_loc, v_loc, add_bias), (kc_ref[:, cols], vc_ref[:, cols], _identity)],
                       None))
    for t, (o,) in enumerate(_attend_groups(groups)):
        o_ref[:, t * LANES:(t + 1) * LANES] = o.astype(o_ref.dtype)


def _na_latent(q, kv, kc, vc, blocks):
    nq = q.shape[1]
    width = NA_TILES * LANES
    n_steps = nq // width
    nj = DEC_SEQ // NA_QT
    rb = N_CTX // DEC_SEQ
    qb = N_CTX // NA_QT
    cspec = pl.BlockSpec((None, SEQ, width), lambda j, p, b: (b, 0, p))
    return pl.pallas_call(
        _na_lat_kernel,
        grid=(nj, n_steps, DEC_BATCH),
        in_specs=[pl.BlockSpec((NA_QT, width), lambda j, p, b: (qb + b * nj + j, p)),
                  pl.BlockSpec((DEC_SEQ, width), lambda j, p, b: (rb + b, p)),
                  pl.BlockSpec((DEC_SEQ, width), lambda j, p, b: (rb + b, n_steps + p)),
                  cspec, cspec,
                  pl.BlockSpec((2 * NA_TILES, NA_DR + 1, GRID_W, LANES), lambda j, p, b: (p, 0, 0, 0))],
        out_specs=pl.BlockSpec((NA_QT, width), lambda j, p, b: (b * nj + j, p)),
        out_shape=jax.ShapeDtypeStruct((N_LAT, nq), BF16),
        scratch_shapes=[pltpu.VMEM((2 * NA_TILES * NA_QT, NA_SPAN), F32)],
        compiler_params=_params("arbitrary", "arbitrary", "arbitrary"),
        name="na_latent",
    )(q, kv, kv, kc, vc, blocks)


def _rope_tables(width):
    quarter = HEAD_DIM // 4
    pos = np.arange(DEC_SEQ)
    inv = np.power(ROPE_BASE, -np.arange(quarter, dtype=np.float32) / quarter).astype(np.float32)
    d = np.arange(width) % HEAD_DIM
    p = np.where((d < HEAD_DIM // 2)[None, :], (pos // GRID_W)[:, None], (pos % GRID_W)[:, None]).astype(np.float32)
    ang = p * inv[d % quarter][None, :]
    sign = np.where((d // quarter) % 2 == 0, -1.0, 1.0)[None, :]
    return jnp.asarray(np.cos(ang), F32), jnp.asarray(np.sin(ang) * sign, F32)


def _cache_rows(cache, dup):
    b, h, s, hd = cache.shape
    rows = jnp.broadcast_to(cache.transpose(0, 2, 1, 3)[:, :, :, None, :], (b, s, h, dup, hd))
    return rows.reshape(b, s, h * dup * hd).astype(BF16)


def _swa_layer(x, mod, nw, w_qkv, sink, w_o, cache_k, cache_v):
    nq, nk = SWA_HEADS * HEAD_DIM, SWA_KV * HEAD_DIM

    def dup_heads(w):
        return jnp.broadcast_to(w.reshape(D_MODEL, SWA_KV, 1, HEAD_DIM),
                                (D_MODEL, SWA_KV, 2, HEAD_DIM)).reshape(D_MODEL, SWA_KV_COLS)

    w = jnp.concatenate([w_qkv[:, :nq], dup_heads(w_qkv[:, nq:nq + nk]), dup_heads(w_qkv[:, nq + nk:])], axis=1)
    q, kv = _qkv_proj(x, mod, nw, w.astype(BF16), nq, rope_cols=nq + SWA_KV_COLS)
    oc, k_new, v_new = _ctx_attn(q, kv, SWA_KV_COLS, SWA_KV, sink)
    ol = _swa_latent(q, kv, _cache_rows(cache_k, 2), _cache_rows(cache_v, 2), sink)
    return _mm_res(oc, ol, w_o, x, mod), k_new, v_new


def _na_layer(x, mod, nw, w_qkv, rpb, w_o, cache_k, cache_v):
    n = NA_HEADS * HEAD_DIM
    q, kv = _qkv_proj(x, mod, nw, w_qkv, n, tn=NA_QKV_TN)
    oc, k_new, v_new = _ctx_attn(q, kv, n, NA_HEADS, None)
    ol = _na_latent(q, kv, _cache_rows(cache_k, 1), _cache_rows(cache_v, 1), _na_bias_blocks(rpb))
    return _mm_res(oc, ol, w_o, x, mod), k_new, v_new


def kernel(x_prompt, x_sample, state_mlstm_C, state_mlstm_n, state_mlstm_m, cache_swa_k, cache_swa_v, cache_na_k, cache_na_v, c, c_ctx, ada_w, ada_b, norm_w, final_norm_w, ffn_w_up, ffn_conv_w, ffn_conv_b, ffn_w_down, ml_w_up, ml_conv_w, ml_conv_b, ml_w_qk, ml_w_v, ml_w_gate, ml_b_gate, ml_norm_w, ml_skip, ml_w_down, swa_w_qkv, swa_sink, swa_w_o, na_w_qkv, na_rpb, na_w_o):
    x = (x_prompt.reshape(N_CTX, D_MODEL), x_sample.reshape(N_LAT, D_MODEL))
    cond = jnp.concatenate([c_ctx[None], c, jnp.zeros((MOD_ROWS - 1 - DEC_BATCH, D_MODEL), F32)], axis=0)
    mods = _ada_mod(cond, ada_w, ada_b)
    ffn_weights = _ffn_weight_blocks(ffn_w_up, ffn_conv_w, ffn_conv_b, ffn_w_down)
    ml_up_blocks = _ml_up_weight_blocks(ml_w_up)
    ml_qk_bf16, ml_v_bf16, ml_down_bf16 = ml_w_qk.astype(BF16), ml_w_v.astype(BF16), ml_w_down.astype(BF16)

    new_c, new_n, new_m = None, [], []
    new_sk = new_sv = new_nk = new_nv = None
    for i in range(DEPTH):
        kind, j = i % N_MIXERS, i // N_MIXERS
        mod = mods[i]
        if kind == 0:
            gate_w = jnp.pad(ml_w_gate[j], ((0, 0), (0, 128 - 4 * ML_HEADS))).astype(BF16)
            gate_b = jnp.pad(ml_b_gate[j], (0, 128 - 4 * ML_HEADS)).reshape(1, 128)
            p = dict(w_up=(ml_up_blocks, j), conv_w=ml_conv_w[j], conv_b=ml_conv_b[j],
                     w_qk=(ml_qk_bf16, j), w_v=(ml_v_bf16, j), w_gate=gate_w, b_gate=gate_b,
                     norm_w=ml_norm_w[j], skip=ml_skip[j], w_down=(ml_down_bf16, j))
            n0 = jnp.broadcast_to(state_mlstm_n[:, j].transpose(0, 2, 1, 3)[..., None],
                                  (DEC_BATCH, ML_HEADS, 2, ML_DK, LANES))
            m0 = jnp.broadcast_to(state_mlstm_m[:, j].transpose(0, 2, 1)[..., None], (DEC_BATCH, ML_HEADS, 2, LANES))
            x, new_c, nf, mf = _mlstm_layer(x, mod, norm_w[i, 0], p, (state_mlstm_C, j, n0, m0), (j, new_c))
            new_n.append(nf)
            new_m.append(mf)
        elif kind == 1:
            x, k_new, v_new = _swa_layer(x, mod, norm_w[i, 0], swa_w_qkv[j].astype(BF16), swa_sink[j],
                                         swa_w_o[j].astype(BF16), cache_swa_k[:, j], cache_swa_v[:, j])
            new_sk, new_sv = k_new, v_new
        else:
            x, k_new, v_new = _na_layer(x, mod, norm_w[i, 0], na_w_qkv[j].astype(BF16), na_rpb[j],
                                        na_w_o[j].astype(BF16), cache_na_k[:, j], cache_na_v[:, j])
            new_nk, new_nv = k_new, v_new
        x = _conv_ffn(x, mod, norm_w[i, 1], i, *ffn_weights, final_w=final_norm_w if i == DEPTH - 1 else None)

    y_ctx, y_lat = x
    return (y_ctx.reshape(BATCH, SEQ, D_MODEL), y_lat.reshape(DEC_BATCH, DEC_SEQ, D_MODEL),
            new_c, jnp.stack(new_n, axis=1), jnp.stack(new_m, axis=1),
            new_sk, new_sv, new_nk, new_nv)
```

```python
import functools

import jax
import jax.numpy as jnp
import numpy as np
from jax import lax
from jax.experimental import pallas as pl
from jax.experimental.pallas import tpu as pltpu

F32 = jnp.float32
BF16 = jnp.bfloat16

D_MODEL = 1024
BATCH = 32
SEQ = 256
DEPTH = 4
DEC_BATCH = 8
DEC_SEQ = 1024
GRID_W = 64
N_MIXERS = 3
N_ML_LAYERS = (DEPTH + 2) // 3
NORM_EPS = 1e-6
D_FF = 2816
ML_D_IN = 2 * D_MODEL
ML_HEADS = 4
ML_DK = ML_D_IN // (2 * ML_HEADS)
ML_DV = ML_D_IN // ML_HEADS
ML_CHUNK = 128
HEAD_DIM = 64
SWA_HEADS = D_MODEL // HEAD_DIM
SWA_KV = SWA_HEADS // 4
SWA_WINDOW = 128
Q_BLOCK = 128
ROPE_BASE = 10000.0
NA_HEADS = D_MODEL // HEAD_DIM
NA_KH = 8
NA_KW = 16
NEG_INF = -1e30

N_CTX = BATCH * SEQ
N_LAT = DEC_BATCH * DEC_SEQ
N_TOK = N_CTX + N_LAT
TM = 1024
N_CTX_TILES = N_CTX // TM
MOD_ROWS = 16
VMEM_LIMIT_BYTES = 56 * 1024 * 1024


def _params(*sem, vmem_limit_bytes=VMEM_LIMIT_BYTES):
    return pltpu.CompilerParams(dimension_semantics=sem, vmem_limit_bytes=vmem_limit_bytes)


def _mod_row(i):
    return jnp.where(i < N_CTX_TILES, 0, i - (N_CTX_TILES - 1))


def _silu(x):
    return x / (1.0 + jnp.exp(-x))


def _norm_mod(x, nw, shift, scale):
    y = x * lax.rsqrt(jnp.mean(x * x, axis=-1, keepdims=True) + NORM_EPS) * nw
    return y * (1.0 + scale) + shift


SUBLANES = 8
LANES = 128


def _dwconv_rows(u, cw, cb, seq):
    r, c = u.shape
    n_groups, per_seq = r // SUBLANES, seq // SUBLANES
    g = u.reshape(n_groups, SUBLANES, c)
    sub = lax.broadcasted_iota(jnp.int32, g.shape, 1)
    down = pltpu.roll(g, 1, 1)
    up = pltpu.roll(g, SUBLANES - 1, 1)
    zero = jnp.zeros((1, SUBLANES, c), F32)
    from_prev, from_next = [], []
    for s in range(0, n_groups, per_seq):
        from_prev += [zero, down[s:s + per_seq - 1]]
        from_next += [up[s + 1:s + per_seq], zero]
    prev = jnp.where(sub == 0, jnp.concatenate(from_prev, axis=0), down)
    nxt = jnp.where(sub == SUBLANES - 1, jnp.concatenate(from_next, axis=0), up)
    out = cw[0:1, :] * prev + cw[1:2, :] * g + cw[2:3, :] * nxt + cb
    return out.reshape(r, c)


def _by_tile_kind(tile, body):
    pl.when(tile < N_CTX_TILES)(functools.partial(body, SEQ))
    pl.when(tile >= N_CTX_TILES)(functools.partial(body, DEC_SEQ))


def _ada_kernel(c_ref, w_ref, b_ref, o_ref):
    s = _silu(c_ref[...]).astype(BF16)
    o_ref[...] = jnp.dot(s, w_ref[...].astype(BF16), preferred_element_type=F32) + b_ref[...]


def _ada_mod(cond, ada_w, ada_b):
    tn = 3072
    n = 6 * D_MODEL
    out = pl.pallas_call(
        _ada_kernel,
        grid=(DEPTH, n // tn),
        in_specs=[
            pl.BlockSpec((MOD_ROWS, D_MODEL), lambda l, j: (0, 0)),
            pl.BlockSpec((None, D_MODEL, tn), lambda l, j: (l, 0, j)),
            pl.BlockSpec((None, 1, tn), lambda l, j: (l, 0, j)),
        ],
        out_specs=pl.BlockSpec((None, MOD_ROWS, tn), lambda l, j: (l, 0, j)),
        out_shape=jax.ShapeDtypeStruct((DEPTH, MOD_ROWS, n), F32),
        compiler_params=_params("arbitrary", "arbitrary"),
        name="ada_mod",
    )(cond, ada_w, ada_b.reshape(DEPTH, 1, n))
    return out.reshape(DEPTH, MOD_ROWS, 6, D_MODEL)


QKV_TN = 512
NA_QKV_TN = 1024


def _rotate_pairs(a, cos, sin):
    lane = lax.broadcasted_iota(jnp.int32, a.shape, 1)
    first = (lane & (HEAD_DIM // 4)) == 0
    n = a.shape[1]
    partner = jnp.where(first, pltpu.roll(a, n - HEAD_DIM // 4, 1), pltpu.roll(a, HEAD_DIM // 4, 1))
    return a * cos + partner * sin


def _qkv_kernel(*refs, q_blocks, rope_blocks):
    if rope_blocks:
        x_ref, mod_ref, nw_ref, w_ref, cos_ref, sin_ref, q_ref, kv_ref, h_scr = refs
    else:
        x_ref, mod_ref, nw_ref, w_ref, q_ref, kv_ref, h_scr = refs
    i = pl.program_id(0)
    j = pl.program_id(1)

    @pl.when(j == 0)
    def _():
        h = _norm_mod(x_ref[...], nw_ref[...], mod_ref[0:1, :], mod_ref[1:2, :])
        h_scr[...] = h.astype(BF16)

    acc = jnp.dot(h_scr[...], w_ref[...], preferred_element_type=F32)

    def emit(val):
        @pl.when(j < q_blocks)
        def _():
            q_ref[...] = val.astype(BF16)

        @pl.when(j >= q_blocks)
        def _():
            kv_ref[...] = val

    if rope_blocks:
        rotate = jnp.logical_and(i >= N_CTX_TILES, j < rope_blocks)

        @pl.when(rotate)
        def _():
            emit(_rotate_pairs(acc, cos_ref[...], sin_ref[...]))

        @pl.when(jnp.logical_not(rotate))
        def _():
            emit(acc)
    else:
        emit(acc)


def _qkv_proj(x, mod, nw, w, n_q, rope_cols=0, tn=QKV_TN):
    n = w.shape[1]
    q_blocks = n_q // tn
    rope_blocks = rope_cols // tn
    in_specs = [
        pl.BlockSpec((TM, D_MODEL), lambda i, j: (i, 0)),
        pl.BlockSpec((None, 6, D_MODEL), lambda i, j: (_mod_row(i), 0, 0)),
        pl.BlockSpec((1, D_MODEL), lambda i, j: (0, 0)),
        pl.BlockSpec((D_MODEL, tn), lambda i, j: (0, j)),
    ]
    args = [x, mod, nw.reshape(1, D_MODEL), w]
    if rope_blocks:
        tab = pl.BlockSpec((DEC_SEQ, tn), lambda i, j: (0, 0))
        in_specs += [tab, tab]
        args += list(_rope_tables(tn))
    return pl.pallas_call(
        functools.partial(_qkv_kernel, q_blocks=q_blocks, rope_blocks=rope_blocks),
        grid=(N_TOK // TM, n // tn),
        in_specs=in_specs,
        out_specs=[pl.BlockSpec((TM, tn), lambda i, j: (i, jnp.minimum(j, q_blocks - 1))),
                   pl.BlockSpec((TM, tn), lambda i, j: (i, jnp.maximum(j - q_blocks, 0)))],
        out_shape=[jax.ShapeDtypeStruct((N_TOK, n_q), BF16), jax.ShapeDtypeStruct((N_TOK, n - n_q), F32)],
        scratch_shapes=[pltpu.VMEM((TM, D_MODEL), BF16)],
        compiler_params=_params("arbitrary", "arbitrary"),
        name="qkv_proj",
    )(*args)


def _weight(w, block, index):
    if isinstance(w, tuple):
        stacked, layer = w
        return pl.BlockSpec((None,) + block, lambda *g: (layer,) + index(*g)), stacked
    return pl.BlockSpec(block, index), w


def _weight_cols(w):
    return w[0].shape[2] if isinstance(w, tuple) else w.shape[1]


def _row_sources(x, width):
    (ctx, ctx_first), (lat, lat_first) = (((x[0], 0), (x[1], 0)) if isinstance(x, tuple)
                                          else ((x, 0), (x, N_CTX_TILES)))

    def spec(first, lo, hi):
        return pl.BlockSpec((TM, width), lambda i, *_: (jnp.clip(i, lo, hi) - lo + first, 0))

    return ([spec(ctx_first, 0, N_CTX_TILES - 1), spec(lat_first, N_CTX_TILES, N_TOK // TM - 1)], [ctx, lat])


def _mm_res_kernel(ac_ref, al_ref, w_ref, xc_ref, xl_ref, mod_ref, o_ref):
    def emit(a_ref, x_ref):
        acc = jnp.dot(a_ref[...], w_ref[...], preferred_element_type=F32)
        o_ref[...] = x_ref[...] + mod_ref[2:3, :] * acc

    is_ctx = pl.program_id(0) < N_CTX_TILES
    pl.when(is_ctx)(functools.partial(emit, ac_ref, xc_ref))
    pl.when(jnp.logical_not(is_ctx))(functools.partial(emit, al_ref, xl_ref))


def _mm_res(a_ctx, a_lat, w, x, mod):
    k = a_ctx.shape[1]
    a_specs, a_arrays = _row_sources((a_ctx, a_lat), k)
    x_specs, x_arrays = _row_sources(x, D_MODEL)
    w_spec, w = _weight(w, (k, D_MODEL), lambda i: (0, 0))
    return pl.pallas_call(
        _mm_res_kernel,
        grid=(N_TOK // TM,),
        in_specs=a_specs + [w_spec] + x_specs
        + [pl.BlockSpec((None, 6, D_MODEL), lambda i: (_mod_row(i), 0, 0))],
        out_specs=pl.BlockSpec((TM, D_MODEL), lambda i: (i, 0)),
        out_shape=jax.ShapeDtypeStruct((N_TOK, D_MODEL), F32),
        compiler_params=_params("arbitrary"),
        name="mm_res",
    )(*a_arrays, w, *x_arrays, mod)


FFN_TF = 256
FFN_BLOCKS = D_FF // FFN_TF


def _ffn_kernel(*refs, final_norm):
    if final_norm:
        (x_ref, mod_ref, nw_ref, wup_ref, cw_ref, cb_ref, wd_ref, fw_ref, oc_ref, ol_ref,
         h_scr, acc_scr, raw_a, raw_b) = refs
    else:
        x_ref, mod_ref, nw_ref, wup_ref, cw_ref, cb_ref, wd_ref, o_ref, h_scr, acc_scr, raw_a, raw_b = refs
    nb = FFN_BLOCKS
    h_scr[...] = _norm_mod(x_ref[...], nw_ref[...], mod_ref[3:4, :], mod_ref[4:5, :]).astype(BF16)
    acc_scr[...] = jnp.zeros_like(acc_scr)

    def run(seq):
        def project(k, raw):
            raw[...] = jnp.dot(h_scr[...], wup_ref[k], preferred_element_type=F32)

        def consume(k, raw):
            u = _dwconv_rows(raw[...], cw_ref[k], cb_ref[k], seq)
            a = (_silu(u[:, :FFN_TF]) * u[:, FFN_TF:]).astype(BF16)
            acc_scr[...] += jnp.dot(a, wd_ref[k], preferred_element_type=F32)

        project(0, raw_a)

        def two_stages(t, carry):
            k = 2 * t
            project(k + 1, raw_b)
            consume(k, raw_a)
            project(k + 2, raw_a)
            consume(k + 1, raw_b)
            return carry

        lax.fori_loop(0, (nb - 1) // 2, two_stages, 0)
        consume(nb - 1, raw_a)

    _by_tile_kind(pl.program_id(0), run)
    y = x_ref[...] + mod_ref[5:6, :] * acc_scr[...]
    if not final_norm:
        o_ref[...] = y
    else:
        y = y * lax.rsqrt(jnp.mean(y * y, axis=-1, keepdims=True) + NORM_EPS) * fw_ref[...]
        is_ctx = pl.program_id(0) < N_CTX_TILES

        @pl.when(is_ctx)
        def _():
            oc_ref[...] = y

        @pl.when(jnp.logical_not(is_ctx))
        def _():
            ol_ref[...] = y


def _column_blocks_kernel(*refs):
    *in_refs, o_ref = refs
    o_ref[...] = jnp.concatenate([r[...] for r in in_refs], axis=1).astype(o_ref.dtype)


def _column_blocks_bf16(w, width, groups):
    layers, r, cols = w.shape
    nb = cols // (groups * width)
    in_specs = [pl.BlockSpec((None, r, width), functools.partial(lambda l, k, g: (l, 0, g * nb + k), g=g))
                for g in range(groups)]
    return pl.pallas_call(
        _column_blocks_kernel,
        grid=(layers, nb),
        in_specs=in_specs,
        out_specs=pl.BlockSpec((None, None, r, groups * width), lambda l, k: (l, k, 0, 0)),
        out_shape=jax.ShapeDtypeStruct((layers, nb, r, groups * width), BF16),
        compiler_params=_params("arbitrary", "arbitrary"),
        name="column_blocks",
    )(*([w] * groups))


def _ffn_weight_blocks(w_up, conv_w, conv_b, w_down):
    nb = FFN_BLOCKS

    def blocks(a):
        r = a.shape[1]
        return a.reshape(DEPTH, r, 2, nb, FFN_TF).transpose(0, 3, 1, 2, 4).reshape(DEPTH, nb, r, 2 * FFN_TF)

    return (_column_blocks_bf16(w_up, FFN_TF, 2), blocks(conv_w), blocks(conv_b.reshape(DEPTH, 1, 2 * D_FF)),
            w_down.reshape(DEPTH, nb, FFN_TF, D_MODEL).astype(BF16))


FFN_FINAL_VMEM_LIMIT_BYTES = 58 * 1024 * 1024


def _conv_ffn(x, mod, nw, layer, w_up, conv_w, conv_b, w_down, final_w=None):
    nb = FFN_BLOCKS
    final = final_w is not None

    def resident(shape):
        return pl.BlockSpec((None,) + shape, lambda i: (layer,) + (0,) * len(shape), pipeline_mode=pl.Buffered(1))

    in_specs = [
        pl.BlockSpec((TM, D_MODEL), lambda i: (i, 0)),
        pl.BlockSpec((None, 6, D_MODEL), lambda i: (_mod_row(i), 0, 0)),
        pl.BlockSpec((1, D_MODEL), lambda i: (0, 0)),
        resident((nb, D_MODEL, 2 * FFN_TF)),
        resident((nb, 3, 2 * FFN_TF)),
        resident((nb, 1, 2 * FFN_TF)),
        resident((nb, FFN_TF, D_MODEL)),
    ]
    args = [x, mod, nw.reshape(1, D_MODEL), w_up, conv_w, conv_b, w_down]
    if final:
        in_specs.append(pl.BlockSpec((1, D_MODEL), lambda i: (0, 0)))
        args.append(final_w.reshape(1, D_MODEL))
        out_specs = [pl.BlockSpec((TM, D_MODEL), lambda i: (jnp.minimum(i, N_CTX_TILES - 1), 0)),
                     pl.BlockSpec((TM, D_MODEL), lambda i: (jnp.maximum(i - N_CTX_TILES, 0), 0))]
        out_shape = [jax.ShapeDtypeStruct((N_CTX, D_MODEL), F32), jax.ShapeDtypeStruct((N_LAT, D_MODEL), F32)]
    else:
        out_specs = pl.BlockSpec((TM, D_MODEL), lambda i: (i, 0))
        out_shape = jax.ShapeDtypeStruct((N_TOK, D_MODEL), F32)
    return pl.pallas_call(
        functools.partial(_ffn_kernel, final_norm=final),
        grid=(N_TOK // TM,),
        in_specs=in_specs,
        out_specs=out_specs,
        out_shape=out_shape,
        scratch_shapes=[pltpu.VMEM((TM, D_MODEL), BF16), pltpu.VMEM((TM, D_MODEL), F32),
                        pltpu.VMEM((TM, 2 * FFN_TF), F32), pltpu.VMEM((TM, 2 * FFN_TF), F32)],
        compiler_params=_params("arbitrary",
                                vmem_limit_bytes=FFN_FINAL_VMEM_LIMIT_BYTES if final else VMEM_LIMIT_BYTES),
        name="conv_ffn",
    )(*args)


ML_TN = 512
ML_EXT = ML_DV + LANES
ML_SEQ_PER_STEP = 2


ML_BLOCKS = ML_D_IN // ML_TN
ML_UP_VMEM_LIMIT_BYTES = 58 * 1024 * 1024


def _ml_up_kernel(xc_src_ref, xl_src_ref, mod_ref, nw_ref, w_ref, cw_ref, cb_ref, xm_ref, xc_ref, z_ref,
                  h_scr, raw_a, raw_b):
    nb = ML_BLOCKS
    i = pl.program_id(0)

    def normalise(x_ref):
        h = _norm_mod(x_ref[...], nw_ref[...], mod_ref[0:1, :], mod_ref[1:2, :])
        h_scr[...] = h.astype(BF16)

    is_ctx = i < N_CTX_TILES
    pl.when(is_ctx)(functools.partial(normalise, xc_src_ref))
    pl.when(jnp.logical_not(is_ctx))(functools.partial(normalise, xl_src_ref))

    def run(seq):
        def project(k, raw):
            h = h_scr[...]
            raw[...] = jnp.dot(h, w_ref[k], preferred_element_type=F32)
            z_ref[:, k * ML_TN:(k + 1) * ML_TN] = jnp.dot(h, w_ref[nb + k], preferred_element_type=F32).astype(BF16)

        def consume(k, raw):
            cols = slice(k * ML_TN, (k + 1) * ML_TN)
            xm = raw[...]
            xm_ref[:, cols] = xm.astype(BF16)
            xc_ref[:, cols] = _silu(_dwconv_rows(xm, cw_ref[:, cols], cb_ref[:, cols], seq)).astype(BF16)

        raws = (raw_a, raw_b)
        project(0, raws[0])
        for k in range(nb):
            if k + 1 < nb:
                project(k + 1, raws[(k + 1) % 2])
            consume(k, raws[k % 2])

    _by_tile_kind(i, run)


def _ml_up_weight_blocks(w_up):
    return _column_blocks_bf16(w_up, ML_TN, 1)


def _ml_up(x, mod, nw, w_up, conv_w, conv_b):
    nb = ML_BLOCKS
    x_specs, x_arrays = _row_sources(x, D_MODEL)
    w_blocks, layer = w_up
    rows = pl.BlockSpec((TM, ML_D_IN), lambda i: (i, 0))
    shp = jax.ShapeDtypeStruct((N_TOK, ML_D_IN), BF16)
    return pl.pallas_call(
        _ml_up_kernel,
        grid=(N_TOK // TM,),
        in_specs=x_specs + [
            pl.BlockSpec((None, 6, D_MODEL), lambda i: (_mod_row(i), 0, 0)),
            pl.BlockSpec((1, D_MODEL), lambda i: (0, 0)),
            pl.BlockSpec((None, 2 * nb, D_MODEL, ML_TN), lambda i: (layer, 0, 0, 0), pipeline_mode=pl.Buffered(1)),
            pl.BlockSpec((3, ML_D_IN), lambda i: (0, 0)),
            pl.BlockSpec((1, ML_D_IN), lambda i: (0, 0)),
        ],
        out_specs=[rows, rows, rows],
        out_shape=[shp, shp, shp],
        scratch_shapes=[pltpu.VMEM((TM, D_MODEL), BF16), pltpu.VMEM((TM, ML_TN), F32), pltpu.VMEM((TM, ML_TN), F32)],
        compiler_params=_params("arbitrary", vmem_limit_bytes=ML_UP_VMEM_LIMIT_BYTES),
        name="ml_up",
    )(*x_arrays, mod, nw.reshape(1, D_MODEL), w_blocks, conv_w, conv_b.reshape(1, ML_D_IN))


def _log_sigmoid(x):
    return jnp.minimum(x, 0.0) - jnp.log(1.0 + jnp.exp(-jnp.abs(x)))


def _ml_scan_kernel(*refs, seq, n_seq, has_init, out_state, n_unused_inputs, fill_layer=None):
    q_ref, kt_ref, v_ref, g_ref, gt_ref, xc_ref, z_ref, nw_ref, skip_ref = refs[:9]
    pos = 9
    if has_init:
        c0_ref, n0_ref, m0_ref = refs[pos:pos + 3]
        pos += 3
    pos += n_unused_inputs
    o_ref = refs[pos]
    pos += 1
    if out_state:
        cout_ref, nout_ref, mout_ref = refs[pos:pos + 3]
        pos += 3
    hs_scr, c_scr = refs[pos:pos + 2]

    L = ML_CHUNK
    nc = seq // L
    ext_tiles = ML_EXT // LANES
    row_i = lax.broadcasted_iota(jnp.int32, (L, L), 0)
    col_i = lax.broadcasted_iota(jnp.int32, (L, L), 1)
    ones_tile = jnp.ones((L, LANES), BF16)

    def lanes(a, n):
        return jnp.concatenate([a] * n, axis=1)

    for d in (0, 1):
        keep = (col_i <= row_i) if d == 0 else (col_i >= row_i)
        keep_f = keep.astype(F32)
        keep_t_f = ((row_i <= col_i) if d == 0 else (row_i >= col_i)).astype(F32)
        end = L - 1 if d == 0 else 0

        ms = []
        for s in range(n_seq):
            if has_init:
                c_scr[s, :, :ML_DV] = c0_ref[s, d]
                c_scr[s, :, ML_DV:] = n0_ref[s, d]
                ms.append(m0_ref[s, d:d + 1, :])
            else:
                c_scr[s] = jnp.zeros((ML_DK, ML_EXT), F32)
                ms.append(jnp.zeros((1, LANES), F32))

        for c, s in [(c, s) for c in range(nc) for s in range(n_seq)]:
            cc = c if d == 0 else nc - 1 - c
            chunk = s * nc + cc
            rows = slice(chunk * L, (chunk + 1) * L)
            m = ms[s]
            qc = q_ref[rows, :]
            ktc = kt_ref[chunk]
            v_ext = jnp.concatenate([v_ref[rows, :], ones_tile], axis=1)
            gcol = g_ref[chunk]
            grow = gt_ref[chunk]
            ig_row = grow[2 * d:2 * d + 1, :]
            lf_col = _log_sigmoid(gcol[:, 2 * d + 1:2 * d + 2])
            lf_row = _log_sigmoid(grow[2 * d + 1:2 * d + 2, :])
            b_col = jnp.broadcast_to(jnp.sum(keep_f * lf_row, axis=1, keepdims=True), (L, LANES))
            b_row = jnp.sum(keep_t_f * lf_col, axis=0, keepdims=True)
            dmat = jnp.where(keep, b_col - b_row + ig_row, NEG_INF)
            m_loc = jnp.broadcast_to(jnp.max(dmat, axis=1, keepdims=True), (L, LANES))
            p_loc = jnp.exp(dmat - m_loc)
            s_loc = jnp.dot(qc, ktc, preferred_element_type=F32) * p_loc
            intra = jnp.dot(s_loc.astype(BF16), v_ext, preferred_element_type=F32)

            m_t = jnp.maximum(b_col + m, m_loc)
            w_inter = jnp.exp(b_col + m - m_t)
            w_intra = jnp.exp(m_loc - m_t)
            inter = jnp.dot(qc, c_scr[s].astype(BF16), preferred_element_type=F32)
            hx = lanes(w_inter, ext_tiles) * inter + lanes(w_intra, ext_tiles) * intra
            inv = 1.0 / jnp.maximum(jnp.abs(hx[:, ML_DV:]), jnp.exp(-m_t))
            h = hx[:, :ML_DV] * lanes(inv, ML_DV // LANES)

            b_end = b_col[end:end + 1, :]
            m_loc_end = m_loc[end:end + 1, :]
            m_new = jnp.maximum(b_end + m, m_loc_end)
            w_c = jnp.exp(b_end + m - m_new)
            w_s = p_loc[end:end + 1, :] * jnp.exp(m_loc_end - m_new)
            upd = jnp.dot((ktc.astype(F32) * w_s).astype(BF16), v_ext, preferred_element_type=F32)
            c_scr[s] = lanes(w_c, ext_tiles) * c_scr[s] + upd
            ms[s] = m_new

            if d == 0:
                hs_scr[rows, :] = h
            else:
                hs = hs_scr[rows, :] + h
                mu = jnp.mean(hs, axis=1, keepdims=True)
                cen = hs - mu
                var = jnp.mean(cen * cen, axis=1, keepdims=True)
                hn = cen * lax.rsqrt(var + NORM_EPS) * nw_ref[...]
                xc = xc_ref[rows, :].astype(F32)
                z = z_ref[rows, :].astype(F32)
                o_ref[rows, :] = ((hn + skip_ref[...] * xc) * _silu(z)).astype(o_ref.dtype)

        if out_state:
            for s in range(n_seq):
                if fill_layer is None:
                    cout_ref[s, d] = c_scr[s, :, :ML_DV]
                else:
                    for layer in range(cout_ref.shape[1]):
                        cout_ref[s, layer, d] = (c_scr[s, :, :ML_DV] if layer == fill_layer
                                                 else jnp.zeros((ML_DK, ML_DV), F32))
                nout_ref[s, d:d + 1, :] = c_scr[s, :, ML_DV:].T[0:1, :]
                mout_ref[s, d:d + 1, :] = ms[s]


def _ml_scan(q, kt, v, g4, g4t, xc, z, norm_w, skip, *, seq, nb, row0, init=None, final_c=None,
             n_seq=ML_SEQ_PER_STEP):
    L = ML_CHUNK
    rows = n_seq * seq
    nc = rows // L
    rb = row0 // rows
    has_init = init is not None
    in_specs = [
        pl.BlockSpec((rows, ML_DK), lambda b, h: (rb + b, h)),
        pl.BlockSpec((None, nc, ML_DK, L), lambda b, h: (h, rb + b, 0, 0)),
        pl.BlockSpec((rows, ML_DV), lambda b, h: (rb + b, h)),
        pl.BlockSpec((None, nc, L, 4), lambda b, h: (h, rb + b, 0, 0)),
        pl.BlockSpec((None, nc, 4, L), lambda b, h: (h, rb + b, 0, 0)),
        pl.BlockSpec((rows, ML_DV), lambda b, h: (rb + b, h)),
        pl.BlockSpec((rows, ML_DV), lambda b, h: (rb + b, h)),
        pl.BlockSpec((1, ML_DV), lambda b, h: (0, h)),
        pl.BlockSpec((1, ML_DV), lambda b, h: (0, h)),
    ]
    args = [q, kt, v, g4, g4t, xc, z, norm_w.reshape(1, ML_D_IN), skip.reshape(1, ML_D_IN)]
    state_n = pl.BlockSpec((n_seq, None, 2, ML_DK, LANES), lambda b, h: (b, h, 0, 0, 0))
    state_m = pl.BlockSpec((n_seq, None, 2, LANES), lambda b, h: (b, h, 0, 0))
    if has_init:
        c0, layer, n0, m0 = init
        in_specs += [pl.BlockSpec((n_seq, None, 2, None, ML_DK, ML_DV), lambda b, h: (b, layer, 0, h, 0, 0)),
                     state_n, state_m]
        args += [c0, n0, m0]
    out_specs = [pl.BlockSpec((rows, ML_DV), lambda b, h: (b, h))]
    out_shape = [jax.ShapeDtypeStruct((nb * seq, ML_D_IN), BF16)]
    aliases = {}
    if not has_init:
        layer_out, c_all = final_c
        if c_all is None:
            c_spec = pl.BlockSpec((n_seq, N_ML_LAYERS, 2, None, ML_DK, ML_DV), lambda b, h: (b, 0, 0, h, 0, 0))
        else:
            c_spec = pl.BlockSpec((n_seq, None, 2, None, ML_DK, ML_DV), lambda b, h: (b, layer_out, 0, h, 0, 0))
        out_specs += [c_spec, pl.BlockSpec((n_seq, None, 2, ML_DK), lambda b, h: (b, h, 0, 0)), state_m]
        out_shape += [
            jax.ShapeDtypeStruct((nb, N_ML_LAYERS, 2, ML_HEADS, ML_DK, ML_DV), F32),
            jax.ShapeDtypeStruct((nb, ML_HEADS, 2, ML_DK), F32),
            jax.ShapeDtypeStruct((nb, ML_HEADS, 2, LANES), F32),
        ]
        if c_all is not None:
            in_specs.append(pl.BlockSpec(memory_space=pl.ANY))
            args.append(c_all)
            aliases = {len(args) - 1: 1}
    return pl.pallas_call(
        functools.partial(_ml_scan_kernel, seq=seq, n_seq=n_seq, has_init=has_init, out_state=not has_init,
                          n_unused_inputs=len(aliases),
                          fill_layer=final_c[0] if (not has_init and final_c[1] is None) else None),
        grid=(nb // n_seq, ML_HEADS),
        in_specs=in_specs,
        out_specs=out_specs,
        out_shape=out_shape,
        scratch_shapes=[pltpu.VMEM((rows, ML_DV), F32), pltpu.VMEM((n_seq, ML_DK, ML_EXT), F32)],
        input_output_aliases=aliases,
        compiler_params=_params("arbitrary", "arbitrary"),
        name="ml_scan_init" if has_init else "ml_scan_zero",
    )(*args)


def _ml_qkvg_kernel(xc_ref, xm_ref, wqk_ref, wv_ref, wg_ref, bg_ref, q_ref, kt_ref, v_ref, g_ref):
    j = pl.program_id(1)

    @pl.when(j == 0)
    def _():
        acc = jnp.dot(xc_ref[...], wqk_ref[...], preferred_element_type=F32)
        q_ref[...] = (acc * ML_DK ** -0.5).astype(BF16)

    @pl.when(j == 1)
    def _():
        acc = jnp.dot(xc_ref[...], wqk_ref[...], preferred_element_type=F32)
        for h in range(ML_HEADS):
            kt = acc[:, h * ML_DK:(h + 1) * ML_DK].T
            for c in range(TM // ML_CHUNK):
                kt_ref[h, c] = kt[:, c * ML_CHUNK:(c + 1) * ML_CHUNK].astype(BF16)

    @pl.when(j >= 2)
    def _():
        v_ref[...] = jnp.dot(xm_ref[...], wv_ref[...], preferred_element_type=F32).astype(BF16)

    @pl.when(j == 2)
    def _():
        g_ref[...] = jnp.dot(xm_ref[...], wg_ref[...], preferred_element_type=F32) + bg_ref[...]


def _ml_qkvg(xc, xm, w_qk, w_v, w_gate, b_gate):
    n = ML_HEADS * ML_DK
    cpt = TM // ML_CHUNK
    wqk_spec, w_qk = _weight(w_qk, (ML_D_IN, n), lambda i, j: (0, jnp.minimum(j, 1)))
    wv_spec, w_v = _weight(w_v, (ML_D_IN, n), lambda i, j: (0, jnp.maximum(j - 2, 0)))
    rows = pl.BlockSpec((TM, ML_D_IN), lambda i, j: (i, 0))
    return pl.pallas_call(
        _ml_qkvg_kernel,
        grid=(N_TOK // TM, 4),
        in_specs=[rows, rows, wqk_spec, wv_spec,
                  pl.BlockSpec((ML_D_IN, LANES), lambda i, j: (0, 0)),
                  pl.BlockSpec((1, LANES), lambda i, j: (0, 0))],
        out_specs=[pl.BlockSpec((TM, n), lambda i, j: (i, 0)),
                   pl.BlockSpec((ML_HEADS, cpt, ML_DK, ML_CHUNK), lambda i, j: (0, i, 0, 0)),
                   pl.BlockSpec((TM, n), lambda i, j: (i, jnp.maximum(j - 2, 0))),
                   pl.BlockSpec((TM, LANES), lambda i, j: (i, 0))],
        out_shape=[jax.ShapeDtypeStruct((N_TOK, n), BF16),
                   jax.ShapeDtypeStruct((ML_HEADS, N_TOK // ML_CHUNK, ML_DK, ML_CHUNK), BF16),
                   jax.ShapeDtypeStruct((N_TOK, ML_D_IN), BF16),
                   jax.ShapeDtypeStruct((N_TOK, LANES), F32)],
        compiler_params=_params("arbitrary", "arbitrary"),
        name="ml_qkvg",
    )(xc, xm, w_qk, w_v, w_gate, b_gate)


def _mlstm_layer(x, mod, nw, p, state, final_c):
    xm, xc, z = _ml_up(x, mod, nw, p["w_up"], p["conv_w"], p["conv_b"])
    q, kt, v, g = _ml_qkvg(xc, xm, p["w_qk"], p["w_v"], p["w_gate"], p["b_gate"])
    g = g[:, :4 * ML_HEADS]
    L = ML_CHUNK
    g4 = g.reshape(N_TOK // L, L, 4, ML_HEADS).transpose(3, 0, 1, 2)
    g4t = g4.transpose(0, 1, 3, 2)
    oc, c_fin, n_fin, m_fin = _ml_scan(q, kt, v, g4, g4t, xc, z, p["norm_w"], p["skip"],
                                       seq=SEQ, nb=BATCH, row0=0, final_c=final_c)
    (ol,) = _ml_scan(q, kt, v, g4, g4t, xc, z, p["norm_w"], p["skip"],
                     seq=DEC_SEQ, nb=DEC_BATCH, row0=N_CTX, init=state, n_seq=1)
    x = _mm_res(oc, ol, p["w_down"], x, mod)
    return x, c_fin, n_fin.transpose(0, 2, 1, 3), m_fin[..., 0].transpose(0, 2, 1)


def _softmax_parts(scores, sink_col):
    m = functools.reduce(jnp.maximum, [jnp.max(s, axis=1, keepdims=True) for s in scores])
    if sink_col is not None:
        m = jnp.maximum(m, sink_col)
    ps = [jnp.exp(s - m) for s in scores]
    den = functools.reduce(jnp.add, [jnp.sum(p, axis=1, keepdims=True) for p in ps])
    if sink_col is not None:
        den = den + jnp.exp(sink_col - m)
    return ps, den


def _qk(q, k):
    return lax.dot_general(q, k, (((1,), (1,)), ((), ())), preferred_element_type=F32) * (HEAD_DIM ** -0.5)


def _attend_groups(groups):
    r = groups[0][0][0].shape[0]
    lo = lax.broadcasted_iota(jnp.int32, (r, LANES), 1) < HEAD_DIM
    zero = jnp.zeros((r, LANES), BF16)
    all_scores = []
    for q_tiles, parts, _ in groups:
        qs = jnp.concatenate([jnp.where(sel, t, zero) for t in q_tiles for sel in (lo, jnp.logical_not(lo))], axis=0)
        all_scores.append([post(_qk(qs, k2)) for k2, _, post in parts])
    outs = []
    for scores, (q_tiles, parts, sinks) in zip(all_scores, groups):
        n_rows = scores[0].shape[0]
        sink_col = None
        if sinks is not None:
            rows = lax.broadcasted_iota(jnp.int32, (n_rows, 1), 0)
            sink_col = jnp.full((n_rows, 1), sinks[0], F32)
            for hi in range(1, len(sinks)):
                sink_col = jnp.where(rows >= hi * r, sinks[hi], sink_col)
        ps, den = _softmax_parts(scores, sink_col)
        o = functools.reduce(jnp.add, [jnp.dot(p.astype(BF16), v2, preferred_element_type=F32)
                                       for p, (_, v2, _) in zip(ps, parts)]) / den
        outs.append([jnp.where(lo, o[2 * a * r:(2 * a + 1) * r], o[(2 * a + 1) * r:(2 * a + 2) * r])
                     for a in range(len(q_tiles))])
    return outs


def _identity(s):
    return s


def _ctx_attn_kernel(*refs, tiles_per_kv, has_sink, kv_head_stride):
    if has_sink:
        sink_ref, q_ref, k_ref, v_ref, o_ref, kout_ref, vout_ref = refs
    else:
        q_ref, k_ref, v_ref, o_ref, kout_ref, vout_ref = refs
    for h in range(kout_ref.shape[0]):
        kout_ref[h] = k_ref[:, h * kv_head_stride:h * kv_head_stride + HEAD_DIM]
        vout_ref[h] = v_ref[:, h * kv_head_stride:h * kv_head_stride + HEAD_DIM]
    groups = []
    for t in range(k_ref.shape[1] // LANES):
        k2 = k_ref[:, t * LANES:(t + 1) * LANES].astype(BF16)
        v2 = v_ref[:, t * LANES:(t + 1) * LANES].astype(BF16)
        first = t * tiles_per_kv
        q_tiles = [q_ref[:, (first + a) * LANES:(first + a + 1) * LANES] for a in range(tiles_per_kv)]
        sinks = [sink_ref[2 * first + hi] for hi in range(2 * tiles_per_kv)] if has_sink else None
        groups.append((q_tiles, [(k2, v2, _identity)], sinks))
    for t, outs in enumerate(_attend_groups(groups)):
        for a, o in enumerate(outs):
            tile = t * tiles_per_kv + a
            o_ref[:, tile * LANES:(tile + 1) * LANES] = o.astype(o_ref.dtype)


def _ctx_attn(q, kv, n_kv_cols, n_kv_heads, sink):
    nq = q.shape[1]
    has_sink = sink is not None
    cache_spec = pl.BlockSpec((None, None, n_kv_heads, SEQ, HEAD_DIM), lambda b: (b, 0, 0, 0, 0))
    cache_shape = jax.ShapeDtypeStruct((BATCH, 1, n_kv_heads, SEQ, HEAD_DIM), F32)
    in_specs = [pl.BlockSpec((SEQ, nq), lambda b: (b, 0)),
                pl.BlockSpec((SEQ, n_kv_cols), lambda b: (b, 0)),
                pl.BlockSpec((SEQ, n_kv_cols), lambda b: (b, 1))]
    args = [q, kv, kv]
    if has_sink:
        in_specs = [pl.BlockSpec(memory_space=pltpu.SMEM)] + in_specs
        args = [sink] + args
    return pl.pallas_call(
        functools.partial(_ctx_attn_kernel, tiles_per_kv=nq // n_kv_cols, has_sink=has_sink,
                          kv_head_stride=n_kv_cols // n_kv_heads),
        grid=(BATCH,),
        in_specs=in_specs,
        out_specs=[pl.BlockSpec((SEQ, nq), lambda b: (b, 0)), cache_spec, cache_spec],
        out_shape=[jax.ShapeDtypeStruct((N_CTX, nq), BF16), cache_shape, cache_shape],
        compiler_params=_params("arbitrary"),
        name="ctx_attn",
    )(*args)


SWA_SPAN = Q_BLOCK + 2 * SWA_WINDOW
SWA_KV_COLS = SWA_KV * LANES


def _swa_lat_kernel(sink_ref, q_ref, k_ref, v_ref, kc_ref, vc_ref, o_ref):
    j = pl.program_id(1)
    start = pl.multiple_of(jnp.clip((j - 1) * Q_BLOCK, 0, DEC_SEQ - SWA_SPAN), Q_BLOCK)
    rows = 4 * Q_BLOCK
    qpos = j * Q_BLOCK + (lax.broadcasted_iota(jnp.int32, (rows, SWA_SPAN), 0) & (Q_BLOCK - 1))
    kpos = start + lax.broadcasted_iota(jnp.int32, (rows, SWA_SPAN), 1)
    in_window = jnp.abs(qpos - kpos) <= SWA_WINDOW

    def window(s):
        return jnp.where(in_window, s, NEG_INF)

    groups = []
    for t in range(SWA_KV):
        cols = slice(t * LANES, (t + 1) * LANES)
        k_loc = k_ref[pl.ds(start, SWA_SPAN), cols].astype(BF16)
        v_loc = v_ref[pl.ds(start, SWA_SPAN), cols].astype(BF16)
        q_tiles = [q_ref[:, (2 * t + a) * LANES:(2 * t + a + 1) * LANES] for a in range(2)]
        sinks = [sink_ref[4 * t + hi] for hi in range(4)]
        groups.append((q_tiles, [(k_loc, v_loc, window), (kc_ref[:, cols], vc_ref[:, cols], _identity)], sinks))
    for t, outs in enumerate(_attend_groups(groups)):
        for a, o in enumerate(outs):
            o_ref[:, (2 * t + a) * LANES:(2 * t + a + 1) * LANES] = o.astype(o_ref.dtype)


def _swa_latent(q, kv, kc, vc, sink):
    nq = q.shape[1]
    rb = N_CTX // DEC_SEQ
    qb = N_CTX // Q_BLOCK
    nj = DEC_SEQ // Q_BLOCK
    cspec = pl.BlockSpec((None, SEQ, SWA_KV_COLS), lambda b, j: (b, 0, 0))
    return pl.pallas_call(
        _swa_lat_kernel,
        grid=(DEC_BATCH, nj),
        in_specs=[pl.BlockSpec(memory_space=pltpu.SMEM),
                  pl.BlockSpec((Q_BLOCK, nq), lambda b, j: (qb + b * nj + j, 0)),
                  pl.BlockSpec((DEC_SEQ, SWA_KV_COLS), lambda b, j: (rb + b, 0)),
                  pl.BlockSpec((DEC_SEQ, SWA_KV_COLS), lambda b, j: (rb + b, 1)),
                  cspec, cspec],
        out_specs=pl.BlockSpec((Q_BLOCK, nq), lambda b, j: (b * nj + j, 0)),
        out_shape=jax.ShapeDtypeStruct((N_LAT, nq), BF16),
        compiler_params=_params("arbitrary", "arbitrary"),
        name="swa_latent",
    )(sink, q, kv, kv, kc, vc)


NA_QT = 256
NA_SPAN = 768
NA_TILES = 4


def _na_start(j):
    return (j // 2) * (DEC_SEQ - NA_SPAN)


NA_ROWS = DEC_SEQ // GRID_W
NA_DR = 2 * NA_KH - 1
NA_DC = 2 * NA_KW - 1


def _na_blocks_kernel(rpb_ref, onehot_ref, valid_ref, o_ref):
    t = jnp.dot(rpb_ref[...], onehot_ref[...], preferred_element_type=F32, precision=lax.Precision.HIGHEST)
    o_ref[...] = jnp.where(valid_ref[...] > 0.5, t, NEG_INF)


def _na_bias_blocks(rpb):
    h = rpb.shape[0]
    kpad = 32
    cq, ck = np.meshgrid(np.arange(GRID_W), np.arange(GRID_W), indexing="ij")
    dc = (np.clip(ck - cq, -(NA_KW - 1), NA_KW - 1) + NA_KW - 1).reshape(-1)
    cs = np.clip(cq - NA_KW // 2, 0, GRID_W - NA_KW)
    valid = ((ck >= cs) & (ck < cs + NA_KW)).reshape(1, -1).astype(np.float32)
    onehot = (np.arange(kpad)[:, None] == dc[None, :]).astype(np.float32)
    rpb2 = jnp.pad(rpb.reshape(h * NA_DR, NA_DC), ((0, 0), (0, kpad - NA_DC)))
    n = GRID_W * GRID_W
    blocks = pl.pallas_call(
        _na_blocks_kernel,
        grid=(1,),
        in_specs=[pl.BlockSpec((h * NA_DR, kpad), lambda i: (0, 0)),
                  pl.BlockSpec((kpad, n), lambda i: (0, 0)),
                  pl.BlockSpec((1, n), lambda i: (0, 0))],
        out_specs=pl.BlockSpec((h * NA_DR, n), lambda i: (0, 0)),
        out_shape=jax.ShapeDtypeStruct((h * NA_DR, n), F32),
        compiler_params=_params("arbitrary"),
        name="na_bias_blocks",
    )(rpb2, jnp.asarray(onehot), jnp.asarray(valid))
    blocks = blocks.reshape(h, NA_DR, GRID_W, GRID_W)
    padded = jnp.pad(blocks, ((0, 0), (1, 1), (0, 0), (0, 0)), constant_values=NEG_INF)
    return jnp.concatenate([padded[:, :-1], padded[:, 1:]], axis=-1)


def _na_lat_kernel(q_ref, k_ref, v_ref, kc_ref, vc_ref, blk_ref, o_ref, bias_scr):
    j = pl.program_id(0)
    start = pl.multiple_of(_na_start(j), 256)

    @pl.when(pl.program_id(2) == 0)
    def _():
        lane_lo = lax.broadcasted_iota(jnp.int32, (GRID_W, LANES), 1) < GRID_W
        for rq_l in range(NA_QT // GRID_W):
            rq = j * (NA_QT // GRID_W) + rq_l
            rs = jnp.clip(rq - NA_KH // 2, 0, NA_ROWS - NA_KH)
            for kp in range(NA_SPAN // LANES):
                rk = start // GRID_W + 2 * kp
                idx = jnp.clip(rk - rq + NA_KH, 0, NA_DR)
                in_band = [jnp.logical_and(r >= rs, r < rs + NA_KH).astype(jnp.int32) for r in (rk, rk + 1)]
                ok = jnp.where(lane_lo, in_band[0], in_band[1]) > 0
                for hh in range(2 * NA_TILES):
                    bias_scr[hh * NA_QT + rq_l * GRID_W:hh * NA_QT + (rq_l + 1) * GRID_W,
                             kp * LANES:(kp + 1) * LANES] = jnp.where(ok, blk_ref[hh, idx], NEG_INF)

    groups = []
    for t in range(NA_TILES):
        cols = slice(t * LANES, (t + 1) * LANES)

        def add_bias(s, t=t):
            return s + bias_scr[2 * t * NA_QT:2 * (t + 1) * NA_QT, :]

        k_loc = k_ref[pl.ds(start, NA_SPAN), cols].astype(BF16)
        v_loc = v_ref[pl.ds(start, NA_SPAN), cols].astype(BF16)
        groups.append(([q_ref[:, cols]], [(k_loc, v_loc, add_bias), (kc_ref[:, cols], vc_ref[:, cols], _identity)],
                       None))
    for t, (o,) in enumerate(_attend_groups(groups)):
        o_ref[:, t * LANES:(t + 1) * LANES] = o.astype(o_ref.dtype)


def _na_latent(q, kv, kc, vc, blocks):
    nq = q.shape[1]
    width = NA_TILES * LANES
    n_steps = nq // width
    nj = DEC_SEQ // NA_QT
    rb = N_CTX // DEC_SEQ
    qb = N_CTX // NA_QT
    cspec = pl.BlockSpec((None, SEQ, width), lambda j, p, b: (b, 0, p))
    return pl.pallas_call(
        _na_lat_kernel,
        grid=(nj, n_steps, DEC_BATCH),
        in_specs=[pl.BlockSpec((NA_QT, width), lambda j, p, b: (qb + b * nj + j, p)),
                  pl.BlockSpec((DEC_SEQ, width), lambda j, p, b: (rb + b, p)),
                  pl.BlockSpec((DEC_SEQ, width), lambda j, p, b: (rb + b, n_steps + p)),
                  cspec, cspec,
                  pl.BlockSpec((2 * NA_TILES, NA_DR + 1, GRID_W, LANES), lambda j, p, b: (p, 0, 0, 0))],
        out_specs=pl.BlockSpec((NA_QT, width), lambda j, p, b: (b * nj + j, p)),
        out_shape=jax.ShapeDtypeStruct((N_LAT, nq), BF16),
        scratch_shapes=[pltpu.VMEM((2 * NA_TILES * NA_QT, NA_SPAN), F32)],
        compiler_params=_params("arbitrary", "arbitrary", "arbitrary"),
        name="na_latent",
    )(q, kv, kv, kc, vc, blocks)


def _rope_tables(width):
    quarter = HEAD_DIM // 4
    pos = np.arange(DEC_SEQ)
    inv = np.power(ROPE_BASE, -np.arange(quarter, dtype=np.float32) / quarter).astype(np.float32)
    d = np.arange(width) % HEAD_DIM
    p = np.where((d < HEAD_DIM // 2)[None, :], (pos // GRID_W)[:, None], (pos % GRID_W)[:, None]).astype(np.float32)
    ang = p * inv[d % quarter][None, :]
    sign = np.where((d // quarter) % 2 == 0, -1.0, 1.0)[None, :]
    return jnp.asarray(np.cos(ang), F32), jnp.asarray(np.sin(ang) * sign, F32)


def _cache_rows(cache, dup):
    b, h, s, hd = cache.shape
    rows = jnp.broadcast_to(cache.transpose(0, 2, 1, 3)[:, :, :, None, :], (b, s, h, dup, hd))
    return rows.reshape(b, s, h * dup * hd).astype(BF16)


def _swa_layer(x, mod, nw, w_qkv, sink, w_o, cache_k, cache_v):
    nq, nk = SWA_HEADS * HEAD_DIM, SWA_KV * HEAD_DIM

    def dup_heads(w):
        return jnp.broadcast_to(w.reshape(D_MODEL, SWA_KV, 1, HEAD_DIM),
                                (D_MODEL, SWA_KV, 2, HEAD_DIM)).reshape(D_MODEL, SWA_KV_COLS)

    w = jnp.concatenate([w_qkv[:, :nq], dup_heads(w_qkv[:, nq:nq + nk]), dup_heads(w_qkv[:, nq + nk:])], axis=1)
    q, kv = _qkv_proj(x, mod, nw, w.astype(BF16), nq, rope_cols=nq + SWA_KV_COLS)
    oc, k_new, v_new = _ctx_attn(q, kv, SWA_KV_COLS, SWA_KV, sink)
    ol = _swa_latent(q, kv, _cache_rows(cache_k, 2), _cache_rows(cache_v, 2), sink)
    return _mm_res(oc, ol, w_o, x, mod), k_new, v_new


def _na_layer(x, mod, nw, w_qkv, rpb, w_o, cache_k, cache_v):
    n = NA_HEADS * HEAD_DIM
    q, kv = _qkv_proj(x, mod, nw, w_qkv, n, tn=NA_QKV_TN)
    oc, k_new, v_new = _ctx_attn(q, kv, n, NA_HEADS, None)
    ol = _na_latent(q, kv, _cache_rows(cache_k, 1), _cache_rows(cache_v, 1), _na_bias_blocks(rpb))
    return _mm_res(oc, ol, w_o, x, mod), k_new, v_new


def kernel(x_prompt, x_sample, state_mlstm_C, state_mlstm_n, state_mlstm_m, cache_swa_k, cache_swa_v, cache_na_k, cache_na_v, c, c_ctx, ada_w, ada_b, norm_w, final_norm_w, ffn_w_up, ffn_conv_w, ffn_conv_b, ffn_w_down, ml_w_up, ml_conv_w, ml_conv_b, ml_w_qk, ml_w_v, ml_w_gate, ml_b_gate, ml_norm_w, ml_skip, ml_w_down, swa_w_qkv, swa_sink, swa_w_o, na_w_qkv, na_rpb, na_w_o):
    x = (x_prompt.reshape(N_CTX, D_MODEL), x_sample.reshape(N_LAT, D_MODEL))
    cond = jnp.concatenate([c_ctx[None], c, jnp.zeros((MOD_ROWS - 1 - DEC_BATCH, D_MODEL), F32)], axis=0)
    mods = _ada_mod(cond, ada_w, ada_b)
    ffn_weights = _ffn_weight_blocks(ffn_w_up, ffn_conv_w, ffn_conv_b, ffn_w_down)
    ml_up_blocks = _ml_up_weight_blocks(ml_w_up)
    ml_qk_bf16, ml_v_bf16, ml_down_bf16 = ml_w_qk.astype(BF16), ml_w_v.astype(BF16), ml_w_down.astype(BF16)

    new_c, new_n, new_m = None, [], []
    new_sk = new_sv = new_nk = new_nv = None
    for i in range(DEPTH):
        kind, j = i % N_MIXERS, i // N_MIXERS
        mod = mods[i]
        if kind == 0:
            gate_w = jnp.pad(ml_w_gate[j], ((0, 0), (0, 128 - 4 * ML_HEADS))).astype(BF16)
            gate_b = jnp.pad(ml_b_gate[j], (0, 128 - 4 * ML_HEADS)).reshape(1, 128)
            p = dict(w_up=(ml_up_blocks, j), conv_w=ml_conv_w[j], conv_b=ml_conv_b[j],
                     w_qk=(ml_qk_bf16, j), w_v=(ml_v_bf16, j), w_gate=gate_w, b_gate=gate_b,
                     norm_w=ml_norm_w[j], skip=ml_skip[j], w_down=(ml_down_bf16, j))
            n0 = jnp.broadcast_to(state_mlstm_n[:, j].transpose(0, 2, 1, 3)[..., None],
                                  (DEC_BATCH, ML_HEADS, 2, ML_DK, LANES))
            m0 = jnp.broadcast_to(state_mlstm_m[:, j].transpose(0, 2, 1)[..., None], (DEC_BATCH, ML_HEADS, 2, LANES))
            x, new_c, nf, mf = _mlstm_layer(x, mod, norm_w[i, 0], p, (state_mlstm_C, j, n0, m0), (j, new_c))
            new_n.append(nf)
            new_m.append(mf)
        elif kind == 1:
            x, k_new, v_new = _swa_layer(x, mod, norm_w[i, 0], swa_w_qkv[j].astype(BF16), swa_sink[j],
                                         swa_w_o[j].astype(BF16), cache_swa_k[:, j], cache_swa_v[:, j])
            new_sk, new_sv = k_new, v_new
        else:
            x, k_new, v_new = _na_layer(x, mod, norm_w[i, 0], na_w_qkv[j].astype(BF16), na_rpb[j],
                                        na_w_o[j].astype(BF16), cache_na_k[:, j], cache_na_v[:, j])
            new_nk, new_nv = k_new, v_new
        x = _conv_ffn(x, mod, norm_w[i, 1], i, *ffn_weights, final_w=final_norm_w if i == DEPTH - 1 else None)

    y_ctx, y_lat = x
    return (y_ctx.reshape(BATCH, SEQ, D_MODEL), y_lat.reshape(DEC_BATCH, DEC_SEQ, D_MODEL),
            new_c, jnp.stack(new_n, axis=1), jnp.stack(new_m, axis=1),
            new_sk, new_sv, new_nk, new_nv)
```
